```python
import jax, jax.numpy as jnp
from jax import lax
import numpy as np

D_MODEL = 1024
BATCH = 8
SEQ = 2048
DEPTH = 2

CONV_WIDTH = D_MODEL // 2
CONV_GROUPS = 8
CONV_KERNEL = 31
N_HEADS = 8
HEAD_DIM = 64
N_KV_HEADS = 2
GQA = N_HEADS // N_KV_HEADS
ATTN_WIDTH = N_HEADS * HEAD_DIM
KV_WIDTH = N_KV_HEADS * HEAD_DIM
MIX_WIDTH = CONV_WIDTH + ATTN_WIDTH
N_BRANCH = 3
IN_WIDTH = 2 * CONV_WIDTH + ATTN_WIDTH + 6 * KV_WIDTH + N_HEADS * N_BRANCH
CMP_BLOCK = 32
CMP_STRIDE = 16
CMP_HIDDEN = 256
SEL_BLOCK = 64
SEL_TOPK = 8
WINDOW = 512
Q_BLOCK = 128
FORCE_BONUS = 1e4
NEG_INF = -1e30
D_FF = 2816
N_EXPERTS = 8
TOP_K = 2
D_FF_EXPERT = 3584
N_DENSE = (DEPTH + 1) // 2
N_MOE = DEPTH // 2
EPS = 1e-6

kernel_name = "hymba_conformer_nsa_moe_block"


def rms_norm(x, g):
    xf = x.astype(jnp.float32)
    xf = xf * lax.rsqrt(jnp.mean(xf * xf, axis=-1, keepdims=True) + EPS)
    return (xf * g.astype(jnp.float32)).astype(x.dtype)


def masked_softmax(s, mask):
    s = jnp.where(mask, s.astype(jnp.float32), NEG_INF)
    m = jnp.max(s, axis=-1, keepdims=True)
    p = jnp.where(mask, jnp.exp(s - m), 0.0)
    return p / jnp.maximum(jnp.sum(p, axis=-1, keepdims=True), 1e-30)


def conv_module(val, gate, dw_w, dw_b, ln_g, ln_b):
    B, S, C = val.shape
    u = val * jax.nn.sigmoid(gate)
    u = jnp.pad(u, ((0, 0), (CONV_KERNEL - 1, 0), (0, 0)))
    y = lax.conv_general_dilated(u, dw_w[:, None, :], window_strides=(1,), padding="VALID",
                                 dimension_numbers=("NWC", "WIO", "NWC"),
                                 feature_group_count=C) + dw_b
    yg = y.reshape(B, S, CONV_GROUPS, C // CONV_GROUPS).astype(jnp.float32)
    mu = jnp.mean(yg, axis=-1, keepdims=True)
    var = jnp.mean(jnp.square(yg - mu), axis=-1, keepdims=True)
    yn = ((yg - mu) * lax.rsqrt(var + EPS)).reshape(B, S, C)
    yn = yn * ln_g.astype(jnp.float32) + ln_b.astype(jnp.float32)
    return jax.nn.silu(yn).astype(val.dtype)


def nsa_attention(q, k_cmp, v_cmp, k_sel, v_sel, k_win, v_win, gate_logits, q_g, k_g,
                  pos_k, pos_v, ck_w1, ck_b1, ck_w2, ck_b2, cv_w1, cv_b1, cv_w2, cv_b2):
    dtype = q.dtype
    B, S = q.shape[:2]
    t = jnp.arange(S)

    def heads_kv(a):
        return a.reshape(B, S, N_KV_HEADS, HEAD_DIM).transpose(0, 2, 1, 3)

    qh = rms_norm(q.reshape(B, S, N_KV_HEADS, GQA, HEAD_DIM), q_g).transpose(0, 2, 3, 1, 4)
    qh = qh * (HEAD_DIM ** -0.5)
    ks = rms_norm(heads_kv(k_sel), k_g[1])
    vs = heads_kv(v_sel)
    kw = rms_norm(heads_kv(k_win), k_g[2])
    vw = heads_kv(v_win)

    n_cmp = (S - CMP_BLOCK) // CMP_STRIDE + 1
    c_start = jnp.arange(n_cmp) * CMP_STRIDE
    cidx = c_start[:, None] + jnp.arange(CMP_BLOCK)[None, :]

    def compress(a, pos, w1, b1, w2, b2):
        blk = heads_kv(a)[:, :, cidx] + pos
        blk = blk.reshape(B, N_KV_HEADS, n_cmp, CMP_BLOCK * HEAD_DIM)
        return jax.nn.silu(blk @ w1 + b1) @ w2 + b2

    kc = rms_norm(compress(k_cmp, pos_k, ck_w1, ck_b1, ck_w2, ck_b2), k_g[0])
    vc = compress(v_cmp, pos_v, cv_w1, cv_b1, cv_w2, cv_b2)
    s_c = jnp.einsum("bkgsd,bkcd->bkgsc", qh, kc)
    mask_c = (c_start + CMP_BLOCK - 1)[None, :] <= t[:, None]
    p_c = masked_softmax(s_c, mask_c)
    o_cmp = jnp.einsum("bkgsc,bkcd->bkgsd", p_c.astype(vc.dtype), vc)

    n_sel = S // SEL_BLOCK
    j = jnp.arange(n_sel)
    overlap = ((c_start[:, None] < (j[None, :] + 1) * SEL_BLOCK)
               & (c_start[:, None] + CMP_BLOCK > j[None, :] * SEL_BLOCK)).astype(jnp.float32)
    imp = jnp.einsum("bkgsc,cj->bksj", p_c, overlap)
    cur = (t // SEL_BLOCK)[:, None]
    forced = (j[None, :] == 0) | (j[None, :] == cur) | (j[None, :] == cur - 1)
    causal_blk = j[None, :] * SEL_BLOCK <= t[:, None]
    score = jnp.where(causal_blk, imp + jnp.where(forced, FORCE_BONUS, 0.0), NEG_INF)
    k_eff = min(SEL_TOPK, n_sel)
    _, sel_idx = lax.top_k(score, k_eff)

    n_qb = S // Q_BLOCK
    kblk = ks.reshape(B, N_KV_HEADS, n_sel, SEL_BLOCK, HEAD_DIM)
    vblk = vs.reshape(B, N_KV_HEADS, n_sel, SEL_BLOCK, HEAD_DIM)
    pad = ((0, 0), (0, 0), (WINDOW, 0), (0, 0))
    kw_pad = jnp.pad(kw, pad)
    vw_pad = jnp.pad(vw, pad)
    q_chunks = qh.reshape(B, N_KV_HEADS, GQA, n_qb, Q_BLOCK, HEAD_DIM).transpose(3, 0, 1, 2, 4, 5)
    idx_chunks = sel_idx.reshape(B, N_KV_HEADS, n_qb, Q_BLOCK, k_eff).transpose(2, 0, 1, 3, 4)
    bi = jnp.arange(B)[:, None, None, None]
    hi = jnp.arange(N_KV_HEADS)[None, :, None, None]
    band = jnp.arange(WINDOW + Q_BLOCK)

    def block_step(args):
        qc, ic, n = args
        start = n * Q_BLOCK
        tq = start + jnp.arange(Q_BLOCK)
        kg = kblk[bi, hi, ic]
        vg = vblk[bi, hi, ic]
        kpos = ic[..., None] * SEL_BLOCK + jnp.arange(SEL_BLOCK)
        m_s = (kpos <= tq[:, None, None]).reshape(B, N_KV_HEADS, 1, Q_BLOCK, k_eff * SEL_BLOCK)
        s_s = jnp.einsum("bkgqd,bkqnld->bkgqnl", qc, kg).reshape(
            B, N_KV_HEADS, GQA, Q_BLOCK, k_eff * SEL_BLOCK)
        p_s = masked_softmax(s_s, m_s).reshape(B, N_KV_HEADS, GQA, Q_BLOCK, k_eff, SEL_BLOCK)
        o_s = jnp.einsum("bkgqnl,bkqnld->bkgqd", p_s.astype(vg.dtype), vg)
        kb = lax.dynamic_slice_in_dim(kw_pad, start, WINDOW + Q_BLOCK, axis=2)
        vb = lax.dynamic_slice_in_dim(vw_pad, start, WINDOW + Q_BLOCK, axis=2)
        wpos = start - WINDOW + band
        m_w = ((wpos[None, :] <= tq[:, None]) & (wpos[None, :] > tq[:, None] - WINDOW)
               & (wpos[None, :] >= 0))
        s_w = jnp.einsum("bkgqd,bkjd->bkgqj", qc, kb)
        p_w = masked_softmax(s_w, m_w)
        o_w = jnp.einsum("bkgqj,bkjd->bkgqd", p_w.astype(vb.dtype), vb)
        return o_s, o_w

    o_sel, o_win = lax.map(block_step, (q_chunks, idx_chunks, jnp.arange(n_qb)))

    def unchunk(o):
        return o.transpose(1, 2, 3, 0, 4, 5).reshape(B, N_KV_HEADS, GQA, S, HEAD_DIM)

    g = jax.nn.sigmoid(gate_logits.astype(jnp.float32)).reshape(
        B, S, N_KV_HEADS, GQA, N_BRANCH).transpose(0, 2, 3, 1, 4)
    o = (g[..., 0:1] * o_cmp.astype(jnp.float32) + g[..., 1:2] * unchunk(o_sel).astype(jnp.float32)
         + g[..., 2:3] * unchunk(o_win).astype(jnp.float32))
    return o.transpose(0, 3, 1, 2, 4).reshape(B, S, ATTN_WIDTH).astype(dtype)


def swiglu(h, w1, w3, w2):
    return (jax.nn.silu(h @ w1) * (h @ w3)) @ w2


def moe_swiglu(h, router, w1, w3, w2):
    logits = (h @ router).astype(jnp.float32)
    top_vals, top_idx = lax.top_k(logits, TOP_K)
    top_w = jax.nn.softmax(top_vals, axis=-1)
    combine = jnp.sum(jax.nn.one_hot(top_idx, N_EXPERTS, dtype=jnp.float32) * top_w[..., None], axis=-2)
    y = jnp.zeros(h.shape, jnp.float32)
    for e in range(N_EXPERTS):
        y = y + combine[..., e:e + 1] * swiglu(h, w1[e], w3[e], w2[e]).astype(jnp.float32)
    return y.astype(h.dtype)


def setup_inputs(seed: int = 0) -> dict:
    key = jax.random.key(seed)
    ks = jax.random.split(key, 32)

    def nrm(k, shape, scale):
        return jax.random.normal(k, shape, jnp.float32) * scale

    L, Dh = DEPTH, HEAD_DIM
    return {
        "x": nrm(ks[0], (BATCH, SEQ, D_MODEL), 1.0),
        "attn_norm_g": 1.0 + nrm(ks[1], (L, D_MODEL), 0.02),
        "w_in": nrm(ks[2], (L, D_MODEL, IN_WIDTH), D_MODEL ** -0.5),
        "conv_dw_w": nrm(ks[3], (L, CONV_KERNEL, CONV_WIDTH), CONV_KERNEL ** -0.5),
        "conv_dw_b": nrm(ks[4], (L, CONV_WIDTH), 0.02),
        "conv_ln_g": 1.0 + nrm(ks[5], (L, CONV_WIDTH), 0.02),
        "conv_ln_b": nrm(ks[6], (L, CONV_WIDTH), 0.02),
        "q_norm_g": 1.0 + nrm(ks[7], (L, Dh), 0.02),
        "k_norm_g": 1.0 + nrm(ks[8], (L, N_BRANCH, Dh), 0.02),
        "cmp_pos_k": nrm(ks[9], (L, CMP_BLOCK, Dh), 0.1),
        "cmp_pos_v": nrm(ks[10], (L, CMP_BLOCK, Dh), 0.1),
        "cmp_k_w1": nrm(ks[11], (L, CMP_BLOCK * Dh, CMP_HIDDEN), (CMP_BLOCK * Dh) ** -0.5),
        "cmp_k_b1": nrm(ks[12], (L, CMP_HIDDEN), 0.02),
        "cmp_k_w2": nrm(ks[13], (L, CMP_HIDDEN, Dh), CMP_HIDDEN ** -0.5),
        "cmp_k_b2": nrm(ks[14], (L, Dh), 0.02),
        "cmp_v_w1": nrm(ks[15], (L, CMP_BLOCK * Dh, CMP_HIDDEN), (CMP_BLOCK * Dh) ** -0.5),
        "cmp_v_b1": nrm(ks[16], (L, CMP_HIDDEN), 0.02),
        "cmp_v_w2": nrm(ks[17], (L, CMP_HIDDEN, Dh), CMP_HIDDEN ** -0.5),
        "cmp_v_b2": nrm(ks[18], (L, Dh), 0.02),
        "w_out": nrm(ks[19], (L, MIX_WIDTH, D_MODEL), MIX_WIDTH ** -0.5),
        "ffn_norm_g": 1.0 + nrm(ks[20], (L, D_MODEL), 0.02),
        "dense_w1": nrm(ks[21], (N_DENSE, D_MODEL, D_FF), D_MODEL ** -0.5),
        "dense_w3": nrm(ks[22], (N_DENSE, D_MODEL, D_FF), D_MODEL ** -0.5),
        "dense_w2": nrm(ks[23], (N_DENSE, D_FF, D_MODEL), D_FF ** -0.5),
        "router_w": nrm(ks[24], (N_MOE, D_MODEL, N_EXPERTS), D_MODEL ** -0.5),
        "moe_w1": nrm(ks[25], (N_MOE, N_EXPERTS, D_MODEL, D_FF_EXPERT), D_MODEL ** -0.5),
        "moe_w3": nrm(ks[26], (N_MOE, N_EXPERTS, D_MODEL, D_FF_EXPERT), D_MODEL ** -0.5),
        "moe_w2": nrm(ks[27], (N_MOE, N_EXPERTS, D_FF_EXPERT, D_MODEL), D_FF_EXPERT ** -0.5),
    }


def reference(x, attn_norm_g, w_in, conv_dw_w, conv_dw_b, conv_ln_g, conv_ln_b, q_norm_g, k_norm_g,
              cmp_pos_k, cmp_pos_v, cmp_k_w1, cmp_k_b1, cmp_k_w2, cmp_k_b2,
              cmp_v_w1, cmp_v_b1, cmp_v_w2, cmp_v_b2, w_out, ffn_norm_g,
              dense_w1, dense_w3, dense_w2, router_w, moe_w1, moe_w3, moe_w2):
    sizes = [CONV_WIDTH, CONV_WIDTH, ATTN_WIDTH] + [KV_WIDTH] * 6 + [N_HEADS * N_BRANCH]
    offsets = [int(o) for o in np.cumsum(sizes)[:-1]]
    for layer in range(DEPTH):
        h = rms_norm(x, attn_norm_g[layer])
        z = h @ w_in[layer]
        c_val, c_gate, q, k_c, v_c, k_s, v_s, k_w, v_w, g_log = jnp.split(z, offsets, axis=-1)
        conv_out = conv_module(c_val, c_gate, conv_dw_w[layer], conv_dw_b[layer],
                               conv_ln_g[layer], conv_ln_b[layer])
        attn_out = nsa_attention(q, k_c, v_c, k_s, v_s, k_w, v_w, g_log,
                                 q_norm_g[layer], k_norm_g[layer], cmp_pos_k[layer], cmp_pos_v[layer],
                                 cmp_k_w1[layer], cmp_k_b1[layer], cmp_k_w2[layer], cmp_k_b2[layer],
                                 cmp_v_w1[layer], cmp_v_b1[layer], cmp_v_w2[layer], cmp_v_b2[layer])
        x = x + jnp.concatenate([conv_out, attn_out], axis=-1) @ w_out[layer]
        h = rms_norm(x, ffn_norm_g[layer])
        if layer % 2 == 0:
            i = layer // 2
            x = x + swiglu(h, dense_w1[i], dense_w3[i], dense_w2[i])
        else:
            i = layer // 2
            x = x + moe_swiglu(h, router_w[i], moe_w1[i], moe_w3[i], moe_w2[i])
    return x
```

```python
import functools

import numpy as np
import jax
import jax.numpy as jnp
from jax import lax
from jax.experimental import pallas as pl
from jax.experimental.pallas import tpu as pltpu

D_MODEL = 1024
BATCH = 8
SEQ = 2048
DEPTH = 2
TOKENS = BATCH * SEQ

CONV_WIDTH = 512
CONV_GROUP = 64
CONV_KERNEL = 31
N_HEADS = 8
HEAD_DIM = 64
N_KV_HEADS = 2
GQA = N_HEADS // N_KV_HEADS
ATTN_WIDTH = N_HEADS * HEAD_DIM
KV_WIDTH = N_KV_HEADS * HEAD_DIM
N_BRANCH = 3
CMP_BLOCK = 32
CMP_STRIDE = 16
CMP_HIDDEN = 256
N_CMP = (SEQ - CMP_BLOCK) // CMP_STRIDE + 1
SEL_BLOCK = 64
SEL_TOPK = 8
N_SEL = SEQ // SEL_BLOCK
WINDOW = 512
Q_BLOCK = 128
FORCE_BONUS = 1e4
NEG_INF = -1e30
D_FF = 2816
N_EXPERTS = 8
D_FF_EXPERT = 3584
EPS = 1e-6

LANES = 128
SUBLANES = 8
VMEM_LIMIT = 48 * 1024 * 1024

F32 = jnp.float32
BF16 = jnp.bfloat16

_OFF_VAL = 0
_OFF_GATE = 512
_OFF_Q = 1024
_OFF_KCVC = 1536
_OFF_KS = 1792
_OFF_VS = 2048
_OFF_KW = 2304
_OFF_VW = 2560
_OFF_GLOG = 2816
IN_COLS = 2944


def _in_proj_columns():
    base_q = 2 * CONV_WIDTH
    base_kv = base_q + ATTN_WIDTH
    cols = list(range(0, base_kv + 2 * KV_WIDTH))
    for piece in range(2, 6):
        start = base_kv + piece * KV_WIDTH
        for head in range(N_KV_HEADS):
            head_cols = list(range(start + head * HEAD_DIM, start + (head + 1) * HEAD_DIM))
            cols += head_cols + head_cols
    glog = base_kv + 6 * KV_WIDTH
    cols += list(range(glog, glog + N_HEADS * N_BRANCH)) + [-1] * (LANES - N_HEADS * N_BRANCH)
    assert len(cols) == IN_COLS
    return np.asarray(cols, np.int32)


_IN_COLS_SRC = _in_proj_columns()


def _sigmoid(v):
    return 1.0 / (1.0 + jnp.exp(-v))


def _dot(a, b):
    return jnp.dot(a, b, preferred_element_type=F32)


def _dot_nt(a, b):
    return lax.dot_general(a, b, (((1,), (1,)), ((), ())), preferred_element_type=F32)


def _split_dot(v, m):
    hi = v.astype(BF16)
    lo = (v - hi.astype(F32)).astype(BF16)
    return _dot(hi, m) + _dot(lo, m)


IN_TM = 512


def _in_proj_kernel(x_ref, g_ref, w_ref, qg_ref, kgs_ref, kgw_ref, bd_ref,
                    u_ref, q_ref, kcvc_ref, ks_ref, vs_ref, kw_ref, vw_ref, gate_ref):
    x = x_ref[...]
    ms = jnp.mean(x * x, axis=-1, keepdims=True)
    h = (x * lax.rsqrt(ms + EPS) * g_ref[...]).astype(BF16)

    def proj(lo, width):
        return _dot(h, w_ref[:, lo:lo + width])

    u_ref[...] = proj(_OFF_VAL, CONV_WIDTH) * _sigmoid(proj(_OFF_GATE, CONV_WIDTH))

    bd = bd_ref[...]
    q = proj(_OFF_Q, ATTN_WIDTH)
    for c in range(ATTN_WIDTH // LANES):
        qc = q[:, c * LANES:(c + 1) * LANES]
        msq = _split_dot(qc * qc, bd)
        q_ref[:, c * LANES:(c + 1) * LANES] = (qc * lax.rsqrt(msq + EPS) * qg_ref[...]).astype(BF16)

    kcvc_ref[...] = proj(_OFF_KCVC, 2 * KV_WIDTH)

    def normed_k(off, kg_ref, out_ref):
        k = proj(off, N_KV_HEADS * LANES)
        for c in range(N_KV_HEADS):
            kc = k[:, c * LANES:(c + 1) * LANES]
            msk = jnp.mean(kc * kc, axis=-1, keepdims=True)
            out_ref[:, c * LANES:(c + 1) * LANES] = (kc * lax.rsqrt(msk + EPS) * kg_ref[...]).astype(BF16)

    normed_k(_OFF_KS, kgs_ref, ks_ref)
    normed_k(_OFF_KW, kgw_ref, kw_ref)
    vs_ref[...] = proj(_OFF_VS, 2 * LANES).astype(BF16)
    vw_ref[...] = proj(_OFF_VW, 2 * LANES).astype(BF16)
    gate_ref[...] = _sigmoid(proj(_OFF_GLOG, LANES))


def _in_proj(x2, g, w, qg, kgs, kgw, bd):
    tm = IN_TM
    row = lambda width: pl.BlockSpec((tm, width), lambda i: (i, 0))
    full = lambda a: pl.BlockSpec(a.shape, lambda i: (0,) * a.ndim)
    out_shape = (
        jax.ShapeDtypeStruct((TOKENS, CONV_WIDTH), F32),
        jax.ShapeDtypeStruct((TOKENS, ATTN_WIDTH), BF16),
        jax.ShapeDtypeStruct((TOKENS, 2 * KV_WIDTH), F32),
        jax.ShapeDtypeStruct((TOKENS, 2 * LANES), BF16),
        jax.ShapeDtypeStruct((TOKENS, 2 * LANES), BF16),
        jax.ShapeDtypeStruct((TOKENS, 2 * LANES), BF16),
        jax.ShapeDtypeStruct((TOKENS, 2 * LANES), BF16),
        jax.ShapeDtypeStruct((TOKENS, LANES), F32),
    )
    return pl.pallas_call(
        _in_proj_kernel,
        grid=(TOKENS // tm,),
        in_specs=[row(D_MODEL), full(g), full(w), full(qg), full(kgs), full(kgw), full(bd)],
        out_specs=tuple(row(s.shape[1]) for s in out_shape),
        out_shape=out_shape,
        compiler_params=pltpu.CompilerParams(
            dimension_semantics=("arbitrary",), vmem_limit_bytes=VMEM_LIMIT),
        name="in_proj",
    )(x2, g, w, qg, kgs, kgw, bd)


CONV_TR = 512
CONV_HALO = 32
CONV_RS = 64


def _conv_kernel(ucur_ref, uprev_ref, w_ref, b_ref, lg_ref, lb_ref, bd_ref, o_ref, win_ref, y_ref):
    i = pl.program_id(1)
    win_ref[0, 0:CONV_HALO, :] = jnp.where(i > 0, uprev_ref[0], 0.0)
    win_ref[0, CONV_HALO:CONV_HALO + CONV_TR, :] = ucur_ref[0]
    first = CONV_HALO - (CONV_KERNEL - 1)
    shifted_rows = CONV_HALO + CONV_TR - SUBLANES
    for s in range(1, SUBLANES):
        win_ref[s, 0:shifted_rows, :] = win_ref[0, s:s + shifted_rows, :]

    for c in range(CONV_WIDTH // LANES):
        lanes = slice(c * LANES, (c + 1) * LANES)

        def body(r, carry):
            base = pl.multiple_of(r * CONV_RS, CONV_RS)
            acc = jnp.zeros((CONV_RS, LANES), F32)
            for k in range(CONV_KERNEL):
                off = first + k
                rows = pl.ds(base + (off // SUBLANES) * SUBLANES, CONV_RS)
                acc = acc + win_ref[off % SUBLANES, rows, lanes] * w_ref[k:k + 1, lanes]
            y_ref[pl.ds(base, CONV_RS), lanes] = acc + b_ref[:, lanes]
            return carry

        lax.fori_loop(0, CONV_TR // CONV_RS, body, 0)

    bd = bd_ref[...]
    for c in range(CONV_WIDTH // LANES):
        lanes = slice(c * LANES, (c + 1) * LANES)
        y = y_ref[:, lanes]
        d = y - _split_dot(y, bd)
        var = _split_dot(d * d, bd)
        yn = d * lax.rsqrt(var + EPS) * lg_ref[:, lanes] + lb_ref[:, lanes]
        o_ref[0, :, lanes] = (yn * _sigmoid(yn)).astype(BF16)


def _conv_module(u3, w, b, lg, lb, bd):
    halo_blocks = CONV_TR // CONV_HALO
    full = lambda a: pl.BlockSpec(a.shape, lambda bi, i: (0,) * a.ndim)
    return pl.pallas_call(
        _conv_kernel,
        grid=(BATCH, SEQ // CONV_TR),
        in_specs=[
            pl.BlockSpec((1, CONV_TR, CONV_WIDTH), lambda bi, i: (bi, i, 0)),
            pl.BlockSpec((1, CONV_HALO, CONV_WIDTH),
                         lambda bi, i: (bi, jnp.maximum(i * halo_blocks - 1, 0), 0)),
            full(w), full(b), full(lg), full(lb), full(bd),
        ],
        out_specs=pl.BlockSpec((1, CONV_TR, CONV_WIDTH), lambda bi, i: (bi, i, 0)),
        out_shape=jax.ShapeDtypeStruct((BATCH, SEQ, CONV_WIDTH), BF16),
        scratch_shapes=[pltpu.VMEM((SUBLANES, CONV_HALO + CONV_TR, CONV_WIDTH), F32),
                        pltpu.VMEM((CONV_TR, CONV_WIDTH), F32)],
        compiler_params=pltpu.CompilerParams(dimension_semantics=("arbitrary", "arbitrary")),
        name="conv_module",
    )(u3, u3, w, b, lg, lb, bd)


CMP_ROWS = SEQ // CMP_STRIDE
CMP_CHUNK = CMP_STRIDE * HEAD_DIM


def _compress_kernel(x_ref, pos_ref, w1_ref, b1_ref, w2_ref, b2_ref, kg_ref, o_ref):
    kind = pl.program_id(0)
    xa = x_ref[0, 0, 0]
    xb = pltpu.roll(xa, CMP_ROWS - 1, axis=0)
    a = (xa + pos_ref[0, :, 0:CMP_CHUNK]).astype(BF16)
    b = (xb + pos_ref[0, :, CMP_CHUNK:2 * CMP_CHUNK]).astype(BF16)
    hid = _dot(a, w1_ref[0, 0:CMP_CHUNK, :]) + _dot(b, w1_ref[0, CMP_CHUNK:2 * CMP_CHUNK, :]) + b1_ref[0]
    hid = hid * _sigmoid(hid)
    out = _dot(hid.astype(BF16), w2_ref[0]) + b2_ref[0]
    ms = jnp.mean(out * out, axis=-1, keepdims=True)
    normed = out * lax.rsqrt(ms + EPS) * kg_ref[...]
    o_ref[0, 0, 0] = jnp.where(kind == 0, normed, out).astype(BF16)


def _compress(xr, pos, w1, b1, w2, b2, kg):
    per_kind = lambda a: pl.BlockSpec((1,) + a.shape[1:], lambda k, bi, h: (k,) + (0,) * (a.ndim - 1))
    return pl.pallas_call(
        _compress_kernel,
        grid=(2, BATCH, N_KV_HEADS),
        in_specs=[
            pl.BlockSpec((1, 1, 1, CMP_ROWS, CMP_CHUNK), lambda k, bi, h: (k, bi, h, 0, 0)),
            per_kind(pos), per_kind(w1), per_kind(b1), per_kind(w2), per_kind(b2),
            pl.BlockSpec(kg.shape, lambda k, bi, h: (0, 0)),
        ],
        out_specs=pl.BlockSpec((1, 1, 1, CMP_ROWS, LANES), lambda k, bi, h: (k, bi, h, 0, 0)),
        out_shape=jax.ShapeDtypeStruct((2, BATCH, N_KV_HEADS, CMP_ROWS, LANES), BF16),
        compiler_params=pltpu.CompilerParams(
            dimension_semantics=("arbitrary", "arbitrary", "arbitrary")),
        name="compress",
    )(xr, pos, w1, b1, w2, b2, kg)


ROWS = GQA * Q_BLOCK
SEL_CHUNK = 256
WIN_CHUNK = 128


def _online_update(s, mask, v, m_sc, l_sc, acc_sc):
    s = jnp.where(mask, s, NEG_INF)
    m_old = m_sc[...]
    m_new = jnp.maximum(m_old, jnp.max(s, axis=-1, keepdims=True))
    alpha = jnp.exp(m_old - m_new)
    p = jnp.where(mask, jnp.exp(s - m_new), 0.0)
    l_sc[...] = alpha * l_sc[...] + jnp.sum(p, axis=-1, keepdims=True)
    acc_sc[...] = alpha * acc_sc[...] + _dot(p.astype(BF16), v)
    m_sc[...] = m_new


def _attn_kernel(q_ref, gate_ref, kc_ref, vc_ref, ks_ref, vs_ref, kw_ref, vw_ref, ovl_ref, e_ref,
                 o_ref, m_sc, l_sc, acc_sc):
    n = pl.program_id(1)
    start = n * Q_BLOCK
    row = lax.broadcasted_iota(jnp.int32, (ROWS, 1), 0)
    t_rows = start + (row & (Q_BLOCK - 1))
    tq = start + lax.broadcasted_iota(jnp.int32, (Q_BLOCK, 1), 0)
    lane = lax.broadcasted_iota(jnp.int32, (1, LANES), 1)
    lane_f = lane.astype(F32)
    upper_half = lax.broadcasted_iota(jnp.int32, (Q_BLOCK, LANES), 1) >= HEAD_DIM
    gate = gate_ref[0]

    def reset():
        m_sc[...] = jnp.full((ROWS, 1), NEG_INF, F32)
        l_sc[...] = jnp.zeros((ROWS, 1), F32)
        acc_sc[...] = jnp.zeros((ROWS, LANES), F32)

    def result():
        return acc_sc[...] / jnp.maximum(l_sc[...], 1e-30)

    heads = []
    for kvh in range(N_KV_HEADS):
        kv_lanes = slice(kvh * LANES, (kvh + 1) * LANES)
        parts = []
        for g in range(GQA):
            h = kvh * GQA + g
            qc = q_ref[0, :, (h // 2) * LANES:(h // 2 + 1) * LANES]
            keep = upper_half if h % 2 == 1 else jnp.logical_not(upper_half)
            parts.append(jnp.where(keep, qc, jnp.zeros_like(qc)))
        qs = jnp.concatenate(parts, axis=0)

        s_c = _dot_nt(qs, kc_ref[0, kvh])
        mask_c = (lane * CMP_STRIDE + (CMP_BLOCK - 1)) <= t_rows
        s_c = jnp.where(mask_c, s_c, NEG_INF)
        m_c = jnp.max(s_c, axis=-1, keepdims=True)
        p_c = jnp.where(mask_c, jnp.exp(s_c - m_c), 0.0)
        p_c = p_c / jnp.maximum(jnp.sum(p_c, axis=-1, keepdims=True), 1e-30)
        o_cmp = _dot(p_c.astype(BF16), vc_ref[0, kvh])

        p_sum = p_c[0:Q_BLOCK]
        for g in range(1, GQA):
            p_sum = p_sum + p_c[g * Q_BLOCK:(g + 1) * Q_BLOCK]
        imp = _split_dot(p_sum, ovl_ref[...])
        cur = tq >> 6
        forced = (lane == 0) | (lane == cur) | (lane == cur - 1)
        causal_blk = lane * SEL_BLOCK <= tq
        score = jnp.where(causal_blk, imp + jnp.where(forced, FORCE_BONUS, 0.0), NEG_INF)
        sel = jnp.zeros((Q_BLOCK, LANES), F32)
        for _ in range(SEL_TOPK):
            best = jnp.max(score, axis=-1, keepdims=True)
            idx = jnp.min(jnp.where(score == best, lane_f, float(LANES)), axis=-1, keepdims=True)
            hit = lane_f == idx
            sel = jnp.where(hit, 1.0, sel)
            score = jnp.where(hit, -3e38, score)
        sel_b = sel.astype(BF16)

        reset()

        def sel_body(i, carry):
            k0 = pl.multiple_of(i * SEL_CHUNK, SEL_CHUNK)
            s = _dot_nt(qs, ks_ref[0, pl.ds(k0, SEL_CHUNK), kv_lanes])
            picked = _dot(sel_b, e_ref[:, pl.ds(k0, SEL_CHUNK)])
            picked = jnp.concatenate([picked] * GQA, axis=0)
            keypos = k0 + lax.broadcasted_iota(jnp.int32, (1, SEL_CHUNK), 1)
            mask = (picked > 0.5) & (keypos <= t_rows)
            _online_update(s, mask, vs_ref[0, pl.ds(k0, SEL_CHUNK), kv_lanes], m_sc, l_sc, acc_sc)
            return carry

        lax.fori_loop(0, n // 2 + 1, sel_body, 0)
        o_sel = result()

        reset()

        def win_body(i, carry):
            k0 = pl.multiple_of((n - WINDOW // WIN_CHUNK + i) * WIN_CHUNK, WIN_CHUNK)
            s = _dot_nt(qs, kw_ref[0, pl.ds(k0, WIN_CHUNK), kv_lanes])
            keypos = k0 + lane
            mask = (keypos <= t_rows) & (keypos > t_rows - WINDOW)
            _online_update(s, mask, vw_ref[0, pl.ds(k0, WIN_CHUNK), kv_lanes], m_sc, l_sc, acc_sc)
            return carry

        lax.fori_loop(jnp.maximum(WINDOW // WIN_CHUNK - n, 0), WINDOW // WIN_CHUNK + 1, win_body, 0)
        o_win = result()

        for g in range(GQA):
            h = kvh * GQA + g
            rows = slice(g * Q_BLOCK, (g + 1) * Q_BLOCK)
            col = h * N_BRANCH
            heads.append(gate[:, col:col + 1] * o_cmp[rows]
                         + gate[:, col + 1:col + 2] * o_sel[rows]
                         + gate[:, col + 2:col + 3] * o_win[rows])

    for c in range(N_HEADS // 2):
        o_ref[0, :, c * LANES:(c + 1) * LANES] = jnp.where(
            upper_half, heads[2 * c + 1], heads[2 * c]).astype(BF16)


def _attention(q3, gate3, kvc, ks3, vs3, kw3, vw3, ovl, expand):
    qblk = lambda width: pl.BlockSpec((1, Q_BLOCK, width), lambda bi, n: (bi, n, 0))
    seq = pl.BlockSpec((1, SEQ, 2 * LANES), lambda bi, n: (bi, 0, 0))
    cmp_spec = lambda kind: pl.BlockSpec((None, 1, N_KV_HEADS, CMP_ROWS, LANES),
                                         lambda bi, n: (kind, bi, 0, 0, 0))
    full = lambda a: pl.BlockSpec(a.shape, lambda bi, n: (0,) * a.ndim)
    return pl.pallas_call(
        _attn_kernel,
        grid=(BATCH, SEQ // Q_BLOCK),
        in_specs=[qblk(ATTN_WIDTH), qblk(LANES), cmp_spec(0), cmp_spec(1),
                  seq, seq, seq, seq, full(ovl), full(expand)],
        out_specs=qblk(ATTN_WIDTH),
        out_shape=jax.ShapeDtypeStruct((BATCH, SEQ, ATTN_WIDTH), BF16),
        scratch_shapes=[pltpu.VMEM((ROWS, 1), F32), pltpu.VMEM((ROWS, 1), F32),
                        pltpu.VMEM((ROWS, LANES), F32)],
        compiler_params=pltpu.CompilerParams(
            dimension_semantics=("arbitrary", "arbitrary"), vmem_limit_bytes=VMEM_LIMIT),
        name="nsa_attention",
    )(q3, gate3, kvc, kvc, ks3, vs3, kw3, vw3, ovl, expand)


OUT_TM = 512


def _out_proj_kernel(x_ref, conv_ref, attn_ref, w_ref, g_ref, x1_ref, h_ref):
    x1 = (x_ref[...] + _dot(conv_ref[...], w_ref[0:CONV_WIDTH, :])
          + _dot(attn_ref[...], w_ref[CONV_WIDTH:CONV_WIDTH + ATTN_WIDTH, :]))
    x1_ref[...] = x1
    ms = jnp.mean(x1 * x1, axis=-1, keepdims=True)
    h_ref[...] = (x1 * lax.rsqrt(ms + EPS) * g_ref[...]).astype(BF16)


def _out_proj_router_kernel(x_ref, conv_ref, attn_ref, w_ref, g_ref, r_ref, x1_ref, h_ref, comb_ref):
    x1 = (x_ref[...] + _dot(conv_ref[...], w_ref[0:CONV_WIDTH, :])
          + _dot(attn_ref[...], w_ref[CONV_WIDTH:CONV_WIDTH + ATTN_WIDTH, :]))
    x1_ref[...] = x1
    ms = jnp.mean(x1 * x1, axis=-1, keepdims=True)
    h = x1 * lax.rsqrt(ms + EPS) * g_ref[...]
    h_ref[...] = h.astype(BF16)
    h_hi = h.astype(BF16)
    h_lo = (h - h_hi.astype(F32)).astype(BF16)
    logits = _dot(h_hi, r_ref[0]) + _dot(h_lo, r_ref[0]) + _dot(h_hi, r_ref[1])
    lane = lax.broadcasted_iota(jnp.int32, (1, LANES), 1).astype(F32)
    logits = jnp.where(lane < N_EXPERTS, logits, NEG_INF)
    v1 = jnp.max(logits, axis=-1, keepdims=True)
    i1 = jnp.min(jnp.where(logits == v1, lane, float(LANES)), axis=-1, keepdims=True)
    hit1 = lane == i1
    rest = jnp.where(hit1, -3e38, logits)
    v2 = jnp.max(rest, axis=-1, keepdims=True)
    i2 = jnp.min(jnp.where(rest == v2, lane, float(LANES)), axis=-1, keepdims=True)
    hit2 = lane == i2
    e2 = jnp.exp(v2 - v1)
    w_top = 1.0 / (1.0 + e2)
    comb_ref[...] = jnp.where(hit1, w_top, 0.0) + jnp.where(hit2, e2 * w_top, 0.0)


def _out_proj(x2, conv2, attn2, w, g, router=None):
    tm = OUT_TM
    row = lambda width: pl.BlockSpec((tm, width), lambda i: (i, 0))
    full = lambda a: pl.BlockSpec(a.shape, lambda i: (0,) * a.ndim)
    in_specs = [row(D_MODEL), row(CONV_WIDTH), row(ATTN_WIDTH), full(w), full(g)]
    out_shape = [jax.ShapeDtypeStruct((TOKENS, D_MODEL), F32), jax.ShapeDtypeStruct((TOKENS, D_MODEL), BF16)]
    out_specs = [row(D_MODEL), row(D_MODEL)]
    args = [x2, conv2, attn2, w, g]
    body = _out_proj_kernel
    if router is not None:
        body = _out_proj_router_kernel
        in_specs.append(full(router))
        args.append(router)
        out_shape.append(jax.ShapeDtypeStruct((TOKENS, LANES), F32))
        out_specs.append(row(LANES))
    return pl.pallas_call(
        body,
        grid=(TOKENS // tm,),
        in_specs=in_specs,
        out_specs=tuple(out_specs),
        out_shape=tuple(out_shape),
        compiler_params=pltpu.CompilerParams(
            dimension_semantics=("arbitrary",), vmem_limit_bytes=VMEM_LIMIT),
        name="out_proj",
    )(*args)


FFN_TM = 512
FFN_TF = 1408


def _ffn_kernel(x1_ref, h_ref, w1_ref, w3_ref, w2_ref, o_ref, acc_ref):
    f = pl.program_id(1)
    h = h_ref[...]
    a = _dot(h, w1_ref[...])
    act = (a * _sigmoid(a) * _dot(h, w3_ref[...])).astype(BF16)
    part = _dot(act, w2_ref[...])

    @pl.when(f == 0)
    def _():
        acc_ref[...] = x1_ref[...] + part

    @pl.when(f > 0)
    def _():
        acc_ref[...] += part

    @pl.when(f == pl.num_programs(1) - 1)
    def _():
        o_ref[...] = acc_ref[...]


def _dense_ffn(x1, h, w1, w3, w2):
    tm, tf = FFN_TM, FFN_TF
    return pl.pallas_call(
        _ffn_kernel,
        grid=(TOKENS // tm, D_FF // tf),
        in_specs=[
            pl.BlockSpec((tm, D_MODEL), lambda i, f: (i, 0)),
            pl.BlockSpec((tm, D_MODEL), lambda i, f: (i, 0)),
            pl.BlockSpec((D_MODEL, tf), lambda i, f: (0, f)),
            pl.BlockSpec((D_MODEL, tf), lambda i, f: (0, f)),
            pl.BlockSpec((tf, D_MODEL), lambda i, f: (f, 0)),
        ],
        out_specs=pl.BlockSpec((tm, D_MODEL), lambda i, f: (i, 0)),
        out_shape=jax.ShapeDtypeStruct((TOKENS, D_MODEL), F32),
        scratch_shapes=[pltpu.VMEM((tm, D_MODEL), F32)],
        compiler_params=pltpu.CompilerParams(
            dimension_semantics=("arbitrary", "arbitrary"), vmem_limit_bytes=VMEM_LIMIT),
        name="dense_ffn",
    )(x1, h, w1, w3, w2)


MOE_TM = 512
MOE_TF = 896


def _moe_kernel(x1_ref, h_ref, comb_ref, w1_ref, w3_ref, w2_ref, o_ref, acc_ref):
    e = pl.program_id(1)
    f = pl.program_id(2)
    h = h_ref[...]
    lane = lax.broadcasted_iota(jnp.int32, (1, LANES), 1)
    c_e = jnp.sum(jnp.where(lane == e, comb_ref[...], 0.0), axis=-1, keepdims=True)
    a = _dot(h, w1_ref[0])
    act = (c_e * (a * _sigmoid(a) * _dot(h, w3_ref[0]))).astype(BF16)
    part = _dot(act, w2_ref[0])
    first = (e == 0) & (f == 0)

    @pl.when(first)
    def _():
        acc_ref[...] = x1_ref[...] + part

    @pl.when(jnp.logical_not(first))
    def _():
        acc_ref[...] += part

    @pl.when((e == pl.num_programs(1) - 1) & (f == pl.num_programs(2) - 1))
    def _():
        o_ref[...] = acc_ref[...]


def _moe_ffn(x1, h, comb, w1, w3, w2):
    tm, tf = MOE_TM, MOE_TF
    return pl.pallas_call(
        _moe_kernel,
        grid=(TOKENS // tm, N_EXPERTS, D_FF_EXPERT // tf),
        in_specs=[
            pl.BlockSpec((tm, D_MODEL), lambda i, e, f: (i, 0)),
            pl.BlockSpec((tm, D_MODEL), lambda i, e, f: (i, 0)),
            pl.BlockSpec((tm, LANES), lambda i, e, f: (i, 0)),
            pl.BlockSpec((1, D_MODEL, tf), lambda i, e, f: (e, 0, f)),
            pl.BlockSpec((1, D_MODEL, tf), lambda i, e, f: (e, 0, f)),
            pl.BlockSpec((1, tf, D_MODEL), lambda i, e, f: (e, f, 0)),
        ],
        out_specs=pl.BlockSpec((tm, D_MODEL), lambda i, e, f: (i, 0)),
        out_shape=jax.ShapeDtypeStruct((TOKENS, D_MODEL), F32),
        scratch_shapes=[pltpu.VMEM((tm, D_MODEL), F32)],
        compiler_params=pltpu.CompilerParams(
            dimension_semantics=("arbitrary", "arbitrary", "arbitrary"), vmem_limit_bytes=VMEM_LIMIT),
        name="moe_ffn",
    )(x1, h, comb, w1, w3, w2)


def _group_mean_matrix():
    idx = np.arange(LANES)
    return jnp.asarray((idx[:, None] // CONV_GROUP == idx[None, :] // CONV_GROUP) / CONV_GROUP, BF16)


def _overlap_matrix():
    c_start = np.arange(CMP_ROWS) * CMP_STRIDE
    j = np.arange(LANES)
    ovl = ((c_start[:, None] < (j[None, :] + 1) * SEL_BLOCK)
           & (c_start[:, None] + CMP_BLOCK > j[None, :] * SEL_BLOCK)
           & (j[None, :] < N_SEL) & (np.arange(CMP_ROWS)[:, None] < N_CMP))
    return jnp.asarray(ovl, BF16)


def _expand_matrix():
    j = np.arange(LANES)
    key = np.arange(SEQ)
    return jnp.asarray(j[:, None] == key[None, :] // SEL_BLOCK, BF16)


def _dup(v):
    return jnp.concatenate([v, v], axis=-1)


def kernel(x, attn_norm_g, w_in, conv_dw_w, conv_dw_b, conv_ln_g, conv_ln_b, q_norm_g, k_norm_g,
           cmp_pos_k, cmp_pos_v, cmp_k_w1, cmp_k_b1, cmp_k_w2, cmp_k_b2,
           cmp_v_w1, cmp_v_b1, cmp_v_w2, cmp_v_b2, w_out, ffn_norm_g,
           dense_w1, dense_w3, dense_w2, router_w, moe_w1, moe_w3, moe_w2):
    assert x.shape == (BATCH, SEQ, D_MODEL) and x.dtype == F32
    bd = _group_mean_matrix()
    ovl = _overlap_matrix()
    expand = _expand_matrix()
    col_src = jnp.asarray(np.maximum(_IN_COLS_SRC, 0))
    col_live = jnp.asarray(_IN_COLS_SRC >= 0)

    x2 = x.reshape(TOKENS, D_MODEL)
    for layer in range(DEPTH):
        w_cols = jnp.where(col_live[None, :], jnp.take(w_in[layer], col_src, axis=1), 0.0).astype(BF16)
        qg = jnp.tile(q_norm_g[layer], 2)[None, :] * (HEAD_DIM ** -0.5)
        kgs = _dup(k_norm_g[layer, 1])[None, :]
        kgw = _dup(k_norm_g[layer, 2])[None, :]
        u, q, kcvc, ks, vs, kw, vw, gate = _in_proj(
            x2, attn_norm_g[layer][None, :], w_cols, qg, kgs, kgw, bd)

        conv = _conv_module(u.reshape(BATCH, SEQ, CONV_WIDTH), conv_dw_w[layer], conv_dw_b[layer][None, :],
                            conv_ln_g[layer][None, :], conv_ln_b[layer][None, :], bd)

        xr = kcvc.reshape(BATCH, SEQ, 2, N_KV_HEADS, HEAD_DIM).transpose(2, 0, 3, 1, 4)
        xr = xr.reshape(2, BATCH, N_KV_HEADS, CMP_ROWS, CMP_CHUNK)
        pos = jnp.stack([cmp_pos_k[layer], cmp_pos_v[layer]]).reshape(2, 1, CMP_BLOCK * HEAD_DIM)
        w1 = jnp.stack([cmp_k_w1[layer], cmp_v_w1[layer]]).astype(BF16)
        b1 = jnp.stack([cmp_k_b1[layer], cmp_v_b1[layer]])[:, None, :]
        w2 = _dup(jnp.stack([cmp_k_w2[layer], cmp_v_w2[layer]])).astype(BF16)
        b2 = _dup(jnp.stack([cmp_k_b2[layer], cmp_v_b2[layer]]))[:, None, :]
        kvc = _compress(xr, pos, w1, b1, w2, b2, _dup(k_norm_g[layer, 0])[None, :])

        seq3 = lambda a: a.reshape(BATCH, SEQ, a.shape[-1])
        attn = _attention(seq3(q), seq3(gate), kvc, seq3(ks), seq3(vs), seq3(kw), seq3(vw), ovl, expand)

        w_o = w_out[layer].astype(BF16)
        conv2 = conv.reshape(TOKENS, CONV_WIDTH)
        attn2 = attn.reshape(TOKENS, ATTN_WIDTH)
        g_ffn = ffn_norm_g[layer][None, :]
        i = layer // 2
        if layer % 2 == 0:
            x1, h = _out_proj(x2, conv2, attn2, w_o, g_ffn)
            x2 = _dense_ffn(x1, h, dense_w1[i].astype(BF16), dense_w3[i].astype(BF16),
                            dense_w2[i].astype(BF16))
        else:
            r = jnp.pad(router_w[i], ((0, 0), (0, LANES - N_EXPERTS)))
            r_hi = r.astype(BF16)
            r_lo = (r - r_hi.astype(F32)).astype(BF16)
            x1, h, comb = _out_proj(x2, conv2, attn2, w_o, g_ffn, jnp.stack([r_hi, r_lo]))
            x2 = _moe_ffn(x1, h, comb, moe_w1[i].astype(BF16), moe_w3[i].astype(BF16),
                          moe_w2[i].astype(BF16))
    return x2.reshape(BATCH, SEQ, D_MODEL)
```

```python
import functools

import numpy as np
import jax
import jax.numpy as jnp
from jax import lax
from jax.experimental import pallas as pl
from jax.experimental.pallas import tpu as pltpu

D_MODEL = 1024
BATCH = 8
SEQ = 2048
DEPTH = 2
TOKENS = BATCH * SEQ

CONV_WIDTH = 512
CONV_GROUP = 64
CONV_KERNEL = 31
N_HEADS = 8
HEAD_DIM = 64
N_KV_HEADS = 2
GQA = N_HEADS // N_KV_HEADS
ATTN_WIDTH = N_HEADS * HEAD_DIM
KV_WIDTH = N_KV_HEADS * HEAD_DIM
N_BRANCH = 3
CMP_BLOCK = 32
CMP_STRIDE = 16
CMP_HIDDEN = 256
N_CMP = (SEQ - CMP_BLOCK) // CMP_STRIDE + 1
SEL_BLOCK = 64
SEL_TOPK = 8
N_SEL = SEQ // SEL_BLOCK
WINDOW = 512
Q_BLOCK = 128
FORCE_BONUS = 1e4
NEG_INF = -1e30
D_FF = 2816
N_EXPERTS = 8
D_FF_EXPERT = 3584
EPS = 1e-6

LANES = 128
SUBLANES = 8
VMEM_LIMIT = 48 * 1024 * 1024

F32 = jnp.float32
BF16 = jnp.bfloat16

_OFF_VAL = 0
_OFF_GATE = 512
_OFF_Q = 1024
_OFF_KCVC = 1536
_OFF_KS = 1792
_OFF_KW = 2048
IN_COLS = 2304
T_ROWS = 5 * LANES

_BASE_KV = 2 * CONV_WIDTH + ATTN_WIDTH


def _in_proj_columns():
    cols = list(range(0, _BASE_KV + 2 * KV_WIDTH))
    for piece in (2, 4):
        start = _BASE_KV + piece * KV_WIDTH
        for head in range(N_KV_HEADS):
            head_cols = list(range(start + head * HEAD_DIM, start + (head + 1) * HEAD_DIM))
            cols += head_cols + head_cols
    assert len(cols) == IN_COLS
    return np.asarray(cols, np.int32)


def _in_proj_t_rows():
    rows = []
    for piece in (3, 5):
        start = _BASE_KV + piece * KV_WIDTH
        for head in range(N_KV_HEADS):
            rows += list(range(start + head * HEAD_DIM, start + (head + 1) * HEAD_DIM)) + [-1] * HEAD_DIM
    glog = _BASE_KV + 6 * KV_WIDTH
    rows += list(range(glog, glog + N_HEADS * N_BRANCH)) + [-1] * (LANES - N_HEADS * N_BRANCH)
    assert len(rows) == T_ROWS
    return np.asarray(rows, np.int32)


_IN_COLS_SRC = _in_proj_columns()
_IN_T_ROWS_SRC = _in_proj_t_rows()


def _sigmoid(v):
    return 1.0 / (1.0 + jnp.exp(-v))


def _dot(a, b):
    return jnp.dot(a, b, preferred_element_type=F32)


def _dot_nt(a, b):
    return lax.dot_general(a, b, (((1,), (1,)), ((), ())), preferred_element_type=F32)


def _split_dot(v, m):
    hi = v.astype(BF16)
    lo = (v - hi.astype(F32)).astype(BF16)
    return _dot(hi, m) + _dot(lo, m)


IN_TM = 512


def _in_proj_kernel(x_ref, g_ref, w_ref, wt_ref, qg_ref, kgs_ref, kgw_ref, bd_ref,
                    u_ref, q_ref, kcvc_ref, ks_ref, kw_ref, vst_ref, vwt_ref, gatet_ref):
    x = x_ref[...]
    ms = jnp.mean(x * x, axis=-1, keepdims=True)
    h = (x * lax.rsqrt(ms + EPS) * g_ref[...]).astype(BF16)

    def proj(lo, width):
        return _dot(h, w_ref[:, lo:lo + width])

    u_ref[...] = proj(_OFF_VAL, CONV_WIDTH) * _sigmoid(proj(_OFF_GATE, CONV_WIDTH))

    bd = bd_ref[...]
    q = proj(_OFF_Q, ATTN_WIDTH)
    for c in range(ATTN_WIDTH // LANES):
        qc = q[:, c * LANES:(c + 1) * LANES]
        msq = _split_dot(qc * qc, bd)
        q_ref[:, c * LANES:(c + 1) * LANES] = (qc * lax.rsqrt(msq + EPS) * qg_ref[...]).astype(BF16)

    kcvc_ref[...] = proj(_OFF_KCVC, 2 * KV_WIDTH)

    def normed_k(off, kg_ref, out_ref):
        k = proj(off, N_KV_HEADS * LANES)
        for c in range(N_KV_HEADS):
            kc = k[:, c * LANES:(c + 1) * LANES]
            msk = jnp.mean(kc * kc, axis=-1, keepdims=True)
            out_ref[:, c * LANES:(c + 1) * LANES] = (kc * lax.rsqrt(msk + EPS) * kg_ref[...]).astype(BF16)

    normed_k(_OFF_KS, kgs_ref, ks_ref)
    normed_k(_OFF_KW, kgw_ref, kw_ref)

    zt = _dot_nt(wt_ref[...], h)
    ones_rows = lax.broadcasted_iota(jnp.int32, (LANES, IN_TM), 0) >= HEAD_DIM
    for j, out_ref in enumerate((vst_ref, vst_ref, vwt_ref, vwt_ref)):
        blk = zt[j * LANES:(j + 1) * LANES]
        out_ref[0, j % N_KV_HEADS] = jnp.where(ones_rows, 1.0, blk).astype(BF16)
    gatet_ref[0] = _sigmoid(zt[4 * LANES:5 * LANES])


def _in_proj(x2, g, w, wt, qg, kgs, kgw, bd):
    tm = IN_TM
    per_seq = SEQ // tm
    row = lambda width: pl.BlockSpec((tm, width), lambda i: (i, 0))
    full = lambda a: pl.BlockSpec(a.shape, lambda i: (0,) * a.ndim)
    vt_spec = pl.BlockSpec((1, N_KV_HEADS, LANES, tm), lambda i: (i // per_seq, 0, 0, i % per_seq))
    vt_shape = jax.ShapeDtypeStruct((BATCH, N_KV_HEADS, LANES, SEQ), BF16)
    out_shape = (
        jax.ShapeDtypeStruct((TOKENS, CONV_WIDTH), F32),
        jax.ShapeDtypeStruct((TOKENS, ATTN_WIDTH), BF16),
        jax.ShapeDtypeStruct((TOKENS, 2 * KV_WIDTH), F32),
        jax.ShapeDtypeStruct((TOKENS, 2 * LANES), BF16),
        jax.ShapeDtypeStruct((TOKENS, 2 * LANES), BF16),
        vt_shape,
        vt_shape,
        jax.ShapeDtypeStruct((BATCH, LANES, SEQ), F32),
    )
    out_specs = (row(CONV_WIDTH), row(ATTN_WIDTH), row(2 * KV_WIDTH), row(2 * LANES), row(2 * LANES),
                 vt_spec, vt_spec,
                 pl.BlockSpec((1, LANES, tm), lambda i: (i // per_seq, 0, i % per_seq)))
    return pl.pallas_call(
        _in_proj_kernel,
        grid=(TOKENS // tm,),
        in_specs=[row(D_MODEL), full(g), full(w), full(wt), full(qg), full(kgs), full(kgw), full(bd)],
        out_specs=out_specs,
        out_shape=out_shape,
        compiler_params=pltpu.CompilerParams(
            dimension_semantics=("arbitrary",), vmem_limit_bytes=VMEM_LIMIT),
        name="in_proj",
    )(x2, g, w, wt, qg, kgs, kgw, bd)


CONV_TR = 512
CONV_HALO = 32
CONV_RS = 64


def _conv_kernel(ucur_ref, uprev_ref, w_ref, b_ref, lg_ref, lb_ref, bd_ref, o_ref, win_ref, y_ref):
    i = pl.program_id(1)
    win_ref[0, 0:CONV_HALO, :] = jnp.where(i > 0, uprev_ref[0], 0.0)
    win_ref[0, CONV_HALO:CONV_HALO + CONV_TR, :] = ucur_ref[0]
    first = CONV_HALO - (CONV_KERNEL - 1)
    shifted_rows = CONV_HALO + CONV_TR - SUBLANES
    for s in range(1, SUBLANES):
        win_ref[s, 0:shifted_rows, :] = win_ref[0, s:s + shifted_rows, :]

    for c in range(CONV_WIDTH // LANES):
        lanes = slice(c * LANES, (c + 1) * LANES)

        def body(r, carry):
            base = pl.multiple_of(r * CONV_RS, CONV_RS)
            acc = jnp.zeros((CONV_RS, LANES), F32)
            for k in range(CONV_KERNEL):
                off = first + k
                rows = pl.ds(base + (off // SUBLANES) * SUBLANES, CONV_RS)
                acc = acc + win_ref[off % SUBLANES, rows, lanes] * w_ref[k:k + 1, lanes]
            y_ref[pl.ds(base, CONV_RS), lanes] = acc + b_ref[:, lanes]
            return carry

        lax.fori_loop(0, CONV_TR // CONV_RS, body, 0)

    bd = bd_ref[...]
    for c in range(CONV_WIDTH // LANES):
        lanes = slice(c * LANES, (c + 1) * LANES)
        y = y_ref[:, lanes]
        d = y - _split_dot(y, bd)
        var = _split_dot(d * d, bd)
        yn = d * lax.rsqrt(var + EPS) * lg_ref[:, lanes] + lb_ref[:, lanes]
        o_ref[0, :, lanes] = (yn * _sigmoid(yn)).astype(BF16)


def _conv_module(u3, w, b, lg, lb, bd):
    halo_blocks = CONV_TR // CONV_HALO
    full = lambda a: pl.BlockSpec(a.shape, lambda bi, i: (0,) * a.ndim)
    return pl.pallas_call(
        _conv_kernel,
        grid=(BATCH, SEQ // CONV_TR),
        in_specs=[
            pl.BlockSpec((1, CONV_TR, CONV_WIDTH), lambda bi, i: (bi, i, 0)),
            pl.BlockSpec((1, CONV_HALO, CONV_WIDTH),
                         lambda bi, i: (bi, jnp.maximum(i * halo_blocks - 1, 0), 0)),
            full(w), full(b), full(lg), full(lb), full(bd),
        ],
        out_specs=pl.BlockSpec((1, CONV_TR, CONV_WIDTH), lambda bi, i: (bi, i, 0)),
        out_shape=jax.ShapeDtypeStruct((BATCH, SEQ, CONV_WIDTH), BF16),
        scratch_shapes=[pltpu.VMEM((SUBLANES, CONV_HALO + CONV_TR, CONV_WIDTH), F32),
                        pltpu.VMEM((CONV_TR, CONV_WIDTH), F32)],
        compiler_params=pltpu.CompilerParams(dimension_semantics=("arbitrary", "arbitrary")),
        name="conv_module",
    )(u3, u3, w, b, lg, lb, bd)


CMP_ROWS = SEQ // CMP_STRIDE
CMP_CHUNK = CMP_STRIDE * HEAD_DIM


def _compress_kernel(x_ref, pos_ref, w1_ref, b1_ref, w2_ref, b2_ref, w2t_ref, b2t_ref, kg_ref, o_ref):
    kind = pl.program_id(0)
    xa = x_ref[0, 0, 0]
    xb = pltpu.roll(xa, CMP_ROWS - 1, axis=0)
    a = (xa + pos_ref[0, :, 0:CMP_CHUNK]).astype(BF16)
    b = (xb + pos_ref[0, :, CMP_CHUNK:2 * CMP_CHUNK]).astype(BF16)
    hid = _dot(a, w1_ref[0, 0:CMP_CHUNK, :]) + _dot(b, w1_ref[0, CMP_CHUNK:2 * CMP_CHUNK, :]) + b1_ref[0]
    hid = (hid * _sigmoid(hid)).astype(BF16)
    out = _dot(hid, w2_ref[0]) + b2_ref[0]
    ms = jnp.mean(out * out, axis=-1, keepdims=True)
    normed = out * lax.rsqrt(ms + EPS) * kg_ref[...]
    out_t = _dot_nt(w2t_ref[0], hid) + b2t_ref[0]
    o_ref[0, 0, 0] = jnp.where(kind == 0, normed, out_t).astype(BF16)


def _compress(xr, pos, w1, b1, w2, b2, kg):
    w2t = jnp.swapaxes(w2, 1, 2)
    b2t = jnp.swapaxes(b2, 1, 2)
    per_kind = lambda a: pl.BlockSpec((1,) + a.shape[1:], lambda k, bi, h: (k,) + (0,) * (a.ndim - 1))
    return pl.pallas_call(
        _compress_kernel,
        grid=(2, BATCH, N_KV_HEADS),
        in_specs=[
            pl.BlockSpec((1, 1, 1, CMP_ROWS, CMP_CHUNK), lambda k, bi, h: (k, bi, h, 0, 0)),
            per_kind(pos), per_kind(w1), per_kind(b1), per_kind(w2), per_kind(b2),
            per_kind(w2t), per_kind(b2t),
            pl.BlockSpec(kg.shape, lambda k, bi, h: (0, 0)),
        ],
        out_specs=pl.BlockSpec((1, 1, 1, CMP_ROWS, LANES), lambda k, bi, h: (k, bi, h, 0, 0)),
        out_shape=jax.ShapeDtypeStruct((2, BATCH, N_KV_HEADS, CMP_ROWS, LANES), BF16),
        compiler_params=pltpu.CompilerParams(
            dimension_semantics=("arbitrary", "arbitrary", "arbitrary")),
        name="compress",
    )(xr, pos, w1, b1, w2, b2, w2t, b2t, kg)


QCOLS = GQA * Q_BLOCK
SEL_CHUNK = 256
WIN_CHUNKS = WINDOW // Q_BLOCK + 1
MASK_BIAS = -1e30


def _attn_kernel(q_ref, gatet_ref, kc_ref, vct_ref, ks_ref, vst_ref, kw_ref, vwt_ref,
                 ovlt_ref, et_ref, tri_ref, o_ref, ssel_sc, swin_sc, acc_sc):
    n = pl.program_id(1)
    start = n * Q_BLOCK
    col = lax.broadcasted_iota(jnp.int32, (1, QCOLS), 1)
    t_cols = start + (col & (Q_BLOCK - 1))
    tq = start + lax.broadcasted_iota(jnp.int32, (1, Q_BLOCK), 1)
    upper_half = lax.broadcasted_iota(jnp.int32, (Q_BLOCK, LANES), 1) >= HEAD_DIM
    blk_row = lax.broadcasted_iota(jnp.int32, (N_SEL, 1), 0)
    blk_row_f = blk_row.astype(F32)
    cmp_end = lax.broadcasted_iota(jnp.int32, (CMP_ROWS, 1), 0) * CMP_STRIDE + (CMP_BLOCK - 1)

    heads = []
    for kvh in range(N_KV_HEADS):
        kv_lanes = slice(kvh * LANES, (kvh + 1) * LANES)
        parts = []
        for g in range(GQA):
            h = kvh * GQA + g
            qc = q_ref[0, :, (h // 2) * LANES:(h // 2 + 1) * LANES]
            keep = upper_half if h % 2 == 1 else jnp.logical_not(upper_half)
            parts.append(jnp.where(keep, qc, jnp.zeros_like(qc)))
        qs = jnp.concatenate(parts, axis=0)

        s_c = _dot_nt(kc_ref[0, kvh], qs)
        mask_c = cmp_end <= t_cols
        s_c = jnp.where(mask_c, s_c, NEG_INF)
        m_c = jnp.max(s_c, axis=0, keepdims=True)
        p_c = jnp.where(mask_c, jnp.exp(s_c - m_c), 0.0)
        p_c = p_c / jnp.maximum(jnp.sum(p_c, axis=0, keepdims=True), 1e-30)
        o_cmp = _dot(vct_ref[0, kvh], p_c.astype(BF16))

        p_sum = p_c[:, 0:Q_BLOCK]
        for g in range(1, GQA):
            p_sum = p_sum + p_c[:, g * Q_BLOCK:(g + 1) * Q_BLOCK]
        p_hi = p_sum.astype(BF16)
        p_lo = (p_sum - p_hi.astype(F32)).astype(BF16)
        imp = _dot(ovlt_ref[...], p_hi) + _dot(ovlt_ref[...], p_lo)
        cur = tq >> 6
        forced = (blk_row == 0) | (blk_row == cur) | (blk_row == cur - 1)
        causal_blk = blk_row * SEL_BLOCK <= tq
        score = jnp.where(causal_blk, imp + jnp.where(forced, FORCE_BONUS, 0.0), NEG_INF)
        sel = jnp.zeros((N_SEL, Q_BLOCK), F32)
        for _ in range(SEL_TOPK):
            best = jnp.max(score, axis=0, keepdims=True)
            idx = jnp.min(jnp.where(score == best, blk_row_f, float(LANES)), axis=0, keepdims=True)
            hit = blk_row_f == idx
            sel = jnp.where(hit, 1.0, sel)
            score = jnp.where(hit, -3e38, score)
        sel_bias = ((sel - 1.0) * (-MASK_BIAS)).astype(BF16)
        sel_bias = jnp.concatenate([sel_bias] * GQA, axis=1)
        sel_bias = jnp.concatenate([sel_bias, jnp.zeros((LANES - N_SEL, QCOLS), BF16)], axis=0)

        def sel_scores(k0):
            return (_dot_nt(ks_ref[0, pl.ds(k0, SEL_CHUNK), kv_lanes], qs)
                    + _dot(et_ref[pl.ds(k0, SEL_CHUNK), :], sel_bias))

        def sel_pass1(i, mx):
            k0 = pl.multiple_of(i * SEL_CHUNK, SEL_CHUNK)
            s = sel_scores(k0)
            ssel_sc[pl.ds(k0, SEL_CHUNK), :] = s
            return jnp.maximum(mx, jnp.max(s, axis=0, keepdims=True))

        last = n // 2
        mx = lax.fori_loop(0, last, sel_pass1, jnp.full((1, QCOLS), NEG_INF, F32))
        k_last = pl.multiple_of(last * SEL_CHUNK, SEL_CHUNK)
        keypos = k_last + lax.broadcasted_iota(jnp.int32, (SEL_CHUNK, 1), 0)
        s = jnp.where(keypos <= t_cols, sel_scores(k_last), NEG_INF)
        ssel_sc[pl.ds(k_last, SEL_CHUNK), :] = s
        mx = jnp.maximum(mx, jnp.max(s, axis=0, keepdims=True))

        acc_sc[...] = jnp.zeros((LANES, QCOLS), F32)

        def sel_pass2(i, carry):
            k0 = pl.multiple_of(i * SEL_CHUNK, SEL_CHUNK)
            p = jnp.exp(ssel_sc[pl.ds(k0, SEL_CHUNK), :] - mx).astype(BF16)
            acc_sc[...] += _dot(vst_ref[0, kvh, :, pl.ds(k0, SEL_CHUNK)], p)
            return carry

        lax.fori_loop(0, last + 1, sel_pass2, 0)
        acc = acc_sc[...]
        o_sel = acc[0:HEAD_DIM] * (1.0 / acc[HEAD_DIM:HEAD_DIM + 1])

        mxw = jnp.full((1, QCOLS), NEG_INF, F32)
        k_win = []
        for i in range(WIN_CHUNKS):
            kb = n - (WIN_CHUNKS - 1) + i
            k0 = pl.multiple_of(jnp.maximum(kb, 0) * Q_BLOCK, Q_BLOCK)
            k_win.append(k0)
            s = _dot_nt(kw_ref[0, pl.ds(k0, Q_BLOCK), kv_lanes], qs)
            if i == 0:
                s = s + tri_ref[1]
            if i == WIN_CHUNKS - 1:
                s = s + tri_ref[0]
            else:
                s = s + jnp.where(kb >= 0, 0.0, MASK_BIAS)
            swin_sc[i * Q_BLOCK:(i + 1) * Q_BLOCK, :] = s
            mxw = jnp.maximum(mxw, jnp.max(s, axis=0, keepdims=True))
        accw = jnp.zeros((LANES, QCOLS), F32)
        for i in range(WIN_CHUNKS):
            p = jnp.exp(swin_sc[i * Q_BLOCK:(i + 1) * Q_BLOCK, :] - mxw).astype(BF16)
            accw = accw + _dot(vwt_ref[0, kvh, :, pl.ds(k_win[i], Q_BLOCK)], p)
        o_win = accw[0:HEAD_DIM] * (1.0 / accw[HEAD_DIM:HEAD_DIM + 1])

        for g in range(GQA):
            r = (kvh * GQA + g) * N_BRANCH
            cols = slice(g * Q_BLOCK, (g + 1) * Q_BLOCK)
            heads.append(gatet_ref[0, r:r + 1, :] * o_cmp[0:HEAD_DIM, cols]
                         + gatet_ref[0, r + 1:r + 2, :] * o_sel[:, cols]
                         + gatet_ref[0, r + 2:r + 3, :] * o_win[:, cols])

    for c in range(N_HEADS // 2):
        pair = jnp.concatenate([heads[2 * c], heads[2 * c + 1]], axis=0)
        o_ref[0, :, c * LANES:(c + 1) * LANES] = pair.T.astype(BF16)


def _attention(q3, gatet, kvc, ks3, vst, kw3, vwt, ovlt, expand_t, tri):
    qblk = lambda width: pl.BlockSpec((1, Q_BLOCK, width), lambda bi, n: (bi, n, 0))
    seq = pl.BlockSpec((1, SEQ, 2 * LANES), lambda bi, n: (bi, 0, 0))
    seq_t = pl.BlockSpec((1, N_KV_HEADS, LANES, SEQ), lambda bi, n: (bi, 0, 0, 0))
    cmp_spec = lambda kind: pl.BlockSpec((None, 1, N_KV_HEADS, CMP_ROWS, LANES),
                                         lambda bi, n: (kind, bi, 0, 0, 0))
    full = lambda a: pl.BlockSpec(a.shape, lambda bi, n: (0,) * a.ndim)
    return pl.pallas_call(
        _attn_kernel,
        grid=(BATCH, SEQ // Q_BLOCK),
        in_specs=[qblk(ATTN_WIDTH), pl.BlockSpec((1, LANES, Q_BLOCK), lambda bi, n: (bi, 0, n)),
                  cmp_spec(0), cmp_spec(1), seq, seq_t, seq, seq_t,
                  full(ovlt), full(expand_t), full(tri)],
        out_specs=qblk(ATTN_WIDTH),
        out_shape=jax.ShapeDtypeStruct((BATCH, SEQ, ATTN_WIDTH), BF16),
        scratch_shapes=[pltpu.VMEM((SEQ, QCOLS), F32),
                        pltpu.VMEM((WIN_CHUNKS * Q_BLOCK, QCOLS), F32),
                        pltpu.VMEM((LANES, QCOLS), F32)],
        compiler_params=pltpu.CompilerParams(
            dimension_semantics=("arbitrary", "arbitrary"), vmem_limit_bytes=VMEM_LIMIT),
        name="nsa_attention",
    )(q3, gatet, kvc, kvc, ks3, vst, kw3, vwt, ovlt, expand_t, tri)


OUT_TM = 512


def _out_proj_kernel(x_ref, conv_ref, attn_ref, w_ref, g_ref, x1_ref, h_ref):
    x1 = (x_ref[...] + _dot(conv_ref[...], w_ref[0:CONV_WIDTH, :])
          + _dot(attn_ref[...], w_ref[CONV_WIDTH:CONV_WIDTH + ATTN_WIDTH, :]))
    x1_ref[...] = x1
    ms = jnp.mean(x1 * x1, axis=-1, keepdims=True)
    h_ref[...] = (x1 * lax.rsqrt(ms + EPS) * g_ref[...]).astype(BF16)


def _out_proj_router_kernel(x_ref, conv_ref, attn_ref, w_ref, g_ref, r_ref, x1_ref, h_ref, comb_ref):
    x1 = (x_ref[...] + _dot(conv_ref[...], w_ref[0:CONV_WIDTH, :])
          + _dot(attn_ref[...], w_ref[CONV_WIDTH:CONV_WIDTH + ATTN_WIDTH, :]))
    x1_ref[...] = x1
    ms = jnp.mean(x1 * x1, axis=-1, keepdims=True)
    h = x1 * lax.rsqrt(ms + EPS) * g_ref[...]
    h_ref[...] = h.astype(BF16)
    h_hi = h.astype(BF16)
    h_lo = (h - h_hi.astype(F32)).astype(BF16)
    logits = _dot(h_hi, r_ref[0]) + _dot(h_lo, r_ref[0]) + _dot(h_hi, r_ref[1])
    lane = lax.broadcasted_iota(jnp.int32, (1, LANES), 1).astype(F32)
    logits = jnp.where(lane < N_EXPERTS, logits, NEG_INF)
    v1 = jnp.max(logits, axis=-1, keepdims=True)
    i1 = jnp.min(jnp.where(logits == v1, lane, float(LANES)), axis=-1, keepdims=True)
    hit1 = lane == i1
    rest = jnp.where(hit1, -3e38, logits)
    v2 = jnp.max(rest, axis=-1, keepdims=True)
    i2 = jnp.min(jnp.where(rest == v2, lane, float(LANES)), axis=-1, keepdims=True)
    hit2 = lane == i2
    e2 = jnp.exp(v2 - v1)
    w_top = 1.0 / (1.0 + e2)
    comb_ref[...] = jnp.where(hit1, w_top, 0.0) + jnp.where(hit2, e2 * w_top, 0.0)


def _out_proj(x2, conv2, attn2, w, g, router=None):
    tm = OUT_TM
    row = lambda width: pl.BlockSpec((tm, width), lambda i: (i, 0))
    full = lambda a: pl.BlockSpec(a.shape, lambda i: (0,) * a.ndim)
    in_specs = [row(D_MODEL), row(CONV_WIDTH), row(ATTN_WIDTH), full(w), full(g)]
    out_shape = [jax.ShapeDtypeStruct((TOKENS, D_MODEL), F32), jax.ShapeDtypeStruct((TOKENS, D_MODEL), BF16)]
    out_specs = [row(D_MODEL), row(D_MODEL)]
    args = [x2, conv2, attn2, w, g]
    body = _out_proj_kernel
    if router is not None:
        body = _out_proj_router_kernel
        in_specs.append(full(router))
        args.append(router)
        out_shape.append(jax.ShapeDtypeStruct((TOKENS, LANES), F32))
        out_specs.append(row(LANES))
    return pl.pallas_call(
        body,
        grid=(TOKENS // tm,),
        in_specs=in_specs,
        out_specs=tuple(out_specs),
        out_shape=tuple(out_shape),
        compiler_params=pltpu.CompilerParams(
            dimension_semantics=("arbitrary",), vmem_limit_bytes=VMEM_LIMIT),
        name="out_proj",
    )(*args)


FFN_TM = 512
FFN_TF = 1408


def _ffn_kernel(x1_ref, h_ref, w1_ref, w3_ref, w2_ref, o_ref, acc_ref):
    f = pl.program_id(1)
    h = h_ref[...]
    a = _dot(h, w1_ref[...])
    act = (a * _sigmoid(a) * _dot(h, w3_ref[...])).astype(BF16)
    part = _dot(act, w2_ref[...])

    @pl.when(f == 0)
    def _():
        acc_ref[...] = x1_ref[...] + part

    @pl.when(f > 0)
    def _():
        acc_ref[...] += part

    @pl.when(f == pl.num_programs(1) - 1)
    def _():
        o_ref[...] = acc_ref[...]


def _dense_ffn(x1, h, w1, w3, w2):
    tm, tf = FFN_TM, FFN_TF
    return pl.pallas_call(
        _ffn_kernel,
        grid=(TOKENS // tm, D_FF // tf),
        in_specs=[
            pl.BlockSpec((tm, D_MODEL), lambda i, f: (i, 0)),
            pl.BlockSpec((tm, D_MODEL), lambda i, f: (i, 0)),
            pl.BlockSpec((D_MODEL, tf), lambda i, f: (0, f)),
            pl.BlockSpec((D_MODEL, tf), lambda i, f: (0, f)),
            pl.BlockSpec((tf, D_MODEL), lambda i, f: (f, 0)),
        ],
        out_specs=pl.BlockSpec((tm, D_MODEL), lambda i, f: (i, 0)),
        out_shape=jax.ShapeDtypeStruct((TOKENS, D_MODEL), F32),
        scratch_shapes=[pltpu.VMEM((tm, D_MODEL), F32)],
        compiler_params=pltpu.CompilerParams(
            dimension_semantics=("arbitrary", "arbitrary"), vmem_limit_bytes=VMEM_LIMIT),
        name="dense_ffn",
    )(x1, h, w1, w3, w2)


MOE_TM = 512
MOE_TF = 896


def _moe_kernel(x1_ref, h_ref, comb_ref, w1_ref, w3_ref, w2_ref, o_ref, acc_ref):
    e = pl.program_id(1)
    f = pl.program_id(2)
    h = h_ref[...]
    lane = lax.broadcasted_iota(jnp.int32, (1, LANES), 1)
    c_e = jnp.sum(jnp.where(lane == e, comb_ref[...], 0.0), axis=-1, keepdims=True)
    a = _dot(h, w1_ref[0])
    act = (c_e * (a * _sigmoid(a) * _dot(h, w3_ref[0]))).astype(BF16)
    part = _dot(act, w2_ref[0])
    first = (e == 0) & (f == 0)

    @pl.when(first)
    def _():
        acc_ref[...] = x1_ref[...] + part

    @pl.when(jnp.logical_not(first))
    def _():
        acc_ref[...] += part

    @pl.when((e == pl.num_programs(1) - 1) & (f == pl.num_programs(2) - 1))
    def _():
        o_ref[...] = acc_ref[...]


def _moe_ffn(x1, h, comb, w1, w3, w2):
    tm, tf = MOE_TM, MOE_TF
    return pl.pallas_call(
        _moe_kernel,
        grid=(TOKENS // tm, N_EXPERTS, D_FF_EXPERT // tf),
        in_specs=[
            pl.BlockSpec((tm, D_MODEL), lambda i, e, f: (i, 0)),
            pl.BlockSpec((tm, D_MODEL), lambda i, e, f: (i, 0)),
            pl.BlockSpec((tm, LANES), lambda i, e, f: (i, 0)),
            pl.BlockSpec((1, D_MODEL, tf), lambda i, e, f: (e, 0, f)),
            pl.BlockSpec((1, D_MODEL, tf), lambda i, e, f: (e, 0, f)),
            pl.BlockSpec((1, tf, D_MODEL), lambda i, e, f: (e, f, 0)),
        ],
        out_specs=pl.BlockSpec((tm, D_MODEL), lambda i, e, f: (i, 0)),
        out_shape=jax.ShapeDtypeStruct((TOKENS, D_MODEL), F32),
        scratch_shapes=[pltpu.VMEM((tm, D_MODEL), F32)],
        compiler_params=pltpu.CompilerParams(
            dimension_semantics=("arbitrary", "arbitrary", "arbitrary"), vmem_limit_bytes=VMEM_LIMIT),
        name="moe_ffn",
    )(x1, h, comb, w1, w3, w2)


def _group_mean_matrix():
    idx = np.arange(LANES)
    return jnp.asarray((idx[:, None] // CONV_GROUP == idx[None, :] // CONV_GROUP) / CONV_GROUP, BF16)


def _overlap_matrix_t():
    c_start = np.arange(CMP_ROWS) * CMP_STRIDE
    j = np.arange(N_SEL)
    ovl = ((c_start[None, :] < (j[:, None] + 1) * SEL_BLOCK)
           & (c_start[None, :] + CMP_BLOCK > j[:, None] * SEL_BLOCK)
           & (np.arange(CMP_ROWS)[None, :] < N_CMP))
    return jnp.asarray(ovl, BF16)


def _expand_matrix_t():
    j = np.arange(LANES)
    key = np.arange(SEQ)
    return jnp.asarray(key[:, None] // SEL_BLOCK == j[None, :], BF16)


def _triangle_biases():
    k = np.arange(Q_BLOCK)[:, None]
    t = (np.arange(QCOLS) % Q_BLOCK)[None, :]
    return jnp.asarray(np.stack([np.where(k <= t, 0.0, MASK_BIAS), np.where(k > t, 0.0, MASK_BIAS)]), F32)


def _dup(v):
    return jnp.concatenate([v, v], axis=-1)


def kernel(x, attn_norm_g, w_in, conv_dw_w, conv_dw_b, conv_ln_g, conv_ln_b, q_norm_g, k_norm_g,
           cmp_pos_k, cmp_pos_v, cmp_k_w1, cmp_k_b1, cmp_k_w2, cmp_k_b2,
           cmp_v_w1, cmp_v_b1, cmp_v_w2, cmp_v_b2, w_out, ffn_norm_g,
           dense_w1, dense_w3, dense_w2, router_w, moe_w1, moe_w3, moe_w2):
    assert x.shape == (BATCH, SEQ, D_MODEL) and x.dtype == F32
    bd = _group_mean_matrix()
    ovlt = _overlap_matrix_t()
    expand_t = _expand_matrix_t()
    tri = _triangle_biases()
    col_src = jnp.asarray(_IN_COLS_SRC)
    row_src = jnp.asarray(np.maximum(_IN_T_ROWS_SRC, 0))
    row_live = jnp.asarray(_IN_T_ROWS_SRC >= 0)

    x2 = x.reshape(TOKENS, D_MODEL)
    for layer in range(DEPTH):
        w_cols = jnp.take(w_in[layer], col_src, axis=1).astype(BF16)
        w_rows = jnp.where(row_live[:, None], jnp.take(w_in[layer], row_src, axis=1).T, 0.0).astype(BF16)
        qg = jnp.tile(q_norm_g[layer], 2)[None, :] * (HEAD_DIM ** -0.5)
        kgs = _dup(k_norm_g[layer, 1])[None, :]
        kgw = _dup(k_norm_g[layer, 2])[None, :]
        u, q, kcvc, ks, kw, vst, vwt, gatet = _in_proj(
            x2, attn_norm_g[layer][None, :], w_cols, w_rows, qg, kgs, kgw, bd)

        conv = _conv_module(u.reshape(BATCH, SEQ, CONV_WIDTH), conv_dw_w[layer], conv_dw_b[layer][None, :],
                            conv_ln_g[layer][None, :], conv_ln_b[layer][None, :], bd)

        xr = kcvc.reshape(BATCH, SEQ, 2, N_KV_HEADS, HEAD_DIM).transpose(2, 0, 3, 1, 4)
        xr = xr.reshape(2, BATCH, N_KV_HEADS, CMP_ROWS, CMP_CHUNK)
        pos = jnp.stack([cmp_pos_k[layer], cmp_pos_v[layer]]).reshape(2, 1, CMP_BLOCK * HEAD_DIM)
        w1 = jnp.stack([cmp_k_w1[layer], cmp_v_w1[layer]]).astype(BF16)
        b1 = jnp.stack([cmp_k_b1[layer], cmp_v_b1[layer]])[:, None, :]
        w2 = _dup(jnp.stack([cmp_k_w2[layer], cmp_v_w2[layer]])).astype(BF16)
        b2 = _dup(jnp.stack([cmp_k_b2[layer], cmp_v_b2[layer]]))[:, None, :]
        kvc = _compress(xr, pos, w1, b1, w2, b2, _dup(k_norm_g[layer, 0])[None, :])

        seq3 = lambda a: a.reshape(BATCH, SEQ, a.shape[-1])
        attn = _attention(seq3(q), gatet, kvc, seq3(ks), vst, seq3(kw), vwt, ovlt, expand_t, tri)

        w_o = w_out[layer].astype(BF16)
        conv2 = conv.reshape(TOKENS, CONV_WIDTH)
        attn2 = attn.reshape(TOKENS, ATTN_WIDTH)
        g_ffn = ffn_norm_g[layer][None, :]
        i = layer // 2
        if layer % 2 == 0:
            x1, h = _out_proj(x2, conv2, attn2, w_o, g_ffn)
            x2 = _dense_ffn(x1, h, dense_w1[i].astype(BF16), dense_w3[i].astype(BF16),
                            dense_w2[i].astype(BF16))
        else:
            r = jnp.pad(router_w[i], ((0, 0), (0, LANES - N_EXPERTS)))
            r_hi = r.astype(BF16)
            r_lo = (r - r_hi.astype(F32)).astype(BF16)
            x1, h, comb = _out_proj(x2, conv2, attn2, w_o, g_ffn, jnp.stack([r_hi, r_lo]))
            x2 = _moe_ffn(x1, h, comb, moe_w1[i].astype(BF16), moe_w3[i].astype(BF16),
                          moe_w2[i].astype(BF16))
    return x2.reshape(BATCH, SEQ, D_MODEL)
```

```python
import functools

import numpy as np
import jax
import jax.numpy as jnp
from jax import lax
from jax.experimental import pallas as pl
from jax.experimental.pallas import tpu as pltpu

D_MODEL = 1024
BATCH = 8
SEQ = 2048
DEPTH = 2
TOKENS = BATCH * SEQ

CONV_WIDTH = 512
CONV_GROUP = 64
CONV_KERNEL = 31
N_HEADS = 8
HEAD_DIM = 64
N_KV_HEADS = 2
GQA = N_HEADS // N_KV_HEADS
ATTN_WIDTH = N_HEADS * HEAD_DIM
KV_WIDTH = N_KV_HEADS * HEAD_DIM
N_BRANCH = 3
CMP_BLOCK = 32
CMP_STRIDE = 16
CMP_HIDDEN = 256
N_CMP = (SEQ - CMP_BLOCK) // CMP_STRIDE + 1
SEL_BLOCK = 64
SEL_TOPK = 8
N_SEL = SEQ // SEL_BLOCK
WINDOW = 512
Q_BLOCK = 128
FORCE_BONUS = 1e4
NEG_INF = -1e30
D_FF = 2816
N_EXPERTS = 8
D_FF_EXPERT = 3584
EPS = 1e-6

LANES = 128
SUBLANES = 8
VMEM_LIMIT = 48 * 1024 * 1024

F32 = jnp.float32
BF16 = jnp.bfloat16

_OFF_VAL = 0
_OFF_GATE = 512
_OFF_Q = 1024
_OFF_KCVC = 1536
_OFF_KS = 1792
_OFF_KW = 2048
IN_COLS = 2304
T_ROWS = 5 * LANES

_BASE_KV = 2 * CONV_WIDTH + ATTN_WIDTH


def _in_proj_columns():
    cols = list(range(0, _BASE_KV + 2 * KV_WIDTH))
    for piece in (2, 4):
        start = _BASE_KV + piece * KV_WIDTH
        for head in range(N_KV_HEADS):
            head_cols = list(range(start + head * HEAD_DIM, start + (head + 1) * HEAD_DIM))
            cols += head_cols + head_cols
    assert len(cols) == IN_COLS
    return np.asarray(cols, np.int32)


def _in_proj_t_rows():
    rows = []
    for piece in (3, 5):
        start = _BASE_KV + piece * KV_WIDTH
        for head in range(N_KV_HEADS):
            rows += list(range(start + head * HEAD_DIM, start + (head + 1) * HEAD_DIM)) + [-1] * HEAD_DIM
    glog = _BASE_KV + 6 * KV_WIDTH
    rows += list(range(glog, glog + N_HEADS * N_BRANCH)) + [-1] * (LANES - N_HEADS * N_BRANCH)
    assert len(rows) == T_ROWS
    return np.asarray(rows, np.int32)


_IN_COLS_SRC = _in_proj_columns()
_IN_T_ROWS_SRC = _in_proj_t_rows()


def _sigmoid(v):
    return 1.0 / (1.0 + jnp.exp(-v))


def _dot(a, b):
    return jnp.dot(a, b, preferred_element_type=F32)


def _dot_nt(a, b):
    return lax.dot_general(a, b, (((1,), (1,)), ((), ())), preferred_element_type=F32)


def _split_dot(v, m):
    hi = v.astype(BF16)
    lo = (v - hi.astype(F32)).astype(BF16)
    return _dot(hi, m) + _dot(lo, m)


IN_TM = 512


def _in_proj_kernel(x_ref, g_ref, w_ref, wt_ref, qg_ref, kgs_ref, kgw_ref, bd_ref,
                    u_ref, q_ref, kcvc_ref, ks_ref, kw_ref, vst_ref, vwt_ref, gatet_ref):
    x = x_ref[...]
    ms = jnp.mean(x * x, axis=-1, keepdims=True)
    h = (x * lax.rsqrt(ms + EPS) * g_ref[...]).astype(BF16)

    def proj(lo, width):
        return _dot(h, w_ref[:, lo:lo + width])

    u_ref[...] = proj(_OFF_VAL, CONV_WIDTH) * _sigmoid(proj(_OFF_GATE, CONV_WIDTH))

    bd = bd_ref[...]
    q = proj(_OFF_Q, ATTN_WIDTH)
    for c in range(ATTN_WIDTH // LANES):
        qc = q[:, c * LANES:(c + 1) * LANES]
        msq = _split_dot(qc * qc, bd)
        q_ref[:, c * LANES:(c + 1) * LANES] = (qc * lax.rsqrt(msq + EPS) * qg_ref[...]).astype(BF16)

    kcvc_ref[...] = proj(_OFF_KCVC, 2 * KV_WIDTH)

    def normed_k(off, kg_ref, out_ref):
        k = proj(off, N_KV_HEADS * LANES)
        for c in range(N_KV_HEADS):
            kc = k[:, c * LANES:(c + 1) * LANES]
            msk = jnp.mean(kc * kc, axis=-1, keepdims=True)
            out_ref[:, c * LANES:(c + 1) * LANES] = (kc * lax.rsqrt(msk + EPS) * kg_ref[...]).astype(BF16)

    normed_k(_OFF_KS, kgs_ref, ks_ref)
    normed_k(_OFF_KW, kgw_ref, kw_ref)

    zt = _dot_nt(wt_ref[...], h)
    ones_rows = lax.broadcasted_iota(jnp.int32, (LANES, IN_TM), 0) >= HEAD_DIM
    for j, out_ref in enumerate((vst_ref, vst_ref, vwt_ref, vwt_ref)):
        blk = zt[j * LANES:(j + 1) * LANES]
        out_ref[0, j % N_KV_HEADS] = jnp.where(ones_rows, 1.0, blk).astype(BF16)
    gatet_ref[0] = _sigmoid(zt[4 * LANES:5 * LANES])


def _in_proj(x2, g, w, wt, qg, kgs, kgw, bd):
    tm = IN_TM
    per_seq = SEQ // tm
    row = lambda width: pl.BlockSpec((tm, width), lambda i: (i, 0))
    full = lambda a: pl.BlockSpec(a.shape, lambda i: (0,) * a.ndim)
    vt_spec = pl.BlockSpec((1, N_KV_HEADS, LANES, tm), lambda i: (i // per_seq, 0, 0, i % per_seq))
    vt_shape = jax.ShapeDtypeStruct((BATCH, N_KV_HEADS, LANES, SEQ), BF16)
    out_shape = (
        jax.ShapeDtypeStruct((TOKENS, CONV_WIDTH), F32),
        jax.ShapeDtypeStruct((TOKENS, ATTN_WIDTH), BF16),
        jax.ShapeDtypeStruct((TOKENS, 2 * KV_WIDTH), F32),
        jax.ShapeDtypeStruct((TOKENS, 2 * LANES), BF16),
        jax.ShapeDtypeStruct((TOKENS, 2 * LANES), BF16),
        vt_shape,
        vt_shape,
        jax.ShapeDtypeStruct((BATCH, LANES, SEQ), F32),
    )
    out_specs = (row(CONV_WIDTH), row(ATTN_WIDTH), row(2 * KV_WIDTH), row(2 * LANES), row(2 * LANES),
                 vt_spec, vt_spec,
                 pl.BlockSpec((1, LANES, tm), lambda i: (i // per_seq, 0, i % per_seq)))
    return pl.pallas_call(
        _in_proj_kernel,
        grid=(TOKENS // tm,),
        in_specs=[row(D_MODEL), full(g), full(w), full(wt), full(qg), full(kgs), full(kgw), full(bd)],
        out_specs=out_specs,
        out_shape=out_shape,
        compiler_params=pltpu.CompilerParams(
            dimension_semantics=("arbitrary",), vmem_limit_bytes=VMEM_LIMIT),
        name="in_proj",
    )(x2, g, w, wt, qg, kgs, kgw, bd)


CONV_TR = 512
CONV_HALO = 32
CONV_RS = 64


def _conv_kernel(ucur_ref, uprev_ref, w_ref, b_ref, lg_ref, lb_ref, bd_ref, o_ref, win_ref, y_ref):
    i = pl.program_id(1)
    win_ref[0, 0:CONV_HALO, :] = jnp.where(i > 0, uprev_ref[0], 0.0)
    win_ref[0, CONV_HALO:CONV_HALO + CONV_TR, :] = ucur_ref[0]
    first = CONV_HALO - (CONV_KERNEL - 1)
    shifted_rows = CONV_HALO + CONV_TR - SUBLANES
    for s in range(1, SUBLANES):
        win_ref[s, 0:shifted_rows, :] = win_ref[0, s:s + shifted_rows, :]

    for c in range(CONV_WIDTH // LANES):
        lanes = slice(c * LANES, (c + 1) * LANES)

        def body(r, carry):
            base = pl.multiple_of(r * CONV_RS, CONV_RS)
            acc = jnp.zeros((CONV_RS, LANES), F32)
            for k in range(CONV_KERNEL):
                off = first + k
                rows = pl.ds(base + (off // SUBLANES) * SUBLANES, CONV_RS)
                acc = acc + win_ref[off % SUBLANES, rows, lanes] * w_ref[k:k + 1, lanes]
            y_ref[pl.ds(base, CONV_RS), lanes] = acc + b_ref[:, lanes]
            return carry

        lax.fori_loop(0, CONV_TR // CONV_RS, body, 0)

    bd = bd_ref[...]
    for c in range(CONV_WIDTH // LANES):
        lanes = slice(c * LANES, (c + 1) * LANES)
        y = y_ref[:, lanes]
        d = y - _split_dot(y, bd)
        var = _split_dot(d * d, bd)
        yn = d * lax.rsqrt(var + EPS) * lg_ref[:, lanes] + lb_ref[:, lanes]
        o_ref[0, :, lanes] = (yn * _sigmoid(yn)).astype(BF16)


def _conv_module(u3, w, b, lg, lb, bd):
    halo_blocks = CONV_TR // CONV_HALO
    full = lambda a: pl.BlockSpec(a.shape, lambda bi, i: (0,) * a.ndim)
    return pl.pallas_call(
        _conv_kernel,
        grid=(BATCH, SEQ // CONV_TR),
        in_specs=[
            pl.BlockSpec((1, CONV_TR, CONV_WIDTH), lambda bi, i: (bi, i, 0)),
            pl.BlockSpec((1, CONV_HALO, CONV_WIDTH),
                         lambda bi, i: (bi, jnp.maximum(i * halo_blocks - 1, 0), 0)),
            full(w), full(b), full(lg), full(lb), full(bd),
        ],
        out_specs=pl.BlockSpec((1, CONV_TR, CONV_WIDTH), lambda bi, i: (bi, i, 0)),
        out_shape=jax.ShapeDtypeStruct((BATCH, SEQ, CONV_WIDTH), BF16),
        scratch_shapes=[pltpu.VMEM((SUBLANES, CONV_HALO + CONV_TR, CONV_WIDTH), F32),
                        pltpu.VMEM((CONV_TR, CONV_WIDTH), F32)],
        compiler_params=pltpu.CompilerParams(dimension_semantics=("arbitrary", "arbitrary")),
        name="conv_module",
    )(u3, u3, w, b, lg, lb, bd)


CMP_ROWS = SEQ // CMP_STRIDE
CMP_CHUNK = CMP_STRIDE * HEAD_DIM


def _compress_kernel(x_ref, pos_ref, w1_ref, b1_ref, w2_ref, b2_ref, w2t_ref, b2t_ref, kg_ref, o_ref):
    kind = pl.program_id(0)
    xa = x_ref[0, 0, 0]
    xb = pltpu.roll(xa, CMP_ROWS - 1, axis=0)
    a = (xa + pos_ref[0, :, 0:CMP_CHUNK]).astype(BF16)
    b = (xb + pos_ref[0, :, CMP_CHUNK:2 * CMP_CHUNK]).astype(BF16)
    hid = _dot(a, w1_ref[0, 0:CMP_CHUNK, :]) + _dot(b, w1_ref[0, CMP_CHUNK:2 * CMP_CHUNK, :]) + b1_ref[0]
    hid = (hid * _sigmoid(hid)).astype(BF16)
    out = _dot(hid, w2_ref[0]) + b2_ref[0]
    ms = jnp.mean(out * out, axis=-1, keepdims=True)
    normed = out * lax.rsqrt(ms + EPS) * kg_ref[...]
    out_t = _dot_nt(w2t_ref[0], hid) + b2t_ref[0]
    o_ref[0, 0, 0] = jnp.where(kind == 0, normed, out_t).astype(BF16)


def _compress(xr, pos, w1, b1, w2, b2, kg):
    w2t = jnp.swapaxes(w2, 1, 2)
    b2t = jnp.swapaxes(b2, 1, 2)
    per_kind = lambda a: pl.BlockSpec((1,) + a.shape[1:], lambda k, bi, h: (k,) + (0,) * (a.ndim - 1))
    return pl.pallas_call(
        _compress_kernel,
        grid=(2, BATCH, N_KV_HEADS),
        in_specs=[
            pl.BlockSpec((1, 1, 1, CMP_ROWS, CMP_CHUNK), lambda k, bi, h: (k, bi, h, 0, 0)),
            per_kind(pos), per_kind(w1), per_kind(b1), per_kind(w2), per_kind(b2),
            per_kind(w2t), per_kind(b2t),
            pl.BlockSpec(kg.shape, lambda k, bi, h: (0, 0)),
        ],
        out_specs=pl.BlockSpec((1, 1, 1, CMP_ROWS, LANES), lambda k, bi, h: (k, bi, h, 0, 0)),
        out_shape=jax.ShapeDtypeStruct((2, BATCH, N_KV_HEADS, CMP_ROWS, LANES), BF16),
        compiler_params=pltpu.CompilerParams(
            dimension_semantics=("arbitrary", "arbitrary", "arbitrary")),
        name="compress",
    )(xr, pos, w1, b1, w2, b2, w2t, b2t, kg)


QCOLS = GQA * Q_BLOCK
SEL_CHUNK = 256
WIN_CHUNKS = WINDOW // Q_BLOCK + 1
MASK_BIAS = -1e30


def _attn_kernel(q_ref, gatet_ref, kc_ref, vct_ref, ks_ref, vst_ref, kw_ref, vwt_ref,
                 ovlt_ref, et_ref, tri_ref, o_ref, ssel_sc, swin_sc, acc_sc):
    n = pl.program_id(1)
    start = n * Q_BLOCK
    col = lax.broadcasted_iota(jnp.int32, (1, QCOLS), 1)
    t_cols = start + (col & (Q_BLOCK - 1))
    tq = start + lax.broadcasted_iota(jnp.int32, (1, Q_BLOCK), 1)
    upper_half = lax.broadcasted_iota(jnp.int32, (Q_BLOCK, LANES), 1) >= HEAD_DIM
    blk_row = lax.broadcasted_iota(jnp.int32, (N_SEL, 1), 0)
    blk_row_f = blk_row.astype(F32)
    cmp_end = lax.broadcasted_iota(jnp.int32, (CMP_ROWS, 1), 0) * CMP_STRIDE + (CMP_BLOCK - 1)

    heads = []
    for kvh in range(N_KV_HEADS):
        kv_lanes = slice(kvh * LANES, (kvh + 1) * LANES)
        parts = []
        for g in range(GQA):
            h = kvh * GQA + g
            qc = q_ref[0, :, (h // 2) * LANES:(h // 2 + 1) * LANES]
            keep = upper_half if h % 2 == 1 else jnp.logical_not(upper_half)
            parts.append(jnp.where(keep, qc, jnp.zeros_like(qc)))
        qs = jnp.concatenate(parts, axis=0)

        s_c = _dot_nt(kc_ref[0, kvh], qs)
        mask_c = cmp_end <= t_cols
        s_c = jnp.where(mask_c, s_c, NEG_INF)
        m_c = jnp.max(s_c, axis=0, keepdims=True)
        p_c = jnp.where(mask_c, jnp.exp(s_c - m_c), 0.0)
        p_c = p_c / jnp.maximum(jnp.sum(p_c, axis=0, keepdims=True), 1e-30)
        o_cmp = _dot(vct_ref[0, kvh], p_c.astype(BF16))

        p_sum = p_c[:, 0:Q_BLOCK]
        for g in range(1, GQA):
            p_sum = p_sum + p_c[:, g * Q_BLOCK:(g + 1) * Q_BLOCK]
        p_hi = p_sum.astype(BF16)
        p_lo = (p_sum - p_hi.astype(F32)).astype(BF16)
        imp = _dot(ovlt_ref[...], p_hi) + _dot(ovlt_ref[...], p_lo)
        cur = tq >> 6
        forced = (blk_row == 0) | (blk_row == cur) | (blk_row == cur - 1)
        causal_blk = blk_row * SEL_BLOCK <= tq
        score = jnp.where(causal_blk, imp + jnp.where(forced, FORCE_BONUS, 0.0), NEG_INF)
        sel = jnp.zeros((N_SEL, Q_BLOCK), F32)
        for _ in range(SEL_TOPK):
            best = jnp.max(score, axis=0, keepdims=True)
            idx = jnp.min(jnp.where(score == best, blk_row_f, float(LANES)), axis=0, keepdims=True)
            hit = blk_row_f == idx
            sel = jnp.where(hit, 1.0, sel)
            score = jnp.where(hit, -3e38, score)
        sel_bias = ((sel - 1.0) * (-MASK_BIAS)).astype(BF16)
        sel_bias = jnp.concatenate([sel_bias] * GQA, axis=1)
        sel_bias = jnp.concatenate([sel_bias, jnp.zeros((LANES - N_SEL, QCOLS), BF16)], axis=0)

        def sel_scores(k0):
            return (_dot_nt(ks_ref[0, pl.ds(k0, SEL_CHUNK), kv_lanes], qs)
                    + _dot(et_ref[pl.ds(k0, SEL_CHUNK), :], sel_bias))

        def sel_pass1(i, mx):
            k0 = pl.multiple_of(i * SEL_CHUNK, SEL_CHUNK)
            s = sel_scores(k0)
            ssel_sc[pl.ds(k0, SEL_CHUNK), :] = s
            return jnp.maximum(mx, jnp.max(s, axis=0, keepdims=True))

        last = n // 2
        mx = lax.fori_loop(0, last, sel_pass1, jnp.full((1, QCOLS), NEG_INF, F32))
        k_last = pl.multiple_of(last * SEL_CHUNK, SEL_CHUNK)
        keypos = k_last + lax.broadcasted_iota(jnp.int32, (SEL_CHUNK, 1), 0)
        s = jnp.where(keypos <= t_cols, sel_scores(k_last), NEG_INF)
        ssel_sc[pl.ds(k_last, SEL_CHUNK), :] = s
        mx = jnp.maximum(mx, jnp.max(s, axis=0, keepdims=True))

        acc_sc[...] = jnp.zeros((LANES, QCOLS), F32)

        def sel_pass2(i, carry):
            k0 = pl.multiple_of(i * SEL_CHUNK, SEL_CHUNK)
            p = jnp.exp(ssel_sc[pl.ds(k0, SEL_CHUNK), :] - mx).astype(BF16)
            acc_sc[...] += _dot(vst_ref[0, kvh, :, pl.ds(k0, SEL_CHUNK)], p)
            return carry

        lax.fori_loop(0, last + 1, sel_pass2, 0)
        acc = acc_sc[...]
        o_sel = acc[0:HEAD_DIM] * (1.0 / acc[HEAD_DIM:HEAD_DIM + 1])

        mxw = jnp.full((1, QCOLS), NEG_INF, F32)
        k_win = []
        for i in range(WIN_CHUNKS):
            kb = n - (WIN_CHUNKS - 1) + i
            k0 = pl.multiple_of(jnp.maximum(kb, 0) * Q_BLOCK, Q_BLOCK)
            k_win.append(k0)
            s = _dot_nt(kw_ref[0, pl.ds(k0, Q_BLOCK), kv_lanes], qs)
            if i == 0:
                s = s + tri_ref[1]
            if i == WIN_CHUNKS - 1:
                s = s + tri_ref[0]
            else:
                s = s + jnp.where(kb >= 0, 0.0, MASK_BIAS)
            swin_sc[i * Q_BLOCK:(i + 1) * Q_BLOCK, :] = s
            mxw = jnp.maximum(mxw, jnp.max(s, axis=0, keepdims=True))
        accw = jnp.zeros((LANES, QCOLS), F32)
        for i in range(WIN_CHUNKS):
            p = jnp.exp(swin_sc[i * Q_BLOCK:(i + 1) * Q_BLOCK, :] - mxw).astype(BF16)
            accw = accw + _dot(vwt_ref[0, kvh, :, pl.ds(k_win[i], Q_BLOCK)], p)
        o_win = accw[0:HEAD_DIM] * (1.0 / accw[HEAD_DIM:HEAD_DIM + 1])

        for g in range(GQA):
            r = (kvh * GQA + g) * N_BRANCH
            cols = slice(g * Q_BLOCK, (g + 1) * Q_BLOCK)
            heads.append(gatet_ref[0, r:r + 1, :] * o_cmp[0:HEAD_DIM, cols]
                         + gatet_ref[0, r + 1:r + 2, :] * o_sel[:, cols]
                         + gatet_ref[0, r + 2:r + 3, :] * o_win[:, cols])

    for c in range(N_HEADS // 2):
        pair = jnp.concatenate([heads[2 * c], heads[2 * c + 1]], axis=0)
        o_ref[0, :, c * LANES:(c + 1) * LANES] = pair.T.astype(BF16)


def _attention(q3, gatet, kvc, ks3, vst, kw3, vwt, ovlt, expand_t, tri):
    qblk = lambda width: pl.BlockSpec((1, Q_BLOCK, width), lambda bi, n: (bi, n, 0))
    seq = pl.BlockSpec((1, SEQ, 2 * LANES), lambda bi, n: (bi, 0, 0))
    seq_t = pl.BlockSpec((1, N_KV_HEADS, LANES, SEQ), lambda bi, n: (bi, 0, 0, 0))
    cmp_spec = lambda kind: pl.BlockSpec((None, 1, N_KV_HEADS, CMP_ROWS, LANES),
                                         lambda bi, n: (kind, bi, 0, 0, 0))
    full = lambda a: pl.BlockSpec(a.shape, lambda bi, n: (0,) * a.ndim)
    return pl.pallas_call(
        _attn_kernel,
        grid=(BATCH, SEQ // Q_BLOCK),
        in_specs=[qblk(ATTN_WIDTH), pl.BlockSpec((1, LANES, Q_BLOCK), lambda bi, n: (bi, 0, n)),
                  cmp_spec(0), cmp_spec(1), seq, seq_t, seq, seq_t,
                  full(ovlt), full(expand_t), full(tri)],
        out_specs=qblk(ATTN_WIDTH),
        out_shape=jax.ShapeDtypeStruct((BATCH, SEQ, ATTN_WIDTH), BF16),
        scratch_shapes=[pltpu.VMEM((SEQ, QCOLS), F32),
                        pltpu.VMEM((WIN_CHUNKS * Q_BLOCK, QCOLS), F32),
                        pltpu.VMEM((LANES, QCOLS), F32)],
        compiler_params=pltpu.CompilerParams(
            dimension_semantics=("arbitrary", "arbitrary"), vmem_limit_bytes=VMEM_LIMIT),
        name="nsa_attention",
    )(q3, gatet, kvc, kvc, ks3, vst, kw3, vwt, ovlt, expand_t, tri)


OUT_TM = 512


def _out_proj_kernel(x_ref, conv_ref, attn_ref, w_ref, g_ref, x1_ref, h_ref):
    x1 = (x_ref[...] + _dot(conv_ref[...], w_ref[0:CONV_WIDTH, :])
          + _dot(attn_ref[...], w_ref[CONV_WIDTH:CONV_WIDTH + ATTN_WIDTH, :]))
    x1_ref[...] = x1
    ms = jnp.mean(x1 * x1, axis=-1, keepdims=True)
    h_ref[...] = (x1 * lax.rsqrt(ms + EPS) * g_ref[...]).astype(BF16)


def _out_proj_router_kernel(x_ref, conv_ref, attn_ref, w_ref, g_ref, r_ref, x1_ref, h_ref, comb_ref):
    x1 = (x_ref[...] + _dot(conv_ref[...], w_ref[0:CONV_WIDTH, :])
          + _dot(attn_ref[...], w_ref[CONV_WIDTH:CONV_WIDTH + ATTN_WIDTH, :]))
    x1_ref[...] = x1
    ms = jnp.mean(x1 * x1, axis=-1, keepdims=True)
    h = x1 * lax.rsqrt(ms + EPS) * g_ref[...]
    h_ref[...] = h.astype(BF16)
    h_hi = h.astype(BF16)
    h_lo = (h - h_hi.astype(F32)).astype(BF16)
    logits = _dot(h_hi, r_ref[0]) + _dot(h_lo, r_ref[0]) + _dot(h_hi, r_ref[1])
    lane = lax.broadcasted_iota(jnp.int32, (1, LANES), 1).astype(F32)
    logits = jnp.where(lane < N_EXPERTS, logits, NEG_INF)
    v1 = jnp.max(logits, axis=-1, keepdims=True)
    i1 = jnp.min(jnp.where(logits == v1, lane, float(LANES)), axis=-1, keepdims=True)
    hit1 = lane == i1
    rest = jnp.where(hit1, -3e38, logits)
    v2 = jnp.max(rest, axis=-1, keepdims=True)
    i2 = jnp.min(jnp.where(rest == v2, lane, float(LANES)), axis=-1, keepdims=True)
    hit2 = lane == i2
    e2 = jnp.exp(v2 - v1)
    w_top = 1.0 / (1.0 + e2)
    comb_ref[...] = jnp.where(hit1, w_top, 0.0) + jnp.where(hit2, e2 * w_top, 0.0)


def _out_proj(x2, conv2, attn2, w, g, router=None):
    tm = OUT_TM
    row = lambda width: pl.BlockSpec((tm, width), lambda i: (i, 0))
    full = lambda a: pl.BlockSpec(a.shape, lambda i: (0,) * a.ndim)
    in_specs = [row(D_MODEL), row(CONV_WIDTH), row(ATTN_WIDTH), full(w), full(g)]
    out_shape = [jax.ShapeDtypeStruct((TOKENS, D_MODEL), F32), jax.ShapeDtypeStruct((TOKENS, D_MODEL), BF16)]
    out_specs = [row(D_MODEL), row(D_MODEL)]
    args = [x2, conv2, attn2, w, g]
    body = _out_proj_kernel
    if router is not None:
        body = _out_proj_router_kernel
        in_specs.append(full(router))
        args.append(router)
        out_shape.append(jax.ShapeDtypeStruct((TOKENS, LANES), F32))
        out_specs.append(row(LANES))
    return pl.pallas_call(
        body,
        grid=(TOKENS // tm,),
        in_specs=in_specs,
        out_specs=tuple(out_specs),
        out_shape=tuple(out_shape),
        compiler_params=pltpu.CompilerParams(
            dimension_semantics=("arbitrary",), vmem_limit_bytes=VMEM_LIMIT),
        name="out_proj",
    )(*args)


FFN_TM = 512
FFN_TF = 1408


def _ffn_kernel(x1_ref, h_ref, w1_ref, w3_ref, w2_ref, o_ref, acc_ref):
    f = pl.program_id(1)
    h = h_ref[...]
    a = _dot(h, w1_ref[...])
    act = (a * _sigmoid(a) * _dot(h, w3_ref[...])).astype(BF16)
    part = _dot(act, w2_ref[...])

    @pl.when(f == 0)
    def _():
        acc_ref[...] = x1_ref[...] + part

    @pl.when(f > 0)
    def _():
        acc_ref[...] += part

    @pl.when(f == pl.num_programs(1) - 1)
    def _():
        o_ref[...] = acc_ref[...]


def _dense_ffn(x1, h, w1, w3, w2):
    tm, tf = FFN_TM, FFN_TF
    return pl.pallas_call(
        _ffn_kernel,
        grid=(TOKENS // tm, D_FF // tf),
        in_specs=[
            pl.BlockSpec((tm, D_MODEL), lambda i, f: (i, 0)),
            pl.BlockSpec((tm, D_MODEL), lambda i, f: (i, 0)),
            pl.BlockSpec((D_MODEL, tf), lambda i, f: (0, f)),
            pl.BlockSpec((D_MODEL, tf), lambda i, f: (0, f)),
            pl.BlockSpec((tf, D_MODEL), lambda i, f: (f, 0)),
        ],
        out_specs=pl.BlockSpec((tm, D_MODEL), lambda i, f: (i, 0)),
        out_shape=jax.ShapeDtypeStruct((TOKENS, D_MODEL), F32),
        scratch_shapes=[pltpu.VMEM((tm, D_MODEL), F32)],
        compiler_params=pltpu.CompilerParams(
            dimension_semantics=("arbitrary", "arbitrary"), vmem_limit_bytes=VMEM_LIMIT),
        name="dense_ffn",
    )(x1, h, w1, w3, w2)


MOE_SB = 512
MOE_TM = 512
MOE_TF = 896
MOE_NB = TOKENS // MOE_SB
MOE_TILES = (2 * TOKENS) // MOE_TM + N_EXPERTS
MOE_PAIRS = MOE_TILES + N_EXPERTS * MOE_NB


def _moe_rank_kernel(comb_ref, tri_ref, rank_ref, cend_ref, carry_ref):
    @pl.when(pl.program_id(0) == 0)
    def _():
        carry_ref[...] = jnp.zeros((1, LANES), F32)

    routed = jnp.where(comb_ref[...] > 0.0, 1.0, 0.0).astype(BF16)
    carry = carry_ref[...]
    rank_ref[...] = carry + _dot(tri_ref[...], routed)
    carry = carry + jnp.sum(routed.astype(F32), axis=0, keepdims=True)
    carry_ref[...] = carry
    cend_ref[0] = jnp.broadcast_to(carry, (SUBLANES, LANES))


def _moe_rank(comb):
    idx = np.arange(MOE_SB)
    tri = jnp.asarray(idx[:, None] > idx[None, :], BF16)
    return pl.pallas_call(
        _moe_rank_kernel,
        grid=(MOE_NB,),
        in_specs=[pl.BlockSpec((MOE_SB, LANES), lambda b: (b, 0)),
                  pl.BlockSpec((MOE_SB, MOE_SB), lambda b: (0, 0))],
        out_specs=(pl.BlockSpec((MOE_SB, LANES), lambda b: (b, 0)),
                   pl.BlockSpec((1, SUBLANES, LANES), lambda b: (b, 0, 0))),
        out_shape=(jax.ShapeDtypeStruct((TOKENS, LANES), F32),
                   jax.ShapeDtypeStruct((MOE_NB, SUBLANES, LANES), F32)),
        scratch_shapes=[pltpu.VMEM((1, LANES), F32)],
        compiler_params=pltpu.CompilerParams(dimension_semantics=("arbitrary",)),
        name="moe_rank",
    )(comb, tri)


def _count_le(ends, v):
    return jnp.sum(ends[None, :] <= v[:, None], axis=1).astype(jnp.int32)


def _moe_schedule(cend):
    i32 = jnp.int32
    counts = cend[-1]
    cstart = jnp.concatenate([jnp.zeros((1, N_EXPERTS), i32), cend[:-1]], axis=0)
    tiles_e = (counts + MOE_TM - 1) // MOE_TM
    tile_end = jnp.cumsum(tiles_e)
    tile_start = tile_end - tiles_e
    n_tiles = tile_end[-1]
    group_start = tile_start * MOE_TM

    d = jnp.arange(MOE_TILES, dtype=i32)
    t_valid = d < n_tiles
    t_exp = jnp.minimum(_count_le(tile_end, d), N_EXPERTS - 1)
    t_exp = jnp.where(t_valid, t_exp, t_exp[jnp.maximum(n_tiles - 1, 0)])

    r0 = (d - tile_start[t_exp]) * MOE_TM
    r1 = jnp.minimum(r0 + MOE_TM, counts[t_exp])
    b_lo = jnp.sum(cend[:, t_exp] <= r0[None, :], axis=0).astype(i32)
    b_hi = jnp.sum(cstart[:, t_exp] < r1[None, :], axis=0).astype(i32) - 1
    n_pairs = jnp.where(t_valid, b_hi - b_lo + 1, 1)
    p_end = jnp.cumsum(n_pairs)
    p_start = p_end - n_pairs
    p = jnp.arange(MOE_PAIRS, dtype=i32)
    g_tile = jnp.minimum(_count_le(p_end, p), MOE_TILES - 1)
    g_live = p < p_end[-1]
    g_valid = g_live & t_valid[g_tile]
    g_block = jnp.where(g_valid, b_lo[g_tile] + p - p_start[g_tile], b_hi[jnp.maximum(n_tiles - 1, 0)])
    g_first = g_valid & (p == p_start[g_tile])
    g_zero = g_live & jnp.logical_not(t_valid[g_tile])
    g_flag = g_valid.astype(i32) + 2 * g_first.astype(i32) + 4 * g_zero.astype(i32)
    gather = (g_tile, g_block, t_exp[g_tile], g_flag)

    lo = group_start[None, :] + cstart
    hi = group_start[None, :] + cend
    t_lo = (lo // MOE_TM).reshape(-1)
    n_t = jnp.where(hi > lo, (hi - 1) // MOE_TM - lo // MOE_TM + 1, 0).reshape(-1)
    q_end = jnp.cumsum(n_t)
    q_start = q_end - n_t
    c_valid = p < q_end[-1]
    be = jnp.minimum(_count_le(q_end, p), MOE_NB * N_EXPERTS - 1)
    last = jnp.maximum(q_end[-1] - 1, 0)
    be = jnp.where(c_valid, be, be[last])
    c_tile = jnp.where(c_valid, t_lo[be] + p - q_start[be], (t_lo[be] + n_t[be] - 1))
    c_block = be // N_EXPERTS
    c_first = c_valid & ((p == 0) | (c_block != jnp.roll(c_block, 1)))
    c_flag = c_valid.astype(i32) + 2 * c_first.astype(i32)
    combine = (c_block, c_tile, be % N_EXPERTS, c_flag)
    return group_start, t_exp, t_valid.astype(i32), gather, combine


def _moe_gather_kernel(tile_ref, block_ref, exp_ref, flag_ref, h_ref, pos_ref, xs_ref):
    p = pl.program_id(0)
    flag = flag_ref[p]

    @pl.when((flag & 1) == 1)
    def _():
        rows = (tile_ref[p] * MOE_TM + lax.broadcasted_iota(jnp.int32, (MOE_TM, 1), 0)).astype(F32)
        pos = pos_ref[pl.ds(exp_ref[p], 1), :]
        onehot = jnp.where(rows == pos, 1.0, 0.0).astype(BF16)
        picked = _dot(onehot, h_ref[...]).astype(BF16)

        @pl.when((flag & 2) == 2)
        def _():
            xs_ref[...] = picked

        @pl.when((flag & 2) == 0)
        def _():
            xs_ref[...] += picked

    @pl.when((flag & 4) == 4)
    def _():
        xs_ref[...] = jnp.zeros((MOE_TM, D_MODEL), BF16)


def _moe_gather(sched, h, pos_t):
    spec = pltpu.PrefetchScalarGridSpec(
        num_scalar_prefetch=4,
        grid=(MOE_PAIRS,),
        in_specs=[pl.BlockSpec((MOE_SB, D_MODEL), lambda p, t, b, e, f: (b[p], 0)),
                  pl.BlockSpec((SUBLANES, MOE_SB), lambda p, t, b, e, f: (0, b[p]))],
        out_specs=pl.BlockSpec((MOE_TM, D_MODEL), lambda p, t, b, e, f: (t[p], 0)),
    )
    return pl.pallas_call(
        _moe_gather_kernel,
        grid_spec=spec,
        out_shape=jax.ShapeDtypeStruct((MOE_TILES * MOE_TM, D_MODEL), BF16),
        compiler_params=pltpu.CompilerParams(dimension_semantics=("arbitrary",)),
        name="moe_gather",
    )(*sched, h, pos_t)


def _moe_expert_kernel(exp_ref, valid_ref, xs_ref, w1_ref, w3_ref, w2_ref, y_ref, acc_ref):
    d = pl.program_id(0)
    f = pl.program_id(1)

    @pl.when(valid_ref[d] == 1)
    def _():
        x = xs_ref[...]
        a = _dot(x, w1_ref[0])
        act = (a * _sigmoid(a) * _dot(x, w3_ref[0])).astype(BF16)
        part = _dot(act, w2_ref[0])

        @pl.when(f == 0)
        def _():
            acc_ref[...] = part

        @pl.when(f > 0)
        def _():
            acc_ref[...] += part

        @pl.when(f == pl.num_programs(1) - 1)
        def _():
            y_ref[...] = acc_ref[...].astype(BF16)

    @pl.when((valid_ref[d] == 0) & (f == 0))
    def _():
        y_ref[...] = jnp.zeros((MOE_TM, D_MODEL), BF16)


def _moe_experts(t_exp, t_valid, xs, w1, w3, w2):
    tf = MOE_TF
    nf = D_FF_EXPERT // tf
    fidx = lambda d, f, v: f * v[d] + (nf - 1) * (1 - v[d])
    spec = pltpu.PrefetchScalarGridSpec(
        num_scalar_prefetch=2,
        grid=(MOE_TILES, nf),
        in_specs=[pl.BlockSpec((MOE_TM, D_MODEL), lambda d, f, e, v: (d, 0)),
                  pl.BlockSpec((1, D_MODEL, tf), lambda d, f, e, v: (e[d], 0, fidx(d, f, v))),
                  pl.BlockSpec((1, D_MODEL, tf), lambda d, f, e, v: (e[d], 0, fidx(d, f, v))),
                  pl.BlockSpec((1, tf, D_MODEL), lambda d, f, e, v: (e[d], fidx(d, f, v), 0))],
        out_specs=pl.BlockSpec((MOE_TM, D_MODEL), lambda d, f, e, v: (d, 0)),
        scratch_shapes=[pltpu.VMEM((MOE_TM, D_MODEL), F32)],
    )
    return pl.pallas_call(
        _moe_expert_kernel,
        grid_spec=spec,
        out_shape=jax.ShapeDtypeStruct((MOE_TILES * MOE_TM, D_MODEL), BF16),
        compiler_params=pltpu.CompilerParams(
            dimension_semantics=("arbitrary", "arbitrary"), vmem_limit_bytes=VMEM_LIMIT),
        name="moe_experts",
    )(t_exp, t_valid, xs, w1, w3, w2)


def _moe_combine_kernel(block_ref, tile_ref, exp_ref, flag_ref, y_ref, pos_ref, comb_ref, x1_ref, o_ref):
    p = pl.program_id(0)
    flag = flag_ref[p]

    @pl.when((flag & 1) == 1)
    def _():
        pick = lax.broadcasted_iota(jnp.int32, (1, LANES), 1) == exp_ref[p]
        pos = jnp.sum(jnp.where(pick, pos_ref[...], 0.0), axis=-1, keepdims=True)
        wgt = jnp.sum(jnp.where(pick, comb_ref[...], 0.0), axis=-1, keepdims=True)
        cols = (tile_ref[p] * MOE_TM + lax.broadcasted_iota(jnp.int32, (1, MOE_TM), 1)).astype(F32)
        onehot = jnp.where(pos == cols, 1.0, 0.0).astype(BF16)
        part = wgt * _dot(onehot, y_ref[...])

        @pl.when((flag & 2) == 2)
        def _():
            o_ref[...] = x1_ref[...] + part

        @pl.when((flag & 2) == 0)
        def _():
            o_ref[...] += part


def _moe_combine(sched, y, pos, comb, x1):
    tok = lambda width: pl.BlockSpec((MOE_SB, width), lambda p, b, t, e, f: (b[p], 0))
    spec = pltpu.PrefetchScalarGridSpec(
        num_scalar_prefetch=4,
        grid=(MOE_PAIRS,),
        in_specs=[pl.BlockSpec((MOE_TM, D_MODEL), lambda p, b, t, e, f: (t[p], 0)),
                  tok(LANES), tok(LANES), tok(D_MODEL)],
        out_specs=tok(D_MODEL),
    )
    return pl.pallas_call(
        _moe_combine_kernel,
        grid_spec=spec,
        out_shape=jax.ShapeDtypeStruct((TOKENS, D_MODEL), F32),
        compiler_params=pltpu.CompilerParams(
            dimension_semantics=("arbitrary",), vmem_limit_bytes=VMEM_LIMIT),
        name="moe_combine",
    )(*sched, y, pos, comb, x1)


def _moe_ffn(x1, h, comb, w1, w3, w2):
    rank, cend = _moe_rank(comb)
    cend = jnp.round(cend[:, 0, :N_EXPERTS]).astype(jnp.int32)
    group_start, t_exp, t_valid, gather, combine = _moe_schedule(cend)
    start = jnp.zeros((LANES,), F32).at[:N_EXPERTS].set(group_start.astype(F32))
    pos = jnp.where(comb > 0.0, start[None, :] + rank, -1.0)
    xs = _moe_gather(gather, h, pos[:, :SUBLANES].T)
    y = _moe_experts(t_exp, t_valid, xs, w1, w3, w2)
    return _moe_combine(combine, y, pos, comb, x1)


def _group_mean_matrix():
    idx = np.arange(LANES)
    return jnp.asarray((idx[:, None] // CONV_GROUP == idx[None, :] // CONV_GROUP) / CONV_GROUP, BF16)


def _overlap_matrix_t():
    c_start = np.arange(CMP_ROWS) * CMP_STRIDE
    j = np.arange(N_SEL)
    ovl = ((c_start[None, :] < (j[:, None] + 1) * SEL_BLOCK)
           & (c_start[None, :] + CMP_BLOCK > j[:, None] * SEL_BLOCK)
           & (np.arange(CMP_ROWS)[None, :] < N_CMP))
    return jnp.asarray(ovl, BF16)


def _expand_matrix_t():
    j = np.arange(LANES)
    key = np.arange(SEQ)
    return jnp.asarray(key[:, None] // SEL_BLOCK == j[None, :], BF16)


def _triangle_biases():
    k = np.arange(Q_BLOCK)[:, None]
    t = (np.arange(QCOLS) % Q_BLOCK)[None, :]
    return jnp.asarray(np.stack([np.where(k <= t, 0.0, MASK_BIAS), np.where(k > t, 0.0, MASK_BIAS)]), F32)


def _dup(v):
    return jnp.concatenate([v, v], axis=-1)


def kernel(x, attn_norm_g, w_in, conv_dw_w, conv_dw_b, conv_ln_g, conv_ln_b, q_norm_g, k_norm_g,
           cmp_pos_k, cmp_pos_v, cmp_k_w1, cmp_k_b1, cmp_k_w2, cmp_k_b2,
           cmp_v_w1, cmp_v_b1, cmp_v_w2, cmp_v_b2, w_out, ffn_norm_g,
           dense_w1, dense_w3, dense_w2, router_w, moe_w1, moe_w3, moe_w2):
    assert x.shape == (BATCH, SEQ, D_MODEL) and x.dtype == F32
    bd = _group_mean_matrix()
    ovlt = _overlap_matrix_t()
    expand_t = _expand_matrix_t()
    tri = _triangle_biases()
    col_src = jnp.asarray(_IN_COLS_SRC)
    row_src = jnp.asarray(np.maximum(_IN_T_ROWS_SRC, 0))
    row_live = jnp.asarray(_IN_T_ROWS_SRC >= 0)

    x2 = x.reshape(TOKENS, D_MODEL)
    for layer in range(DEPTH):
        w_cols = jnp.take(w_in[layer], col_src, axis=1).astype(BF16)
        w_rows = jnp.where(row_live[:, None], jnp.take(w_in[layer], row_src, axis=1).T, 0.0).astype(BF16)
        qg = jnp.tile(q_norm_g[layer], 2)[None, :] * (HEAD_DIM ** -0.5)
        kgs = _dup(k_norm_g[layer, 1])[None, :]
        kgw = _dup(k_norm_g[layer, 2])[None, :]
        u, q, kcvc, ks, kw, vst, vwt, gatet = _in_proj(
            x2, attn_norm_g[layer][None, :], w_cols, w_rows, qg, kgs, kgw, bd)

        conv = _conv_module(u.reshape(BATCH, SEQ, CONV_WIDTH), conv_dw_w[layer], conv_dw_b[layer][None, :],
                            conv_ln_g[layer][None, :], conv_ln_b[layer][None, :], bd)

        xr = kcvc.reshape(BATCH, SEQ, 2, N_KV_HEADS, HEAD_DIM).transpose(2, 0, 3, 1, 4)
        xr = xr.reshape(2, BATCH, N_KV_HEADS, CMP_ROWS, CMP_CHUNK)
        pos = jnp.stack([cmp_pos_k[layer], cmp_pos_v[layer]]).reshape(2, 1, CMP_BLOCK * HEAD_DIM)
        w1 = jnp.stack([cmp_k_w1[layer], cmp_v_w1[layer]]).astype(BF16)
        b1 = jnp.stack([cmp_k_b1[layer], cmp_v_b1[layer]])[:, None, :]
        w2 = _dup(jnp.stack([cmp_k_w2[layer], cmp_v_w2[layer]])).astype(BF16)
        b2 = _dup(jnp.stack([cmp_k_b2[layer], cmp_v_b2[layer]]))[:, None, :]
        kvc = _compress(xr, pos, w1, b1, w2, b2, _dup(k_norm_g[layer, 0])[None, :])

        seq3 = lambda a: a.reshape(BATCH, SEQ, a.shape[-1])
        attn = _attention(seq3(q), gatet, kvc, seq3(ks), vst, seq3(kw), vwt, ovlt, expand_t, tri)

        w_o = w_out[layer].astype(BF16)
        conv2 = conv.reshape(TOKENS, CONV_WIDTH)
        attn2 = attn.reshape(TOKENS, ATTN_WIDTH)
        g_ffn = ffn_norm_g[layer][None, :]
        i = layer // 2
        if layer % 2 == 0:
            x1, h = _out_proj(x2, conv2, attn2, w_o, g_ffn)
            x2 = _dense_ffn(x1, h, dense_w1[i].astype(BF16), dense_w3[i].astype(BF16),
                            dense_w2[i].astype(BF16))
        else:
            r = jnp.pad(router_w[i], ((0, 0), (0, LANES - N_EXPERTS)))
            r_hi = r.astype(BF16)
            r_lo = (r - r_hi.astype(F32)).astype(BF16)
            x1, h, comb = _out_proj(x2, conv2, attn2, w_o, g_ffn, jnp.stack([r_hi, r_lo]))
            x2 = _moe_ffn(x1, h, comb, moe_w1[i].astype(BF16), moe_w3[i].astype(BF16),
                          moe_w2[i].astype(BF16))
    return x2.reshape(BATCH, SEQ, D_MODEL)
```

```python
import functools

import numpy as np
import jax
import jax.numpy as jnp
from jax import lax
from jax.experimental import pallas as pl
from jax.experimental.pallas import tpu as pltpu

D_MODEL = 1024
BATCH = 8
SEQ = 2048
DEPTH = 2
TOKENS = BATCH * SEQ

CONV_WIDTH = 512
CONV_GROUP = 64
CONV_KERNEL = 31
N_HEADS = 8
HEAD_DIM = 64
N_KV_HEADS = 2
GQA = N_HEADS // N_KV_HEADS
ATTN_WIDTH = N_HEADS * HEAD_DIM
KV_WIDTH = N_KV_HEADS * HEAD_DIM
N_BRANCH = 3
CMP_BLOCK = 32
CMP_STRIDE = 16
CMP_HIDDEN = 256
N_CMP = (SEQ - CMP_BLOCK) // CMP_STRIDE + 1
SEL_BLOCK = 64
SEL_TOPK = 8
N_SEL = SEQ // SEL_BLOCK
WINDOW = 512
Q_BLOCK = 128
FORCE_BONUS = 1e4
NEG_INF = -1e30
LOG2_E = 1.4426950408889634
D_FF = 2816
N_EXPERTS = 8
D_FF_EXPERT = 3584
EPS = 1e-6

LANES = 128
SUBLANES = 8
VMEM_LIMIT = 48 * 1024 * 1024

F32 = jnp.float32
BF16 = jnp.bfloat16

_OFF_VAL = 0
_OFF_GATE = 512
_OFF_Q = 1024
_OFF_KCVC = 1536
_OFF_KS = 1792
_OFF_KW = 1920
IN_COLS = 2048
T_ROWS = 5 * LANES

_BASE_KV = 2 * CONV_WIDTH + ATTN_WIDTH


def _in_proj_columns():
    cols = list(range(0, _BASE_KV + 2 * KV_WIDTH))
    for piece in (2, 4):
        start = _BASE_KV + piece * KV_WIDTH
        cols += list(range(start, start + KV_WIDTH))
    assert len(cols) == IN_COLS
    return np.asarray(cols, np.int32)


def _in_proj_t_rows():
    rows = []
    for piece in (3, 5):
        start = _BASE_KV + piece * KV_WIDTH
        for head in range(N_KV_HEADS):
            rows += list(range(start + head * HEAD_DIM, start + (head + 1) * HEAD_DIM)) + [-1] * HEAD_DIM
    glog = _BASE_KV + 6 * KV_WIDTH
    rows += list(range(glog, glog + N_HEADS * N_BRANCH)) + [-1] * (LANES - N_HEADS * N_BRANCH)
    assert len(rows) == T_ROWS
    return np.asarray(rows, np.int32)


_IN_COLS_SRC = _in_proj_columns()
_IN_T_ROWS_SRC = _in_proj_t_rows()


def _sigmoid(v):
    return 1.0 / (1.0 + jnp.exp(-v))


def _dot(a, b):
    return jnp.dot(a, b, preferred_element_type=F32)


def _dot_nt(a, b):
    return lax.dot_general(a, b, (((1,), (1,)), ((), ())), preferred_element_type=F32)


def _split_dot(v, m):
    hi = v.astype(BF16)
    lo = (v - hi.astype(F32)).astype(BF16)
    return _dot(hi, m) + _dot(lo, m)


IN_TM = 512


def _in_proj_kernel(x_ref, g_ref, w_ref, wt_ref, qg_ref, kgs_ref, kgw_ref, bd_ref,
                    u_ref, q_ref, kcvc_ref, ks_ref, kw_ref, vst_ref, vwt_ref, gatet_ref):
    x = x_ref[...]
    ms = jnp.mean(x * x, axis=-1, keepdims=True)
    h = (x * lax.rsqrt(ms + EPS) * g_ref[...]).astype(BF16)

    def proj(lo, width):
        return _dot(h, w_ref[:, lo:lo + width])

    u_ref[...] = proj(_OFF_VAL, CONV_WIDTH) * _sigmoid(proj(_OFF_GATE, CONV_WIDTH))

    bd = bd_ref[...]
    lane = lax.broadcasted_iota(jnp.int32, (IN_TM, LANES), 1)
    lower = lane < HEAD_DIM

    def head_slots(pair, upper, out_ref, slot):
        out_ref[:, slot * LANES:(slot + 1) * LANES] = jnp.where(lower, pair, upper).astype(BF16)
        swapped = pltpu.roll(pair, HEAD_DIM, axis=1)
        out_ref[:, (slot + 1) * LANES:(slot + 2) * LANES] = jnp.where(lower, swapped, upper).astype(BF16)

    q = proj(_OFF_Q, ATTN_WIDTH)
    for c in range(ATTN_WIDTH // LANES):
        qc = q[:, c * LANES:(c + 1) * LANES]
        msq = _split_dot(qc * qc, bd)
        head_slots(qc * lax.rsqrt(msq + EPS) * qg_ref[...], 0.0, q_ref, 2 * c)

    kcvc_ref[...] = proj(_OFF_KCVC, 2 * KV_WIDTH)

    def normed_k(off, kg_ref):
        k = proj(off, KV_WIDTH)
        return k * lax.rsqrt(_split_dot(k * k, bd) + EPS) * kg_ref[...]

    tok = (pl.program_id(0) % (SEQ // IN_TM)) * IN_TM + lax.broadcasted_iota(jnp.int32, (IN_TM, 1), 0)
    block_hot = jnp.where(lane - HEAD_DIM == tok // SEL_BLOCK, 1.0, 0.0)
    head_slots(normed_k(_OFF_KS, kgs_ref), block_hot, ks_ref, 0)
    head_slots(normed_k(_OFF_KW, kgw_ref), 0.0, kw_ref, 0)

    zt = _dot_nt(wt_ref[...], h)
    ones_rows = lax.broadcasted_iota(jnp.int32, (LANES, IN_TM), 0) >= HEAD_DIM
    for j, out_ref in enumerate((vst_ref, vst_ref, vwt_ref, vwt_ref)):
        blk = zt[j * LANES:(j + 1) * LANES]
        out_ref[0, j % N_KV_HEADS] = jnp.where(ones_rows, 1.0, blk).astype(BF16)
    gatet_ref[0] = _sigmoid(zt[4 * LANES:5 * LANES])


def _in_proj(x2, g, w, wt, qg, kgs, kgw, bd):
    tm = IN_TM
    per_seq = SEQ // tm
    row = lambda width: pl.BlockSpec((tm, width), lambda i: (i, 0))
    full = lambda a: pl.BlockSpec(a.shape, lambda i: (0,) * a.ndim)
    vt_spec = pl.BlockSpec((1, N_KV_HEADS, LANES, tm), lambda i: (i // per_seq, 0, 0, i % per_seq))
    vt_shape = jax.ShapeDtypeStruct((BATCH, N_KV_HEADS, LANES, SEQ), BF16)
    out_shape = (
        jax.ShapeDtypeStruct((TOKENS, CONV_WIDTH), F32),
        jax.ShapeDtypeStruct((TOKENS, N_HEADS * LANES), BF16),
        jax.ShapeDtypeStruct((TOKENS, 2 * KV_WIDTH), F32),
        jax.ShapeDtypeStruct((TOKENS, 2 * LANES), BF16),
        jax.ShapeDtypeStruct((TOKENS, 2 * LANES), BF16),
        vt_shape,
        vt_shape,
        jax.ShapeDtypeStruct((BATCH, LANES, SEQ), F32),
    )
    out_specs = (row(CONV_WIDTH), row(N_HEADS * LANES), row(2 * KV_WIDTH), row(2 * LANES), row(2 * LANES),
                 vt_spec, vt_spec,
                 pl.BlockSpec((1, LANES, tm), lambda i: (i // per_seq, 0, i % per_seq)))
    return pl.pallas_call(
        _in_proj_kernel,
        grid=(TOKENS // tm,),
        in_specs=[row(D_MODEL), full(g), full(w), full(wt), full(qg), full(kgs), full(kgw), full(bd)],
        out_specs=out_specs,
        out_shape=out_shape,
        compiler_params=pltpu.CompilerParams(
            dimension_semantics=("arbitrary",), vmem_limit_bytes=VMEM_LIMIT),
        name="in_proj",
    )(x2, g, w, wt, qg, kgs, kgw, bd)


CONV_TR = 512
CONV_HALO = 32
CONV_RS = 64


def _conv_kernel(ucur_ref, uprev_ref, w_ref, b_ref, lg_ref, lb_ref, bd_ref, o_ref, win_ref, y_ref):
    i = pl.program_id(1)
    win_ref[0, 0:CONV_HALO, :] = jnp.where(i > 0, uprev_ref[0], 0.0)
    win_ref[0, CONV_HALO:CONV_HALO + CONV_TR, :] = ucur_ref[0]
    first = CONV_HALO - (CONV_KERNEL - 1)
    shifted_rows = CONV_HALO + CONV_TR - SUBLANES
    for s in range(1, SUBLANES):
        win_ref[s, 0:shifted_rows, :] = win_ref[0, s:s + shifted_rows, :]

    for c in range(CONV_WIDTH // LANES):
        lanes = slice(c * LANES, (c + 1) * LANES)

        def body(r, carry):
            base = pl.multiple_of(r * CONV_RS, CONV_RS)
            acc = jnp.zeros((CONV_RS, LANES), F32)
            for k in range(CONV_KERNEL):
                off = first + k
                rows = pl.ds(base + (off // SUBLANES) * SUBLANES, CONV_RS)
                acc = acc + win_ref[off % SUBLANES, rows, lanes] * w_ref[k:k + 1, lanes]
            y_ref[pl.ds(base, CONV_RS), lanes] = acc + b_ref[:, lanes]
            return carry

        lax.fori_loop(0, CONV_TR // CONV_RS, body, 0)

    bd = bd_ref[...]
    for c in range(CONV_WIDTH // LANES):
        lanes = slice(c * LANES, (c + 1) * LANES)
        y = y_ref[:, lanes]
        d = y - _split_dot(y, bd)
        var = _split_dot(d * d, bd)
        yn = d * lax.rsqrt(var + EPS) * lg_ref[:, lanes] + lb_ref[:, lanes]
        o_ref[0, :, lanes] = (yn * _sigmoid(yn)).astype(BF16)


def _conv_module(u3, w, b, lg, lb, bd):
    halo_blocks = CONV_TR // CONV_HALO
    full = lambda a: pl.BlockSpec(a.shape, lambda bi, i: (0,) * a.ndim)
    return pl.pallas_call(
        _conv_kernel,
        grid=(BATCH, SEQ // CONV_TR),
        in_specs=[
            pl.BlockSpec((1, CONV_TR, CONV_WIDTH), lambda bi, i: (bi, i, 0)),
            pl.BlockSpec((1, CONV_HALO, CONV_WIDTH),
                         lambda bi, i: (bi, jnp.maximum(i * halo_blocks - 1, 0), 0)),
            full(w), full(b), full(lg), full(lb), full(bd),
        ],
        out_specs=pl.BlockSpec((1, CONV_TR, CONV_WIDTH), lambda bi, i: (bi, i, 0)),
        out_shape=jax.ShapeDtypeStruct((BATCH, SEQ, CONV_WIDTH), BF16),
        scratch_shapes=[pltpu.VMEM((SUBLANES, CONV_HALO + CONV_TR, CONV_WIDTH), F32),
                        pltpu.VMEM((CONV_TR, CONV_WIDTH), F32)],
        compiler_params=pltpu.CompilerParams(dimension_semantics=("arbitrary", "arbitrary")),
        name="conv_module",
    )(u3, u3, w, b, lg, lb, bd)


CMP_ROWS = SEQ // CMP_STRIDE
CMP_CHUNK = CMP_STRIDE * HEAD_DIM


def _compress_kernel(x_ref, pos_ref, w1_ref, b1_ref, w2_ref, b2_ref, w2t_ref, b2t_ref, kg_ref, o_ref):
    kind = pl.program_id(0)
    xa = x_ref[0, 0, 0]
    xb = pltpu.roll(xa, CMP_ROWS - 1, axis=0)
    a = (xa + pos_ref[0, :, 0:CMP_CHUNK]).astype(BF16)
    b = (xb + pos_ref[0, :, CMP_CHUNK:2 * CMP_CHUNK]).astype(BF16)
    hid = _dot(a, w1_ref[0, 0:CMP_CHUNK, :]) + _dot(b, w1_ref[0, CMP_CHUNK:2 * CMP_CHUNK, :]) + b1_ref[0]
    hid = (hid * _sigmoid(hid)).astype(BF16)
    out = _dot(hid, w2_ref[0]) + b2_ref[0]
    ms = jnp.sum(out * out, axis=-1, keepdims=True) * (1.0 / HEAD_DIM)
    normed = out * lax.rsqrt(ms + EPS) * kg_ref[...]
    out_t = _dot_nt(w2t_ref[0], hid) + b2t_ref[0]
    o_ref[0, 0, 0] = jnp.where(kind == 0, normed, out_t).astype(BF16)


def _compress(xr, pos, w1, b1, w2, b2, kg):
    w2t = jnp.swapaxes(w2, 1, 2)
    b2t = jnp.swapaxes(b2, 1, 2)
    per_kind = lambda a: pl.BlockSpec((1,) + a.shape[1:], lambda k, bi, h: (k,) + (0,) * (a.ndim - 1))
    return pl.pallas_call(
        _compress_kernel,
        grid=(2, BATCH, N_KV_HEADS),
        in_specs=[
            pl.BlockSpec((1, 1, 1, CMP_ROWS, CMP_CHUNK), lambda k, bi, h: (k, bi, h, 0, 0)),
            per_kind(pos), per_kind(w1), per_kind(b1), per_kind(w2), per_kind(b2),
            per_kind(w2t), per_kind(b2t),
            pl.BlockSpec(kg.shape, lambda k, bi, h: (0, 0)),
        ],
        out_specs=pl.BlockSpec((1, 1, 1, CMP_ROWS, LANES), lambda k, bi, h: (k, bi, h, 0, 0)),
        out_shape=jax.ShapeDtypeStruct((2, BATCH, N_KV_HEADS, CMP_ROWS, LANES), BF16),
        compiler_params=pltpu.CompilerParams(
            dimension_semantics=("arbitrary", "arbitrary", "arbitrary")),
        name="compress",
    )(xr, pos, w1, b1, w2, b2, w2t, b2t, kg)


QCOLS = GQA * Q_BLOCK
SEL_CHUNK = 256
WIN_CHUNKS = WINDOW // Q_BLOCK + 1
MASK_BIAS = -1e30


def _attn_kernel(q_ref, gatet_ref, kc_ref, vct_ref, ks_ref, vst_ref, kw_ref, vwt_ref,
                 ovlt_ref, tri_ref, o_ref, ssel_sc, swin_sc, acc_sc, out_sc):
    n = pl.program_id(1)
    start = n * Q_BLOCK
    col = lax.broadcasted_iota(jnp.int32, (1, QCOLS), 1)
    t_cols = start + (col & (Q_BLOCK - 1))
    tq = start + lax.broadcasted_iota(jnp.int32, (1, Q_BLOCK), 1)
    blk_row = lax.broadcasted_iota(jnp.int32, (N_SEL, 1), 0)
    blk_row_f = blk_row.astype(F32)
    cmp_end = lax.broadcasted_iota(jnp.int32, (CMP_ROWS, 1), 0) * CMP_STRIDE + (CMP_BLOCK - 1)

    kv_heads = range(N_KV_HEADS)
    kv_lanes = [slice(kvh * LANES, (kvh + 1) * LANES) for kvh in kv_heads]

    def gate_row(kvh, g, branch):
        r = (kvh * GQA + g) * N_BRANCH + branch
        return gatet_ref[0, r:r + 1, :]

    def head_rows(kvh, g):
        h = kvh * GQA + g
        return slice(h * HEAD_DIM, (h + 1) * HEAD_DIM)

    qs = []
    for kvh in kv_heads:
        h0 = kvh * GQA
        qs.append(jnp.concatenate(
            [q_ref[0, :, (h0 + g) * LANES:(h0 + g + 1) * LANES] for g in range(GQA)], axis=0))

    o_cmp, imp = [], []
    mask_c = cmp_end <= t_cols
    for kvh in kv_heads:
        s_c = jnp.where(mask_c, _dot_nt(kc_ref[0, kvh], qs[kvh]), NEG_INF)
        m_c = jnp.max(s_c, axis=0, keepdims=True)
        p_c = jnp.where(mask_c, jnp.exp2(s_c - m_c), 0.0)
        p_c = p_c / jnp.maximum(jnp.sum(p_c, axis=0, keepdims=True), 1e-30)
        o_cmp.append(_dot(vct_ref[0, kvh], p_c.astype(BF16)))
        p_sum = p_c[:, 0:Q_BLOCK]
        for g in range(1, GQA):
            p_sum = p_sum + p_c[:, g * Q_BLOCK:(g + 1) * Q_BLOCK]
        p_hi = p_sum.astype(BF16)
        p_lo = (p_sum - p_hi.astype(F32)).astype(BF16)
        imp.append(_dot(ovlt_ref[...], p_hi) + _dot(ovlt_ref[...], p_lo))

    tq2 = jnp.concatenate([tq] * N_KV_HEADS, axis=1)
    cur = tq2 >> 6
    forced = (blk_row == 0) | (blk_row == cur) | (blk_row == cur - 1)
    causal_blk = blk_row * SEL_BLOCK <= tq2
    score = jnp.where(causal_blk, jnp.concatenate(imp, axis=1) + jnp.where(forced, FORCE_BONUS, 0.0), NEG_INF)
    sel = jnp.zeros((N_SEL, N_KV_HEADS * Q_BLOCK), F32)
    for _ in range(SEL_TOPK):
        best = jnp.max(score, axis=0, keepdims=True)
        idx = jnp.min(jnp.where(score == best, blk_row_f, float(LANES)), axis=0, keepdims=True)
        hit = blk_row_f == idx
        sel = jnp.where(hit, 1.0, sel)
        score = jnp.where(hit, -3e38, score)
    sel_bias = (sel - 1.0) * (-MASK_BIAS)
    qs_sel = []
    for kvh in kv_heads:
        rows = jnp.concatenate([jnp.zeros((HEAD_DIM, Q_BLOCK), F32), sel_bias[:, kv_lanes[kvh]],
                                jnp.zeros((LANES - HEAD_DIM - N_SEL, Q_BLOCK), F32)], axis=0)
        bias_q = rows.T.astype(BF16)
        qs_sel.append(qs[kvh] + jnp.concatenate([bias_q] * GQA, axis=0))

    for kvh in kv_heads:
        mxw = jnp.full((1, QCOLS), NEG_INF, F32)
        k_win = []
        for i in range(WIN_CHUNKS):
            kb = n - (WIN_CHUNKS - 1) + i
            k0 = pl.multiple_of(jnp.maximum(kb, 0) * Q_BLOCK, Q_BLOCK)
            k_win.append(k0)
            s = _dot_nt(kw_ref[0, pl.ds(k0, Q_BLOCK), kv_lanes[kvh]], qs[kvh])
            if i == 0:
                s = s + tri_ref[1]
            if i == WIN_CHUNKS - 1:
                s = s + tri_ref[0]
            else:
                s = s + jnp.where(kb >= 0, 0.0, MASK_BIAS)
            swin_sc[kvh, i * Q_BLOCK:(i + 1) * Q_BLOCK, :] = s
            mxw = jnp.maximum(mxw, jnp.max(s, axis=0, keepdims=True))
        accw = jnp.zeros((LANES, QCOLS), F32)
        for i in range(WIN_CHUNKS):
            p = jnp.exp2(swin_sc[kvh, i * Q_BLOCK:(i + 1) * Q_BLOCK, :] - mxw).astype(BF16)
            accw = accw + _dot(vwt_ref[0, kvh, :, pl.ds(k_win[i], Q_BLOCK)], p)
        o_win = accw[0:HEAD_DIM] * (1.0 / accw[HEAD_DIM:HEAD_DIM + 1])
        for g in range(GQA):
            cols = slice(g * Q_BLOCK, (g + 1) * Q_BLOCK)
            out_sc[head_rows(kvh, g), :] = (gate_row(kvh, g, 0) * o_cmp[kvh][0:HEAD_DIM, cols]
                                            + gate_row(kvh, g, 2) * o_win[:, cols])

    def sel_scores(kvh, k0):
        return _dot_nt(ks_ref[0, pl.ds(k0, SEL_CHUNK), kv_lanes[kvh]], qs_sel[kvh])

    def sel_pass1(i, mx):
        k0 = pl.multiple_of(i * SEL_CHUNK, SEL_CHUNK)
        out = []
        for kvh in kv_heads:
            s = sel_scores(kvh, k0)
            ssel_sc[kvh, pl.ds(k0, SEL_CHUNK), :] = s
            out.append(jnp.maximum(mx[kvh], jnp.max(s, axis=0, keepdims=True)))
        return tuple(out)

    last = n // 2
    mx = lax.fori_loop(0, last, sel_pass1, (jnp.full((1, QCOLS), NEG_INF, F32),) * N_KV_HEADS)
    k_last = pl.multiple_of(last * SEL_CHUNK, SEL_CHUNK)
    causal = k_last + lax.broadcasted_iota(jnp.int32, (SEL_CHUNK, 1), 0) <= t_cols
    mx = list(mx)
    for kvh in kv_heads:
        s = jnp.where(causal, sel_scores(kvh, k_last), NEG_INF)
        ssel_sc[kvh, pl.ds(k_last, SEL_CHUNK), :] = s
        mx[kvh] = jnp.maximum(mx[kvh], jnp.max(s, axis=0, keepdims=True))
        acc_sc[kvh] = jnp.zeros((LANES, QCOLS), F32)

    def sel_pass2(i, carry):
        k0 = pl.multiple_of(i * SEL_CHUNK, SEL_CHUNK)
        for kvh in kv_heads:
            p = jnp.exp2(ssel_sc[kvh, pl.ds(k0, SEL_CHUNK), :] - mx[kvh]).astype(BF16)
            acc_sc[kvh] += _dot(vst_ref[0, kvh, :, pl.ds(k0, SEL_CHUNK)], p)
        return carry

    lax.fori_loop(0, last + 1, sel_pass2, 0)
    for kvh in kv_heads:
        acc = acc_sc[kvh]
        o_sel = acc[0:HEAD_DIM] * (1.0 / acc[HEAD_DIM:HEAD_DIM + 1])
        for g in range(GQA):
            out_sc[head_rows(kvh, g), :] += gate_row(kvh, g, 1) * o_sel[:, g * Q_BLOCK:(g + 1) * Q_BLOCK]

    for c in range(N_HEADS // 2):
        o_ref[0, :, c * LANES:(c + 1) * LANES] = out_sc[c * LANES:(c + 1) * LANES, :].T.astype(BF16)


def _attention(q3, gatet, kvc, ks3, vst, kw3, vwt, ovlt, tri):
    qblk = lambda width: pl.BlockSpec((1, Q_BLOCK, width), lambda bi, n: (bi, n, 0))
    seq = pl.BlockSpec((1, SEQ, 2 * LANES), lambda bi, n: (bi, 0, 0))
    seq_t = pl.BlockSpec((1, N_KV_HEADS, LANES, SEQ), lambda bi, n: (bi, 0, 0, 0))
    cmp_spec = lambda kind: pl.BlockSpec((None, 1, N_KV_HEADS, CMP_ROWS, LANES),
                                         lambda bi, n: (kind, bi, 0, 0, 0))
    full = lambda a: pl.BlockSpec(a.shape, lambda bi, n: (0,) * a.ndim)
    return pl.pallas_call(
        _attn_kernel,
        grid=(BATCH, SEQ // Q_BLOCK),
        in_specs=[qblk(N_HEADS * LANES), pl.BlockSpec((1, LANES, Q_BLOCK), lambda bi, n: (bi, 0, n)),
                  cmp_spec(0), cmp_spec(1), seq, seq_t, seq, seq_t, full(ovlt), full(tri)],
        out_specs=qblk(ATTN_WIDTH),
        out_shape=jax.ShapeDtypeStruct((BATCH, SEQ, ATTN_WIDTH), BF16),
        scratch_shapes=[pltpu.VMEM((N_KV_HEADS, SEQ, QCOLS), F32),
                        pltpu.VMEM((N_KV_HEADS, WIN_CHUNKS * Q_BLOCK, QCOLS), F32),
                        pltpu.VMEM((N_KV_HEADS, LANES, QCOLS), F32),
                        pltpu.VMEM((ATTN_WIDTH, Q_BLOCK), F32)],
        compiler_params=pltpu.CompilerParams(
            dimension_semantics=("arbitrary", "arbitrary"), vmem_limit_bytes=VMEM_LIMIT),
        name="nsa_attention",
    )(q3, gatet, kvc, kvc, ks3, vst, kw3, vwt, ovlt, tri)


OUT_TM = 512


def _out_proj_kernel(x_ref, conv_ref, attn_ref, w_ref, g_ref, x1_ref, h_ref):
    x1 = (x_ref[...] + _dot(conv_ref[...], w_ref[0:CONV_WIDTH, :])
          + _dot(attn_ref[...], w_ref[CONV_WIDTH:CONV_WIDTH + ATTN_WIDTH, :]))
    x1_ref[...] = x1
    ms = jnp.mean(x1 * x1, axis=-1, keepdims=True)
    h_ref[...] = (x1 * lax.rsqrt(ms + EPS) * g_ref[...]).astype(BF16)


def _out_proj_router_kernel(x_ref, conv_ref, attn_ref, w_ref, g_ref, r_ref, x1_ref, h_ref, comb_ref):
    x1 = (x_ref[...] + _dot(conv_ref[...], w_ref[0:CONV_WIDTH, :])
          + _dot(attn_ref[...], w_ref[CONV_WIDTH:CONV_WIDTH + ATTN_WIDTH, :]))
    x1_ref[...] = x1
    ms = jnp.mean(x1 * x1, axis=-1, keepdims=True)
    h = x1 * lax.rsqrt(ms + EPS) * g_ref[...]
    h_ref[...] = h.astype(BF16)
    h_hi = h.astype(BF16)
    h_lo = (h - h_hi.astype(F32)).astype(BF16)
    logits = _dot(h_hi, r_ref[0]) + _dot(h_lo, r_ref[0]) + _dot(h_hi, r_ref[1])
    lane = lax.broadcasted_iota(jnp.int32, (1, LANES), 1).astype(F32)
    logits = jnp.where(lane < N_EXPERTS, logits, NEG_INF)
    v1 = jnp.max(logits, axis=-1, keepdims=True)
    i1 = jnp.min(jnp.where(logits == v1, lane, float(LANES)), axis=-1, keepdims=True)
    hit1 = lane == i1
    rest = jnp.where(hit1, -3e38, logits)
    v2 = jnp.max(rest, axis=-1, keepdims=True)
    i2 = jnp.min(jnp.where(rest == v2, lane, float(LANES)), axis=-1, keepdims=True)
    hit2 = lane == i2
    e2 = jnp.exp(v2 - v1)
    w_top = 1.0 / (1.0 + e2)
    comb_ref[...] = jnp.where(hit1, w_top, 0.0) + jnp.where(hit2, e2 * w_top, 0.0)


def _out_proj(x2, conv2, attn2, w, g, router=None):
    tm = OUT_TM
    row = lambda width: pl.BlockSpec((tm, width), lambda i: (i, 0))
    full = lambda a: pl.BlockSpec(a.shape, lambda i: (0,) * a.ndim)
    in_specs = [row(D_MODEL), row(CONV_WIDTH), row(ATTN_WIDTH), full(w), full(g)]
    out_shape = [jax.ShapeDtypeStruct((TOKENS, D_MODEL), F32), jax.ShapeDtypeStruct((TOKENS, D_MODEL), BF16)]
    out_specs = [row(D_MODEL), row(D_MODEL)]
    args = [x2, conv2, attn2, w, g]
    body = _out_proj_kernel
    if router is not None:
        body = _out_proj_router_kernel
        in_specs.append(full(router))
        args.append(router)
        out_shape.append(jax.ShapeDtypeStruct((TOKENS, LANES), F32))
        out_specs.append(row(LANES))
    return pl.pallas_call(
        body,
        grid=(TOKENS // tm,),
        in_specs=in_specs,
        out_specs=tuple(out_specs),
        out_shape=tuple(out_shape),
        compiler_params=pltpu.CompilerParams(
            dimension_semantics=("arbitrary",), vmem_limit_bytes=VMEM_LIMIT),
        name="out_proj",
    )(*args)


FFN_TM = 512
FFN_TF = 1408


def _ffn_kernel(x1_ref, h_ref, w1_ref, w3_ref, w2_ref, o_ref, acc_ref):
    f = pl.program_id(1)
    h = h_ref[...]
    a = _dot(h, w1_ref[...])
    act = (a * _sigmoid(a) * _dot(h, w3_ref[...])).astype(BF16)
    part = _dot(act, w2_ref[...])

    @pl.when(f == 0)
    def _():
        acc_ref[...] = x1_ref[...] + part

    @pl.when(f > 0)
    def _():
        acc_ref[...] += part

    @pl.when(f == pl.num_programs(1) - 1)
    def _():
        o_ref[...] = acc_ref[...]


def _dense_ffn(x1, h, w1, w3, w2):
    tm, tf = FFN_TM, FFN_TF
    return pl.pallas_call(
        _ffn_kernel,
        grid=(TOKENS // tm, D_FF // tf),
        in_specs=[
            pl.BlockSpec((tm, D_MODEL), lambda i, f: (i, 0)),
            pl.BlockSpec((tm, D_MODEL), lambda i, f: (i, 0)),
            pl.BlockSpec((D_MODEL, tf), lambda i, f: (0, f)),
            pl.BlockSpec((D_MODEL, tf), lambda i, f: (0, f)),
            pl.BlockSpec((tf, D_MODEL), lambda i, f: (f, 0)),
        ],
        out_specs=pl.BlockSpec((tm, D_MODEL), lambda i, f: (i, 0)),
        out_shape=jax.ShapeDtypeStruct((TOKENS, D_MODEL), F32),
        scratch_shapes=[pltpu.VMEM((tm, D_MODEL), F32)],
        compiler_params=pltpu.CompilerParams(
            dimension_semantics=("arbitrary", "arbitrary"), vmem_limit_bytes=VMEM_LIMIT),
        name="dense_ffn",
    )(x1, h, w1, w3, w2)


MOE_SB = 512
MOE_TM = 512
MOE_TF = 896
MOE_NB = TOKENS // MOE_SB
MOE_TILES = (2 * TOKENS) // MOE_TM + N_EXPERTS
MOE_PAIRS = MOE_TILES + N_EXPERTS * MOE_NB


def _moe_rank_kernel(comb_ref, tri_ref, rank_ref, cend_ref, carry_ref):
    @pl.when(pl.program_id(0) == 0)
    def _():
        carry_ref[...] = jnp.zeros((1, LANES), F32)

    routed = jnp.where(comb_ref[...] > 0.0, 1.0, 0.0).astype(BF16)
    carry = carry_ref[...]
    rank_ref[...] = carry + _dot(tri_ref[...], routed)
    carry = carry + jnp.sum(routed.astype(F32), axis=0, keepdims=True)
    carry_ref[...] = carry
    cend_ref[0] = jnp.broadcast_to(carry, (SUBLANES, LANES))


def _moe_rank(comb):
    idx = np.arange(MOE_SB)
    tri = jnp.asarray(idx[:, None] > idx[None, :], BF16)
    return pl.pallas_call(
        _moe_rank_kernel,
        grid=(MOE_NB,),
        in_specs=[pl.BlockSpec((MOE_SB, LANES), lambda b: (b, 0)),
                  pl.BlockSpec((MOE_SB, MOE_SB), lambda b: (0, 0))],
        out_specs=(pl.BlockSpec((MOE_SB, LANES), lambda b: (b, 0)),
                   pl.BlockSpec((1, SUBLANES, LANES), lambda b: (b, 0, 0))),
        out_shape=(jax.ShapeDtypeStruct((TOKENS, LANES), F32),
                   jax.ShapeDtypeStruct((MOE_NB, SUBLANES, LANES), F32)),
        scratch_shapes=[pltpu.VMEM((1, LANES), F32)],
        compiler_params=pltpu.CompilerParams(dimension_semantics=("arbitrary",)),
        name="moe_rank",
    )(comb, tri)


def _count_le(ends, v):
    return jnp.sum(ends[None, :] <= v[:, None], axis=1).astype(jnp.int32)


def _moe_schedule(cend):
    i32 = jnp.int32
    counts = cend[-1]
    cstart = jnp.concatenate([jnp.zeros((1, N_EXPERTS), i32), cend[:-1]], axis=0)
    tiles_e = (counts + MOE_TM - 1) // MOE_TM
    tile_end = jnp.cumsum(tiles_e)
    tile_start = tile_end - tiles_e
    n_tiles = tile_end[-1]
    group_start = tile_start * MOE_TM

    d = jnp.arange(MOE_TILES, dtype=i32)
    t_valid = d < n_tiles
    t_exp = jnp.minimum(_count_le(tile_end, d), N_EXPERTS - 1)
    t_exp = jnp.where(t_valid, t_exp, t_exp[jnp.maximum(n_tiles - 1, 0)])

    r0 = (d - tile_start[t_exp]) * MOE_TM
    r1 = jnp.minimum(r0 + MOE_TM, counts[t_exp])
    b_lo = jnp.sum(cend[:, t_exp] <= r0[None, :], axis=0).astype(i32)
    b_hi = jnp.sum(cstart[:, t_exp] < r1[None, :], axis=0).astype(i32) - 1
    n_pairs = jnp.where(t_valid, b_hi - b_lo + 1, 1)
    p_end = jnp.cumsum(n_pairs)
    p_start = p_end - n_pairs
    p = jnp.arange(MOE_PAIRS, dtype=i32)
    g_tile = jnp.minimum(_count_le(p_end, p), MOE_TILES - 1)
    g_live = p < p_end[-1]
    g_valid = g_live & t_valid[g_tile]
    g_block = jnp.where(g_valid, b_lo[g_tile] + p - p_start[g_tile], b_hi[jnp.maximum(n_tiles - 1, 0)])
    g_first = g_valid & (p == p_start[g_tile])
    g_zero = g_live & jnp.logical_not(t_valid[g_tile])
    g_flag = g_valid.astype(i32) + 2 * g_first.astype(i32) + 4 * g_zero.astype(i32)
    gather = (g_tile, g_block, t_exp[g_tile], g_flag)

    lo = group_start[None, :] + cstart
    hi = group_start[None, :] + cend
    t_lo = (lo // MOE_TM).reshape(-1)
    n_t = jnp.where(hi > lo, (hi - 1) // MOE_TM - lo // MOE_TM + 1, 0).reshape(-1)
    q_end = jnp.cumsum(n_t)
    q_start = q_end - n_t
    c_valid = p < q_end[-1]
    be = jnp.minimum(_count_le(q_end, p), MOE_NB * N_EXPERTS - 1)
    last = jnp.maximum(q_end[-1] - 1, 0)
    be = jnp.where(c_valid, be, be[last])
    c_tile = jnp.where(c_valid, t_lo[be] + p - q_start[be], (t_lo[be] + n_t[be] - 1))
    c_block = be // N_EXPERTS
    c_first = c_valid & ((p == 0) | (c_block != jnp.roll(c_block, 1)))
    c_flag = c_valid.astype(i32) + 2 * c_first.astype(i32)
    combine = (c_block, c_tile, be % N_EXPERTS, c_flag)
    return group_start, t_exp, t_valid.astype(i32), gather, combine


def _moe_gather_kernel(tile_ref, block_ref, exp_ref, flag_ref, h_ref, pos_ref, xs_ref):
    p = pl.program_id(0)
    flag = flag_ref[p]

    @pl.when((flag & 1) == 1)
    def _():
        rows = (tile_ref[p] * MOE_TM + lax.broadcasted_iota(jnp.int32, (MOE_TM, 1), 0)).astype(F32)
        pos = pos_ref[pl.ds(exp_ref[p], 1), :]
        onehot = jnp.where(rows == pos, 1.0, 0.0).astype(BF16)
        picked = _dot(onehot, h_ref[...]).astype(BF16)

        @pl.when((flag & 2) == 2)
        def _():
            xs_ref[...] = picked

        @pl.when((flag & 2) == 0)
        def _():
            xs_ref[...] += picked

    @pl.when((flag & 4) == 4)
    def _():
        xs_ref[...] = jnp.zeros((MOE_TM, D_MODEL), BF16)


def _moe_gather(sched, h, pos_t):
    spec = pltpu.PrefetchScalarGridSpec(
        num_scalar_prefetch=4,
        grid=(MOE_PAIRS,),
        in_specs=[pl.BlockSpec((MOE_SB, D_MODEL), lambda p, t, b, e, f: (b[p], 0)),
                  pl.BlockSpec((SUBLANES, MOE_SB), lambda p, t, b, e, f: (0, b[p]))],
        out_specs=pl.BlockSpec((MOE_TM, D_MODEL), lambda p, t, b, e, f: (t[p], 0)),
    )
    return pl.pallas_call(
        _moe_gather_kernel,
        grid_spec=spec,
        out_shape=jax.ShapeDtypeStruct((MOE_TILES * MOE_TM, D_MODEL), BF16),
        compiler_params=pltpu.CompilerParams(dimension_semantics=("arbitrary",)),
        name="moe_gather",
    )(*sched, h, pos_t)


def _moe_expert_kernel(exp_ref, valid_ref, xs_ref, w1_ref, w3_ref, w2_ref, y_ref, acc_ref):
    d = pl.program_id(0)
    f = pl.program_id(1)

    @pl.when(valid_ref[d] == 1)
    def _():
        x = xs_ref[...]
        a = _dot(x, w1_ref[0])
        act = (a * _sigmoid(a) * _dot(x, w3_ref[0])).astype(BF16)
        part = _dot(act, w2_ref[0])

        @pl.when(f == 0)
        def _():
            acc_ref[...] = part

        @pl.when(f > 0)
        def _():
            acc_ref[...] += part

        @pl.when(f == pl.num_programs(1) - 1)
        def _():
            y_ref[...] = acc_ref[...].astype(BF16)

    @pl.when((valid_ref[d] == 0) & (f == 0))
    def _():
        y_ref[...] = jnp.zeros((MOE_TM, D_MODEL), BF16)


def _moe_experts(t_exp, t_valid, xs, w1, w3, w2):
    tf = MOE_TF
    nf = D_FF_EXPERT // tf
    fidx = lambda d, f, v: f * v[d] + (nf - 1) * (1 - v[d])
    spec = pltpu.PrefetchScalarGridSpec(
        num_scalar_prefetch=2,
        grid=(MOE_TILES, nf),
        in_specs=[pl.BlockSpec((MOE_TM, D_MODEL), lambda d, f, e, v: (d, 0)),
                  pl.BlockSpec((1, D_MODEL, tf), lambda d, f, e, v: (e[d], 0, fidx(d, f, v))),
                  pl.BlockSpec((1, D_MODEL, tf), lambda d, f, e, v: (e[d], 0, fidx(d, f, v))),
                  pl.BlockSpec((1, tf, D_MODEL), lambda d, f, e, v: (e[d], fidx(d, f, v), 0))],
        out_specs=pl.BlockSpec((MOE_TM, D_MODEL), lambda d, f, e, v: (d, 0)),
        scratch_shapes=[pltpu.VMEM((MOE_TM, D_MODEL), F32)],
    )
    return pl.pallas_call(
        _moe_expert_kernel,
        grid_spec=spec,
        out_shape=jax.ShapeDtypeStruct((MOE_TILES * MOE_TM, D_MODEL), BF16),
        compiler_params=pltpu.CompilerParams(
            dimension_semantics=("arbitrary", "arbitrary"), vmem_limit_bytes=VMEM_LIMIT),
        name="moe_experts",
    )(t_exp, t_valid, xs, w1, w3, w2)


def _moe_combine_kernel(block_ref, tile_ref, exp_ref, flag_ref, y_ref, pos_ref, comb_ref, x1_ref, o_ref):
    p = pl.program_id(0)
    flag = flag_ref[p]

    @pl.when((flag & 1) == 1)
    def _():
        pick = lax.broadcasted_iota(jnp.int32, (1, LANES), 1) == exp_ref[p]
        pos = jnp.sum(jnp.where(pick, pos_ref[...], 0.0), axis=-1, keepdims=True)
        wgt = jnp.sum(jnp.where(pick, comb_ref[...], 0.0), axis=-1, keepdims=True)
        cols = (tile_ref[p] * MOE_TM + lax.broadcasted_iota(jnp.int32, (1, MOE_TM), 1)).astype(F32)
        onehot = jnp.where(pos == cols, 1.0, 0.0).astype(BF16)
        part = wgt * _dot(onehot, y_ref[...])

        @pl.when((flag & 2) == 2)
        def _():
            o_ref[...] = x1_ref[...] + part

        @pl.when((flag & 2) == 0)
        def _():
            o_ref[...] += part


def _moe_combine(sched, y, pos, comb, x1):
    tok = lambda width: pl.BlockSpec((MOE_SB, width), lambda p, b, t, e, f: (b[p], 0))
    spec = pltpu.PrefetchScalarGridSpec(
        num_scalar_prefetch=4,
        grid=(MOE_PAIRS,),
        in_specs=[pl.BlockSpec((MOE_TM, D_MODEL), lambda p, b, t, e, f: (t[p], 0)),
                  tok(LANES), tok(LANES), tok(D_MODEL)],
        out_specs=tok(D_MODEL),
    )
    return pl.pallas_call(
        _moe_combine_kernel,
        grid_spec=spec,
        out_shape=jax.ShapeDtypeStruct((TOKENS, D_MODEL), F32),
        compiler_params=pltpu.CompilerParams(
            dimension_semantics=("arbitrary",), vmem_limit_bytes=VMEM_LIMIT),
        name="moe_combine",
    )(*sched, y, pos, comb, x1)


def _moe_ffn(x1, h, comb, w1, w3, w2):
    rank, cend = _moe_rank(comb)
    cend = jnp.round(cend[:, 0, :N_EXPERTS]).astype(jnp.int32)
    group_start, t_exp, t_valid, gather, combine = _moe_schedule(cend)
    start = jnp.zeros((LANES,), F32).at[:N_EXPERTS].set(group_start.astype(F32))
    pos = jnp.where(comb > 0.0, start[None, :] + rank, -1.0)
    xs = _moe_gather(gather, h, pos[:, :SUBLANES].T)
    y = _moe_experts(t_exp, t_valid, xs, w1, w3, w2)
    return _moe_combine(combine, y, pos, comb, x1)


def _group_mean_matrix():
    idx = np.arange(LANES)
    return jnp.asarray((idx[:, None] // CONV_GROUP == idx[None, :] // CONV_GROUP) / CONV_GROUP, BF16)


def _overlap_matrix_t():
    c_start = np.arange(CMP_ROWS) * CMP_STRIDE
    j = np.arange(N_SEL)
    ovl = ((c_start[None, :] < (j[:, None] + 1) * SEL_BLOCK)
           & (c_start[None, :] + CMP_BLOCK > j[:, None] * SEL_BLOCK)
           & (np.arange(CMP_ROWS)[None, :] < N_CMP))
    return jnp.asarray(ovl, BF16)


def _triangle_biases():
    k = np.arange(Q_BLOCK)[:, None]
    t = (np.arange(QCOLS) % Q_BLOCK)[None, :]
    return jnp.asarray(np.stack([np.where(k <= t, 0.0, MASK_BIAS), np.where(k > t, 0.0, MASK_BIAS)]), F32)


def _dup(v):
    return jnp.concatenate([v, v], axis=-1)


def _zero_pad(v):
    return jnp.concatenate([v, jnp.zeros_like(v)], axis=-1)


def kernel(x, attn_norm_g, w_in, conv_dw_w, conv_dw_b, conv_ln_g, conv_ln_b, q_norm_g, k_norm_g,
           cmp_pos_k, cmp_pos_v, cmp_k_w1, cmp_k_b1, cmp_k_w2, cmp_k_b2,
           cmp_v_w1, cmp_v_b1, cmp_v_w2, cmp_v_b2, w_out, ffn_norm_g,
           dense_w1, dense_w3, dense_w2, router_w, moe_w1, moe_w3, moe_w2):
    assert x.shape == (BATCH, SEQ, D_MODEL) and x.dtype == F32
    bd = _group_mean_matrix()
    ovlt = _overlap_matrix_t()
    tri = _triangle_biases()
    col_src = jnp.asarray(_IN_COLS_SRC)
    row_src = jnp.asarray(np.maximum(_IN_T_ROWS_SRC, 0))
    row_live = jnp.asarray(_IN_T_ROWS_SRC >= 0)

    x2 = x.reshape(TOKENS, D_MODEL)
    for layer in range(DEPTH):
        w_cols = jnp.take(w_in[layer], col_src, axis=1).astype(BF16)
        w_rows = jnp.where(row_live[:, None], jnp.take(w_in[layer], row_src, axis=1).T, 0.0).astype(BF16)
        qg = _dup(q_norm_g[layer])[None, :] * (LOG2_E * HEAD_DIM ** -0.5)
        kgs = _dup(k_norm_g[layer, 1])[None, :]
        kgw = _dup(k_norm_g[layer, 2])[None, :]
        u, q, kcvc, ks, kw, vst, vwt, gatet = _in_proj(
            x2, attn_norm_g[layer][None, :], w_cols, w_rows, qg, kgs, kgw, bd)

        conv = _conv_module(u.reshape(BATCH, SEQ, CONV_WIDTH), conv_dw_w[layer], conv_dw_b[layer][None, :],
                            conv_ln_g[layer][None, :], conv_ln_b[layer][None, :], bd)

        xr = kcvc.reshape(BATCH, SEQ, 2, N_KV_HEADS, HEAD_DIM).transpose(2, 0, 3, 1, 4)
        xr = xr.reshape(2, BATCH, N_KV_HEADS, CMP_ROWS, CMP_CHUNK)
        pos = jnp.stack([cmp_pos_k[layer], cmp_pos_v[layer]]).reshape(2, 1, CMP_BLOCK * HEAD_DIM)
        w1 = jnp.stack([cmp_k_w1[layer], cmp_v_w1[layer]]).astype(BF16)
        b1 = jnp.stack([cmp_k_b1[layer], cmp_v_b1[layer]])[:, None, :]
        w2 = _zero_pad(jnp.stack([cmp_k_w2[layer], cmp_v_w2[layer]])).astype(BF16)
        b2 = _zero_pad(jnp.stack([cmp_k_b2[layer], cmp_v_b2[layer]]))[:, None, :]
        kvc = _compress(xr, pos, w1, b1, w2, b2, _zero_pad(k_norm_g[layer, 0])[None, :])

        seq3 = lambda a: a.reshape(BATCH, SEQ, a.shape[-1])
        attn = _attention(seq3(q), gatet, kvc, seq3(ks), vst, seq3(kw), vwt, ovlt, tri)

        w_o = w_out[layer].astype(BF16)
        conv2 = conv.reshape(TOKENS, CONV_WIDTH)
        attn2 = attn.reshape(TOKENS, ATTN_WIDTH)
        g_ffn = ffn_norm_g[layer][None, :]
        i = layer // 2
        if layer % 2 == 0:
            x1, h = _out_proj(x2, conv2, attn2, w_o, g_ffn)
            x2 = _dense_ffn(x1, h, dense_w1[i].astype(BF16), dense_w3[i].astype(BF16),
                            dense_w2[i].astype(BF16))
        else:
            r = jnp.pad(router_w[i], ((0, 0), (0, LANES - N_EXPERTS)))
            r_hi = r.astype(BF16)
            r_lo = (r - r_hi.astype(F32)).astype(BF16)
            x1, h, comb = _out_proj(x2, conv2, attn2, w_o, g_ffn, jnp.stack([r_hi, r_lo]))
            x2 = _moe_ffn(x1, h, comb, moe_w1[i].astype(BF16), moe_w3[i].astype(BF16),
                          moe_w2[i].astype(BF16))
    return x2.reshape(BATCH, SEQ, D_MODEL)
```

```python
import functools

import numpy as np
import jax
import jax.numpy as jnp
from jax import lax
from jax.experimental import pallas as pl
from jax.experimental.pallas import tpu as pltpu

D_MODEL = 1024
BATCH = 8
SEQ = 2048
DEPTH = 2
TOKENS = BATCH * SEQ

CONV_WIDTH = 512
CONV_GROUP = 64
CONV_KERNEL = 31
N_HEADS = 8
HEAD_DIM = 64
N_KV_HEADS = 2
GQA = N_HEADS // N_KV_HEADS
ATTN_WIDTH = N_HEADS * HEAD_DIM
KV_WIDTH = N_KV_HEADS * HEAD_DIM
N_BRANCH = 3
CMP_BLOCK = 32
CMP_STRIDE = 16
CMP_HIDDEN = 256
N_CMP = (SEQ - CMP_BLOCK) // CMP_STRIDE + 1
SEL_BLOCK = 64
SEL_TOPK = 8
N_SEL = SEQ // SEL_BLOCK
WINDOW = 512
Q_BLOCK = 256
FORCE_BONUS = 1e4
NEG_INF = -1e30
LOG2_E = 1.4426950408889634
D_FF = 2816
N_EXPERTS = 8
D_FF_EXPERT = 3584
EPS = 1e-6

LANES = 128
SUBLANES = 8
VMEM_LIMIT = 48 * 1024 * 1024

F32 = jnp.float32
BF16 = jnp.bfloat16

_OFF_VAL = 0
_OFF_GATE = 512
_OFF_Q = 1024
_OFF_KCVC = 1536
_OFF_KS = 1792
_OFF_KW = 1920
IN_COLS = 2048
T_ROWS = 5 * LANES

_BASE_KV = 2 * CONV_WIDTH + ATTN_WIDTH


def _in_proj_columns():
    cols = list(range(0, _BASE_KV + 2 * KV_WIDTH))
    for piece in (2, 4):
        start = _BASE_KV + piece * KV_WIDTH
        cols += list(range(start, start + KV_WIDTH))
    assert len(cols) == IN_COLS
    return np.asarray(cols, np.int32)


def _in_proj_t_rows():
    rows = []
    for piece in (3, 5):
        start = _BASE_KV + piece * KV_WIDTH
        for head in range(N_KV_HEADS):
            rows += list(range(start + head * HEAD_DIM, start + (head + 1) * HEAD_DIM)) + [-1] * HEAD_DIM
    glog = _BASE_KV + 6 * KV_WIDTH
    rows += list(range(glog, glog + N_HEADS * N_BRANCH)) + [-1] * (LANES - N_HEADS * N_BRANCH)
    assert len(rows) == T_ROWS
    return np.asarray(rows, np.int32)


_IN_COLS_SRC = _in_proj_columns()
_IN_T_ROWS_SRC = _in_proj_t_rows()


def _sigmoid(v):
    return 1.0 / (1.0 + jnp.exp(-v))


def _dot(a, b):
    return jnp.dot(a, b, preferred_element_type=F32)


def _dot_nt(a, b):
    return lax.dot_general(a, b, (((1,), (1,)), ((), ())), preferred_element_type=F32)


def _split_dot(v, m):
    hi = v.astype(BF16)
    lo = (v - hi.astype(F32)).astype(BF16)
    return _dot(hi, m) + _dot(lo, m)


IN_TM = 512


def _in_proj_kernel(x_ref, g_ref, w_ref, wt_ref, qg_ref, kgs_ref, kgw_ref, bd_ref,
                    u_ref, q_ref, kcvc_ref, ks_ref, kw_ref, vst_ref, vwt_ref, gatet_ref):
    x = x_ref[...]
    ms = jnp.mean(x * x, axis=-1, keepdims=True)
    h = (x * lax.rsqrt(ms + EPS) * g_ref[...]).astype(BF16)

    def proj(lo, width):
        return _dot(h, w_ref[:, lo:lo + width])

    u_ref[...] = proj(_OFF_VAL, CONV_WIDTH) * _sigmoid(proj(_OFF_GATE, CONV_WIDTH))

    bd = bd_ref[...]
    lane = lax.broadcasted_iota(jnp.int32, (IN_TM, LANES), 1)
    lower = lane < HEAD_DIM

    def head_slots(pair, upper, out_ref, slot):
        out_ref[:, slot * LANES:(slot + 1) * LANES] = jnp.where(lower, pair, upper).astype(BF16)
        swapped = pltpu.roll(pair, HEAD_DIM, axis=1)
        out_ref[:, (slot + 1) * LANES:(slot + 2) * LANES] = jnp.where(lower, swapped, upper).astype(BF16)

    q = proj(_OFF_Q, ATTN_WIDTH)
    for c in range(ATTN_WIDTH // LANES):
        qc = q[:, c * LANES:(c + 1) * LANES]
        msq = _split_dot(qc * qc, bd)
        head_slots(qc * lax.rsqrt(msq + EPS) * qg_ref[...], 0.0, q_ref, 2 * c)

    kcvc_ref[...] = proj(_OFF_KCVC, 2 * KV_WIDTH)

    def normed_k(off, kg_ref):
        k = proj(off, KV_WIDTH)
        return k * lax.rsqrt(_split_dot(k * k, bd) + EPS) * kg_ref[...]

    tok = (pl.program_id(0) % (SEQ // IN_TM)) * IN_TM + lax.broadcasted_iota(jnp.int32, (IN_TM, 1), 0)
    block_hot = jnp.where(lane - HEAD_DIM == tok // SEL_BLOCK, 1.0, 0.0)
    head_slots(normed_k(_OFF_KS, kgs_ref), block_hot, ks_ref, 0)
    head_slots(normed_k(_OFF_KW, kgw_ref), 0.0, kw_ref, 0)

    zt = _dot_nt(wt_ref[...], h)
    ones_rows = lax.broadcasted_iota(jnp.int32, (LANES, IN_TM), 0) >= HEAD_DIM
    for j, out_ref in enumerate((vst_ref, vst_ref, vwt_ref, vwt_ref)):
        blk = zt[j * LANES:(j + 1) * LANES]
        out_ref[0, j % N_KV_HEADS] = jnp.where(ones_rows, 1.0, blk).astype(BF16)
    gatet_ref[0] = _sigmoid(zt[4 * LANES:5 * LANES])


def _in_proj(x2, g, w, wt, qg, kgs, kgw, bd):
    tm = IN_TM
    per_seq = SEQ // tm
    row = lambda width: pl.BlockSpec((tm, width), lambda i: (i, 0))
    full = lambda a: pl.BlockSpec(a.shape, lambda i: (0,) * a.ndim)
    vt_spec = pl.BlockSpec((1, N_KV_HEADS, LANES, tm), lambda i: (i // per_seq, 0, 0, i % per_seq))
    vt_shape = jax.ShapeDtypeStruct((BATCH, N_KV_HEADS, LANES, SEQ), BF16)
    out_shape = (
        jax.ShapeDtypeStruct((TOKENS, CONV_WIDTH), F32),
        jax.ShapeDtypeStruct((TOKENS, N_HEADS * LANES), BF16),
        jax.ShapeDtypeStruct((TOKENS, 2 * KV_WIDTH), F32),
        jax.ShapeDtypeStruct((TOKENS, 2 * LANES), BF16),
        jax.ShapeDtypeStruct((TOKENS, 2 * LANES), BF16),
        vt_shape,
        vt_shape,
        jax.ShapeDtypeStruct((BATCH, LANES, SEQ), F32),
    )
    out_specs = (row(CONV_WIDTH), row(N_HEADS * LANES), row(2 * KV_WIDTH), row(2 * LANES), row(2 * LANES),
                 vt_spec, vt_spec,
                 pl.BlockSpec((1, LANES, tm), lambda i: (i // per_seq, 0, i % per_seq)))
    return pl.pallas_call(
        _in_proj_kernel,
        grid=(TOKENS // tm,),
        in_specs=[row(D_MODEL), full(g), full(w), full(wt), full(qg), full(kgs), full(kgw), full(bd)],
        out_specs=out_specs,
        out_shape=out_shape,
        compiler_params=pltpu.CompilerParams(
            dimension_semantics=("arbitrary",), vmem_limit_bytes=VMEM_LIMIT),
        name="in_proj",
    )(x2, g, w, wt, qg, kgs, kgw, bd)


CONV_TR = 512
CONV_HALO = 32
CONV_RS = 64


def _conv_kernel(ucur_ref, uprev_ref, w_ref, b_ref, lg_ref, lb_ref, bd_ref, o_ref, win_ref, y_ref):
    i = pl.program_id(1)
    win_ref[0, 0:CONV_HALO, :] = jnp.where(i > 0, uprev_ref[0], 0.0)
    win_ref[0, CONV_HALO:CONV_HALO + CONV_TR, :] = ucur_ref[0]
    first = CONV_HALO - (CONV_KERNEL - 1)
    shifted_rows = CONV_HALO + CONV_TR - SUBLANES
    for s in range(1, SUBLANES):
        win_ref[s, 0:shifted_rows, :] = win_ref[0, s:s + shifted_rows, :]

    for c in range(CONV_WIDTH // LANES):
        lanes = slice(c * LANES, (c + 1) * LANES)

        def body(r, carry):
            base = pl.multiple_of(r * CONV_RS, CONV_RS)
            acc = jnp.zeros((CONV_RS, LANES), F32)
            for k in range(CONV_KERNEL):
                off = first + k
                rows = pl.ds(base + (off // SUBLANES) * SUBLANES, CONV_RS)
                acc = acc + win_ref[off % SUBLANES, rows, lanes] * w_ref[k:k + 1, lanes]
            y_ref[pl.ds(base, CONV_RS), lanes] = acc + b_ref[:, lanes]
            return carry

        lax.fori_loop(0, CONV_TR // CONV_RS, body, 0)

    bd = bd_ref[...]
    for c in range(CONV_WIDTH // LANES):
        lanes = slice(c * LANES, (c + 1) * LANES)
        y = y_ref[:, lanes]
        d = y - _split_dot(y, bd)
        var = _split_dot(d * d, bd)
        yn = d * lax.rsqrt(var + EPS) * lg_ref[:, lanes] + lb_ref[:, lanes]
        o_ref[0, :, lanes] = (yn * _sigmoid(yn)).astype(BF16)


def _conv_module(u3, w, b, lg, lb, bd):
    halo_blocks = CONV_TR // CONV_HALO
    full = lambda a: pl.BlockSpec(a.shape, lambda bi, i: (0,) * a.ndim)
    return pl.pallas_call(
        _conv_kernel,
        grid=(BATCH, SEQ // CONV_TR),
        in_specs=[
            pl.BlockSpec((1, CONV_TR, CONV_WIDTH), lambda bi, i: (bi, i, 0)),
            pl.BlockSpec((1, CONV_HALO, CONV_WIDTH),
                         lambda bi, i: (bi, jnp.maximum(i * halo_blocks - 1, 0), 0)),
            full(w), full(b), full(lg), full(lb), full(bd),
        ],
        out_specs=pl.BlockSpec((1, CONV_TR, CONV_WIDTH), lambda bi, i: (bi, i, 0)),
        out_shape=jax.ShapeDtypeStruct((BATCH, SEQ, CONV_WIDTH), BF16),
        scratch_shapes=[pltpu.VMEM((SUBLANES, CONV_HALO + CONV_TR, CONV_WIDTH), F32),
                        pltpu.VMEM((CONV_TR, CONV_WIDTH), F32)],
        compiler_params=pltpu.CompilerParams(dimension_semantics=("arbitrary", "arbitrary")),
        name="conv_module",
    )(u3, u3, w, b, lg, lb, bd)


CMP_ROWS = SEQ // CMP_STRIDE
CMP_CHUNK = CMP_STRIDE * HEAD_DIM


def _compress_kernel(x_ref, pos_ref, w1_ref, b1_ref, w2_ref, b2_ref, w2t_ref, b2t_ref, kg_ref, o_ref):
    kind = pl.program_id(0)
    xa = x_ref[0, 0, 0]
    xb = pltpu.roll(xa, CMP_ROWS - 1, axis=0)
    a = (xa + pos_ref[0, :, 0:CMP_CHUNK]).astype(BF16)
    b = (xb + pos_ref[0, :, CMP_CHUNK:2 * CMP_CHUNK]).astype(BF16)
    hid = _dot(a, w1_ref[0, 0:CMP_CHUNK, :]) + _dot(b, w1_ref[0, CMP_CHUNK:2 * CMP_CHUNK, :]) + b1_ref[0]
    hid = (hid * _sigmoid(hid)).astype(BF16)
    out = _dot(hid, w2_ref[0]) + b2_ref[0]
    ms = jnp.sum(out * out, axis=-1, keepdims=True) * (1.0 / HEAD_DIM)
    normed = out * lax.rsqrt(ms + EPS) * kg_ref[...]
    out_t = _dot_nt(w2t_ref[0], hid) + b2t_ref[0]
    o_ref[0, 0, 0] = jnp.where(kind == 0, normed, out_t).astype(BF16)


def _compress(xr, pos, w1, b1, w2, b2, kg):
    w2t = jnp.swapaxes(w2, 1, 2)
    b2t = jnp.swapaxes(b2, 1, 2)
    per_kind = lambda a: pl.BlockSpec((1,) + a.shape[1:], lambda k, bi, h: (k,) + (0,) * (a.ndim - 1))
    return pl.pallas_call(
        _compress_kernel,
        grid=(2, BATCH, N_KV_HEADS),
        in_specs=[
            pl.BlockSpec((1, 1, 1, CMP_ROWS, CMP_CHUNK), lambda k, bi, h: (k, bi, h, 0, 0)),
            per_kind(pos), per_kind(w1), per_kind(b1), per_kind(w2), per_kind(b2),
            per_kind(w2t), per_kind(b2t),
            pl.BlockSpec(kg.shape, lambda k, bi, h: (0, 0)),
        ],
        out_specs=pl.BlockSpec((1, 1, 1, CMP_ROWS, LANES), lambda k, bi, h: (k, bi, h, 0, 0)),
        out_shape=jax.ShapeDtypeStruct((2, BATCH, N_KV_HEADS, CMP_ROWS, LANES), BF16),
        compiler_params=pltpu.CompilerParams(
            dimension_semantics=("arbitrary", "arbitrary", "arbitrary")),
        name="compress",
    )(xr, pos, w1, b1, w2, b2, w2t, b2t, kg)


QCOLS = GQA * Q_BLOCK
SEL_CHUNK = 256
WIN_CHUNK = 128
WIN_CHUNKS = (WINDOW + Q_BLOCK) // WIN_CHUNK
MASK_BIAS = -1e30


def _window_biases():
    k = np.arange(WIN_CHUNK)[:, None]
    t = (np.arange(QCOLS) % Q_BLOCK)[None, :]
    tables, index = [], {}
    for i in range(WIN_CHUNKS):
        key = i * WIN_CHUNK - WINDOW + k
        bias = np.where((key <= t) & (key > t - WINDOW), 0.0, MASK_BIAS)
        if bias.any():
            index[i] = len(tables)
            tables.append(bias)
    return np.stack(tables).astype(np.float32), index


_WIN_BIAS, _WIN_BIAS_INDEX = _window_biases()


def _attn_kernel(q_ref, gatet_ref, kc_ref, vct_ref, ks_ref, vst_ref, kw_ref, vwt_ref,
                 ovlt_ref, wbias_ref, o_ref, ssel_sc, swin_sc, acc_sc, out_sc):
    n = pl.program_id(1)
    start = n * Q_BLOCK
    col = lax.broadcasted_iota(jnp.int32, (1, QCOLS), 1)
    t_cols = start + (col & (Q_BLOCK - 1))
    tq = start + lax.broadcasted_iota(jnp.int32, (1, Q_BLOCK), 1)
    blk_row = lax.broadcasted_iota(jnp.int32, (N_SEL, 1), 0)
    blk_row_f = blk_row.astype(F32)
    cmp_end = lax.broadcasted_iota(jnp.int32, (CMP_ROWS, 1), 0) * CMP_STRIDE + (CMP_BLOCK - 1)

    kv_heads = range(N_KV_HEADS)
    kv_lanes = [slice(kvh * LANES, (kvh + 1) * LANES) for kvh in kv_heads]

    def gate_row(kvh, g, branch):
        r = (kvh * GQA + g) * N_BRANCH + branch
        return gatet_ref[0, r:r + 1, :]

    def head_rows(kvh, g):
        h = kvh * GQA + g
        return slice(h * HEAD_DIM, (h + 1) * HEAD_DIM)

    qs = []
    for kvh in kv_heads:
        h0 = kvh * GQA
        qs.append(jnp.concatenate(
            [q_ref[0, :, (h0 + g) * LANES:(h0 + g + 1) * LANES] for g in range(GQA)], axis=0))

    o_cmp, imp = [], []
    mask_c = cmp_end <= t_cols
    for kvh in kv_heads:
        s_c = jnp.where(mask_c, _dot_nt(kc_ref[0, kvh], qs[kvh]), NEG_INF)
        m_c = jnp.max(s_c, axis=0, keepdims=True)
        p_c = jnp.where(mask_c, jnp.exp2(s_c - m_c), 0.0)
        p_c = p_c / jnp.maximum(jnp.sum(p_c, axis=0, keepdims=True), 1e-30)
        o_cmp.append(_dot(vct_ref[0, kvh], p_c.astype(BF16)))
        p_sum = p_c[:, 0:Q_BLOCK]
        for g in range(1, GQA):
            p_sum = p_sum + p_c[:, g * Q_BLOCK:(g + 1) * Q_BLOCK]
        p_hi = p_sum.astype(BF16)
        p_lo = (p_sum - p_hi.astype(F32)).astype(BF16)
        imp.append(_dot(ovlt_ref[...], p_hi) + _dot(ovlt_ref[...], p_lo))

    tq2 = jnp.concatenate([tq] * N_KV_HEADS, axis=1)
    cur = tq2 >> 6
    forced = (blk_row == 0) | (blk_row == cur) | (blk_row == cur - 1)
    causal_blk = blk_row * SEL_BLOCK <= tq2
    score = jnp.where(causal_blk, jnp.concatenate(imp, axis=1) + jnp.where(forced, FORCE_BONUS, 0.0), NEG_INF)
    sel = jnp.zeros((N_SEL, N_KV_HEADS * Q_BLOCK), F32)
    for _ in range(SEL_TOPK):
        best = jnp.max(score, axis=0, keepdims=True)
        idx = jnp.min(jnp.where(score == best, blk_row_f, float(LANES)), axis=0, keepdims=True)
        hit = blk_row_f == idx
        sel = jnp.where(hit, 1.0, sel)
        score = jnp.where(hit, -3e38, score)
    sel_bias = (sel - 1.0) * (-MASK_BIAS)
    qs_sel = []
    for kvh in kv_heads:
        rows = jnp.concatenate([jnp.zeros((HEAD_DIM, Q_BLOCK), F32),
                                sel_bias[:, kvh * Q_BLOCK:(kvh + 1) * Q_BLOCK],
                                jnp.zeros((LANES - HEAD_DIM - N_SEL, Q_BLOCK), F32)], axis=0)
        bias_q = rows.T.astype(BF16)
        qs_sel.append(qs[kvh] + jnp.concatenate([bias_q] * GQA, axis=0))

    for kvh in kv_heads:
        mxw = jnp.full((1, QCOLS), NEG_INF, F32)
        k_win = []
        for i in range(WIN_CHUNKS):
            kb = n * (Q_BLOCK // WIN_CHUNK) - WINDOW // WIN_CHUNK + i
            k0 = pl.multiple_of(jnp.maximum(kb, 0) * WIN_CHUNK, WIN_CHUNK)
            k_win.append(k0)
            s = _dot_nt(kw_ref[0, pl.ds(k0, WIN_CHUNK), kv_lanes[kvh]], qs[kvh])
            if i in _WIN_BIAS_INDEX:
                s = s + wbias_ref[_WIN_BIAS_INDEX[i]]
            if i < WINDOW // WIN_CHUNK:
                s = s + jnp.where(kb >= 0, 0.0, MASK_BIAS)
            swin_sc[kvh, i * WIN_CHUNK:(i + 1) * WIN_CHUNK, :] = s
            mxw = jnp.maximum(mxw, jnp.max(s, axis=0, keepdims=True))
        accw = jnp.zeros((LANES, QCOLS), F32)
        for i in range(WIN_CHUNKS):
            p = jnp.exp2(swin_sc[kvh, i * WIN_CHUNK:(i + 1) * WIN_CHUNK, :] - mxw).astype(BF16)
            accw = accw + _dot(vwt_ref[0, kvh, :, pl.ds(k_win[i], WIN_CHUNK)], p)
        o_win = accw[0:HEAD_DIM] * (1.0 / accw[HEAD_DIM:HEAD_DIM + 1])
        for g in range(GQA):
            cols = slice(g * Q_BLOCK, (g + 1) * Q_BLOCK)
            out_sc[head_rows(kvh, g), :] = (gate_row(kvh, g, 0) * o_cmp[kvh][0:HEAD_DIM, cols]
                                            + gate_row(kvh, g, 2) * o_win[:, cols])

    def sel_scores(kvh, k0):
        return _dot_nt(ks_ref[0, pl.ds(k0, SEL_CHUNK), kv_lanes[kvh]], qs_sel[kvh])

    def sel_pass1(i, mx):
        k0 = pl.multiple_of(i * SEL_CHUNK, SEL_CHUNK)
        out = []
        for kvh in kv_heads:
            s = sel_scores(kvh, k0)
            ssel_sc[kvh, pl.ds(k0, SEL_CHUNK), :] = s
            out.append(jnp.maximum(mx[kvh], jnp.max(s, axis=0, keepdims=True)))
        return tuple(out)

    last = (start + Q_BLOCK - 1) // SEL_CHUNK
    mx = lax.fori_loop(0, last, sel_pass1, (jnp.full((1, QCOLS), NEG_INF, F32),) * N_KV_HEADS)
    k_last = pl.multiple_of(last * SEL_CHUNK, SEL_CHUNK)
    causal = k_last + lax.broadcasted_iota(jnp.int32, (SEL_CHUNK, 1), 0) <= t_cols
    mx = list(mx)
    for kvh in kv_heads:
        s = jnp.where(causal, sel_scores(kvh, k_last), NEG_INF)
        ssel_sc[kvh, pl.ds(k_last, SEL_CHUNK), :] = s
        mx[kvh] = jnp.maximum(mx[kvh], jnp.max(s, axis=0, keepdims=True))
        acc_sc[kvh] = jnp.zeros((LANES, QCOLS), F32)

    def sel_pass2(i, carry):
        k0 = pl.multiple_of(i * SEL_CHUNK, SEL_CHUNK)
        for kvh in kv_heads:
            p = jnp.exp2(ssel_sc[kvh, pl.ds(k0, SEL_CHUNK), :] - mx[kvh]).astype(BF16)
            acc_sc[kvh] += _dot(vst_ref[0, kvh, :, pl.ds(k0, SEL_CHUNK)], p)
        return carry

    lax.fori_loop(0, last + 1, sel_pass2, 0)
    for kvh in kv_heads:
        acc = acc_sc[kvh]
        o_sel = acc[0:HEAD_DIM] * (1.0 / acc[HEAD_DIM:HEAD_DIM + 1])
        for g in range(GQA):
            out_sc[head_rows(kvh, g), :] += gate_row(kvh, g, 1) * o_sel[:, g * Q_BLOCK:(g + 1) * Q_BLOCK]

    for c in range(N_HEADS // 2):
        o_ref[0, :, c * LANES:(c + 1) * LANES] = out_sc[c * LANES:(c + 1) * LANES, :].T.astype(BF16)


def _attention(q3, gatet, kvc, ks3, vst, kw3, vwt, ovlt):
    wbias = jnp.asarray(_WIN_BIAS)
    qblk = lambda width: pl.BlockSpec((1, Q_BLOCK, width), lambda bi, n: (bi, n, 0))
    seq = pl.BlockSpec((1, SEQ, 2 * LANES), lambda bi, n: (bi, 0, 0))
    seq_t = pl.BlockSpec((1, N_KV_HEADS, LANES, SEQ), lambda bi, n: (bi, 0, 0, 0))
    cmp_spec = lambda kind: pl.BlockSpec((None, 1, N_KV_HEADS, CMP_ROWS, LANES),
                                         lambda bi, n: (kind, bi, 0, 0, 0))
    full = lambda a: pl.BlockSpec(a.shape, lambda bi, n: (0,) * a.ndim)
    return pl.pallas_call(
        _attn_kernel,
        grid=(BATCH, SEQ // Q_BLOCK),
        in_specs=[qblk(N_HEADS * LANES), pl.BlockSpec((1, LANES, Q_BLOCK), lambda bi, n: (bi, 0, n)),
                  cmp_spec(0), cmp_spec(1), seq, seq_t, seq, seq_t, full(ovlt), full(wbias)],
        out_specs=qblk(ATTN_WIDTH),
        out_shape=jax.ShapeDtypeStruct((BATCH, SEQ, ATTN_WIDTH), BF16),
        scratch_shapes=[pltpu.VMEM((N_KV_HEADS, SEQ, QCOLS), F32),
                        pltpu.VMEM((N_KV_HEADS, WIN_CHUNKS * WIN_CHUNK, QCOLS), F32),
                        pltpu.VMEM((N_KV_HEADS, LANES, QCOLS), F32),
                        pltpu.VMEM((ATTN_WIDTH, Q_BLOCK), F32)],
        compiler_params=pltpu.CompilerParams(
            dimension_semantics=("arbitrary", "arbitrary"), vmem_limit_bytes=VMEM_LIMIT),
        name="nsa_attention",
    )(q3, gatet, kvc, kvc, ks3, vst, kw3, vwt, ovlt, wbias)


OUT_TM = 512


def _out_proj_kernel(x_ref, conv_ref, attn_ref, w_ref, g_ref, x1_ref, h_ref):
    x1 = (x_ref[...] + _dot(conv_ref[...], w_ref[0:CONV_WIDTH, :])
          + _dot(attn_ref[...], w_ref[CONV_WIDTH:CONV_WIDTH + ATTN_WIDTH, :]))
    x1_ref[...] = x1
    ms = jnp.mean(x1 * x1, axis=-1, keepdims=True)
    h_ref[...] = (x1 * lax.rsqrt(ms + EPS) * g_ref[...]).astype(BF16)


def _out_proj_router_kernel(x_ref, conv_ref, attn_ref, w_ref, g_ref, r_ref, x1_ref, h_ref, comb_ref):
    x1 = (x_ref[...] + _dot(conv_ref[...], w_ref[0:CONV_WIDTH, :])
          + _dot(attn_ref[...], w_ref[CONV_WIDTH:CONV_WIDTH + ATTN_WIDTH, :]))
    x1_ref[...] = x1
    ms = jnp.mean(x1 * x1, axis=-1, keepdims=True)
    h = x1 * lax.rsqrt(ms + EPS) * g_ref[...]
    h_ref[...] = h.astype(BF16)
    h_hi = h.astype(BF16)
    h_lo = (h - h_hi.astype(F32)).astype(BF16)
    logits = _dot(h_hi, r_ref[0]) + _dot(h_lo, r_ref[0]) + _dot(h_hi, r_ref[1])
    lane = lax.broadcasted_iota(jnp.int32, (1, LANES), 1).astype(F32)
    logits = jnp.where(lane < N_EXPERTS, logits, NEG_INF)
    v1 = jnp.max(logits, axis=-1, keepdims=True)
    i1 = jnp.min(jnp.where(logits == v1, lane, float(LANES)), axis=-1, keepdims=True)
    hit1 = lane == i1
    rest = jnp.where(hit1, -3e38, logits)
    v2 = jnp.max(rest, axis=-1, keepdims=True)
    i2 = jnp.min(jnp.where(rest == v2, lane, float(LANES)), axis=-1, keepdims=True)
    hit2 = lane == i2
    e2 = jnp.exp(v2 - v1)
    w_top = 1.0 / (1.0 + e2)
    comb_ref[...] = jnp.where(hit1, w_top, 0.0) + jnp.where(hit2, e2 * w_top, 0.0)


def _out_proj(x2, conv2, attn2, w, g, router=None):
    tm = OUT_TM
    row = lambda width: pl.BlockSpec((tm, width), lambda i: (i, 0))
    full = lambda a: pl.BlockSpec(a.shape, lambda i: (0,) * a.ndim)
    in_specs = [row(D_MODEL), row(CONV_WIDTH), row(ATTN_WIDTH), full(w), full(g)]
    out_shape = [jax.ShapeDtypeStruct((TOKENS, D_MODEL), F32), jax.ShapeDtypeStruct((TOKENS, D_MODEL), BF16)]
    out_specs = [row(D_MODEL), row(D_MODEL)]
    args = [x2, conv2, attn2, w, g]
    body = _out_proj_kernel
    if router is not None:
        body = _out_proj_router_kernel
        in_specs.append(full(router))
        args.append(router)
        out_shape.append(jax.ShapeDtypeStruct((TOKENS, LANES), F32))
        out_specs.append(row(LANES))
    return pl.pallas_call(
        body,
        grid=(TOKENS // tm,),
        in_specs=in_specs,
        out_specs=tuple(out_specs),
        out_shape=tuple(out_shape),
        compiler_params=pltpu.CompilerParams(
            dimension_semantics=("arbitrary",), vmem_limit_bytes=VMEM_LIMIT),
        name="out_proj",
    )(*args)


FFN_TM = 512
FFN_TF = 1408


def _ffn_kernel(x1_ref, h_ref, w1_ref, w3_ref, w2_ref, o_ref, acc_ref):
    f = pl.program_id(1)
    h = h_ref[...]
    a = _dot(h, w1_ref[...])
    act = (a * _sigmoid(a) * _dot(h, w3_ref[...])).astype(BF16)
    part = _dot(act, w2_ref[...])

    @pl.when(f == 0)
    def _():
        acc_ref[...] = x1_ref[...] + part

    @pl.when(f > 0)
    def _():
        acc_ref[...] += part

    @pl.when(f == pl.num_programs(1) - 1)
    def _():
        o_ref[...] = acc_ref[...]


def _dense_ffn(x1, h, w1, w3, w2):
    tm, tf = FFN_TM, FFN_TF
    return pl.pallas_call(
        _ffn_kernel,
        grid=(TOKENS // tm, D_FF // tf),
        in_specs=[
            pl.BlockSpec((tm, D_MODEL), lambda i, f: (i, 0)),
            pl.BlockSpec((tm, D_MODEL), lambda i, f: (i, 0)),
            pl.BlockSpec((D_MODEL, tf), lambda i, f: (0, f)),
            pl.BlockSpec((D_MODEL, tf), lambda i, f: (0, f)),
            pl.BlockSpec((tf, D_MODEL), lambda i, f: (f, 0)),
        ],
        out_specs=pl.BlockSpec((tm, D_MODEL), lambda i, f: (i, 0)),
        out_shape=jax.ShapeDtypeStruct((TOKENS, D_MODEL), F32),
        scratch_shapes=[pltpu.VMEM((tm, D_MODEL), F32)],
        compiler_params=pltpu.CompilerParams(
            dimension_semantics=("arbitrary", "arbitrary"), vmem_limit_bytes=VMEM_LIMIT),
        name="dense_ffn",
    )(x1, h, w1, w3, w2)


MOE_SB = 512
MOE_TM = 512
MOE_TF = 1792
MOE_NB = TOKENS // MOE_SB
MOE_TILES = (2 * TOKENS) // MOE_TM + N_EXPERTS
MOE_PAIRS = MOE_TILES + N_EXPERTS * MOE_NB


def _moe_rank_kernel(comb_ref, tri_ref, rank_ref, cend_ref, carry_ref):
    @pl.when(pl.program_id(0) == 0)
    def _():
        carry_ref[...] = jnp.zeros((1, LANES), F32)

    routed = jnp.where(comb_ref[...] > 0.0, 1.0, 0.0).astype(BF16)
    carry = carry_ref[...]
    rank_ref[...] = carry + _dot(tri_ref[...], routed)
    carry = carry + jnp.sum(routed.astype(F32), axis=0, keepdims=True)
    carry_ref[...] = carry
    cend_ref[0] = jnp.broadcast_to(carry, (SUBLANES, LANES))


def _moe_rank(comb):
    idx = np.arange(MOE_SB)
    tri = jnp.asarray(idx[:, None] > idx[None, :], BF16)
    return pl.pallas_call(
        _moe_rank_kernel,
        grid=(MOE_NB,),
        in_specs=[pl.BlockSpec((MOE_SB, LANES), lambda b: (b, 0)),
                  pl.BlockSpec((MOE_SB, MOE_SB), lambda b: (0, 0))],
        out_specs=(pl.BlockSpec((MOE_SB, LANES), lambda b: (b, 0)),
                   pl.BlockSpec((1, SUBLANES, LANES), lambda b: (b, 0, 0))),
        out_shape=(jax.ShapeDtypeStruct((TOKENS, LANES), F32),
                   jax.ShapeDtypeStruct((MOE_NB, SUBLANES, LANES), F32)),
        scratch_shapes=[pltpu.VMEM((1, LANES), F32)],
        compiler_params=pltpu.CompilerParams(dimension_semantics=("arbitrary",)),
        name="moe_rank",
    )(comb, tri)


def _count_le(ends, v):
    return jnp.sum(ends[None, :] <= v[:, None], axis=1).astype(jnp.int32)


def _moe_schedule(cend):
    i32 = jnp.int32
    counts = cend[-1]
    cstart = jnp.concatenate([jnp.zeros((1, N_EXPERTS), i32), cend[:-1]], axis=0)
    tiles_e = (counts + MOE_TM - 1) // MOE_TM
    tile_end = jnp.cumsum(tiles_e)
    tile_start = tile_end - tiles_e
    n_tiles = tile_end[-1]
    group_start = tile_start * MOE_TM

    d = jnp.arange(MOE_TILES, dtype=i32)
    t_valid = d < n_tiles
    t_exp = jnp.minimum(_count_le(tile_end, d), N_EXPERTS - 1)
    t_exp = jnp.where(t_valid, t_exp, t_exp[jnp.maximum(n_tiles - 1, 0)])

    r0 = (d - tile_start[t_exp]) * MOE_TM
    r1 = jnp.minimum(r0 + MOE_TM, counts[t_exp])
    b_lo = jnp.sum(cend[:, t_exp] <= r0[None, :], axis=0).astype(i32)
    b_hi = jnp.sum(cstart[:, t_exp] < r1[None, :], axis=0).astype(i32) - 1
    n_pairs = jnp.where(t_valid, b_hi - b_lo + 1, 1)
    p_end = jnp.cumsum(n_pairs)
    p_start = p_end - n_pairs
    p = jnp.arange(MOE_PAIRS, dtype=i32)
    g_tile = jnp.minimum(_count_le(p_end, p), MOE_TILES - 1)
    g_live = p < p_end[-1]
    g_valid = g_live & t_valid[g_tile]
    g_block = jnp.where(g_valid, b_lo[g_tile] + p - p_start[g_tile], b_hi[jnp.maximum(n_tiles - 1, 0)])
    g_first = g_valid & (p == p_start[g_tile])
    g_zero = g_live & jnp.logical_not(t_valid[g_tile])
    g_flag = g_valid.astype(i32) + 2 * g_first.astype(i32) + 4 * g_zero.astype(i32)
    gather = (g_tile, g_block, t_exp[g_tile], g_flag)

    lo = group_start[None, :] + cstart
    hi = group_start[None, :] + cend
    t_lo = (lo // MOE_TM).reshape(-1)
    n_t = jnp.where(hi > lo, (hi - 1) // MOE_TM - lo // MOE_TM + 1, 0).reshape(-1)
    q_end = jnp.cumsum(n_t)
    q_start = q_end - n_t
    c_valid = p < q_end[-1]
    be = jnp.minimum(_count_le(q_end, p), MOE_NB * N_EXPERTS - 1)
    last = jnp.maximum(q_end[-1] - 1, 0)
    be = jnp.where(c_valid, be, be[last])
    c_tile = jnp.where(c_valid, t_lo[be] + p - q_start[be], (t_lo[be] + n_t[be] - 1))
    c_block = be // N_EXPERTS
    c_first = c_valid & ((p == 0) | (c_block != jnp.roll(c_block, 1)))
    c_flag = c_valid.astype(i32) + 2 * c_first.astype(i32)
    combine = (c_block, c_tile, be % N_EXPERTS, c_flag)
    return group_start, t_exp, t_valid.astype(i32), gather, combine


def _moe_gather_kernel(tile_ref, block_ref, exp_ref, flag_ref, h_ref, pos_ref, xs_ref):
    p = pl.program_id(0)
    flag = flag_ref[p]

    @pl.when((flag & 1) == 1)
    def _():
        rows = (tile_ref[p] * MOE_TM + lax.broadcasted_iota(jnp.int32, (MOE_TM, 1), 0)).astype(F32)
        pos = pos_ref[pl.ds(exp_ref[p], 1), :]
        onehot = jnp.where(rows == pos, 1.0, 0.0).astype(BF16)
        picked = _dot(onehot, h_ref[...]).astype(BF16)

        @pl.when((flag & 2) == 2)
        def _():
            xs_ref[...] = picked

        @pl.when((flag & 2) == 0)
        def _():
            xs_ref[...] += picked

    @pl.when((flag & 4) == 4)
    def _():
        xs_ref[...] = jnp.zeros((MOE_TM, D_MODEL), BF16)


def _moe_gather(sched, h, pos_t):
    spec = pltpu.PrefetchScalarGridSpec(
        num_scalar_prefetch=4,
        grid=(MOE_PAIRS,),
        in_specs=[pl.BlockSpec((MOE_SB, D_MODEL), lambda p, t, b, e, f: (b[p], 0)),
                  pl.BlockSpec((SUBLANES, MOE_SB), lambda p, t, b, e, f: (0, b[p]))],
        out_specs=pl.BlockSpec((MOE_TM, D_MODEL), lambda p, t, b, e, f: (t[p], 0)),
    )
    return pl.pallas_call(
        _moe_gather_kernel,
        grid_spec=spec,
        out_shape=jax.ShapeDtypeStruct((MOE_TILES * MOE_TM, D_MODEL), BF16),
        compiler_params=pltpu.CompilerParams(dimension_semantics=("arbitrary",)),
        name="moe_gather",
    )(*sched, h, pos_t)


def _moe_expert_kernel(exp_ref, valid_ref, xs_ref, w1_ref, w3_ref, w2_ref, y_ref, acc_ref):
    d = pl.program_id(0)
    f = pl.program_id(1)

    @pl.when(valid_ref[d] == 1)
    def _():
        x = xs_ref[...]
        a = _dot(x, w1_ref[0])
        act = (a * _sigmoid(a) * _dot(x, w3_ref[0])).astype(BF16)
        part = _dot(act, w2_ref[0])

        @pl.when(f == 0)
        def _():
            acc_ref[...] = part

        @pl.when(f > 0)
        def _():
            acc_ref[...] += part

        @pl.when(f == pl.num_programs(1) - 1)
        def _():
            y_ref[...] = acc_ref[...].astype(BF16)

    @pl.when((valid_ref[d] == 0) & (f == 0))
    def _():
        y_ref[...] = jnp.zeros((MOE_TM, D_MODEL), BF16)


def _moe_experts(t_exp, t_valid, xs, w1, w3, w2):
    tf = MOE_TF
    nf = D_FF_EXPERT // tf
    fidx = lambda d, f, v: f * v[d] + (nf - 1) * (1 - v[d])
    spec = pltpu.PrefetchScalarGridSpec(
        num_scalar_prefetch=2,
        grid=(MOE_TILES, nf),
        in_specs=[pl.BlockSpec((MOE_TM, D_MODEL), lambda d, f, e, v: (d, 0)),
                  pl.BlockSpec((1, D_MODEL, tf), lambda d, f, e, v: (e[d], 0, fidx(d, f, v))),
                  pl.BlockSpec((1, D_MODEL, tf), lambda d, f, e, v: (e[d], 0, fidx(d, f, v))),
                  pl.BlockSpec((1, tf, D_MODEL), lambda d, f, e, v: (e[d], fidx(d, f, v), 0))],
        out_specs=pl.BlockSpec((MOE_TM, D_MODEL), lambda d, f, e, v: (d, 0)),
        scratch_shapes=[pltpu.VMEM((MOE_TM, D_MODEL), F32)],
    )
    return pl.pallas_call(
        _moe_expert_kernel,
        grid_spec=spec,
        out_shape=jax.ShapeDtypeStruct((MOE_TILES * MOE_TM, D_MODEL), BF16),
        compiler_params=pltpu.CompilerParams(
            dimension_semantics=("arbitrary", "arbitrary"), vmem_limit_bytes=VMEM_LIMIT),
        name="moe_experts",
    )(t_exp, t_valid, xs, w1, w3, w2)


def _moe_combine_kernel(block_ref, tile_ref, exp_ref, flag_ref, y_ref, pos_ref, comb_ref, x1_ref, o_ref):
    p = pl.program_id(0)
    flag = flag_ref[p]

    @pl.when((flag & 1) == 1)
    def _():
        pick = lax.broadcasted_iota(jnp.int32, (1, LANES), 1) == exp_ref[p]
        pos = jnp.sum(jnp.where(pick, pos_ref[...], 0.0), axis=-1, keepdims=True)
        wgt = jnp.sum(jnp.where(pick, comb_ref[...], 0.0), axis=-1, keepdims=True)
        cols = (tile_ref[p] * MOE_TM + lax.broadcasted_iota(jnp.int32, (1, MOE_TM), 1)).astype(F32)
        onehot = jnp.where(pos == cols, 1.0, 0.0).astype(BF16)
        part = wgt * _dot(onehot, y_ref[...])

        @pl.when((flag & 2) == 2)
        def _():
            o_ref[...] = x1_ref[...] + part

        @pl.when((flag & 2) == 0)
        def _():
            o_ref[...] += part


def _moe_combine(sched, y, pos, comb, x1):
    tok = lambda width: pl.BlockSpec((MOE_SB, width), lambda p, b, t, e, f: (b[p], 0))
    spec = pltpu.PrefetchScalarGridSpec(
        num_scalar_prefetch=4,
        grid=(MOE_PAIRS,),
        in_specs=[pl.BlockSpec((MOE_TM, D_MODEL), lambda p, b, t, e, f: (t[p], 0)),
                  tok(LANES), tok(LANES), tok(D_MODEL)],
        out_specs=tok(D_MODEL),
    )
    return pl.pallas_call(
        _moe_combine_kernel,
        grid_spec=spec,
        out_shape=jax.ShapeDtypeStruct((TOKENS, D_MODEL), F32),
        compiler_params=pltpu.CompilerParams(
            dimension_semantics=("arbitrary",), vmem_limit_bytes=VMEM_LIMIT),
        name="moe_combine",
    )(*sched, y, pos, comb, x1)


def _moe_ffn(x1, h, comb, w1, w3, w2):
    rank, cend = _moe_rank(comb)
    cend = jnp.round(cend[:, 0, :N_EXPERTS]).astype(jnp.int32)
    group_start, t_exp, t_valid, gather, combine = _moe_schedule(cend)
    start = jnp.zeros((LANES,), F32).at[:N_EXPERTS].set(group_start.astype(F32))
    pos = jnp.where(comb > 0.0, start[None, :] + rank, -1.0)
    xs = _moe_gather(gather, h, pos[:, :SUBLANES].T)
    y = _moe_experts(t_exp, t_valid, xs, w1, w3, w2)
    return _moe_combine(combine, y, pos, comb, x1)


def _group_mean_matrix():
    idx = np.arange(LANES)
    return jnp.asarray((idx[:, None] // CONV_GROUP == idx[None, :] // CONV_GROUP) / CONV_GROUP, BF16)


def _overlap_matrix_t():
    c_start = np.arange(CMP_ROWS) * CMP_STRIDE
    j = np.arange(N_SEL)
    ovl = ((c_start[None, :] < (j[:, None] + 1) * SEL_BLOCK)
           & (c_start[None, :] + CMP_BLOCK > j[:, None] * SEL_BLOCK)
           & (np.arange(CMP_ROWS)[None, :] < N_CMP))
    return jnp.asarray(ovl, BF16)


def _dup(v):
    return jnp.concatenate([v, v], axis=-1)


def _zero_pad(v):
    return jnp.concatenate([v, jnp.zeros_like(v)], axis=-1)


def kernel(x, attn_norm_g, w_in, conv_dw_w, conv_dw_b, conv_ln_g, conv_ln_b, q_norm_g, k_norm_g,
           cmp_pos_k, cmp_pos_v, cmp_k_w1, cmp_k_b1, cmp_k_w2, cmp_k_b2,
           cmp_v_w1, cmp_v_b1, cmp_v_w2, cmp_v_b2, w_out, ffn_norm_g,
           dense_w1, dense_w3, dense_w2, router_w, moe_w1, moe_w3, moe_w2):
    assert x.shape == (BATCH, SEQ, D_MODEL) and x.dtype == F32
    bd = _group_mean_matrix()
    ovlt = _overlap_matrix_t()
    col_src = jnp.asarray(_IN_COLS_SRC)
    row_src = jnp.asarray(np.maximum(_IN_T_ROWS_SRC, 0))
    row_live = jnp.asarray(_IN_T_ROWS_SRC >= 0)

    x2 = x.reshape(TOKENS, D_MODEL)
    for layer in range(DEPTH):
        w_cols = jnp.take(w_in[layer], col_src, axis=1).astype(BF16)
        w_rows = jnp.where(row_live[:, None], jnp.take(w_in[layer], row_src, axis=1).T, 0.0).astype(BF16)
        qg = _dup(q_norm_g[layer])[None, :] * (LOG2_E * HEAD_DIM ** -0.5)
        kgs = _dup(k_norm_g[layer, 1])[None, :]
        kgw = _dup(k_norm_g[layer, 2])[None, :]
        u, q, kcvc, ks, kw, vst, vwt, gatet = _in_proj(
            x2, attn_norm_g[layer][None, :], w_cols, w_rows, qg, kgs, kgw, bd)

        conv = _conv_module(u.reshape(BATCH, SEQ, CONV_WIDTH), conv_dw_w[layer], conv_dw_b[layer][None, :],
                            conv_ln_g[layer][None, :], conv_ln_b[layer][None, :], bd)

        xr = kcvc.reshape(BATCH, SEQ, 2, N_KV_HEADS, HEAD_DIM).transpose(2, 0, 3, 1, 4)
        xr = xr.reshape(2, BATCH, N_KV_HEADS, CMP_ROWS, CMP_CHUNK)
        pos = jnp.stack([cmp_pos_k[layer], cmp_pos_v[layer]]).reshape(2, 1, CMP_BLOCK * HEAD_DIM)
        w1 = jnp.stack([cmp_k_w1[layer], cmp_v_w1[layer]]).astype(BF16)
        b1 = jnp.stack([cmp_k_b1[layer], cmp_v_b1[layer]])[:, None, :]
        w2 = _zero_pad(jnp.stack([cmp_k_w2[layer], cmp_v_w2[layer]])).astype(BF16)
        b2 = _zero_pad(jnp.stack([cmp_k_b2[layer], cmp_v_b2[layer]]))[:, None, :]
        kvc = _compress(xr, pos, w1, b1, w2, b2, _zero_pad(k_norm_g[layer, 0])[None, :])

        seq3 = lambda a: a.reshape(BATCH, SEQ, a.shape[-1])
        attn = _attention(seq3(q), gatet, kvc, seq3(ks), vst, seq3(kw), vwt, ovlt)

        w_o = w_out[layer].astype(BF16)
        conv2 = conv.reshape(TOKENS, CONV_WIDTH)
        attn2 = attn.reshape(TOKENS, ATTN_WIDTH)
        g_ffn = ffn_norm_g[layer][None, :]
        i = layer // 2
        if layer % 2 == 0:
            x1, h = _out_proj(x2, conv2, attn2, w_o, g_ffn)
            x2 = _dense_ffn(x1, h, dense_w1[i].astype(BF16), dense_w3[i].astype(BF16),
                            dense_w2[i].astype(BF16))
        else:
            r = jnp.pad(router_w[i], ((0, 0), (0, LANES - N_EXPERTS)))
            r_hi = r.astype(BF16)
            r_lo = (r - r_hi.astype(F32)).astype(BF16)
            x1, h, comb = _out_proj(x2, conv2, attn2, w_o, g_ffn, jnp.stack([r_hi, r_lo]))
            x2 = _moe_ffn(x1, h, comb, moe_w1[i].astype(BF16), moe_w3[i].astype(BF16),
                          moe_w2[i].astype(BF16))
    return x2.reshape(BATCH, SEQ, D_MODEL)
```

```python
import functools

import numpy as np
import jax
import jax.numpy as jnp
from jax import lax
from jax.experimental import pallas as pl
from jax.experimental.pallas import tpu as pltpu

D_MODEL = 1024
BATCH = 8
SEQ = 2048
DEPTH = 2
TOKENS = BATCH * SEQ

CONV_WIDTH = 512
CONV_GROUP = 64
CONV_KERNEL = 31
N_HEADS = 8
HEAD_DIM = 64
N_KV_HEADS = 2
GQA = N_HEADS // N_KV_HEADS
ATTN_WIDTH = N_HEADS * HEAD_DIM
KV_WIDTH = N_KV_HEADS * HEAD_DIM
N_BRANCH = 3
CMP_BLOCK = 32
CMP_STRIDE = 16
CMP_HIDDEN = 256
N_CMP = (SEQ - CMP_BLOCK) // CMP_STRIDE + 1
SEL_BLOCK = 64
SEL_TOPK = 8
N_SEL = SEQ // SEL_BLOCK
WINDOW = 512
Q_BLOCK = 256
FORCE_BONUS = 1e4
NEG_INF = -1e30
LOG2_E = 1.4426950408889634
D_FF = 2816
N_EXPERTS = 8
D_FF_EXPERT = 3584
EPS = 1e-6

LANES = 128
SUBLANES = 8
VMEM_LIMIT = 48 * 1024 * 1024

F32 = jnp.float32
BF16 = jnp.bfloat16

_OFF_VAL = 0
_OFF_GATE = 512
_OFF_Q = 1024
_OFF_KCVC = 1536
_OFF_KS = 1792
_OFF_KW = 1920
IN_COLS = 2048
T_ROWS = 5 * LANES

_BASE_KV = 2 * CONV_WIDTH + ATTN_WIDTH


def _in_proj_weights(w):
    piece = lambda i: w[:, _BASE_KV + i * KV_WIDTH:_BASE_KV + (i + 1) * KV_WIDTH]
    w_cols = jnp.concatenate([w[:, :_BASE_KV + 2 * KV_WIDTH], piece(2), piece(4)], axis=1)
    assert w_cols.shape[1] == IN_COLS
    rows = []
    for i in (3, 5):
        vt = piece(i).T
        for head in range(N_KV_HEADS):
            rows += [vt[head * HEAD_DIM:(head + 1) * HEAD_DIM], jnp.zeros((HEAD_DIM, D_MODEL), w.dtype)]
    n_gate = N_HEADS * N_BRANCH
    glog = _BASE_KV + 6 * KV_WIDTH
    rows += [w[:, glog:glog + n_gate].T, jnp.zeros((LANES - n_gate, D_MODEL), w.dtype)]
    w_rows = jnp.concatenate(rows, axis=0)
    assert w_rows.shape[0] == T_ROWS
    return w_cols.astype(BF16), w_rows.astype(BF16)


def _sigmoid(v):
    return 1.0 / (1.0 + jnp.exp(-v))


def _dot(a, b):
    return jnp.dot(a, b, preferred_element_type=F32)


def _dot_nt(a, b):
    return lax.dot_general(a, b, (((1,), (1,)), ((), ())), preferred_element_type=F32)


def _split_dot(v, m):
    hi = v.astype(BF16)
    lo = (v - hi.astype(F32)).astype(BF16)
    return _dot(hi, m) + _dot(lo, m)


IN_TM = 512


def _in_proj_kernel(x_ref, g_ref, w_ref, wt_ref, qg_ref, kgs_ref, kgw_ref, bd_ref,
                    u_ref, q_ref, kcvc_ref, ks_ref, kw_ref, vst_ref, vwt_ref, gatet_ref):
    x = x_ref[...]
    ms = jnp.mean(x * x, axis=-1, keepdims=True)
    h = (x * lax.rsqrt(ms + EPS) * g_ref[...]).astype(BF16)

    def proj(lo, width):
        return _dot(h, w_ref[:, lo:lo + width])

    u_ref[...] = proj(_OFF_VAL, CONV_WIDTH) * _sigmoid(proj(_OFF_GATE, CONV_WIDTH))

    bd = bd_ref[...]
    lane = lax.broadcasted_iota(jnp.int32, (IN_TM, LANES), 1)
    lower = lane < HEAD_DIM

    def head_slots(pair, upper, out_ref, slot):
        out_ref[:, slot * LANES:(slot + 1) * LANES] = jnp.where(lower, pair, upper).astype(BF16)
        swapped = pltpu.roll(pair, HEAD_DIM, axis=1)
        out_ref[:, (slot + 1) * LANES:(slot + 2) * LANES] = jnp.where(lower, swapped, upper).astype(BF16)

    q = proj(_OFF_Q, ATTN_WIDTH)
    for c in range(ATTN_WIDTH // LANES):
        qc = q[:, c * LANES:(c + 1) * LANES]
        msq = _split_dot(qc * qc, bd)
        head_slots(qc * lax.rsqrt(msq + EPS) * qg_ref[...], 0.0, q_ref, 2 * c)

    kcvc_ref[...] = proj(_OFF_KCVC, 2 * KV_WIDTH)

    def normed_k(off, kg_ref):
        k = proj(off, KV_WIDTH)
        return k * lax.rsqrt(_split_dot(k * k, bd) + EPS) * kg_ref[...]

    tok = (pl.program_id(0) % (SEQ // IN_TM)) * IN_TM + lax.broadcasted_iota(jnp.int32, (IN_TM, 1), 0)
    block_hot = jnp.where(lane - HEAD_DIM == tok // SEL_BLOCK, 1.0, 0.0)
    head_slots(normed_k(_OFF_KS, kgs_ref), block_hot, ks_ref, 0)
    head_slots(normed_k(_OFF_KW, kgw_ref), 0.0, kw_ref, 0)

    zt = _dot_nt(wt_ref[...], h)
    ones_rows = lax.broadcasted_iota(jnp.int32, (LANES, IN_TM), 0) >= HEAD_DIM
    for j, out_ref in enumerate((vst_ref, vst_ref, vwt_ref, vwt_ref)):
        blk = zt[j * LANES:(j + 1) * LANES]
        out_ref[0, j % N_KV_HEADS] = jnp.where(ones_rows, 1.0, blk).astype(BF16)
    gatet_ref[0] = _sigmoid(zt[4 * LANES:5 * LANES])


def _in_proj(x2, g, w, wt, qg, kgs, kgw, bd):
    tm = IN_TM
    per_seq = SEQ // tm
    row = lambda width: pl.BlockSpec((tm, width), lambda i: (i, 0))
    full = lambda a: pl.BlockSpec(a.shape, lambda i: (0,) * a.ndim)
    vt_spec = pl.BlockSpec((1, N_KV_HEADS, LANES, tm), lambda i: (i // per_seq, 0, 0, i % per_seq))
    vt_shape = jax.ShapeDtypeStruct((BATCH, N_KV_HEADS, LANES, SEQ), BF16)
    out_shape = (
        jax.ShapeDtypeStruct((TOKENS, CONV_WIDTH), F32),
        jax.ShapeDtypeStruct((TOKENS, N_HEADS * LANES), BF16),
        jax.ShapeDtypeStruct((TOKENS, 2 * KV_WIDTH), F32),
        jax.ShapeDtypeStruct((TOKENS, 2 * LANES), BF16),
        jax.ShapeDtypeStruct((TOKENS, 2 * LANES), BF16),
        vt_shape,
        vt_shape,
        jax.ShapeDtypeStruct((BATCH, LANES, SEQ), F32),
    )
    out_specs = (row(CONV_WIDTH), row(N_HEADS * LANES), row(2 * KV_WIDTH), row(2 * LANES), row(2 * LANES),
                 vt_spec, vt_spec,
                 pl.BlockSpec((1, LANES, tm), lambda i: (i // per_seq, 0, i % per_seq)))
    return pl.pallas_call(
        _in_proj_kernel,
        grid=(TOKENS // tm,),
        in_specs=[row(D_MODEL), full(g), full(w), full(wt), full(qg), full(kgs), full(kgw), full(bd)],
        out_specs=out_specs,
        out_shape=out_shape,
        compiler_params=pltpu.CompilerParams(
            dimension_semantics=("arbitrary",), vmem_limit_bytes=VMEM_LIMIT),
        name="in_proj",
    )(x2, g, w, wt, qg, kgs, kgw, bd)


CONV_TR = 512
CONV_HALO = 32
CONV_RS = 64


def _conv_kernel(ucur_ref, uprev_ref, w_ref, b_ref, lg_ref, lb_ref, bd_ref, o_ref, win_ref, y_ref):
    i = pl.program_id(1)
    win_ref[0, 0:CONV_HALO, :] = jnp.where(i > 0, uprev_ref[0], 0.0)
    win_ref[0, CONV_HALO:CONV_HALO + CONV_TR, :] = ucur_ref[0]
    first = CONV_HALO - (CONV_KERNEL - 1)
    shifted_rows = CONV_HALO + CONV_TR - SUBLANES
    for s in range(1, SUBLANES):
        win_ref[s, 0:shifted_rows, :] = win_ref[0, s:s + shifted_rows, :]

    for c in range(CONV_WIDTH // LANES):
        lanes = slice(c * LANES, (c + 1) * LANES)

        def body(r, carry):
            base = pl.multiple_of(r * CONV_RS, CONV_RS)
            acc = jnp.zeros((CONV_RS, LANES), F32)
            for k in range(CONV_KERNEL):
                off = first + k
                rows = pl.ds(base + (off // SUBLANES) * SUBLANES, CONV_RS)
                acc = acc + win_ref[off % SUBLANES, rows, lanes] * w_ref[k:k + 1, lanes]
            y_ref[pl.ds(base, CONV_RS), lanes] = acc + b_ref[:, lanes]
            return carry

        lax.fori_loop(0, CONV_TR // CONV_RS, body, 0)

    bd = bd_ref[...]
    for c in range(CONV_WIDTH // LANES):
        lanes = slice(c * LANES, (c + 1) * LANES)
        y = y_ref[:, lanes]
        d = y - _split_dot(y, bd)
        var = _split_dot(d * d, bd)
        yn = d * lax.rsqrt(var + EPS) * lg_ref[:, lanes] + lb_ref[:, lanes]
        o_ref[0, :, lanes] = (yn * _sigmoid(yn)).astype(BF16)


def _conv_module(u3, w, b, lg, lb, bd):
    halo_blocks = CONV_TR // CONV_HALO
    full = lambda a: pl.BlockSpec(a.shape, lambda bi, i: (0,) * a.ndim)
    return pl.pallas_call(
        _conv_kernel,
        grid=(BATCH, SEQ // CONV_TR),
        in_specs=[
            pl.BlockSpec((1, CONV_TR, CONV_WIDTH), lambda bi, i: (bi, i, 0)),
            pl.BlockSpec((1, CONV_HALO, CONV_WIDTH),
                         lambda bi, i: (bi, jnp.maximum(i * halo_blocks - 1, 0), 0)),
            full(w), full(b), full(lg), full(lb), full(bd),
        ],
        out_specs=pl.BlockSpec((1, CONV_TR, CONV_WIDTH), lambda bi, i: (bi, i, 0)),
        out_shape=jax.ShapeDtypeStruct((BATCH, SEQ, CONV_WIDTH), BF16),
        scratch_shapes=[pltpu.VMEM((SUBLANES, CONV_HALO + CONV_TR, CONV_WIDTH), F32),
                        pltpu.VMEM((CONV_TR, CONV_WIDTH), F32)],
        compiler_params=pltpu.CompilerParams(dimension_semantics=("arbitrary", "arbitrary")),
        name="conv_module",
    )(u3, u3, w, b, lg, lb, bd)


CMP_ROWS = SEQ // CMP_STRIDE
CMP_CHUNK = CMP_STRIDE * HEAD_DIM


def _compress_kernel(x_ref, pos_ref, w1_ref, b1_ref, w2_ref, b2_ref, w2t_ref, b2t_ref, kg_ref, o_ref):
    kind = pl.program_id(0)
    xa = x_ref[0, 0, 0]
    xb = pltpu.roll(xa, CMP_ROWS - 1, axis=0)
    a = (xa + pos_ref[0, :, 0:CMP_CHUNK]).astype(BF16)
    b = (xb + pos_ref[0, :, CMP_CHUNK:2 * CMP_CHUNK]).astype(BF16)
    hid = _dot(a, w1_ref[0, 0:CMP_CHUNK, :]) + _dot(b, w1_ref[0, CMP_CHUNK:2 * CMP_CHUNK, :]) + b1_ref[0]
    hid = (hid * _sigmoid(hid)).astype(BF16)
    out = _dot(hid, w2_ref[0]) + b2_ref[0]
    ms = jnp.sum(out * out, axis=-1, keepdims=True) * (1.0 / HEAD_DIM)
    normed = out * lax.rsqrt(ms + EPS) * kg_ref[...]
    out_t = _dot_nt(w2t_ref[0], hid) + b2t_ref[0]
    o_ref[0, 0, 0] = jnp.where(kind == 0, normed, out_t).astype(BF16)


def _compress(xr, pos, w1, b1, w2, b2, kg):
    w2t = jnp.swapaxes(w2, 1, 2)
    b2t = jnp.swapaxes(b2, 1, 2)
    per_kind = lambda a: pl.BlockSpec((1,) + a.shape[1:], lambda k, bi, h: (k,) + (0,) * (a.ndim - 1))
    return pl.pallas_call(
        _compress_kernel,
        grid=(2, BATCH, N_KV_HEADS),
        in_specs=[
            pl.BlockSpec((1, 1, 1, CMP_ROWS, CMP_CHUNK), lambda k, bi, h: (k, bi, h, 0, 0)),
            per_kind(pos), per_kind(w1), per_kind(b1), per_kind(w2), per_kind(b2),
            per_kind(w2t), per_kind(b2t),
            pl.BlockSpec(kg.shape, lambda k, bi, h: (0, 0)),
        ],
        out_specs=pl.BlockSpec((1, 1, 1, CMP_ROWS, LANES), lambda k, bi, h: (k, bi, h, 0, 0)),
        out_shape=jax.ShapeDtypeStruct((2, BATCH, N_KV_HEADS, CMP_ROWS, LANES), BF16),
        compiler_params=pltpu.CompilerParams(
            dimension_semantics=("arbitrary", "arbitrary", "arbitrary")),
        name="compress",
    )(xr, pos, w1, b1, w2, b2, w2t, b2t, kg)


QCOLS = GQA * Q_BLOCK
SEL_CHUNK = 256
WIN_CHUNK = 128
WIN_CHUNKS = (WINDOW + Q_BLOCK) // WIN_CHUNK
MASK_BIAS = -1e30


def _window_biases():
    k = np.arange(WIN_CHUNK)[:, None]
    t = (np.arange(QCOLS) % Q_BLOCK)[None, :]
    tables, index = [], {}
    for i in range(WIN_CHUNKS):
        key = i * WIN_CHUNK - WINDOW + k
        bias = np.where((key <= t) & (key > t - WINDOW), 0.0, MASK_BIAS)
        if bias.any():
            index[i] = len(tables)
            tables.append(bias)
    return np.stack(tables).astype(np.float32), index


_WIN_BIAS, _WIN_BIAS_INDEX = _window_biases()


def _attn_kernel(q_ref, gatet_ref, kc_ref, vct_ref, ks_ref, vst_ref, kw_ref, vwt_ref,
                 ovlt_ref, wbias_ref, o_ref, ssel_sc, swin_sc, acc_sc, out_sc):
    n = pl.program_id(1)
    start = n * Q_BLOCK
    col = lax.broadcasted_iota(jnp.int32, (1, QCOLS), 1)
    t_cols = start + (col & (Q_BLOCK - 1))
    tq = start + lax.broadcasted_iota(jnp.int32, (1, Q_BLOCK), 1)
    blk_row = lax.broadcasted_iota(jnp.int32, (N_SEL, 1), 0)
    blk_row_f = blk_row.astype(F32)
    cmp_end = lax.broadcasted_iota(jnp.int32, (CMP_ROWS, 1), 0) * CMP_STRIDE + (CMP_BLOCK - 1)

    kv_heads = range(N_KV_HEADS)
    kv_lanes = [slice(kvh * LANES, (kvh + 1) * LANES) for kvh in kv_heads]

    def gate_row(kvh, g, branch):
        r = (kvh * GQA + g) * N_BRANCH + branch
        return gatet_ref[0, r:r + 1, :]

    def head_rows(kvh, g):
        h = kvh * GQA + g
        return slice(h * HEAD_DIM, (h + 1) * HEAD_DIM)

    qs = []
    for kvh in kv_heads:
        h0 = kvh * GQA
        qs.append(jnp.concatenate(
            [q_ref[0, :, (h0 + g) * LANES:(h0 + g + 1) * LANES] for g in range(GQA)], axis=0))

    imp = []
    mask_c = cmp_end <= t_cols
    for kvh in kv_heads:
        s_c = jnp.where(mask_c, _dot_nt(kc_ref[0, kvh], qs[kvh]), NEG_INF)
        m_c = jnp.max(s_c, axis=0, keepdims=True)
        p_c = jnp.where(mask_c, jnp.exp2(s_c - m_c), 0.0)
        p_c = p_c / jnp.maximum(jnp.sum(p_c, axis=0, keepdims=True), 1e-30)
        o_cmp = _dot(vct_ref[0, kvh], p_c.astype(BF16))
        for g in range(GQA):
            out_sc[head_rows(kvh, g), :] = gate_row(kvh, g, 0) * o_cmp[0:HEAD_DIM, g * Q_BLOCK:(g + 1) * Q_BLOCK]
        p_sum = p_c[:, 0:Q_BLOCK]
        for g in range(1, GQA):
            p_sum = p_sum + p_c[:, g * Q_BLOCK:(g + 1) * Q_BLOCK]
        p_hi = p_sum.astype(BF16)
        p_lo = (p_sum - p_hi.astype(F32)).astype(BF16)
        imp.append(_dot(ovlt_ref[...], p_hi) + _dot(ovlt_ref[...], p_lo))

    tq2 = jnp.concatenate([tq] * N_KV_HEADS, axis=1)
    cur = tq2 >> 6
    forced = (blk_row == 0) | (blk_row == cur) | (blk_row == cur - 1)
    causal_blk = blk_row * SEL_BLOCK <= tq2
    score = jnp.where(causal_blk, jnp.concatenate(imp, axis=1) + jnp.where(forced, FORCE_BONUS, 0.0), NEG_INF)
    sel = jnp.zeros((N_SEL, N_KV_HEADS * Q_BLOCK), F32)
    for _ in range(SEL_TOPK):
        best = jnp.max(score, axis=0, keepdims=True)
        idx = jnp.min(jnp.where(score == best, blk_row_f, float(LANES)), axis=0, keepdims=True)
        hit = blk_row_f == idx
        sel = jnp.where(hit, 1.0, sel)
        score = jnp.where(hit, -3e38, score)
    sel_bias = (sel - 1.0) * (-MASK_BIAS)
    qs_sel = []
    for kvh in kv_heads:
        rows = jnp.concatenate([jnp.zeros((HEAD_DIM, Q_BLOCK), F32),
                                sel_bias[:, kvh * Q_BLOCK:(kvh + 1) * Q_BLOCK],
                                jnp.zeros((LANES - HEAD_DIM - N_SEL, Q_BLOCK), F32)], axis=0)
        bias_q = rows.T.astype(BF16)
        qs_sel.append(qs[kvh] + jnp.concatenate([bias_q] * GQA, axis=0))

    for kvh in kv_heads:
        mxw = jnp.full((1, QCOLS), NEG_INF, F32)
        kbs = [n * (Q_BLOCK // WIN_CHUNK) - WINDOW // WIN_CHUNK + i for i in range(WIN_CHUNKS)]
        k_win = [pl.multiple_of(jnp.maximum(kb, 0) * WIN_CHUNK, WIN_CHUNK) for kb in kbs]
        k_band = jnp.concatenate([kw_ref[0, pl.ds(k0, WIN_CHUNK), kv_lanes[kvh]] for k0 in k_win], axis=0)
        s_band = _dot_nt(k_band, qs[kvh])
        for i, kb in enumerate(kbs):
            s = s_band[i * WIN_CHUNK:(i + 1) * WIN_CHUNK]
            if i in _WIN_BIAS_INDEX:
                s = s + wbias_ref[_WIN_BIAS_INDEX[i]]
            if i < WINDOW // WIN_CHUNK:
                s = s + jnp.where(kb >= 0, 0.0, MASK_BIAS)
            swin_sc[kvh, i * WIN_CHUNK:(i + 1) * WIN_CHUNK, :] = s
            mxw = jnp.maximum(mxw, jnp.max(s, axis=0, keepdims=True))
        p = jnp.exp2(swin_sc[kvh] - mxw).astype(BF16)
        v_band = jnp.concatenate([vwt_ref[0, kvh, :, pl.ds(k0, WIN_CHUNK)] for k0 in k_win], axis=1)
        accw = _dot(v_band, p)
        o_win = accw[0:HEAD_DIM] * (1.0 / accw[HEAD_DIM:HEAD_DIM + 1])
        for g in range(GQA):
            out_sc[head_rows(kvh, g), :] += gate_row(kvh, g, 2) * o_win[:, g * Q_BLOCK:(g + 1) * Q_BLOCK]

    def sel_scores(kvh, k0):
        return _dot_nt(ks_ref[0, pl.ds(k0, SEL_CHUNK), kv_lanes[kvh]], qs_sel[kvh])

    def sel_pass1(i, mx):
        k0 = pl.multiple_of(i * SEL_CHUNK, SEL_CHUNK)
        out = []
        for kvh in kv_heads:
            s = sel_scores(kvh, k0)
            ssel_sc[kvh, pl.ds(k0, SEL_CHUNK), :] = s
            out.append(jnp.maximum(mx[kvh], jnp.max(s, axis=0, keepdims=True)))
        return tuple(out)

    last = (start + Q_BLOCK - 1) // SEL_CHUNK
    mx = lax.fori_loop(0, last, sel_pass1, (jnp.full((1, QCOLS), NEG_INF, F32),) * N_KV_HEADS)
    k_last = pl.multiple_of(last * SEL_CHUNK, SEL_CHUNK)
    causal = k_last + lax.broadcasted_iota(jnp.int32, (SEL_CHUNK, 1), 0) <= t_cols
    mx = list(mx)
    for kvh in kv_heads:
        s = jnp.where(causal, sel_scores(kvh, k_last), NEG_INF)
        ssel_sc[kvh, pl.ds(k_last, SEL_CHUNK), :] = s
        mx[kvh] = jnp.maximum(mx[kvh], jnp.max(s, axis=0, keepdims=True))
        acc_sc[kvh] = jnp.zeros((LANES, QCOLS), F32)

    def sel_pass2(i, carry):
        k0 = pl.multiple_of(i * SEL_CHUNK, SEL_CHUNK)
        for kvh in kv_heads:
            p = jnp.exp2(ssel_sc[kvh, pl.ds(k0, SEL_CHUNK), :] - mx[kvh]).astype(BF16)
            acc_sc[kvh] += _dot(vst_ref[0, kvh, :, pl.ds(k0, SEL_CHUNK)], p)
        return carry

    lax.fori_loop(0, last + 1, sel_pass2, 0)
    for kvh in kv_heads:
        acc = acc_sc[kvh]
        o_sel = acc[0:HEAD_DIM] * (1.0 / acc[HEAD_DIM:HEAD_DIM + 1])
        for g in range(GQA):
            out_sc[head_rows(kvh, g), :] += gate_row(kvh, g, 1) * o_sel[:, g * Q_BLOCK:(g + 1) * Q_BLOCK]

    for c in range(N_HEADS // 2):
        o_ref[0, :, c * LANES:(c + 1) * LANES] = out_sc[c * LANES:(c + 1) * LANES, :].T.astype(BF16)


def _attention(q3, gatet, kvc, ks3, vst, kw3, vwt, ovlt):
    wbias = jnp.asarray(_WIN_BIAS)
    qblk = lambda width: pl.BlockSpec((1, Q_BLOCK, width), lambda bi, n: (bi, n, 0))
    seq = pl.BlockSpec((1, SEQ, 2 * LANES), lambda bi, n: (bi, 0, 0))
    seq_t = pl.BlockSpec((1, N_KV_HEADS, LANES, SEQ), lambda bi, n: (bi, 0, 0, 0))
    cmp_spec = lambda kind: pl.BlockSpec((None, 1, N_KV_HEADS, CMP_ROWS, LANES),
                                         lambda bi, n: (kind, bi, 0, 0, 0))
    full = lambda a: pl.BlockSpec(a.shape, lambda bi, n: (0,) * a.ndim)
    return pl.pallas_call(
        _attn_kernel,
        grid=(BATCH, SEQ // Q_BLOCK),
        in_specs=[qblk(N_HEADS * LANES), pl.BlockSpec((1, LANES, Q_BLOCK), lambda bi, n: (bi, 0, n)),
                  cmp_spec(0), cmp_spec(1), seq, seq_t, seq, seq_t, full(ovlt), full(wbias)],
        out_specs=qblk(ATTN_WIDTH),
        out_shape=jax.ShapeDtypeStruct((BATCH, SEQ, ATTN_WIDTH), BF16),
        scratch_shapes=[pltpu.VMEM((N_KV_HEADS, SEQ, QCOLS), F32),
                        pltpu.VMEM((N_KV_HEADS, WIN_CHUNKS * WIN_CHUNK, QCOLS), F32),
                        pltpu.VMEM((N_KV_HEADS, LANES, QCOLS), F32),
                        pltpu.VMEM((ATTN_WIDTH, Q_BLOCK), F32)],
        compiler_params=pltpu.CompilerParams(
            dimension_semantics=("arbitrary", "arbitrary"), vmem_limit_bytes=VMEM_LIMIT),
        name="nsa_attention",
    )(q3, gatet, kvc, kvc, ks3, vst, kw3, vwt, ovlt, wbias)


OUT_TM = 512


def _out_proj_kernel(x_ref, conv_ref, attn_ref, w_ref, g_ref, x1_ref, h_ref):
    x1 = (x_ref[...] + _dot(conv_ref[...], w_ref[0:CONV_WIDTH, :])
          + _dot(attn_ref[...], w_ref[CONV_WIDTH:CONV_WIDTH + ATTN_WIDTH, :]))
    x1_ref[...] = x1
    ms = jnp.mean(x1 * x1, axis=-1, keepdims=True)
    h_ref[...] = (x1 * lax.rsqrt(ms + EPS) * g_ref[...]).astype(BF16)


def _out_proj_router_kernel(x_ref, conv_ref, attn_ref, w_ref, g_ref, r_ref, x1_ref, h_ref, comb_ref):
    x1 = (x_ref[...] + _dot(conv_ref[...], w_ref[0:CONV_WIDTH, :])
          + _dot(attn_ref[...], w_ref[CONV_WIDTH:CONV_WIDTH + ATTN_WIDTH, :]))
    x1_ref[...] = x1
    ms = jnp.mean(x1 * x1, axis=-1, keepdims=True)
    h = x1 * lax.rsqrt(ms + EPS) * g_ref[...]
    h_ref[...] = h.astype(BF16)
    h_hi = h.astype(BF16)
    h_lo = (h - h_hi.astype(F32)).astype(BF16)
    logits = _dot(h_hi, r_ref[0]) + _dot(h_lo, r_ref[0]) + _dot(h_hi, r_ref[1])
    lane = lax.broadcasted_iota(jnp.int32, (1, LANES), 1).astype(F32)
    logits = jnp.where(lane < N_EXPERTS, logits, NEG_INF)
    v1 = jnp.max(logits, axis=-1, keepdims=True)
    i1 = jnp.min(jnp.where(logits == v1, lane, float(LANES)), axis=-1, keepdims=True)
    hit1 = lane == i1
    rest = jnp.where(hit1, -3e38, logits)
    v2 = jnp.max(rest, axis=-1, keepdims=True)
    i2 = jnp.min(jnp.where(rest == v2, lane, float(LANES)), axis=-1, keepdims=True)
    hit2 = lane == i2
    e2 = jnp.exp(v2 - v1)
    w_top = 1.0 / (1.0 + e2)
    comb_ref[...] = jnp.where(hit1, w_top, 0.0) + jnp.where(hit2, e2 * w_top, 0.0)


def _out_proj(x2, conv2, attn2, w, g, router=None):
    tm = OUT_TM
    row = lambda width: pl.BlockSpec((tm, width), lambda i: (i, 0))
    full = lambda a: pl.BlockSpec(a.shape, lambda i: (0,) * a.ndim)
    in_specs = [row(D_MODEL), row(CONV_WIDTH), row(ATTN_WIDTH), full(w), full(g)]
    out_shape = [jax.ShapeDtypeStruct((TOKENS, D_MODEL), F32), jax.ShapeDtypeStruct((TOKENS, D_MODEL), BF16)]
    out_specs = [row(D_MODEL), row(D_MODEL)]
    args = [x2, conv2, attn2, w, g]
    body = _out_proj_kernel
    if router is not None:
        body = _out_proj_router_kernel
        in_specs.append(full(router))
        args.append(router)
        out_shape.append(jax.ShapeDtypeStruct((TOKENS, LANES), F32))
        out_specs.append(row(LANES))
    return pl.pallas_call(
        body,
        grid=(TOKENS // tm,),
        in_specs=in_specs,
        out_specs=tuple(out_specs),
        out_shape=tuple(out_shape),
        compiler_params=pltpu.CompilerParams(
            dimension_semantics=("arbitrary",), vmem_limit_bytes=VMEM_LIMIT),
        name="out_proj",
    )(*args)


FFN_TM = 512
FFN_TF = 1408


def _ffn_kernel(x1_ref, h_ref, w1_ref, w3_ref, w2_ref, o_ref, acc_ref):
    f = pl.program_id(1)
    h = h_ref[...]
    a = _dot(h, w1_ref[...])
    act = (a * _sigmoid(a) * _dot(h, w3_ref[...])).astype(BF16)
    part = _dot(act, w2_ref[...])

    @pl.when(f == 0)
    def _():
        acc_ref[...] = x1_ref[...] + part

    @pl.when(f > 0)
    def _():
        acc_ref[...] += part

    @pl.when(f == pl.num_programs(1) - 1)
    def _():
        o_ref[...] = acc_ref[...]


def _dense_ffn(x1, h, w1, w3, w2):
    tm, tf = FFN_TM, FFN_TF
    return pl.pallas_call(
        _ffn_kernel,
        grid=(TOKENS // tm, D_FF // tf),
        in_specs=[
            pl.BlockSpec((tm, D_MODEL), lambda i, f: (i, 0)),
            pl.BlockSpec((tm, D_MODEL), lambda i, f: (i, 0)),
            pl.BlockSpec((D_MODEL, tf), lambda i, f: (0, f)),
            pl.BlockSpec((D_MODEL, tf), lambda i, f: (0, f)),
            pl.BlockSpec((tf, D_MODEL), lambda i, f: (f, 0)),
        ],
        out_specs=pl.BlockSpec((tm, D_MODEL), lambda i, f: (i, 0)),
        out_shape=jax.ShapeDtypeStruct((TOKENS, D_MODEL), F32),
        scratch_shapes=[pltpu.VMEM((tm, D_MODEL), F32)],
        compiler_params=pltpu.CompilerParams(
            dimension_semantics=("arbitrary", "arbitrary"), vmem_limit_bytes=VMEM_LIMIT),
        name="dense_ffn",
    )(x1, h, w1, w3, w2)


MOE_SB = 512
MOE_TM = 512
MOE_GT = 256
MOE_CT = 512
MOE_TF = 1792
MOE_NB = TOKENS // MOE_SB
MOE_TILES = (2 * TOKENS) // MOE_TM + N_EXPERTS
MOE_GTILES = MOE_TILES * (MOE_TM // MOE_GT)
MOE_PAIRS = MOE_GTILES + N_EXPERTS * MOE_NB
MOE_CPAIRS = MOE_TILES * (MOE_TM // MOE_CT) + N_EXPERTS * MOE_NB


def _moe_rank_kernel(comb_ref, tri_ref, rank_ref, cend_ref, carry_ref):
    @pl.when(pl.program_id(0) == 0)
    def _():
        carry_ref[...] = jnp.zeros((1, LANES), F32)

    routed = jnp.where(comb_ref[...] > 0.0, 1.0, 0.0).astype(BF16)
    carry = carry_ref[...]
    rank_ref[...] = carry + _dot(tri_ref[...], routed)
    carry = carry + jnp.sum(routed.astype(F32), axis=0, keepdims=True)
    carry_ref[...] = carry
    cend_ref[0] = jnp.broadcast_to(carry, (SUBLANES, LANES))


def _moe_rank(comb):
    idx = np.arange(MOE_SB)
    tri = jnp.asarray(idx[:, None] > idx[None, :], BF16)
    return pl.pallas_call(
        _moe_rank_kernel,
        grid=(MOE_NB,),
        in_specs=[pl.BlockSpec((MOE_SB, LANES), lambda b: (b, 0)),
                  pl.BlockSpec((MOE_SB, MOE_SB), lambda b: (0, 0))],
        out_specs=(pl.BlockSpec((MOE_SB, LANES), lambda b: (b, 0)),
                   pl.BlockSpec((1, SUBLANES, LANES), lambda b: (b, 0, 0))),
        out_shape=(jax.ShapeDtypeStruct((TOKENS, LANES), F32),
                   jax.ShapeDtypeStruct((MOE_NB, SUBLANES, LANES), F32)),
        scratch_shapes=[pltpu.VMEM((1, LANES), F32)],
        compiler_params=pltpu.CompilerParams(dimension_semantics=("arbitrary",)),
        name="moe_rank",
    )(comb, tri)


def _count_le(ends, v):
    return jnp.sum(ends[None, :] <= v[:, None], axis=1).astype(jnp.int32)


def _moe_schedule(cend):
    i32 = jnp.int32
    counts = cend[-1]
    cstart = jnp.concatenate([jnp.zeros((1, N_EXPERTS), i32), cend[:-1]], axis=0)
    tiles_e = (counts + MOE_TM - 1) // MOE_TM
    tile_end = jnp.cumsum(tiles_e)
    tile_start = tile_end - tiles_e
    n_tiles = tile_end[-1]
    group_start = tile_start * MOE_TM

    d = jnp.arange(MOE_TILES, dtype=i32)
    t_valid = d < n_tiles
    t_exp = jnp.minimum(_count_le(tile_end, d), N_EXPERTS - 1)
    t_exp = jnp.where(t_valid, t_exp, t_exp[jnp.maximum(n_tiles - 1, 0)])

    per_tile = MOE_TM // MOE_GT
    g = jnp.arange(MOE_GTILES, dtype=i32)
    g_exp = t_exp[g // per_tile]
    r0 = (g // per_tile - tile_start[g_exp]) * MOE_TM + (g % per_tile) * MOE_GT
    r1 = jnp.minimum(r0 + MOE_GT, counts[g_exp])
    gt_valid = t_valid[g // per_tile] & (r0 < counts[g_exp])
    b_lo = jnp.sum(cend[:, g_exp] <= r0[None, :], axis=0).astype(i32)
    b_hi = jnp.sum(cstart[:, g_exp] < r1[None, :], axis=0).astype(i32) - 1
    n_pairs = jnp.where(gt_valid, b_hi - b_lo + 1, 1)
    p_end = jnp.cumsum(n_pairs)
    p_start = p_end - n_pairs
    p = jnp.arange(MOE_PAIRS, dtype=i32)
    g_tile = jnp.minimum(_count_le(p_end, p), MOE_GTILES - 1)
    g_live = p < p_end[-1]
    g_valid = g_live & gt_valid[g_tile]
    idle_block = jnp.max(jnp.where(gt_valid, b_hi, 0))
    g_block = jnp.where(g_valid, b_lo[g_tile] + p - p_start[g_tile], idle_block)
    g_first = g_valid & (p == p_start[g_tile])
    g_zero = g_live & jnp.logical_not(gt_valid[g_tile])
    g_flag = g_valid.astype(i32) + 2 * g_first.astype(i32) + 4 * g_zero.astype(i32)
    gather = (g_tile, g_block, g_exp[g_tile], g_flag)

    lo = group_start[None, :] + cstart
    hi = group_start[None, :] + cend
    t_lo = (lo // MOE_CT).reshape(-1)
    n_t = jnp.where(hi > lo, (hi - 1) // MOE_CT - lo // MOE_CT + 1, 0).reshape(-1)
    q_end = jnp.cumsum(n_t)
    q_start = q_end - n_t
    p = jnp.arange(MOE_CPAIRS, dtype=i32)
    c_valid = p < q_end[-1]
    be = jnp.minimum(_count_le(q_end, p), MOE_NB * N_EXPERTS - 1)
    last = jnp.maximum(q_end[-1] - 1, 0)
    be = jnp.where(c_valid, be, be[last])
    c_tile = jnp.where(c_valid, t_lo[be] + p - q_start[be], (t_lo[be] + n_t[be] - 1))
    c_block = be // N_EXPERTS
    c_first = c_valid & ((p == 0) | (c_block != jnp.roll(c_block, 1)))
    c_flag = c_valid.astype(i32) + 2 * c_first.astype(i32)
    combine = (c_block, c_tile, be % N_EXPERTS, c_flag)
    return group_start, t_exp, t_valid.astype(i32), gather, combine


def _moe_gather_kernel(tile_ref, block_ref, exp_ref, flag_ref, h_ref, pos_ref, xs_ref):
    p = pl.program_id(0)
    flag = flag_ref[p]

    @pl.when((flag & 1) == 1)
    def _():
        rows = (tile_ref[p] * MOE_GT + lax.broadcasted_iota(jnp.int32, (MOE_GT, 1), 0)).astype(F32)
        pos = pos_ref[pl.ds(exp_ref[p], 1), :]
        onehot = jnp.where(rows == pos, 1.0, 0.0).astype(BF16)
        picked = _dot(onehot, h_ref[...]).astype(BF16)

        @pl.when((flag & 2) == 2)
        def _():
            xs_ref[...] = picked

        @pl.when((flag & 2) == 0)
        def _():
            xs_ref[...] += picked

    @pl.when((flag & 4) == 4)
    def _():
        xs_ref[...] = jnp.zeros((MOE_GT, D_MODEL), BF16)


def _moe_gather(sched, h, pos_t):
    spec = pltpu.PrefetchScalarGridSpec(
        num_scalar_prefetch=4,
        grid=(MOE_PAIRS,),
        in_specs=[pl.BlockSpec((MOE_SB, D_MODEL), lambda p, t, b, e, f: (b[p], 0)),
                  pl.BlockSpec((SUBLANES, MOE_SB), lambda p, t, b, e, f: (0, b[p]))],
        out_specs=pl.BlockSpec((MOE_GT, D_MODEL), lambda p, t, b, e, f: (t[p], 0)),
    )
    return pl.pallas_call(
        _moe_gather_kernel,
        grid_spec=spec,
        out_shape=jax.ShapeDtypeStruct((MOE_TILES * MOE_TM, D_MODEL), BF16),
        compiler_params=pltpu.CompilerParams(dimension_semantics=("arbitrary",)),
        name="moe_gather",
    )(*sched, h, pos_t)


def _moe_expert_kernel(exp_ref, valid_ref, xs_ref, w1_ref, w3_ref, w2_ref, y_ref, acc_ref):
    d = pl.program_id(0)
    f = pl.program_id(1)

    @pl.when(valid_ref[d] == 1)
    def _():
        x = xs_ref[...]
        a = _dot(x, w1_ref[0])
        act = (a * _sigmoid(a) * _dot(x, w3_ref[0])).astype(BF16)
        part = _dot(act, w2_ref[0])

        @pl.when(f == 0)
        def _():
            acc_ref[...] = part

        @pl.when(f > 0)
        def _():
            acc_ref[...] += part

        @pl.when(f == pl.num_programs(1) - 1)
        def _():
            y_ref[...] = acc_ref[...].astype(BF16)

    @pl.when((valid_ref[d] == 0) & (f == 0))
    def _():
        y_ref[...] = jnp.zeros((MOE_TM, D_MODEL), BF16)


def _moe_experts(t_exp, t_valid, xs, w1, w3, w2):
    tf = MOE_TF
    nf = D_FF_EXPERT // tf
    fidx = lambda d, f, v: f * v[d] + (nf - 1) * (1 - v[d])
    spec = pltpu.PrefetchScalarGridSpec(
        num_scalar_prefetch=2,
        grid=(MOE_TILES, nf),
        in_specs=[pl.BlockSpec((MOE_TM, D_MODEL), lambda d, f, e, v: (d, 0)),
                  pl.BlockSpec((1, D_MODEL, tf), lambda d, f, e, v: (e[d], 0, fidx(d, f, v))),
                  pl.BlockSpec((1, D_MODEL, tf), lambda d, f, e, v: (e[d], 0, fidx(d, f, v))),
                  pl.BlockSpec((1, tf, D_MODEL), lambda d, f, e, v: (e[d], fidx(d, f, v), 0))],
        out_specs=pl.BlockSpec((MOE_TM, D_MODEL), lambda d, f, e, v: (d, 0)),
        scratch_shapes=[pltpu.VMEM((MOE_TM, D_MODEL), F32)],
    )
    return pl.pallas_call(
        _moe_expert_kernel,
        grid_spec=spec,
        out_shape=jax.ShapeDtypeStruct((MOE_TILES * MOE_TM, D_MODEL), BF16),
        compiler_params=pltpu.CompilerParams(
            dimension_semantics=("arbitrary", "arbitrary"), vmem_limit_bytes=VMEM_LIMIT),
        name="moe_experts",
    )(t_exp, t_valid, xs, w1, w3, w2)


def _moe_combine_kernel(block_ref, tile_ref, exp_ref, flag_ref, y_ref, pos_ref, comb_ref, x1_ref, o_ref):
    p = pl.program_id(0)
    flag = flag_ref[p]

    @pl.when((flag & 1) == 1)
    def _():
        pick = lax.broadcasted_iota(jnp.int32, (1, LANES), 1) == exp_ref[p]
        pos = jnp.sum(jnp.where(pick, pos_ref[...], 0.0), axis=-1, keepdims=True)
        wgt = jnp.sum(jnp.where(pick, comb_ref[...], 0.0), axis=-1, keepdims=True)
        cols = (tile_ref[p] * MOE_CT + lax.broadcasted_iota(jnp.int32, (1, MOE_CT), 1)).astype(F32)
        onehot = jnp.where(pos == cols, 1.0, 0.0).astype(BF16)
        part = wgt * _dot(onehot, y_ref[...])

        @pl.when((flag & 2) == 2)
        def _():
            o_ref[...] = x1_ref[...] + part

        @pl.when((flag & 2) == 0)
        def _():
            o_ref[...] += part


def _moe_combine(sched, y, pos, comb, x1):
    tok = lambda width: pl.BlockSpec((MOE_SB, width), lambda p, b, t, e, f: (b[p], 0))
    spec = pltpu.PrefetchScalarGridSpec(
        num_scalar_prefetch=4,
        grid=(MOE_CPAIRS,),
        in_specs=[pl.BlockSpec((MOE_CT, D_MODEL), lambda p, b, t, e, f: (t[p], 0)),
                  tok(LANES), tok(LANES), tok(D_MODEL)],
        out_specs=tok(D_MODEL),
    )
    return pl.pallas_call(
        _moe_combine_kernel,
        grid_spec=spec,
        out_shape=jax.ShapeDtypeStruct((TOKENS, D_MODEL), F32),
        compiler_params=pltpu.CompilerParams(
            dimension_semantics=("arbitrary",), vmem_limit_bytes=VMEM_LIMIT),
        name="moe_combine",
    )(*sched, y, pos, comb, x1)


def _moe_ffn(x1, h, comb, w1, w3, w2):
    rank, cend = _moe_rank(comb)
    cend = jnp.round(cend[:, 0, :N_EXPERTS]).astype(jnp.int32)
    group_start, t_exp, t_valid, gather, combine = _moe_schedule(cend)
    start = jnp.zeros((LANES,), F32).at[:N_EXPERTS].set(group_start.astype(F32))
    pos = jnp.where(comb > 0.0, start[None, :] + rank, -1.0)
    xs = _moe_gather(gather, h, pos[:, :SUBLANES].T)
    y = _moe_experts(t_exp, t_valid, xs, w1, w3, w2)
    return _moe_combine(combine, y, pos, comb, x1)


def _group_mean_matrix():
    idx = np.arange(LANES)
    return jnp.asarray((idx[:, None] // CONV_GROUP == idx[None, :] // CONV_GROUP) / CONV_GROUP, BF16)


def _overlap_matrix_t():
    c_start = np.arange(CMP_ROWS) * CMP_STRIDE
    j = np.arange(N_SEL)
    ovl = ((c_start[None, :] < (j[:, None] + 1) * SEL_BLOCK)
           & (c_start[None, :] + CMP_BLOCK > j[:, None] * SEL_BLOCK)
           & (np.arange(CMP_ROWS)[None, :] < N_CMP))
    return jnp.asarray(ovl, BF16)


def _dup(v):
    return jnp.concatenate([v, v], axis=-1)


def _zero_pad(v):
    return jnp.concatenate([v, jnp.zeros_like(v)], axis=-1)


def kernel(x, attn_norm_g, w_in, conv_dw_w, conv_dw_b, conv_ln_g, conv_ln_b, q_norm_g, k_norm_g,
           cmp_pos_k, cmp_pos_v, cmp_k_w1, cmp_k_b1, cmp_k_w2, cmp_k_b2,
           cmp_v_w1, cmp_v_b1, cmp_v_w2, cmp_v_b2, w_out, ffn_norm_g,
           dense_w1, dense_w3, dense_w2, router_w, moe_w1, moe_w3, moe_w2):
    assert x.shape == (BATCH, SEQ, D_MODEL) and x.dtype == F32
    bd = _group_mean_matrix()
    ovlt = _overlap_matrix_t()

    x2 = x.reshape(TOKENS, D_MODEL)
    for layer in range(DEPTH):
        w_cols, w_rows = _in_proj_weights(w_in[layer])
        qg = _dup(q_norm_g[layer])[None, :] * (LOG2_E * HEAD_DIM ** -0.5)
        kgs = _dup(k_norm_g[layer, 1])[None, :]
        kgw = _dup(k_norm_g[layer, 2])[None, :]
        u, q, kcvc, ks, kw, vst, vwt, gatet = _in_proj(
            x2, attn_norm_g[layer][None, :], w_cols, w_rows, qg, kgs, kgw, bd)

        conv = _conv_module(u.reshape(BATCH, SEQ, CONV_WIDTH), conv_dw_w[layer], conv_dw_b[layer][None, :],
                            conv_ln_g[layer][None, :], conv_ln_b[layer][None, :], bd)

        xr = kcvc.reshape(BATCH, SEQ, 2, N_KV_HEADS, HEAD_DIM).transpose(2, 0, 3, 1, 4)
        xr = xr.reshape(2, BATCH, N_KV_HEADS, CMP_ROWS, CMP_CHUNK)
        pos = jnp.stack([cmp_pos_k[layer], cmp_pos_v[layer]]).reshape(2, 1, CMP_BLOCK * HEAD_DIM)
        w1 = jnp.stack([cmp_k_w1[layer], cmp_v_w1[layer]]).astype(BF16)
        b1 = jnp.stack([cmp_k_b1[layer], cmp_v_b1[layer]])[:, None, :]
        w2 = _zero_pad(jnp.stack([cmp_k_w2[layer], cmp_v_w2[layer]])).astype(BF16)
        b2 = _zero_pad(jnp.stack([cmp_k_b2[layer], cmp_v_b2[layer]]))[:, None, :]
        kvc = _compress(xr, pos, w1, b1, w2, b2, _zero_pad(k_norm_g[layer, 0])[None, :])

        seq3 = lambda a: a.reshape(BATCH, SEQ, a.shape[-1])
        attn = _attention(seq3(q), gatet, kvc, seq3(ks), vst, seq3(kw), vwt, ovlt)

        w_o = w_out[layer].astype(BF16)
        conv2 = conv.reshape(TOKENS, CONV_WIDTH)
        attn2 = attn.reshape(TOKENS, ATTN_WIDTH)
        g_ffn = ffn_norm_g[layer][None, :]
        i = layer // 2
        if layer % 2 == 0:
            x1, h = _out_proj(x2, conv2, attn2, w_o, g_ffn)
            x2 = _dense_ffn(x1, h, dense_w1[i].astype(BF16), dense_w3[i].astype(BF16),
                            dense_w2[i].astype(BF16))
        else:
            r = jnp.pad(router_w[i], ((0, 0), (0, LANES - N_EXPERTS)))
            r_hi = r.astype(BF16)
            r_lo = (r - r_hi.astype(F32)).astype(BF16)
            x1, h, comb = _out_proj(x2, conv2, attn2, w_o, g_ffn, jnp.stack([r_hi, r_lo]))
            x2 = _moe_ffn(x1, h, comb, moe_w1[i].astype(BF16), moe_w3[i].astype(BF16),
                          moe_w2[i].astype(BF16))
    return x2.reshape(BATCH, SEQ, D_MODEL)
```

```python
import functools

import numpy as np
import jax
import jax.numpy as jnp
from jax import lax
from jax.experimental import pallas as pl
from jax.experimental.pallas import tpu as pltpu

D_MODEL = 1024
BATCH = 8
SEQ = 2048
DEPTH = 2
TOKENS = BATCH * SEQ

CONV_WIDTH = 512
CONV_GROUP = 64
CONV_KERNEL = 31
N_HEADS = 8
HEAD_DIM = 64
N_KV_HEADS = 2
GQA = N_HEADS // N_KV_HEADS
ATTN_WIDTH = N_HEADS * HEAD_DIM
KV_WIDTH = N_KV_HEADS * HEAD_DIM
N_BRANCH = 3
CMP_BLOCK = 32
CMP_STRIDE = 16
CMP_HIDDEN = 256
N_CMP = (SEQ - CMP_BLOCK) // CMP_STRIDE + 1
SEL_BLOCK = 64
SEL_TOPK = 8
N_SEL = SEQ // SEL_BLOCK
WINDOW = 512
Q_BLOCK = 256
FORCE_BONUS = 1e4
NEG_INF = -1e30
LOG2_E = 1.4426950408889634
D_FF = 2816
N_EXPERTS = 8
D_FF_EXPERT = 3584
EPS = 1e-6

LANES = 128
SUBLANES = 8
VMEM_LIMIT = 48 * 1024 * 1024

F32 = jnp.float32
BF16 = jnp.bfloat16

_OFF_VAL = 0
_OFF_GATE = 512
_OFF_Q = 1024
_OFF_KCVC = 1536
_OFF_KS = 1792
_OFF_KW = 1920
IN_COLS = 2048
T_ROWS = 5 * LANES

_BASE_KV = 2 * CONV_WIDTH + ATTN_WIDTH


def _in_proj_weights(w):
    piece = lambda i: w[:, _BASE_KV + i * KV_WIDTH:_BASE_KV + (i + 1) * KV_WIDTH]
    w_cols = jnp.concatenate([w[:, :_BASE_KV + 2 * KV_WIDTH], piece(2), piece(4)], axis=1)
    assert w_cols.shape[1] == IN_COLS
    rows = []
    for i in (3, 5):
        vt = piece(i).T
        for head in range(N_KV_HEADS):
            rows += [vt[head * HEAD_DIM:(head + 1) * HEAD_DIM], jnp.zeros((HEAD_DIM, D_MODEL), w.dtype)]
    n_gate = N_HEADS * N_BRANCH
    glog = _BASE_KV + 6 * KV_WIDTH
    rows += [w[:, glog:glog + n_gate].T, jnp.zeros((LANES - n_gate, D_MODEL), w.dtype)]
    w_rows = jnp.concatenate(rows, axis=0)
    assert w_rows.shape[0] == T_ROWS
    return w_cols.astype(BF16), w_rows.astype(BF16)


def _sigmoid(v):
    return 1.0 / (1.0 + jnp.exp(-v))


def _dot(a, b):
    return jnp.dot(a, b, preferred_element_type=F32)


def _dot_nt(a, b):
    return lax.dot_general(a, b, (((1,), (1,)), ((), ())), preferred_element_type=F32)


def _split_dot(v, m):
    hi = v.astype(BF16)
    lo = (v - hi.astype(F32)).astype(BF16)
    return _dot(hi, m) + _dot(lo, m)


IN_TM = 512


def _in_proj_kernel(x_ref, g_ref, w_ref, wt_ref, qg_ref, kgs_ref, kgw_ref, bd_ref,
                    u_ref, q_ref, kcvc_ref, ks_ref, kw_ref, vst_ref, vwt_ref, gatet_ref):
    x = x_ref[...]
    ms = jnp.mean(x * x, axis=-1, keepdims=True)
    h = (x * lax.rsqrt(ms + EPS) * g_ref[...]).astype(BF16)

    def proj(lo, width):
        return _dot(h, w_ref[:, lo:lo + width])

    u_ref[...] = proj(_OFF_VAL, CONV_WIDTH) * _sigmoid(proj(_OFF_GATE, CONV_WIDTH))

    bd = bd_ref[...]
    lane = lax.broadcasted_iota(jnp.int32, (IN_TM, LANES), 1)
    lower = lane < HEAD_DIM

    def head_slots(pair, upper, out_ref, slot):
        out_ref[:, slot * LANES:(slot + 1) * LANES] = jnp.where(lower, pair, upper).astype(BF16)
        swapped = pltpu.roll(pair, HEAD_DIM, axis=1)
        out_ref[:, (slot + 1) * LANES:(slot + 2) * LANES] = jnp.where(lower, swapped, upper).astype(BF16)

    q = proj(_OFF_Q, ATTN_WIDTH)
    for c in range(ATTN_WIDTH // LANES):
        qc = q[:, c * LANES:(c + 1) * LANES]
        msq = _split_dot(qc * qc, bd)
        head_slots(qc * lax.rsqrt(msq + EPS) * qg_ref[...], 0.0, q_ref, 2 * c)

    kcvc_ref[...] = proj(_OFF_KCVC, 2 * KV_WIDTH)

    def normed_k(off, kg_ref):
        k = proj(off, KV_WIDTH)
        return k * lax.rsqrt(_split_dot(k * k, bd) + EPS) * kg_ref[...]

    tok = (pl.program_id(0) % (SEQ // IN_TM)) * IN_TM + lax.broadcasted_iota(jnp.int32, (IN_TM, 1), 0)
    block_hot = jnp.where(lane - HEAD_DIM == tok // SEL_BLOCK, 1.0, 0.0)
    head_slots(normed_k(_OFF_KS, kgs_ref), block_hot, ks_ref, 0)
    head_slots(normed_k(_OFF_KW, kgw_ref), 0.0, kw_ref, 0)

    zt = _dot_nt(wt_ref[...], h)
    ones_rows = lax.broadcasted_iota(jnp.int32, (LANES, IN_TM), 0) >= HEAD_DIM
    for j, out_ref in enumerate((vst_ref, vst_ref, vwt_ref, vwt_ref)):
        blk = zt[j * LANES:(j + 1) * LANES]
        out_ref[0, j % N_KV_HEADS] = jnp.where(ones_rows, 1.0, blk).astype(BF16)
    gatet_ref[0] = _sigmoid(zt[4 * LANES:5 * LANES])


def _in_proj(x2, g, w, wt, qg, kgs, kgw, bd):
    tm = IN_TM
    per_seq = SEQ // tm
    row = lambda width: pl.BlockSpec((tm, width), lambda i: (i, 0))
    full = lambda a: pl.BlockSpec(a.shape, lambda i: (0,) * a.ndim)
    vt_spec = pl.BlockSpec((1, N_KV_HEADS, LANES, tm), lambda i: (i // per_seq, 0, 0, i % per_seq))
    vt_shape = jax.ShapeDtypeStruct((BATCH, N_KV_HEADS, LANES, SEQ), BF16)
    out_shape = (
        jax.ShapeDtypeStruct((TOKENS, CONV_WIDTH), F32),
        jax.ShapeDtypeStruct((TOKENS, N_HEADS * LANES), BF16),
        jax.ShapeDtypeStruct((TOKENS, 2 * KV_WIDTH), F32),
        jax.ShapeDtypeStruct((TOKENS, 2 * LANES), BF16),
        jax.ShapeDtypeStruct((TOKENS, 2 * LANES), BF16),
        vt_shape,
        vt_shape,
        jax.ShapeDtypeStruct((BATCH, LANES, SEQ), F32),
    )
    out_specs = (row(CONV_WIDTH), row(N_HEADS * LANES), row(2 * KV_WIDTH), row(2 * LANES), row(2 * LANES),
                 vt_spec, vt_spec,
                 pl.BlockSpec((1, LANES, tm), lambda i: (i // per_seq, 0, i % per_seq)))
    return pl.pallas_call(
        _in_proj_kernel,
        grid=(TOKENS // tm,),
        in_specs=[row(D_MODEL), full(g), full(w), full(wt), full(qg), full(kgs), full(kgw), full(bd)],
        out_specs=out_specs,
        out_shape=out_shape,
        compiler_params=pltpu.CompilerParams(
            dimension_semantics=("arbitrary",), vmem_limit_bytes=VMEM_LIMIT),
        name="in_proj",
    )(x2, g, w, wt, qg, kgs, kgw, bd)


CONV_TR = 512
CONV_HALO = 32
CONV_RS = 64
CONV_PARTIALS = 4


def _conv_kernel(ucur_ref, uprev_ref, w_ref, b_ref, lg_ref, lb_ref, bd_ref, o_ref, win_ref, y_ref):
    i = pl.program_id(1)
    first = CONV_HALO - (CONV_KERNEL - 1)
    shifted_rows = CONV_HALO + CONV_TR - SUBLANES
    for c in range(CONV_WIDTH // LANES):
        lanes = slice(c * LANES, (c + 1) * LANES)
        win_ref[0, c, 0:CONV_HALO, :] = jnp.where(i > 0, uprev_ref[0, :, lanes], 0.0)
        win_ref[0, c, CONV_HALO:CONV_HALO + CONV_TR, :] = ucur_ref[0, :, lanes]
        for s in range(1, SUBLANES):
            win_ref[s, c, 0:shifted_rows, :] = win_ref[0, c, s:s + shifted_rows, :]

        def body(r, carry):
            base = pl.multiple_of(r * CONV_RS, CONV_RS)
            accs = [None] * CONV_PARTIALS
            for k in range(CONV_KERNEL):
                off = first + k
                rows = pl.ds(base + (off // SUBLANES) * SUBLANES, CONV_RS)
                term = win_ref[off % SUBLANES, c, rows, :] * w_ref[k:k + 1, lanes]
                j = k % CONV_PARTIALS
                accs[j] = term if accs[j] is None else accs[j] + term
            y_ref[c, pl.ds(base, CONV_RS), :] = (accs[0] + accs[1]) + (accs[2] + accs[3]) + b_ref[:, lanes]
            return carry

        lax.fori_loop(0, CONV_TR // CONV_RS, body, 0)

    bd = bd_ref[...]
    for c in range(CONV_WIDTH // LANES):
        lanes = slice(c * LANES, (c + 1) * LANES)
        y = y_ref[c]
        d = y - _split_dot(y, bd)
        var = _split_dot(d * d, bd)
        yn = d * lax.rsqrt(var + EPS) * lg_ref[:, lanes] + lb_ref[:, lanes]
        o_ref[0, :, lanes] = (yn * _sigmoid(yn)).astype(BF16)


def _conv_module(u3, w, b, lg, lb, bd):
    halo_blocks = CONV_TR // CONV_HALO
    full = lambda a: pl.BlockSpec(a.shape, lambda bi, i: (0,) * a.ndim)
    return pl.pallas_call(
        _conv_kernel,
        grid=(BATCH, SEQ // CONV_TR),
        in_specs=[
            pl.BlockSpec((1, CONV_TR, CONV_WIDTH), lambda bi, i: (bi, i, 0)),
            pl.BlockSpec((1, CONV_HALO, CONV_WIDTH),
                         lambda bi, i: (bi, jnp.maximum(i * halo_blocks - 1, 0), 0)),
            full(w), full(b), full(lg), full(lb), full(bd),
        ],
        out_specs=pl.BlockSpec((1, CONV_TR, CONV_WIDTH), lambda bi, i: (bi, i, 0)),
        out_shape=jax.ShapeDtypeStruct((BATCH, SEQ, CONV_WIDTH), BF16),
        scratch_shapes=[pltpu.VMEM((SUBLANES, CONV_WIDTH // LANES, CONV_HALO + CONV_TR, LANES), F32),
                        pltpu.VMEM((CONV_WIDTH // LANES, CONV_TR, LANES), F32)],
        compiler_params=pltpu.CompilerParams(dimension_semantics=("arbitrary", "arbitrary")),
        name="conv_module",
    )(u3, u3, w, b, lg, lb, bd)


CMP_ROWS = SEQ // CMP_STRIDE
CMP_CHUNK = CMP_STRIDE * HEAD_DIM


def _compress_kernel(x_ref, pos_ref, w1_ref, b1_ref, w2_ref, b2_ref, w2t_ref, b2t_ref, kg_ref, o_ref):
    kind = pl.program_id(0)
    xa = x_ref[0, 0, 0]
    xb = pltpu.roll(xa, CMP_ROWS - 1, axis=0)
    a = (xa + pos_ref[0, :, 0:CMP_CHUNK]).astype(BF16)
    b = (xb + pos_ref[0, :, CMP_CHUNK:2 * CMP_CHUNK]).astype(BF16)
    hid = _dot(a, w1_ref[0, 0:CMP_CHUNK, :]) + _dot(b, w1_ref[0, CMP_CHUNK:2 * CMP_CHUNK, :]) + b1_ref[0]
    hid = (hid * _sigmoid(hid)).astype(BF16)
    out = _dot(hid, w2_ref[0]) + b2_ref[0]
    ms = jnp.sum(out * out, axis=-1, keepdims=True) * (1.0 / HEAD_DIM)
    normed = out * lax.rsqrt(ms + EPS) * kg_ref[...]
    out_t = _dot_nt(w2t_ref[0], hid) + b2t_ref[0]
    o_ref[0, 0, 0] = jnp.where(kind == 0, normed, out_t).astype(BF16)


def _compress(xr, pos, w1, b1, w2, b2, kg):
    w2t = jnp.swapaxes(w2, 1, 2)
    b2t = jnp.swapaxes(b2, 1, 2)
    per_kind = lambda a: pl.BlockSpec((1,) + a.shape[1:], lambda k, bi, h: (k,) + (0,) * (a.ndim - 1))
    return pl.pallas_call(
        _compress_kernel,
        grid=(2, BATCH, N_KV_HEADS),
        in_specs=[
            pl.BlockSpec((1, 1, 1, CMP_ROWS, CMP_CHUNK), lambda k, bi, h: (k, bi, h, 0, 0)),
            per_kind(pos), per_kind(w1), per_kind(b1), per_kind(w2), per_kind(b2),
            per_kind(w2t), per_kind(b2t),
            pl.BlockSpec(kg.shape, lambda k, bi, h: (0, 0)),
        ],
        out_specs=pl.BlockSpec((1, 1, 1, CMP_ROWS, LANES), lambda k, bi, h: (k, bi, h, 0, 0)),
        out_shape=jax.ShapeDtypeStruct((2, BATCH, N_KV_HEADS, CMP_ROWS, LANES), BF16),
        compiler_params=pltpu.CompilerParams(
            dimension_semantics=("arbitrary", "arbitrary", "arbitrary")),
        name="compress",
    )(xr, pos, w1, b1, w2, b2, w2t, b2t, kg)


QCOLS = GQA * Q_BLOCK
SEL_CHUNK = 256
WIN_CHUNK = 128
WIN_CHUNKS = (WINDOW + Q_BLOCK) // WIN_CHUNK
MASK_BIAS = -1e30


def _window_biases():
    k = np.arange(WIN_CHUNK)[:, None]
    t = (np.arange(QCOLS) % Q_BLOCK)[None, :]
    tables, index = [], {}
    for i in range(WIN_CHUNKS):
        key = i * WIN_CHUNK - WINDOW + k
        bias = np.where((key <= t) & (key > t - WINDOW), 0.0, MASK_BIAS)
        if bias.any():
            index[i] = len(tables)
            tables.append(bias)
    return np.stack(tables).astype(np.float32), index


_WIN_BIAS, _WIN_BIAS_INDEX = _window_biases()


def _attn_kernel(q_ref, gatet_ref, kc_ref, vct_ref, ks_ref, vst_ref, kw_ref, vwt_ref,
                 ovlt_ref, wbias_ref, o_ref, ssel_sc, swin_sc, acc_sc, out_sc):
    n = pl.program_id(1)
    start = n * Q_BLOCK
    col = lax.broadcasted_iota(jnp.int32, (1, QCOLS), 1)
    t_cols = start + (col & (Q_BLOCK - 1))
    tq = start + lax.broadcasted_iota(jnp.int32, (1, Q_BLOCK), 1)
    blk_row = lax.broadcasted_iota(jnp.int32, (N_SEL, 1), 0)
    blk_row_f = blk_row.astype(F32)
    cmp_end = lax.broadcasted_iota(jnp.int32, (CMP_ROWS, 1), 0) * CMP_STRIDE + (CMP_BLOCK - 1)

    kv_heads = range(N_KV_HEADS)
    kv_lanes = [slice(kvh * LANES, (kvh + 1) * LANES) for kvh in kv_heads]

    def gate_row(kvh, g, branch):
        r = (kvh * GQA + g) * N_BRANCH + branch
        return gatet_ref[0, r:r + 1, :]

    def head_rows(kvh, g):
        h = kvh * GQA + g
        return slice(h * HEAD_DIM, (h + 1) * HEAD_DIM)

    qs = []
    for kvh in kv_heads:
        h0 = kvh * GQA
        qs.append(jnp.concatenate(
            [q_ref[0, :, (h0 + g) * LANES:(h0 + g + 1) * LANES] for g in range(GQA)], axis=0))

    imp = []
    mask_c = cmp_end <= t_cols
    for kvh in kv_heads:
        s_c = jnp.where(mask_c, _dot_nt(kc_ref[0, kvh], qs[kvh]), NEG_INF)
        m_c = jnp.max(s_c, axis=0, keepdims=True)
        p_c = jnp.where(mask_c, jnp.exp2(s_c - m_c), 0.0)
        p_c = p_c / jnp.maximum(jnp.sum(p_c, axis=0, keepdims=True), 1e-30)
        o_cmp = _dot(vct_ref[0, kvh], p_c.astype(BF16))
        for g in range(GQA):
            out_sc[head_rows(kvh, g), :] = gate_row(kvh, g, 0) * o_cmp[0:HEAD_DIM, g * Q_BLOCK:(g + 1) * Q_BLOCK]
        p_sum = p_c[:, 0:Q_BLOCK]
        for g in range(1, GQA):
            p_sum = p_sum + p_c[:, g * Q_BLOCK:(g + 1) * Q_BLOCK]
        p_hi = p_sum.astype(BF16)
        p_lo = (p_sum - p_hi.astype(F32)).astype(BF16)
        imp.append(_dot(ovlt_ref[...], p_hi) + _dot(ovlt_ref[...], p_lo))

    tq2 = jnp.concatenate([tq] * N_KV_HEADS, axis=1)
    cur = tq2 >> 6
    forced = (blk_row == 0) | (blk_row == cur) | (blk_row == cur - 1)
    causal_blk = blk_row * SEL_BLOCK <= tq2
    score = jnp.where(causal_blk, jnp.concatenate(imp, axis=1) + jnp.where(forced, FORCE_BONUS, 0.0), NEG_INF)
    sel = jnp.zeros((N_SEL, N_KV_HEADS * Q_BLOCK), F32)
    for _ in range(SEL_TOPK):
        best = jnp.max(score, axis=0, keepdims=True)
        idx = jnp.min(jnp.where(score == best, blk_row_f, float(LANES)), axis=0, keepdims=True)
        hit = blk_row_f == idx
        sel = jnp.where(hit, 1.0, sel)
        score = jnp.where(hit, -3e38, score)
    sel_bias = (sel - 1.0) * (-MASK_BIAS)
    qs_sel = []
    for kvh in kv_heads:
        rows = jnp.concatenate([jnp.zeros((HEAD_DIM, Q_BLOCK), F32),
                                sel_bias[:, kvh * Q_BLOCK:(kvh + 1) * Q_BLOCK],
                                jnp.zeros((LANES - HEAD_DIM - N_SEL, Q_BLOCK), F32)], axis=0)
        bias_q = rows.T.astype(BF16)
        qs_sel.append(qs[kvh] + jnp.concatenate([bias_q] * GQA, axis=0))

    for kvh in kv_heads:
        mxw = jnp.full((1, QCOLS), NEG_INF, F32)
        kbs = [n * (Q_BLOCK // WIN_CHUNK) - WINDOW // WIN_CHUNK + i for i in range(WIN_CHUNKS)]
        k_win = [pl.multiple_of(jnp.maximum(kb, 0) * WIN_CHUNK, WIN_CHUNK) for kb in kbs]
        k_band = jnp.concatenate([kw_ref[0, pl.ds(k0, WIN_CHUNK), kv_lanes[kvh]] for k0 in k_win], axis=0)
        s_band = _dot_nt(k_band, qs[kvh])
        for i, kb in enumerate(kbs):
            s = s_band[i * WIN_CHUNK:(i + 1) * WIN_CHUNK]
            if i in _WIN_BIAS_INDEX:
                s = s + wbias_ref[_WIN_BIAS_INDEX[i]]
            if i < WINDOW // WIN_CHUNK:
                s = s + jnp.where(kb >= 0, 0.0, MASK_BIAS)
            swin_sc[kvh, i * WIN_CHUNK:(i + 1) * WIN_CHUNK, :] = s
            mxw = jnp.maximum(mxw, jnp.max(s, axis=0, keepdims=True))
        p = jnp.exp2(swin_sc[kvh] - mxw).astype(BF16)
        v_band = jnp.concatenate([vwt_ref[0, kvh, :, pl.ds(k0, WIN_CHUNK)] for k0 in k_win], axis=1)
        accw = _dot(v_band, p)
        o_win = accw[0:HEAD_DIM] * (1.0 / accw[HEAD_DIM:HEAD_DIM + 1])
        for g in range(GQA):
            out_sc[head_rows(kvh, g), :] += gate_row(kvh, g, 2) * o_win[:, g * Q_BLOCK:(g + 1) * Q_BLOCK]

    def sel_scores(kvh, k0):
        return _dot_nt(ks_ref[0, pl.ds(k0, SEL_CHUNK), kv_lanes[kvh]], qs_sel[kvh])

    def sel_pass1(i, mx):
        k0 = pl.multiple_of(i * SEL_CHUNK, SEL_CHUNK)
        out = []
        for kvh in kv_heads:
            s = sel_scores(kvh, k0)
            ssel_sc[kvh, pl.ds(k0, SEL_CHUNK), :] = s
            out.append(jnp.maximum(mx[kvh], jnp.max(s, axis=0, keepdims=True)))
        return tuple(out)

    last = (start + Q_BLOCK - 1) // SEL_CHUNK
    mx = lax.fori_loop(0, last, sel_pass1, (jnp.full((1, QCOLS), NEG_INF, F32),) * N_KV_HEADS)
    k_last = pl.multiple_of(last * SEL_CHUNK, SEL_CHUNK)
    causal = k_last + lax.broadcasted_iota(jnp.int32, (SEL_CHUNK, 1), 0) <= t_cols
    mx = list(mx)
    for kvh in kv_heads:
        s = jnp.where(causal, sel_scores(kvh, k_last), NEG_INF)
        ssel_sc[kvh, pl.ds(k_last, SEL_CHUNK), :] = s
        mx[kvh] = jnp.maximum(mx[kvh], jnp.max(s, axis=0, keepdims=True))
        acc_sc[kvh] = jnp.zeros((LANES, QCOLS), F32)

    def sel_pass2(i, carry):
        k0 = pl.multiple_of(i * SEL_CHUNK, SEL_CHUNK)
        for kvh in kv_heads:
            p = jnp.exp2(ssel_sc[kvh, pl.ds(k0, SEL_CHUNK), :] - mx[kvh]).astype(BF16)
            acc_sc[kvh] += _dot(vst_ref[0, kvh, :, pl.ds(k0, SEL_CHUNK)], p)
        return carry

    lax.fori_loop(0, last + 1, sel_pass2, 0)
    for kvh in kv_heads:
        acc = acc_sc[kvh]
        o_sel = acc[0:HEAD_DIM] * (1.0 / acc[HEAD_DIM:HEAD_DIM + 1])
        for g in range(GQA):
            out_sc[head_rows(kvh, g), :] += gate_row(kvh, g, 1) * o_sel[:, g * Q_BLOCK:(g + 1) * Q_BLOCK]

    for c in range(N_HEADS // 2):
        o_ref[0, :, c * LANES:(c + 1) * LANES] = out_sc[c * LANES:(c + 1) * LANES, :].T.astype(BF16)


def _attention(q3, gatet, kvc, ks3, vst, kw3, vwt, ovlt):
    wbias = jnp.asarray(_WIN_BIAS)
    qblk = lambda width: pl.BlockSpec((1, Q_BLOCK, width), lambda bi, n: (bi, n, 0))
    seq = pl.BlockSpec((1, SEQ, 2 * LANES), lambda bi, n: (bi, 0, 0))
    seq_t = pl.BlockSpec((1, N_KV_HEADS, LANES, SEQ), lambda bi, n: (bi, 0, 0, 0))
    cmp_spec = lambda kind: pl.BlockSpec((None, 1, N_KV_HEADS, CMP_ROWS, LANES),
                                         lambda bi, n: (kind, bi, 0, 0, 0))
    full = lambda a: pl.BlockSpec(a.shape, lambda bi, n: (0,) * a.ndim)
    return pl.pallas_call(
        _attn_kernel,
        grid=(BATCH, SEQ // Q_BLOCK),
        in_specs=[qblk(N_HEADS * LANES), pl.BlockSpec((1, LANES, Q_BLOCK), lambda bi, n: (bi, 0, n)),
                  cmp_spec(0), cmp_spec(1), seq, seq_t, seq, seq_t, full(ovlt), full(wbias)],
        out_specs=qblk(ATTN_WIDTH),
        out_shape=jax.ShapeDtypeStruct((BATCH, SEQ, ATTN_WIDTH), BF16),
        scratch_shapes=[pltpu.VMEM((N_KV_HEADS, SEQ, QCOLS), F32),
                        pltpu.VMEM((N_KV_HEADS, WIN_CHUNKS * WIN_CHUNK, QCOLS), F32),
                        pltpu.VMEM((N_KV_HEADS, LANES, QCOLS), F32),
                        pltpu.VMEM((ATTN_WIDTH, Q_BLOCK), F32)],
        compiler_params=pltpu.CompilerParams(
            dimension_semantics=("arbitrary", "arbitrary"), vmem_limit_bytes=VMEM_LIMIT),
        name="nsa_attention",
    )(q3, gatet, kvc, kvc, ks3, vst, kw3, vwt, ovlt, wbias)


OUT_TM = 512


def _out_proj_kernel(x_ref, conv_ref, attn_ref, w_ref, g_ref, x1_ref, h_ref):
    x1 = (x_ref[...] + _dot(conv_ref[...], w_ref[0:CONV_WIDTH, :])
          + _dot(attn_ref[...], w_ref[CONV_WIDTH:CONV_WIDTH + ATTN_WIDTH, :]))
    x1_ref[...] = x1
    ms = jnp.mean(x1 * x1, axis=-1, keepdims=True)
    h_ref[...] = (x1 * lax.rsqrt(ms + EPS) * g_ref[...]).astype(BF16)


def _out_proj_router_kernel(x_ref, conv_ref, attn_ref, w_ref, g_ref, r_ref, x1_ref, h_ref, comb_ref):
    x1 = (x_ref[...] + _dot(conv_ref[...], w_ref[0:CONV_WIDTH, :])
          + _dot(attn_ref[...], w_ref[CONV_WIDTH:CONV_WIDTH + ATTN_WIDTH, :]))
    x1_ref[...] = x1
    ms = jnp.mean(x1 * x1, axis=-1, keepdims=True)
    h = x1 * lax.rsqrt(ms + EPS) * g_ref[...]
    h_ref[...] = h.astype(BF16)
    h_hi = h.astype(BF16)
    h_lo = (h - h_hi.astype(F32)).astype(BF16)
    logits = _dot(h_hi, r_ref[0]) + _dot(h_lo, r_ref[0]) + _dot(h_hi, r_ref[1])
    lane = lax.broadcasted_iota(jnp.int32, (1, LANES), 1).astype(F32)
    logits = jnp.where(lane < N_EXPERTS, logits, NEG_INF)
    v1 = jnp.max(logits, axis=-1, keepdims=True)
    i1 = jnp.min(jnp.where(logits == v1, lane, float(LANES)), axis=-1, keepdims=True)
    hit1 = lane == i1
    rest = jnp.where(hit1, -3e38, logits)
    v2 = jnp.max(rest, axis=-1, keepdims=True)
    i2 = jnp.min(jnp.where(rest == v2, lane, float(LANES)), axis=-1, keepdims=True)
    hit2 = lane == i2
    e2 = jnp.exp(v2 - v1)
    w_top = 1.0 / (1.0 + e2)
    comb_ref[...] = jnp.where(hit1, w_top, 0.0) + jnp.where(hit2, e2 * w_top, 0.0)


def _out_proj(x2, conv2, attn2, w, g, router=None):
    tm = OUT_TM
    row = lambda width: pl.BlockSpec((tm, width), lambda i: (i, 0))
    full = lambda a: pl.BlockSpec(a.shape, lambda i: (0,) * a.ndim)
    in_specs = [row(D_MODEL), row(CONV_WIDTH), row(ATTN_WIDTH), full(w), full(g)]
    out_shape = [jax.ShapeDtypeStruct((TOKENS, D_MODEL), F32), jax.ShapeDtypeStruct((TOKENS, D_MODEL), BF16)]
    out_specs = [row(D_MODEL), row(D_MODEL)]
    args = [x2, conv2, attn2, w, g]
    body = _out_proj_kernel
    if router is not None:
        body = _out_proj_router_kernel
        in_specs.append(full(router))
        args.append(router)
        out_shape.append(jax.ShapeDtypeStruct((TOKENS, LANES), F32))
        out_specs.append(row(LANES))
    return pl.pallas_call(
        body,
        grid=(TOKENS // tm,),
        in_specs=in_specs,
        out_specs=tuple(out_specs),
        out_shape=tuple(out_shape),
        compiler_params=pltpu.CompilerParams(
            dimension_semantics=("arbitrary",), vmem_limit_bytes=VMEM_LIMIT),
        name="out_proj",
    )(*args)


FFN_TM = 512
FFN_TF = 1408


def _ffn_kernel(x1_ref, h_ref, w1_ref, w3_ref, w2_ref, o_ref, acc_ref):
    f = pl.program_id(1)
    h = h_ref[...]
    a = _dot(h, w1_ref[...])
    act = (a * _sigmoid(a) * _dot(h, w3_ref[...])).astype(BF16)
    part = _dot(act, w2_ref[...])

    @pl.when(f == 0)
    def _():
        acc_ref[...] = x1_ref[...] + part

    @pl.when(f > 0)
    def _():
        acc_ref[...] += part

    @pl.when(f == pl.num_programs(1) - 1)
    def _():
        o_ref[...] = acc_ref[...]


def _dense_ffn(x1, h, w1, w3, w2):
    tm, tf = FFN_TM, FFN_TF
    return pl.pallas_call(
        _ffn_kernel,
        grid=(TOKENS // tm, D_FF // tf),
        in_specs=[
            pl.BlockSpec((tm, D_MODEL), lambda i, f: (i, 0)),
            pl.BlockSpec((tm, D_MODEL), lambda i, f: (i, 0)),
            pl.BlockSpec((D_MODEL, tf), lambda i, f: (0, f)),
            pl.BlockSpec((D_MODEL, tf), lambda i, f: (0, f)),
            pl.BlockSpec((tf, D_MODEL), lambda i, f: (f, 0)),
        ],
        out_specs=pl.BlockSpec((tm, D_MODEL), lambda i, f: (i, 0)),
        out_shape=jax.ShapeDtypeStruct((TOKENS, D_MODEL), F32),
        scratch_shapes=[pltpu.VMEM((tm, D_MODEL), F32)],
        compiler_params=pltpu.CompilerParams(
            dimension_semantics=("arbitrary", "arbitrary"), vmem_limit_bytes=VMEM_LIMIT),
        name="dense_ffn",
    )(x1, h, w1, w3, w2)


MOE_SB = 512
MOE_TM = 512
MOE_GT = 256
MOE_CT = 512
MOE_TF = 1792
MOE_NB = TOKENS // MOE_SB
MOE_TILES = (2 * TOKENS) // MOE_TM + N_EXPERTS
MOE_GTILES = MOE_TILES * (MOE_TM // MOE_GT)
MOE_PAIRS = MOE_GTILES + N_EXPERTS * MOE_NB
MOE_CPAIRS = MOE_TILES * (MOE_TM // MOE_CT) + N_EXPERTS * MOE_NB


def _moe_rank_kernel(comb_ref, tri_ref, rank_ref, cend_ref, carry_ref):
    @pl.when(pl.program_id(0) == 0)
    def _():
        carry_ref[...] = jnp.zeros((1, LANES), F32)

    routed = jnp.where(comb_ref[...] > 0.0, 1.0, 0.0).astype(BF16)
    carry = carry_ref[...]
    rank_ref[...] = carry + _dot(tri_ref[...], routed)
    carry = carry + jnp.sum(routed.astype(F32), axis=0, keepdims=True)
    carry_ref[...] = carry
    cend_ref[0] = jnp.broadcast_to(carry, (SUBLANES, LANES))


def _moe_rank(comb):
    idx = np.arange(MOE_SB)
    tri = jnp.asarray(idx[:, None] > idx[None, :], BF16)
    return pl.pallas_call(
        _moe_rank_kernel,
        grid=(MOE_NB,),
        in_specs=[pl.BlockSpec((MOE_SB, LANES), lambda b: (b, 0)),
                  pl.BlockSpec((MOE_SB, MOE_SB), lambda b: (0, 0))],
        out_specs=(pl.BlockSpec((MOE_SB, LANES), lambda b: (b, 0)),
                   pl.BlockSpec((1, SUBLANES, LANES), lambda b: (b, 0, 0))),
        out_shape=(jax.ShapeDtypeStruct((TOKENS, LANES), F32),
                   jax.ShapeDtypeStruct((MOE_NB, SUBLANES, LANES), F32)),
        scratch_shapes=[pltpu.VMEM((1, LANES), F32)],
        compiler_params=pltpu.CompilerParams(dimension_semantics=("arbitrary",)),
        name="moe_rank",
    )(comb, tri)


def _count_le(ends, v):
    return jnp.sum(ends[None, :] <= v[:, None], axis=1).astype(jnp.int32)


def _moe_schedule(cend):
    i32 = jnp.int32
    counts = cend[-1]
    cstart = jnp.concatenate([jnp.zeros((1, N_EXPERTS), i32), cend[:-1]], axis=0)
    tiles_e = (counts + MOE_TM - 1) // MOE_TM
    tile_end = jnp.cumsum(tiles_e)
    tile_start = tile_end - tiles_e
    n_tiles = tile_end[-1]
    group_start = tile_start * MOE_TM

    d = jnp.arange(MOE_TILES, dtype=i32)
    t_valid = d < n_tiles
    t_exp = jnp.minimum(_count_le(tile_end, d), N_EXPERTS - 1)
    t_exp = jnp.where(t_valid, t_exp, t_exp[jnp.maximum(n_tiles - 1, 0)])

    per_tile = MOE_TM // MOE_GT
    g = jnp.arange(MOE_GTILES, dtype=i32)
    g_exp = t_exp[g // per_tile]
    r0 = (g // per_tile - tile_start[g_exp]) * MOE_TM + (g % per_tile) * MOE_GT
    r1 = jnp.minimum(r0 + MOE_GT, counts[g_exp])
    gt_valid = t_valid[g // per_tile] & (r0 < counts[g_exp])
    b_lo = jnp.sum(cend[:, g_exp] <= r0[None, :], axis=0).astype(i32)
    b_hi = jnp.sum(cstart[:, g_exp] < r1[None, :], axis=0).astype(i32) - 1
    n_pairs = jnp.where(gt_valid, b_hi - b_lo + 1, 1)
    p_end = jnp.cumsum(n_pairs)
    p_start = p_end - n_pairs
    p = jnp.arange(MOE_PAIRS, dtype=i32)
    g_tile = jnp.minimum(_count_le(p_end, p), MOE_GTILES - 1)
    g_live = p < p_end[-1]
    g_valid = g_live & gt_valid[g_tile]
    idle_block = jnp.max(jnp.where(gt_valid, b_hi, 0))
    g_block = jnp.where(g_valid, b_lo[g_tile] + p - p_start[g_tile], idle_block)
    g_first = g_valid & (p == p_start[g_tile])
    g_zero = g_live & jnp.logical_not(gt_valid[g_tile])
    g_flag = g_valid.astype(i32) + 2 * g_first.astype(i32) + 4 * g_zero.astype(i32)
    gather = (g_tile, g_block, g_exp[g_tile], g_flag)

    lo = group_start[None, :] + cstart
    hi = group_start[None, :] + cend
    t_lo = (lo // MOE_CT).reshape(-1)
    n_t = jnp.where(hi > lo, (hi - 1) // MOE_CT - lo // MOE_CT + 1, 0).reshape(-1)
    q_end = jnp.cumsum(n_t)
    q_start = q_end - n_t
    p = jnp.arange(MOE_CPAIRS, dtype=i32)
    c_valid = p < q_end[-1]
    be = jnp.minimum(_count_le(q_end, p), MOE_NB * N_EXPERTS - 1)
    last = jnp.maximum(q_end[-1] - 1, 0)
    be = jnp.where(c_valid, be, be[last])
    c_tile = jnp.where(c_valid, t_lo[be] + p - q_start[be], (t_lo[be] + n_t[be] - 1))
    c_block = be // N_EXPERTS
    c_first = c_valid & ((p == 0) | (c_block != jnp.roll(c_block, 1)))
    c_flag = c_valid.astype(i32) + 2 * c_first.astype(i32)
    combine = (c_block, c_tile, be % N_EXPERTS, c_flag)
    return group_start, t_exp, t_valid.astype(i32), gather, combine


def _moe_gather_kernel(tile_ref, block_ref, exp_ref, flag_ref, h_ref, pos_ref, xs_ref):
    p = pl.program_id(0)
    flag = flag_ref[p]

    @pl.when((flag & 1) == 1)
    def _():
        rows = (tile_ref[p] * MOE_GT + lax.broadcasted_iota(jnp.int32, (MOE_GT, 1), 0)).astype(F32)
        pos = pos_ref[pl.ds(exp_ref[p], 1), :]
        onehot = jnp.where(rows == pos, 1.0, 0.0).astype(BF16)
        picked = _dot(onehot, h_ref[...]).astype(BF16)

        @pl.when((flag & 2) == 2)
        def _():
            xs_ref[...] = picked

        @pl.when((flag & 2) == 0)
        def _():
            xs_ref[...] += picked

    @pl.when((flag & 4) == 4)
    def _():
        xs_ref[...] = jnp.zeros((MOE_GT, D_MODEL), BF16)


def _moe_gather(sched, h, pos_t):
    spec = pltpu.PrefetchScalarGridSpec(
        num_scalar_prefetch=4,
        grid=(MOE_PAIRS,),
        in_specs=[pl.BlockSpec((MOE_SB, D_MODEL), lambda p, t, b, e, f: (b[p], 0)),
                  pl.BlockSpec((SUBLANES, MOE_SB), lambda p, t, b, e, f: (0, b[p]))],
        out_specs=pl.BlockSpec((MOE_GT, D_MODEL), lambda p, t, b, e, f: (t[p], 0)),
    )
    return pl.pallas_call(
        _moe_gather_kernel,
        grid_spec=spec,
        out_shape=jax.ShapeDtypeStruct((MOE_TILES * MOE_TM, D_MODEL), BF16),
        compiler_params=pltpu.CompilerParams(dimension_semantics=("arbitrary",)),
        name="moe_gather",
    )(*sched, h, pos_t)


def _moe_expert_kernel(exp_ref, valid_ref, xs_ref, w1_ref, w3_ref, w2_ref, y_ref, acc_ref):
    d = pl.program_id(0)
    f = pl.program_id(1)

    @pl.when(valid_ref[d] == 1)
    def _():
        x = xs_ref[...]
        a = _dot(x, w1_ref[0])
        act = (a * _sigmoid(a) * _dot(x, w3_ref[0])).astype(BF16)
        part = _dot(act, w2_ref[0])

        @pl.when(f == 0)
        def _():
            acc_ref[...] = part

        @pl.when(f > 0)
        def _():
            acc_ref[...] += part

        @pl.when(f == pl.num_programs(1) - 1)
        def _():
            y_ref[...] = acc_ref[...].astype(BF16)

    @pl.when((valid_ref[d] == 0) & (f == 0))
    def _():
        y_ref[...] = jnp.zeros((MOE_TM, D_MODEL), BF16)


def _moe_experts(t_exp, t_valid, xs, w1, w3, w2):
    tf = MOE_TF
    nf = D_FF_EXPERT // tf
    fidx = lambda d, f, v: f * v[d] + (nf - 1) * (1 - v[d])
    spec = pltpu.PrefetchScalarGridSpec(
        num_scalar_prefetch=2,
        grid=(MOE_TILES, nf),
        in_specs=[pl.BlockSpec((MOE_TM, D_MODEL), lambda d, f, e, v: (d, 0)),
                  pl.BlockSpec((1, D_MODEL, tf), lambda d, f, e, v: (e[d], 0, fidx(d, f, v))),
                  pl.BlockSpec((1, D_MODEL, tf), lambda d, f, e, v: (e[d], 0, fidx(d, f, v))),
                  pl.BlockSpec((1, tf, D_MODEL), lambda d, f, e, v: (e[d], fidx(d, f, v), 0))],
        out_specs=pl.BlockSpec((MOE_TM, D_MODEL), lambda d, f, e, v: (d, 0)),
        scratch_shapes=[pltpu.VMEM((MOE_TM, D_MODEL), F32)],
    )
    return pl.pallas_call(
        _moe_expert_kernel,
        grid_spec=spec,
        out_shape=jax.ShapeDtypeStruct((MOE_TILES * MOE_TM, D_MODEL), BF16),
        compiler_params=pltpu.CompilerParams(
            dimension_semantics=("arbitrary", "arbitrary"), vmem_limit_bytes=VMEM_LIMIT),
        name="moe_experts",
    )(t_exp, t_valid, xs, w1, w3, w2)


def _moe_combine_kernel(block_ref, tile_ref, exp_ref, flag_ref, y_ref, pos_ref, comb_ref, x1_ref, o_ref):
    p = pl.program_id(0)
    flag = flag_ref[p]

    @pl.when((flag & 1) == 1)
    def _():
        pick = lax.broadcasted_iota(jnp.int32, (1, LANES), 1) == exp_ref[p]
        pos = jnp.sum(jnp.where(pick, pos_ref[...], 0.0), axis=-1, keepdims=True)
        wgt = jnp.sum(jnp.where(pick, comb_ref[...], 0.0), axis=-1, keepdims=True)
        cols = (tile_ref[p] * MOE_CT + lax.broadcasted_iota(jnp.int32, (1, MOE_CT), 1)).astype(F32)
        onehot = jnp.where(pos == cols, 1.0, 0.0).astype(BF16)
        part = wgt * _dot(onehot, y_ref[...])

        @pl.when((flag & 2) == 2)
        def _():
            o_ref[...] = x1_ref[...] + part

        @pl.when((flag & 2) == 0)
        def _():
            o_ref[...] += part


def _moe_combine(sched, y, pos, comb, x1):
    tok = lambda width: pl.BlockSpec((MOE_SB, width), lambda p, b, t, e, f: (b[p], 0))
    spec = pltpu.PrefetchScalarGridSpec(
        num_scalar_prefetch=4,
        grid=(MOE_CPAIRS,),
        in_specs=[pl.BlockSpec((MOE_CT, D_MODEL), lambda p, b, t, e, f: (t[p], 0)),
                  tok(LANES), tok(LANES), tok(D_MODEL)],
        out_specs=tok(D_MODEL),
    )
    return pl.pallas_call(
        _moe_combine_kernel,
        grid_spec=spec,
        out_shape=jax.ShapeDtypeStruct((TOKENS, D_MODEL), F32),
        compiler_params=pltpu.CompilerParams(
            dimension_semantics=("arbitrary",), vmem_limit_bytes=VMEM_LIMIT),
        name="moe_combine",
    )(*sched, y, pos, comb, x1)


def _moe_ffn(x1, h, comb, w1, w3, w2):
    rank, cend = _moe_rank(comb)
    cend = jnp.round(cend[:, 0, :N_EXPERTS]).astype(jnp.int32)
    group_start, t_exp, t_valid, gather, combine = _moe_schedule(cend)
    start = jnp.zeros((LANES,), F32).at[:N_EXPERTS].set(group_start.astype(F32))
    pos = jnp.where(comb > 0.0, start[None, :] + rank, -1.0)
    xs = _moe_gather(gather, h, pos[:, :SUBLANES].T)
    y = _moe_experts(t_exp, t_valid, xs, w1, w3, w2)
    return _moe_combine(combine, y, pos, comb, x1)


def _group_mean_matrix():
    idx = np.arange(LANES)
    return jnp.asarray((idx[:, None] // CONV_GROUP == idx[None, :] // CONV_GROUP) / CONV_GROUP, BF16)


def _overlap_matrix_t():
    c_start = np.arange(CMP_ROWS) * CMP_STRIDE
    j = np.arange(N_SEL)
    ovl = ((c_start[None, :] < (j[:, None] + 1) * SEL_BLOCK)
           & (c_start[None, :] + CMP_BLOCK > j[:, None] * SEL_BLOCK)
           & (np.arange(CMP_ROWS)[None, :] < N_CMP))
    return jnp.asarray(ovl, BF16)


def _dup(v):
    return jnp.concatenate([v, v], axis=-1)


def _zero_pad(v):
    return jnp.concatenate([v, jnp.zeros_like(v)], axis=-1)


def kernel(x, attn_norm_g, w_in, conv_dw_w, conv_dw_b, conv_ln_g, conv_ln_b, q_norm_g, k_norm_g,
           cmp_pos_k, cmp_pos_v, cmp_k_w1, cmp_k_b1, cmp_k_w2, cmp_k_b2,
           cmp_v_w1, cmp_v_b1, cmp_v_w2, cmp_v_b2, w_out, ffn_norm_g,
           dense_w1, dense_w3, dense_w2, router_w, moe_w1, moe_w3, moe_w2):
    assert x.shape == (BATCH, SEQ, D_MODEL) and x.dtype == F32
    bd = _group_mean_matrix()
    ovlt = _overlap_matrix_t()

    x2 = x.reshape(TOKENS, D_MODEL)
    for layer in range(DEPTH):
        w_cols, w_rows = _in_proj_weights(w_in[layer])
        qg = _dup(q_norm_g[layer])[None, :] * (LOG2_E * HEAD_DIM ** -0.5)
        kgs = _dup(k_norm_g[layer, 1])[None, :]
        kgw = _dup(k_norm_g[layer, 2])[None, :]
        u, q, kcvc, ks, kw, vst, vwt, gatet = _in_proj(
            x2, attn_norm_g[layer][None, :], w_cols, w_rows, qg, kgs, kgw, bd)

        conv = _conv_module(u.reshape(BATCH, SEQ, CONV_WIDTH), conv_dw_w[layer], conv_dw_b[layer][None, :],
                            conv_ln_g[layer][None, :], conv_ln_b[layer][None, :], bd)

        xr = kcvc.reshape(BATCH, SEQ, 2, N_KV_HEADS, HEAD_DIM).transpose(2, 0, 3, 1, 4)
        xr = xr.reshape(2, BATCH, N_KV_HEADS, CMP_ROWS, CMP_CHUNK)
        pos = jnp.stack([cmp_pos_k[layer], cmp_pos_v[layer]]).reshape(2, 1, CMP_BLOCK * HEAD_DIM)
        w1 = jnp.stack([cmp_k_w1[layer], cmp_v_w1[layer]]).astype(BF16)
        b1 = jnp.stack([cmp_k_b1[layer], cmp_v_b1[layer]])[:, None, :]
        w2 = _zero_pad(jnp.stack([cmp_k_w2[layer], cmp_v_w2[layer]])).astype(BF16)
        b2 = _zero_pad(jnp.stack([cmp_k_b2[layer], cmp_v_b2[layer]]))[:, None, :]
        kvc = _compress(xr, pos, w1, b1, w2, b2, _zero_pad(k_norm_g[layer, 0])[None, :])

        seq3 = lambda a: a.reshape(BATCH, SEQ, a.shape[-1])
        attn = _attention(seq3(q), gatet, kvc, seq3(ks), vst, seq3(kw), vwt, ovlt)

        w_o = w_out[layer].astype(BF16)
        conv2 = conv.reshape(TOKENS, CONV_WIDTH)
        attn2 = attn.reshape(TOKENS, ATTN_WIDTH)
        g_ffn = ffn_norm_g[layer][None, :]
        i = layer // 2
        if layer % 2 == 0:
            x1, h = _out_proj(x2, conv2, attn2, w_o, g_ffn)
            x2 = _dense_ffn(x1, h, dense_w1[i].astype(BF16), dense_w3[i].astype(BF16),
                            dense_w2[i].astype(BF16))
        else:
            r = jnp.pad(router_w[i], ((0, 0), (0, LANES - N_EXPERTS)))
            r_hi = r.astype(BF16)
            r_lo = (r - r_hi.astype(F32)).astype(BF16)
            x1, h, comb = _out_proj(x2, conv2, attn2, w_o, g_ffn, jnp.stack([r_hi, r_lo]))
            x2 = _moe_ffn(x1, h, comb, moe_w1[i].astype(BF16), moe_w3[i].astype(BF16),
                          moe_w2[i].astype(BF16))
    return x2.reshape(BATCH, SEQ, D_MODEL)
```

```python
import functools

import numpy as np
import jax
import jax.numpy as jnp
from jax import lax
from jax.experimental import pallas as pl
from jax.experimental.pallas import tpu as pltpu

D_MODEL = 1024
BATCH = 8
SEQ = 2048
DEPTH = 2
TOKENS = BATCH * SEQ

CONV_WIDTH = 512
CONV_GROUP = 64
CONV_KERNEL = 31
N_HEADS = 8
HEAD_DIM = 64
N_KV_HEADS = 2
GQA = N_HEADS // N_KV_HEADS
ATTN_WIDTH = N_HEADS * HEAD_DIM
KV_WIDTH = N_KV_HEADS * HEAD_DIM
N_BRANCH = 3
CMP_BLOCK = 32
CMP_STRIDE = 16
CMP_HIDDEN = 256
N_CMP = (SEQ - CMP_BLOCK) // CMP_STRIDE + 1
SEL_BLOCK = 64
SEL_TOPK = 8
N_SEL = SEQ // SEL_BLOCK
WINDOW = 512
Q_BLOCK = 256
FORCE_BONUS = 1e4
NEG_INF = -1e30
LOG2_E = 1.4426950408889634
D_FF = 2816
N_EXPERTS = 8
D_FF_EXPERT = 3584
EPS = 1e-6

LANES = 128
SUBLANES = 8
VMEM_LIMIT = 48 * 1024 * 1024

F32 = jnp.float32
BF16 = jnp.bfloat16

_OFF_VAL = 0
_OFF_GATE = 512
_OFF_Q = 1024
_OFF_KCVC = 1536
_OFF_KS = 1792
_OFF_KW = 1920
IN_COLS = 2048
T_ROWS = 5 * LANES

_BASE_KV = 2 * CONV_WIDTH + ATTN_WIDTH


def _in_proj_weights(w):
    piece = lambda i: w[:, _BASE_KV + i * KV_WIDTH:_BASE_KV + (i + 1) * KV_WIDTH]
    w_cols = jnp.concatenate([w[:, :_BASE_KV + 2 * KV_WIDTH], piece(2), piece(4)], axis=1)
    assert w_cols.shape[1] == IN_COLS
    rows = []
    for i in (3, 5):
        vt = piece(i).T
        for head in range(N_KV_HEADS):
            rows += [vt[head * HEAD_DIM:(head + 1) * HEAD_DIM], jnp.zeros((HEAD_DIM, D_MODEL), w.dtype)]
    n_gate = N_HEADS * N_BRANCH
    glog = _BASE_KV + 6 * KV_WIDTH
    rows += [w[:, glog:glog + n_gate].T, jnp.zeros((LANES - n_gate, D_MODEL), w.dtype)]
    w_rows = jnp.concatenate(rows, axis=0)
    assert w_rows.shape[0] == T_ROWS
    return w_cols.astype(BF16), w_rows.astype(BF16)


def _sigmoid(v):
    return 1.0 / (1.0 + jnp.exp(-v))


def _dot(a, b):
    return jnp.dot(a, b, preferred_element_type=F32)


def _dot_nt(a, b):
    return lax.dot_general(a, b, (((1,), (1,)), ((), ())), preferred_element_type=F32)


def _split_dot(v, m):
    hi = v.astype(BF16)
    lo = (v - hi.astype(F32)).astype(BF16)
    return _dot(hi, m) + _dot(lo, m)


IN_TM = 512


def _in_proj_kernel(x_ref, g_ref, w_ref, wt_ref, qg_ref, kgs_ref, kgw_ref, bd_ref,
                    u_ref, q_ref, kcvc_ref, ks_ref, kw_ref, vst_ref, vwt_ref, gatet_ref):
    x = x_ref[...]
    ms = jnp.mean(x * x, axis=-1, keepdims=True)
    h = (x * lax.rsqrt(ms + EPS) * g_ref[...]).astype(BF16)

    def proj(lo, width):
        return _dot(h, w_ref[:, lo:lo + width])

    u_ref[...] = proj(_OFF_VAL, CONV_WIDTH) * _sigmoid(proj(_OFF_GATE, CONV_WIDTH))

    bd = bd_ref[...]
    lane = lax.broadcasted_iota(jnp.int32, (IN_TM, LANES), 1)
    lower = lane < HEAD_DIM

    def head_slots(pair, upper, out_ref, slot):
        out_ref[:, slot * LANES:(slot + 1) * LANES] = jnp.where(lower, pair, upper).astype(BF16)
        swapped = pltpu.roll(pair, HEAD_DIM, axis=1)
        out_ref[:, (slot + 1) * LANES:(slot + 2) * LANES] = jnp.where(lower, swapped, upper).astype(BF16)

    q = proj(_OFF_Q, ATTN_WIDTH)
    for c in range(ATTN_WIDTH // LANES):
        qc = q[:, c * LANES:(c + 1) * LANES]
        msq = _split_dot(qc * qc, bd)
        head_slots(qc * lax.rsqrt(msq + EPS) * qg_ref[...], 0.0, q_ref, 2 * c)

    kcvc_ref[...] = proj(_OFF_KCVC, 2 * KV_WIDTH)

    def normed_k(off, kg_ref):
        k = proj(off, KV_WIDTH)
        return k * lax.rsqrt(_split_dot(k * k, bd) + EPS) * kg_ref[...]

    tok = (pl.program_id(0) % (SEQ // IN_TM)) * IN_TM + lax.broadcasted_iota(jnp.int32, (IN_TM, 1), 0)
    block_hot = jnp.where(lane - HEAD_DIM == tok // SEL_BLOCK, 1.0, 0.0)
    head_slots(normed_k(_OFF_KS, kgs_ref), block_hot, ks_ref, 0)
    head_slots(normed_k(_OFF_KW, kgw_ref), 0.0, kw_ref, 0)

    zt = _dot_nt(wt_ref[...], h)
    ones_rows = lax.broadcasted_iota(jnp.int32, (LANES, IN_TM), 0) >= HEAD_DIM
    for j, out_ref in enumerate((vst_ref, vst_ref, vwt_ref, vwt_ref)):
        blk = zt[j * LANES:(j + 1) * LANES]
        out_ref[0, j % N_KV_HEADS] = jnp.where(ones_rows, 1.0, blk).astype(BF16)
    gatet_ref[0] = _sigmoid(zt[4 * LANES:5 * LANES])


def _in_proj(x2, g, w, wt, qg, kgs, kgw, bd):
    tm = IN_TM
    per_seq = SEQ // tm
    row = lambda width: pl.BlockSpec((tm, width), lambda i: (i, 0))
    full = lambda a: pl.BlockSpec(a.shape, lambda i: (0,) * a.ndim)
    vt_spec = pl.BlockSpec((1, N_KV_HEADS, LANES, tm), lambda i: (i // per_seq, 0, 0, i % per_seq))
    vt_shape = jax.ShapeDtypeStruct((BATCH, N_KV_HEADS, LANES, SEQ), BF16)
    out_shape = (
        jax.ShapeDtypeStruct((TOKENS, CONV_WIDTH), F32),
        jax.ShapeDtypeStruct((TOKENS, N_HEADS * LANES), BF16),
        jax.ShapeDtypeStruct((TOKENS, 2 * KV_WIDTH), F32),
        jax.ShapeDtypeStruct((TOKENS, 2 * LANES), BF16),
        jax.ShapeDtypeStruct((TOKENS, 2 * LANES), BF16),
        vt_shape,
        vt_shape,
        jax.ShapeDtypeStruct((BATCH, LANES, SEQ), F32),
    )
    out_specs = (row(CONV_WIDTH), row(N_HEADS * LANES), row(2 * KV_WIDTH), row(2 * LANES), row(2 * LANES),
                 vt_spec, vt_spec,
                 pl.BlockSpec((1, LANES, tm), lambda i: (i // per_seq, 0, i % per_seq)))
    return pl.pallas_call(
        _in_proj_kernel,
        grid=(TOKENS // tm,),
        in_specs=[row(D_MODEL), full(g), full(w), full(wt), full(qg), full(kgs), full(kgw), full(bd)],
        out_specs=out_specs,
        out_shape=out_shape,
        compiler_params=pltpu.CompilerParams(
            dimension_semantics=("arbitrary",), vmem_limit_bytes=VMEM_LIMIT),
        name="in_proj",
    )(x2, g, w, wt, qg, kgs, kgw, bd)


CONV_TR = 512
CONV_HALO = 32
CONV_RS = 64
CONV_PARTIALS = 4


def _conv_kernel(ucur_ref, uprev_ref, w_ref, b_ref, lg_ref, lb_ref, bd_ref, o_ref, win_ref, y_ref):
    i = pl.program_id(1)
    first = CONV_HALO - (CONV_KERNEL - 1)
    shifted_rows = CONV_HALO + CONV_TR - SUBLANES
    for c in range(CONV_WIDTH // LANES):
        lanes = slice(c * LANES, (c + 1) * LANES)
        win_ref[0, c, 0:CONV_HALO, :] = jnp.where(i > 0, uprev_ref[0, :, lanes], 0.0)
        win_ref[0, c, CONV_HALO:CONV_HALO + CONV_TR, :] = ucur_ref[0, :, lanes]
        for s in range(1, SUBLANES):
            win_ref[s, c, 0:shifted_rows, :] = win_ref[0, c, s:s + shifted_rows, :]

        def body(r, carry):
            base = pl.multiple_of(r * CONV_RS, CONV_RS)
            accs = [None] * CONV_PARTIALS
            for k in range(CONV_KERNEL):
                off = first + k
                rows = pl.ds(base + (off // SUBLANES) * SUBLANES, CONV_RS)
                term = win_ref[off % SUBLANES, c, rows, :] * w_ref[k:k + 1, lanes]
                j = k % CONV_PARTIALS
                accs[j] = term if accs[j] is None else accs[j] + term
            y_ref[c, pl.ds(base, CONV_RS), :] = (accs[0] + accs[1]) + (accs[2] + accs[3]) + b_ref[:, lanes]
            return carry

        lax.fori_loop(0, CONV_TR // CONV_RS, body, 0)

    bd = bd_ref[...]
    for c in range(CONV_WIDTH // LANES):
        lanes = slice(c * LANES, (c + 1) * LANES)
        y = y_ref[c]
        d = y - _split_dot(y, bd)
        var = _split_dot(d * d, bd)
        yn = d * lax.rsqrt(var + EPS) * lg_ref[:, lanes] + lb_ref[:, lanes]
        o_ref[0, :, lanes] = (yn * _sigmoid(yn)).astype(BF16)


def _conv_module(u3, w, b, lg, lb, bd):
    halo_blocks = CONV_TR // CONV_HALO
    full = lambda a: pl.BlockSpec(a.shape, lambda bi, i: (0,) * a.ndim)
    return pl.pallas_call(
        _conv_kernel,
        grid=(BATCH, SEQ // CONV_TR),
        in_specs=[
            pl.BlockSpec((1, CONV_TR, CONV_WIDTH), lambda bi, i: (bi, i, 0)),
            pl.BlockSpec((1, CONV_HALO, CONV_WIDTH),
                         lambda bi, i: (bi, jnp.maximum(i * halo_blocks - 1, 0), 0)),
            full(w), full(b), full(lg), full(lb), full(bd),
        ],
        out_specs=pl.BlockSpec((1, CONV_TR, CONV_WIDTH), lambda bi, i: (bi, i, 0)),
        out_shape=jax.ShapeDtypeStruct((BATCH, SEQ, CONV_WIDTH), BF16),
        scratch_shapes=[pltpu.VMEM((SUBLANES, CONV_WIDTH // LANES, CONV_HALO + CONV_TR, LANES), F32),
                        pltpu.VMEM((CONV_WIDTH // LANES, CONV_TR, LANES), F32)],
        compiler_params=pltpu.CompilerParams(dimension_semantics=("arbitrary", "arbitrary")),
        name="conv_module",
    )(u3, u3, w, b, lg, lb, bd)


CMP_ROWS = SEQ // CMP_STRIDE
CMP_CHUNK = CMP_STRIDE * HEAD_DIM


def _compress_kernel(x_ref, pos_ref, w1_ref, b1_ref, w2_ref, b2_ref, w2t_ref, b2t_ref, kg_ref, o_ref):
    kind = pl.program_id(0)
    xa = x_ref[0, 0, 0]
    xb = pltpu.roll(xa, CMP_ROWS - 1, axis=0)
    a = (xa + pos_ref[0, :, 0:CMP_CHUNK]).astype(BF16)
    b = (xb + pos_ref[0, :, CMP_CHUNK:2 * CMP_CHUNK]).astype(BF16)
    hid = _dot(a, w1_ref[0, 0:CMP_CHUNK, :]) + _dot(b, w1_ref[0, CMP_CHUNK:2 * CMP_CHUNK, :]) + b1_ref[0]
    hid = (hid * _sigmoid(hid)).astype(BF16)
    out = _dot(hid, w2_ref[0]) + b2_ref[0]
    ms = jnp.sum(out * out, axis=-1, keepdims=True) * (1.0 / HEAD_DIM)
    normed = out * lax.rsqrt(ms + EPS) * kg_ref[...]
    out_t = _dot_nt(w2t_ref[0], hid) + b2t_ref[0]
    o_ref[0, 0, 0] = jnp.where(kind == 0, normed, out_t).astype(BF16)


def _compress(xr, pos, w1, b1, w2, b2, kg):
    w2t = jnp.swapaxes(w2, 1, 2)
    b2t = jnp.swapaxes(b2, 1, 2)
    per_kind = lambda a: pl.BlockSpec((1,) + a.shape[1:], lambda k, bi, h: (k,) + (0,) * (a.ndim - 1))
    return pl.pallas_call(
        _compress_kernel,
        grid=(2, BATCH, N_KV_HEADS),
        in_specs=[
            pl.BlockSpec((1, 1, 1, CMP_ROWS, CMP_CHUNK), lambda k, bi, h: (k, bi, h, 0, 0)),
            per_kind(pos), per_kind(w1), per_kind(b1), per_kind(w2), per_kind(b2),
            per_kind(w2t), per_kind(b2t),
            pl.BlockSpec(kg.shape, lambda k, bi, h: (0, 0)),
        ],
        out_specs=pl.BlockSpec((1, 1, 1, CMP_ROWS, LANES), lambda k, bi, h: (k, bi, h, 0, 0)),
        out_shape=jax.ShapeDtypeStruct((2, BATCH, N_KV_HEADS, CMP_ROWS, LANES), BF16),
        compiler_params=pltpu.CompilerParams(
            dimension_semantics=("arbitrary", "arbitrary", "arbitrary")),
        name="compress",
    )(xr, pos, w1, b1, w2, b2, w2t, b2t, kg)


QCOLS = GQA * Q_BLOCK
SEL_CHUNK = 256
WIN_CHUNK = 128
WIN_CHUNKS = (WINDOW + Q_BLOCK) // WIN_CHUNK
MASK_BIAS = -1e30


def _window_biases():
    k = np.arange(WIN_CHUNK)[:, None]
    t = (np.arange(QCOLS) % Q_BLOCK)[None, :]
    tables, index = [], {}
    for i in range(WIN_CHUNKS):
        key = i * WIN_CHUNK - WINDOW + k
        bias = np.where((key <= t) & (key > t - WINDOW), 0.0, MASK_BIAS)
        if bias.any():
            index[i] = len(tables)
            tables.append(bias)
    return np.stack(tables).astype(np.float32), index


_WIN_BIAS, _WIN_BIAS_INDEX = _window_biases()


def _attn_kernel(q_ref, gatet_ref, kc_ref, vct_ref, ks_ref, vst_ref, kw_ref, vwt_ref,
                 ovlt_ref, wbias_ref, o_ref, ssel_sc, swin_sc, acc_sc, out_sc):
    n = pl.program_id(1)
    start = n * Q_BLOCK
    col = lax.broadcasted_iota(jnp.int32, (1, QCOLS), 1)
    t_cols = start + (col & (Q_BLOCK - 1))
    tq = start + lax.broadcasted_iota(jnp.int32, (1, Q_BLOCK), 1)
    blk_row = lax.broadcasted_iota(jnp.int32, (N_SEL, 1), 0)
    blk_row_f = blk_row.astype(F32)
    cmp_end = lax.broadcasted_iota(jnp.int32, (CMP_ROWS, 1), 0) * CMP_STRIDE + (CMP_BLOCK - 1)

    kv_heads = range(N_KV_HEADS)
    kv_lanes = [slice(kvh * LANES, (kvh + 1) * LANES) for kvh in kv_heads]

    def gate_row(kvh, g, branch):
        r = (kvh * GQA + g) * N_BRANCH + branch
        return gatet_ref[0, r:r + 1, :]

    def head_rows(kvh, g):
        h = kvh * GQA + g
        return slice(h * HEAD_DIM, (h + 1) * HEAD_DIM)

    qs = []
    for kvh in kv_heads:
        h0 = kvh * GQA
        qs.append(jnp.concatenate(
            [q_ref[0, :, (h0 + g) * LANES:(h0 + g + 1) * LANES] for g in range(GQA)], axis=0))

    imp = []
    mask_c = cmp_end <= t_cols
    for kvh in kv_heads:
        s_c = jnp.where(mask_c, _dot_nt(kc_ref[0, kvh], qs[kvh]), NEG_INF)
        m_c = jnp.max(s_c, axis=0, keepdims=True)
        p_c = jnp.where(mask_c, jnp.exp2(s_c - m_c), 0.0)
        p_c = p_c / jnp.maximum(jnp.sum(p_c, axis=0, keepdims=True), 1e-30)
        o_cmp = _dot(vct_ref[0, kvh], p_c.astype(BF16))
        for g in range(GQA):
            out_sc[head_rows(kvh, g), :] = gate_row(kvh, g, 0) * o_cmp[0:HEAD_DIM, g * Q_BLOCK:(g + 1) * Q_BLOCK]
        p_sum = p_c[:, 0:Q_BLOCK]
        for g in range(1, GQA):
            p_sum = p_sum + p_c[:, g * Q_BLOCK:(g + 1) * Q_BLOCK]
        p_hi = p_sum.astype(BF16)
        p_lo = (p_sum - p_hi.astype(F32)).astype(BF16)
        imp.append(_dot(ovlt_ref[...], p_hi) + _dot(ovlt_ref[...], p_lo))

    tq2 = jnp.concatenate([tq] * N_KV_HEADS, axis=1)
    cur = tq2 >> 6
    forced = (blk_row == 0) | (blk_row == cur) | (blk_row == cur - 1)
    causal_blk = blk_row * SEL_BLOCK <= tq2
    score = jnp.where(causal_blk, jnp.concatenate(imp, axis=1) + jnp.where(forced, FORCE_BONUS, 0.0), NEG_INF)
    sel = jnp.zeros((N_SEL, N_KV_HEADS * Q_BLOCK), F32)
    for _ in range(SEL_TOPK):
        best = jnp.max(score, axis=0, keepdims=True)
        idx = jnp.min(jnp.where(score == best, blk_row_f, float(LANES)), axis=0, keepdims=True)
        hit = blk_row_f == idx
        sel = jnp.where(hit, 1.0, sel)
        score = jnp.where(hit, -3e38, score)
    sel_bias = (sel - 1.0) * (-MASK_BIAS)
    qs_sel = []
    for kvh in kv_heads:
        rows = jnp.concatenate([jnp.zeros((HEAD_DIM, Q_BLOCK), F32),
                                sel_bias[:, kvh * Q_BLOCK:(kvh + 1) * Q_BLOCK],
                                jnp.zeros((LANES - HEAD_DIM - N_SEL, Q_BLOCK), F32)], axis=0)
        bias_q = rows.T.astype(BF16)
        qs_sel.append(qs[kvh] + jnp.concatenate([bias_q] * GQA, axis=0))

    for kvh in kv_heads:
        mxw = jnp.full((1, QCOLS), NEG_INF, F32)
        kbs = [n * (Q_BLOCK // WIN_CHUNK) - WINDOW // WIN_CHUNK + i for i in range(WIN_CHUNKS)]
        k_win = [pl.multiple_of(jnp.maximum(kb, 0) * WIN_CHUNK, WIN_CHUNK) for kb in kbs]
        k_band = jnp.concatenate([kw_ref[0, pl.ds(k0, WIN_CHUNK), kv_lanes[kvh]] for k0 in k_win], axis=0)
        s_band = _dot_nt(k_band, qs[kvh])
        for i, kb in enumerate(kbs):
            s = s_band[i * WIN_CHUNK:(i + 1) * WIN_CHUNK]
            if i in _WIN_BIAS_INDEX:
                s = s + wbias_ref[_WIN_BIAS_INDEX[i]]
            if i < WINDOW // WIN_CHUNK:
                s = s + jnp.where(kb >= 0, 0.0, MASK_BIAS)
            swin_sc[kvh, i * WIN_CHUNK:(i + 1) * WIN_CHUNK, :] = s
            mxw = jnp.maximum(mxw, jnp.max(s, axis=0, keepdims=True))
        p = jnp.exp2(swin_sc[kvh] - mxw).astype(BF16)
        v_band = jnp.concatenate([vwt_ref[0, kvh, :, pl.ds(k0, WIN_CHUNK)] for k0 in k_win], axis=1)
        accw = _dot(v_band, p)
        o_win = accw[0:HEAD_DIM] * (1.0 / accw[HEAD_DIM:HEAD_DIM + 1])
        for g in range(GQA):
            out_sc[head_rows(kvh, g), :] += gate_row(kvh, g, 2) * o_win[:, g * Q_BLOCK:(g + 1) * Q_BLOCK]

    def sel_scores(kvh, k0):
        return _dot_nt(ks_ref[0, pl.ds(k0, SEL_CHUNK), kv_lanes[kvh]], qs_sel[kvh])

    def sel_pass1(i, mx):
        k0 = pl.multiple_of(i * SEL_CHUNK, SEL_CHUNK)
        out = []
        for kvh in kv_heads:
            s = sel_scores(kvh, k0)
            ssel_sc[kvh, pl.ds(k0, SEL_CHUNK), :] = s
            out.append(jnp.maximum(mx[kvh], jnp.max(s, axis=0, keepdims=True)))
        return tuple(out)

    last = (start + Q_BLOCK - 1) // SEL_CHUNK
    mx = lax.fori_loop(0, last, sel_pass1, (jnp.full((1, QCOLS), NEG_INF, F32),) * N_KV_HEADS)
    k_last = pl.multiple_of(last * SEL_CHUNK, SEL_CHUNK)
    causal = k_last + lax.broadcasted_iota(jnp.int32, (SEL_CHUNK, 1), 0) <= t_cols
    mx = list(mx)
    for kvh in kv_heads:
        s = jnp.where(causal, sel_scores(kvh, k_last), NEG_INF)
        ssel_sc[kvh, pl.ds(k_last, SEL_CHUNK), :] = s
        mx[kvh] = jnp.maximum(mx[kvh], jnp.max(s, axis=0, keepdims=True))
        acc_sc[kvh] = jnp.zeros((LANES, QCOLS), F32)

    def sel_pass2(i, carry):
        k0 = pl.multiple_of(i * SEL_CHUNK, SEL_CHUNK)
        for kvh in kv_heads:
            p = jnp.exp2(ssel_sc[kvh, pl.ds(k0, SEL_CHUNK), :] - mx[kvh]).astype(BF16)
            acc_sc[kvh] += _dot(vst_ref[0, kvh, :, pl.ds(k0, SEL_CHUNK)], p)
        return carry

    lax.fori_loop(0, last + 1, sel_pass2, 0)
    for kvh in kv_heads:
        acc = acc_sc[kvh]
        o_sel = acc[0:HEAD_DIM] * (1.0 / acc[HEAD_DIM:HEAD_DIM + 1])
        for g in range(GQA):
            out_sc[head_rows(kvh, g), :] += gate_row(kvh, g, 1) * o_sel[:, g * Q_BLOCK:(g + 1) * Q_BLOCK]

    for c in range(N_HEADS // 2):
        o_ref[0, :, c * LANES:(c + 1) * LANES] = out_sc[c * LANES:(c + 1) * LANES, :].T.astype(BF16)


def _attention(q3, gatet, kvc, ks3, vst, kw3, vwt, ovlt):
    wbias = jnp.asarray(_WIN_BIAS)
    qblk = lambda width: pl.BlockSpec((1, Q_BLOCK, width), lambda bi, n: (bi, n, 0))
    seq = pl.BlockSpec((1, SEQ, 2 * LANES), lambda bi, n: (bi, 0, 0))
    seq_t = pl.BlockSpec((1, N_KV_HEADS, LANES, SEQ), lambda bi, n: (bi, 0, 0, 0))
    cmp_spec = lambda kind: pl.BlockSpec((None, 1, N_KV_HEADS, CMP_ROWS, LANES),
                                         lambda bi, n: (kind, bi, 0, 0, 0))
    full = lambda a: pl.BlockSpec(a.shape, lambda bi, n: (0,) * a.ndim)
    return pl.pallas_call(
        _attn_kernel,
        grid=(BATCH, SEQ // Q_BLOCK),
        in_specs=[qblk(N_HEADS * LANES), pl.BlockSpec((1, LANES, Q_BLOCK), lambda bi, n: (bi, 0, n)),
                  cmp_spec(0), cmp_spec(1), seq, seq_t, seq, seq_t, full(ovlt), full(wbias)],
        out_specs=qblk(ATTN_WIDTH),
        out_shape=jax.ShapeDtypeStruct((BATCH, SEQ, ATTN_WIDTH), BF16),
        scratch_shapes=[pltpu.VMEM((N_KV_HEADS, SEQ, QCOLS), F32),
                        pltpu.VMEM((N_KV_HEADS, WIN_CHUNKS * WIN_CHUNK, QCOLS), F32),
                        pltpu.VMEM((N_KV_HEADS, LANES, QCOLS), F32),
                        pltpu.VMEM((ATTN_WIDTH, Q_BLOCK), F32)],
        compiler_params=pltpu.CompilerParams(
            dimension_semantics=("arbitrary", "arbitrary"), vmem_limit_bytes=VMEM_LIMIT),
        name="nsa_attention",
    )(q3, gatet, kvc, kvc, ks3, vst, kw3, vwt, ovlt, wbias)


OUT_TM = 512


def _out_proj_kernel(x_ref, conv_ref, attn_ref, w_ref, g_ref, x1_ref, h_ref):
    x1 = (x_ref[...] + _dot(conv_ref[...], w_ref[0:CONV_WIDTH, :])
          + _dot(attn_ref[...], w_ref[CONV_WIDTH:CONV_WIDTH + ATTN_WIDTH, :]))
    x1_ref[...] = x1
    ms = jnp.mean(x1 * x1, axis=-1, keepdims=True)
    h_ref[...] = (x1 * lax.rsqrt(ms + EPS) * g_ref[...]).astype(BF16)


def _out_proj_router_kernel(x_ref, conv_ref, attn_ref, w_ref, g_ref, r_ref, x1_ref, h_ref, comb_ref):
    x1 = (x_ref[...] + _dot(conv_ref[...], w_ref[0:CONV_WIDTH, :])
          + _dot(attn_ref[...], w_ref[CONV_WIDTH:CONV_WIDTH + ATTN_WIDTH, :]))
    x1_ref[...] = x1
    ms = jnp.mean(x1 * x1, axis=-1, keepdims=True)
    h = x1 * lax.rsqrt(ms + EPS) * g_ref[...]
    h_ref[...] = h.astype(BF16)
    h_hi = h.astype(BF16)
    h_lo = (h - h_hi.astype(F32)).astype(BF16)
    logits = _dot(h_hi, r_ref[0]) + _dot(h_lo, r_ref[0]) + _dot(h_hi, r_ref[1])
    lane = lax.broadcasted_iota(jnp.int32, (1, LANES), 1).astype(F32)
    logits = jnp.where(lane < N_EXPERTS, logits, NEG_INF)
    v1 = jnp.max(logits, axis=-1, keepdims=True)
    i1 = jnp.min(jnp.where(logits == v1, lane, float(LANES)), axis=-1, keepdims=True)
    hit1 = lane == i1
    rest = jnp.where(hit1, -3e38, logits)
    v2 = jnp.max(rest, axis=-1, keepdims=True)
    i2 = jnp.min(jnp.where(rest == v2, lane, float(LANES)), axis=-1, keepdims=True)
    hit2 = lane == i2
    e2 = jnp.exp(v2 - v1)
    w_top = 1.0 / (1.0 + e2)
    comb_ref[...] = jnp.where(hit1, w_top, 0.0) + jnp.where(hit2, e2 * w_top, 0.0)


def _out_proj(x2, conv2, attn2, w, g, router=None):
    tm = OUT_TM
    row = lambda width: pl.BlockSpec((tm, width), lambda i: (i, 0))
    full = lambda a: pl.BlockSpec(a.shape, lambda i: (0,) * a.ndim)
    in_specs = [row(D_MODEL), row(CONV_WIDTH), row(ATTN_WIDTH), full(w), full(g)]
    out_shape = [jax.ShapeDtypeStruct((TOKENS, D_MODEL), F32), jax.ShapeDtypeStruct((TOKENS, D_MODEL), BF16)]
    out_specs = [row(D_MODEL), row(D_MODEL)]
    args = [x2, conv2, attn2, w, g]
    body = _out_proj_kernel
    if router is not None:
        body = _out_proj_router_kernel
        in_specs.append(full(router))
        args.append(router)
        out_shape.append(jax.ShapeDtypeStruct((TOKENS, LANES), F32))
        out_specs.append(row(LANES))
    return pl.pallas_call(
        body,
        grid=(TOKENS // tm,),
        in_specs=in_specs,
        out_specs=tuple(out_specs),
        out_shape=tuple(out_shape),
        compiler_params=pltpu.CompilerParams(
            dimension_semantics=("arbitrary",), vmem_limit_bytes=VMEM_LIMIT),
        name="out_proj",
    )(*args)


FFN_TM = 512
FFN_TF = 1408


def _ffn_kernel(x1_ref, h_ref, w1_ref, w3_ref, w2_ref, o_ref):
    h = h_ref[...]
    out = x1_ref[...]
    for f in range(D_FF // FFN_TF):
        cols = slice(f * FFN_TF, (f + 1) * FFN_TF)
        a = _dot(h, w1_ref[:, cols])
        act = (a * _sigmoid(a) * _dot(h, w3_ref[:, cols])).astype(BF16)
        out = out + _dot(act, w2_ref[cols, :])
    o_ref[...] = out


def _dense_ffn(x1, h, w1, w3, w2):
    tm = FFN_TM
    resident = pl.BlockSpec(memory_space=pltpu.VMEM)
    return pl.pallas_call(
        _ffn_kernel,
        grid=(TOKENS // tm,),
        in_specs=[pl.BlockSpec((tm, D_MODEL), lambda i: (i, 0)), pl.BlockSpec((tm, D_MODEL), lambda i: (i, 0)),
                  resident, resident, resident],
        out_specs=pl.BlockSpec((tm, D_MODEL), lambda i: (i, 0)),
        out_shape=jax.ShapeDtypeStruct((TOKENS, D_MODEL), F32),
        compiler_params=pltpu.CompilerParams(
            dimension_semantics=("arbitrary",), vmem_limit_bytes=VMEM_LIMIT),
        name="dense_ffn",
    )(x1, h, w1, w3, w2)


MOE_SB = 512
MOE_TM = 512
MOE_GT = 256
MOE_CT = 512
MOE_TF = 1792
MOE_NB = TOKENS // MOE_SB
MOE_TILES = (2 * TOKENS) // MOE_TM + N_EXPERTS
MOE_GTILES = MOE_TILES * (MOE_TM // MOE_GT)
MOE_CPAIRS = MOE_TILES * (MOE_TM // MOE_CT) + N_EXPERTS * MOE_NB


def _moe_rank_kernel(comb_ref, tri_ref, rank_ref, cend_ref, carry_ref):
    @pl.when(pl.program_id(0) == 0)
    def _():
        carry_ref[...] = jnp.zeros((1, LANES), F32)

    routed = jnp.where(comb_ref[...] > 0.0, 1.0, 0.0).astype(BF16)
    carry = carry_ref[...]
    rank_ref[...] = carry + _dot(tri_ref[...], routed)
    carry = carry + jnp.sum(routed.astype(F32), axis=0, keepdims=True)
    carry_ref[...] = carry
    cend_ref[0] = jnp.broadcast_to(carry, (SUBLANES, LANES))


def _moe_rank(comb):
    idx = np.arange(MOE_SB)
    tri = jnp.asarray(idx[:, None] > idx[None, :], BF16)
    return pl.pallas_call(
        _moe_rank_kernel,
        grid=(MOE_NB,),
        in_specs=[pl.BlockSpec((MOE_SB, LANES), lambda b: (b, 0)),
                  pl.BlockSpec((MOE_SB, MOE_SB), lambda b: (0, 0))],
        out_specs=(pl.BlockSpec((MOE_SB, LANES), lambda b: (b, 0)),
                   pl.BlockSpec((1, SUBLANES, LANES), lambda b: (b, 0, 0))),
        out_shape=(jax.ShapeDtypeStruct((TOKENS, LANES), F32),
                   jax.ShapeDtypeStruct((MOE_NB, SUBLANES, LANES), F32)),
        scratch_shapes=[pltpu.VMEM((1, LANES), F32)],
        compiler_params=pltpu.CompilerParams(dimension_semantics=("arbitrary",)),
        name="moe_rank",
    )(comb, tri)


def _count_le(ends, v):
    return jnp.sum(ends[None, :] <= v[:, None], axis=1).astype(jnp.int32)


def _moe_schedule(cend):
    i32 = jnp.int32
    counts = cend[-1]
    cstart = jnp.concatenate([jnp.zeros((1, N_EXPERTS), i32), cend[:-1]], axis=0)
    tiles_e = (counts + MOE_TM - 1) // MOE_TM
    tile_end = jnp.cumsum(tiles_e)
    tile_start = tile_end - tiles_e
    n_tiles = tile_end[-1]
    group_start = tile_start * MOE_TM

    d = jnp.arange(MOE_TILES, dtype=i32)
    t_valid = d < n_tiles
    t_exp = jnp.minimum(_count_le(tile_end, d), N_EXPERTS - 1)
    t_exp = jnp.where(t_valid, t_exp, t_exp[jnp.maximum(n_tiles - 1, 0)])

    per_tile = MOE_TM // MOE_GT
    g = jnp.arange(MOE_GTILES, dtype=i32)
    g_exp = t_exp[g // per_tile]
    r0 = (g // per_tile - tile_start[g_exp]) * MOE_TM + (g % per_tile) * MOE_GT
    r1 = jnp.minimum(r0 + MOE_GT, counts[g_exp])
    gt_valid = t_valid[g // per_tile] & (r0 < counts[g_exp])
    b_lo = jnp.sum(cend[:, g_exp] <= r0[None, :], axis=0).astype(i32)
    b_hi = jnp.sum(cstart[:, g_exp] < r1[None, :], axis=0).astype(i32) - 1
    gather = (g_exp, jnp.where(gt_valid, b_lo, 0), jnp.where(gt_valid, b_hi - b_lo + 1, 0))

    lo = group_start[None, :] + cstart
    hi = group_start[None, :] + cend
    t_lo = (lo // MOE_CT).reshape(-1)
    n_t = jnp.where(hi > lo, (hi - 1) // MOE_CT - lo // MOE_CT + 1, 0).reshape(-1)
    q_end = jnp.cumsum(n_t)
    q_start = q_end - n_t
    p = jnp.arange(MOE_CPAIRS, dtype=i32)
    c_valid = p < q_end[-1]
    be = jnp.minimum(_count_le(q_end, p), MOE_NB * N_EXPERTS - 1)
    last = jnp.maximum(q_end[-1] - 1, 0)
    be = jnp.where(c_valid, be, be[last])
    c_tile = jnp.where(c_valid, t_lo[be] + p - q_start[be], (t_lo[be] + n_t[be] - 1))
    c_block = be // N_EXPERTS
    c_first = c_valid & ((p == 0) | (c_block != jnp.roll(c_block, 1)))
    c_flag = c_valid.astype(i32) + 2 * c_first.astype(i32)
    combine = (c_block, c_tile, be % N_EXPERTS, c_flag)
    return group_start, t_exp, t_valid.astype(i32), gather, combine


def _moe_gather_kernel(exp_ref, first_ref, count_ref, h_ref, pos_ref, xs_ref):
    g = pl.program_id(0)
    rows = (g * MOE_GT + lax.broadcasted_iota(jnp.int32, (MOE_GT, 1), 0)).astype(F32)
    xs_ref[...] = jnp.zeros((MOE_GT, D_MODEL), BF16)

    def body(j, carry):
        t0 = pl.multiple_of((first_ref[g] + j) * MOE_SB, MOE_SB)
        pos = pos_ref[pl.ds(exp_ref[g], 1), pl.ds(t0, MOE_SB)]
        onehot = jnp.where(rows == pos, 1.0, 0.0).astype(BF16)
        xs_ref[...] += _dot(onehot, h_ref[pl.ds(t0, MOE_SB), :]).astype(BF16)
        return carry

    lax.fori_loop(0, count_ref[g], body, 0)


def _moe_gather(sched, h, pos_t):
    spec = pltpu.PrefetchScalarGridSpec(
        num_scalar_prefetch=3,
        grid=(MOE_GTILES,),
        in_specs=[pl.BlockSpec(memory_space=pltpu.VMEM), pl.BlockSpec(memory_space=pltpu.VMEM)],
        out_specs=pl.BlockSpec((MOE_GT, D_MODEL), lambda g, e, b, c: (g, 0)),
    )
    return pl.pallas_call(
        _moe_gather_kernel,
        grid_spec=spec,
        out_shape=jax.ShapeDtypeStruct((MOE_TILES * MOE_TM, D_MODEL), BF16),
        compiler_params=pltpu.CompilerParams(
            dimension_semantics=("arbitrary",), vmem_limit_bytes=VMEM_LIMIT),
        name="moe_gather",
    )(*sched, h, pos_t)


def _moe_expert_kernel(exp_ref, valid_ref, xs_ref, w1_ref, w3_ref, w2_ref, y_ref, acc_ref):
    d = pl.program_id(0)
    f = pl.program_id(1)

    @pl.when(valid_ref[d] == 1)
    def _():
        x = xs_ref[...]
        a = _dot(x, w1_ref[0])
        act = (a * _sigmoid(a) * _dot(x, w3_ref[0])).astype(BF16)
        part = _dot(act, w2_ref[0])

        @pl.when(f == 0)
        def _():
            acc_ref[...] = part

        @pl.when(f > 0)
        def _():
            acc_ref[...] += part

        @pl.when(f == pl.num_programs(1) - 1)
        def _():
            y_ref[...] = acc_ref[...].astype(BF16)

    @pl.when((valid_ref[d] == 0) & (f == 0))
    def _():
        y_ref[...] = jnp.zeros((MOE_TM, D_MODEL), BF16)


def _moe_experts(t_exp, t_valid, xs, w1, w3, w2):
    tf = MOE_TF
    nf = D_FF_EXPERT // tf
    fidx = lambda d, f, v: f * v[d] + (nf - 1) * (1 - v[d])
    spec = pltpu.PrefetchScalarGridSpec(
        num_scalar_prefetch=2,
        grid=(MOE_TILES, nf),
        in_specs=[pl.BlockSpec((MOE_TM, D_MODEL), lambda d, f, e, v: (d, 0)),
                  pl.BlockSpec((1, D_MODEL, tf), lambda d, f, e, v: (e[d], 0, fidx(d, f, v))),
                  pl.BlockSpec((1, D_MODEL, tf), lambda d, f, e, v: (e[d], 0, fidx(d, f, v))),
                  pl.BlockSpec((1, tf, D_MODEL), lambda d, f, e, v: (e[d], fidx(d, f, v), 0))],
        out_specs=pl.BlockSpec((MOE_TM, D_MODEL), lambda d, f, e, v: (d, 0)),
        scratch_shapes=[pltpu.VMEM((MOE_TM, D_MODEL), F32)],
    )
    return pl.pallas_call(
        _moe_expert_kernel,
        grid_spec=spec,
        out_shape=jax.ShapeDtypeStruct((MOE_TILES * MOE_TM, D_MODEL), BF16),
        compiler_params=pltpu.CompilerParams(
            dimension_semantics=("arbitrary", "arbitrary"), vmem_limit_bytes=VMEM_LIMIT),
        name="moe_experts",
    )(t_exp, t_valid, xs, w1, w3, w2)


def _moe_combine_kernel(block_ref, tile_ref, exp_ref, flag_ref, y_ref, pos_ref, comb_ref, x1_ref, o_ref):
    p = pl.program_id(0)
    flag = flag_ref[p]

    @pl.when((flag & 1) == 1)
    def _():
        pick = lax.broadcasted_iota(jnp.int32, (1, LANES), 1) == exp_ref[p]
        pos = jnp.sum(jnp.where(pick, pos_ref[...], 0.0), axis=-1, keepdims=True)
        wgt = jnp.sum(jnp.where(pick, comb_ref[...], 0.0), axis=-1, keepdims=True)
        cols = (tile_ref[p] * MOE_CT + lax.broadcasted_iota(jnp.int32, (1, MOE_CT), 1)).astype(F32)
        onehot = jnp.where(pos == cols, 1.0, 0.0).astype(BF16)
        part = wgt * _dot(onehot, y_ref[...])

        @pl.when((flag & 2) == 2)
        def _():
            o_ref[...] = x1_ref[...] + part

        @pl.when((flag & 2) == 0)
        def _():
            o_ref[...] += part


def _moe_combine(sched, y, pos, comb, x1):
    tok = lambda width: pl.BlockSpec((MOE_SB, width), lambda p, b, t, e, f: (b[p], 0))
    spec = pltpu.PrefetchScalarGridSpec(
        num_scalar_prefetch=4,
        grid=(MOE_CPAIRS,),
        in_specs=[pl.BlockSpec((MOE_CT, D_MODEL), lambda p, b, t, e, f: (t[p], 0)),
                  tok(LANES), tok(LANES), tok(D_MODEL)],
        out_specs=tok(D_MODEL),
    )
    return pl.pallas_call(
        _moe_combine_kernel,
        grid_spec=spec,
        out_shape=jax.ShapeDtypeStruct((TOKENS, D_MODEL), F32),
        compiler_params=pltpu.CompilerParams(
            dimension_semantics=("arbitrary",), vmem_limit_bytes=VMEM_LIMIT),
        name="moe_combine",
    )(*sched, y, pos, comb, x1)


def _moe_ffn(x1, h, comb, w1, w3, w2):
    rank, cend = _moe_rank(comb)
    cend = jnp.round(cend[:, 0, :N_EXPERTS]).astype(jnp.int32)
    group_start, t_exp, t_valid, gather, combine = _moe_schedule(cend)
    start = jnp.zeros((LANES,), F32).at[:N_EXPERTS].set(group_start.astype(F32))
    pos = jnp.where(comb > 0.0, start[None, :] + rank, -1.0)
    xs = _moe_gather(gather, h, pos[:, :SUBLANES].T)
    y = _moe_experts(t_exp, t_valid, xs, w1, w3, w2)
    return _moe_combine(combine, y, pos, comb, x1)


def _group_mean_matrix():
    idx = np.arange(LANES)
    return jnp.asarray((idx[:, None] // CONV_GROUP == idx[None, :] // CONV_GROUP) / CONV_GROUP, BF16)


def _overlap_matrix_t():
    c_start = np.arange(CMP_ROWS) * CMP_STRIDE
    j = np.arange(N_SEL)
    ovl = ((c_start[None, :] < (j[:, None] + 1) * SEL_BLOCK)
           & (c_start[None, :] + CMP_BLOCK > j[:, None] * SEL_BLOCK)
           & (np.arange(CMP_ROWS)[None, :] < N_CMP))
    return jnp.asarray(ovl, BF16)


def _dup(v):
    return jnp.concatenate([v, v], axis=-1)


def _zero_pad(v):
    return jnp.concatenate([v, jnp.zeros_like(v)], axis=-1)


def kernel(x, attn_norm_g, w_in, conv_dw_w, conv_dw_b, conv_ln_g, conv_ln_b, q_norm_g, k_norm_g,
           cmp_pos_k, cmp_pos_v, cmp_k_w1, cmp_k_b1, cmp_k_w2, cmp_k_b2,
           cmp_v_w1, cmp_v_b1, cmp_v_w2, cmp_v_b2, w_out, ffn_norm_g,
           dense_w1, dense_w3, dense_w2, router_w, moe_w1, moe_w3, moe_w2):
    assert x.shape == (BATCH, SEQ, D_MODEL) and x.dtype == F32
    bd = _group_mean_matrix()
    ovlt = _overlap_matrix_t()

    x2 = x.reshape(TOKENS, D_MODEL)
    for layer in range(DEPTH):
        w_cols, w_rows = _in_proj_weights(w_in[layer])
        qg = _dup(q_norm_g[layer])[None, :] * (LOG2_E * HEAD_DIM ** -0.5)
        kgs = _dup(k_norm_g[layer, 1])[None, :]
        kgw = _dup(k_norm_g[layer, 2])[None, :]
        u, q, kcvc, ks, kw, vst, vwt, gatet = _in_proj(
            x2, attn_norm_g[layer][None, :], w_cols, w_rows, qg, kgs, kgw, bd)

        conv = _conv_module(u.reshape(BATCH, SEQ, CONV_WIDTH), conv_dw_w[layer], conv_dw_b[layer][None, :],
                            conv_ln_g[layer][None, :], conv_ln_b[layer][None, :], bd)

        xr = kcvc.reshape(BATCH, SEQ, 2, N_KV_HEADS, HEAD_DIM).transpose(2, 0, 3, 1, 4)
        xr = xr.reshape(2, BATCH, N_KV_HEADS, CMP_ROWS, CMP_CHUNK)
        pos = jnp.stack([cmp_pos_k[layer], cmp_pos_v[layer]]).reshape(2, 1, CMP_BLOCK * HEAD_DIM)
        w1 = jnp.stack([cmp_k_w1[layer], cmp_v_w1[layer]]).astype(BF16)
        b1 = jnp.stack([cmp_k_b1[layer], cmp_v_b1[layer]])[:, None, :]
        w2 = _zero_pad(jnp.stack([cmp_k_w2[layer], cmp_v_w2[layer]])).astype(BF16)
        b2 = _zero_pad(jnp.stack([cmp_k_b2[layer], cmp_v_b2[layer]]))[:, None, :]
        kvc = _compress(xr, pos, w1, b1, w2, b2, _zero_pad(k_norm_g[layer, 0])[None, :])

        seq3 = lambda a: a.reshape(BATCH, SEQ, a.shape[-1])
        attn = _attention(seq3(q), gatet, kvc, seq3(ks), vst, seq3(kw), vwt, ovlt)

        w_o = w_out[layer].astype(BF16)
        conv2 = conv.reshape(TOKENS, CONV_WIDTH)
        attn2 = attn.reshape(TOKENS, ATTN_WIDTH)
        g_ffn = ffn_norm_g[layer][None, :]
        i = layer // 2
        if layer % 2 == 0:
            x1, h = _out_proj(x2, conv2, attn2, w_o, g_ffn)
            x2 = _dense_ffn(x1, h, dense_w1[i].astype(BF16), dense_w3[i].astype(BF16),
                            dense_w2[i].astype(BF16))
        else:
            r = jnp.pad(router_w[i], ((0, 0), (0, LANES - N_EXPERTS)))
            r_hi = r.astype(BF16)
            r_lo = (r - r_hi.astype(F32)).astype(BF16)
            x1, h, comb = _out_proj(x2, conv2, attn2, w_o, g_ffn, jnp.stack([r_hi, r_lo]))
            x2 = _moe_ffn(x1, h, comb, moe_w1[i].astype(BF16), moe_w3[i].astype(BF16),
                          moe_w2[i].astype(BF16))
    return x2.reshape(BATCH, SEQ, D_MODEL)
```

```python
import functools

import numpy as np
import jax
import jax.numpy as jnp
from jax import lax
from jax.experimental import pallas as pl
from jax.experimental.pallas import tpu as pltpu

D_MODEL = 1024
BATCH = 8
SEQ = 2048
DEPTH = 2
TOKENS = BATCH * SEQ

CONV_WIDTH = 512
CONV_GROUP = 64
CONV_KERNEL = 31
N_HEADS = 8
HEAD_DIM = 64
N_KV_HEADS = 2
GQA = N_HEADS // N_KV_HEADS
ATTN_WIDTH = N_HEADS * HEAD_DIM
KV_WIDTH = N_KV_HEADS * HEAD_DIM
N_BRANCH = 3
CMP_BLOCK = 32
CMP_STRIDE = 16
CMP_HIDDEN = 256
N_CMP = (SEQ - CMP_BLOCK) // CMP_STRIDE + 1
SEL_BLOCK = 64
SEL_TOPK = 8
N_SEL = SEQ // SEL_BLOCK
WINDOW = 512
Q_BLOCK = 256
FORCE_BONUS = 1e4
NEG_INF = -1e30
LOG2_E = 1.4426950408889634
D_FF = 2816
N_EXPERTS = 8
D_FF_EXPERT = 3584
EPS = 1e-6

LANES = 128
SUBLANES = 8
VMEM_LIMIT = 48 * 1024 * 1024

F32 = jnp.float32
BF16 = jnp.bfloat16

_OFF_VAL = 0
_OFF_GATE = 512
_OFF_Q = 1024
_OFF_KCVC = 1536
_OFF_KS = 1792
_OFF_KW = 1920
IN_COLS = 2048
T_ROWS = 5 * LANES

_BASE_KV = 2 * CONV_WIDTH + ATTN_WIDTH


def _in_proj_weights(w):
    piece = lambda i: w[:, _BASE_KV + i * KV_WIDTH:_BASE_KV + (i + 1) * KV_WIDTH]
    w_cols = jnp.concatenate([w[:, :_BASE_KV + 2 * KV_WIDTH], piece(2), piece(4)], axis=1)
    assert w_cols.shape[1] == IN_COLS
    rows = []
    for i in (3, 5):
        vt = piece(i).T
        for head in range(N_KV_HEADS):
            rows += [vt[head * HEAD_DIM:(head + 1) * HEAD_DIM], jnp.zeros((HEAD_DIM, D_MODEL), w.dtype)]
    n_gate = N_HEADS * N_BRANCH
    glog = _BASE_KV + 6 * KV_WIDTH
    rows += [w[:, glog:glog + n_gate].T, jnp.zeros((LANES - n_gate, D_MODEL), w.dtype)]
    w_rows = jnp.concatenate(rows, axis=0)
    assert w_rows.shape[0] == T_ROWS
    return w_cols.astype(BF16), w_rows.astype(BF16)


def _sigmoid(v):
    return 1.0 / (1.0 + jnp.exp(-v))


def _dot(a, b):
    return jnp.dot(a, b, preferred_element_type=F32)


def _dot_nt(a, b):
    return lax.dot_general(a, b, (((1,), (1,)), ((), ())), preferred_element_type=F32)


def _split_dot(v, m):
    hi = v.astype(BF16)
    lo = (v - hi.astype(F32)).astype(BF16)
    return _dot(hi, m) + _dot(lo, m)


IN_TM = 512


def _in_proj_kernel(x_ref, g_ref, w_ref, wt_ref, qg_ref, kgs_ref, kgw_ref, bd_ref,
                    u_ref, q_ref, kcvc_ref, ks_ref, kw_ref, vst_ref, vwt_ref, gatet_ref):
    x = x_ref[...]
    ms = jnp.mean(x * x, axis=-1, keepdims=True)
    h = (x * lax.rsqrt(ms + EPS) * g_ref[...]).astype(BF16)

    def proj(lo, width):
        return _dot(h, w_ref[:, lo:lo + width])

    u_ref[...] = proj(_OFF_VAL, CONV_WIDTH) * _sigmoid(proj(_OFF_GATE, CONV_WIDTH))

    bd = bd_ref[...]
    lane = lax.broadcasted_iota(jnp.int32, (IN_TM, LANES), 1)
    lower = lane < HEAD_DIM

    def head_slots(pair, upper, out_ref, slot):
        out_ref[:, slot * LANES:(slot + 1) * LANES] = jnp.where(lower, pair, upper).astype(BF16)
        swapped = pltpu.roll(pair, HEAD_DIM, axis=1)
        out_ref[:, (slot + 1) * LANES:(slot + 2) * LANES] = jnp.where(lower, swapped, upper).astype(BF16)

    q = proj(_OFF_Q, ATTN_WIDTH)
    for c in range(ATTN_WIDTH // LANES):
        qc = q[:, c * LANES:(c + 1) * LANES]
        msq = _split_dot(qc * qc, bd)
        head_slots(qc * lax.rsqrt(msq + EPS) * qg_ref[...], 0.0, q_ref, 2 * c)

    kcvc_ref[...] = proj(_OFF_KCVC, 2 * KV_WIDTH)

    def normed_k(off, kg_ref):
        k = proj(off, KV_WIDTH)
        return k * lax.rsqrt(_split_dot(k * k, bd) + EPS) * kg_ref[...]

    tok = (pl.program_id(0) % (SEQ // IN_TM)) * IN_TM + lax.broadcasted_iota(jnp.int32, (IN_TM, 1), 0)
    block_hot = jnp.where(lane - HEAD_DIM == tok // SEL_BLOCK, 1.0, 0.0)
    head_slots(normed_k(_OFF_KS, kgs_ref), block_hot, ks_ref, 0)
    head_slots(normed_k(_OFF_KW, kgw_ref), 0.0, kw_ref, 0)

    zt = _dot_nt(wt_ref[...], h)
    ones_rows = lax.broadcasted_iota(jnp.int32, (LANES, IN_TM), 0) >= HEAD_DIM
    for j, out_ref in enumerate((vst_ref, vst_ref, vwt_ref, vwt_ref)):
        blk = zt[j * LANES:(j + 1) * LANES]
        out_ref[0, j % N_KV_HEADS] = jnp.where(ones_rows, 1.0, blk).astype(BF16)
    gatet_ref[0] = _sigmoid(zt[4 * LANES:5 * LANES])


def _in_proj(x2, g, w, wt, qg, kgs, kgw, bd):
    tm = IN_TM
    per_seq = SEQ // tm
    row = lambda width: pl.BlockSpec((tm, width), lambda i: (i, 0))
    full = lambda a: pl.BlockSpec(a.shape, lambda i: (0,) * a.ndim)
    vt_spec = pl.BlockSpec((1, N_KV_HEADS, LANES, tm), lambda i: (i // per_seq, 0, 0, i % per_seq))
    vt_shape = jax.ShapeDtypeStruct((BATCH, N_KV_HEADS, LANES, SEQ), BF16)
    out_shape = (
        jax.ShapeDtypeStruct((TOKENS, CONV_WIDTH), F32),
        jax.ShapeDtypeStruct((TOKENS, N_HEADS * LANES), BF16),
        jax.ShapeDtypeStruct((TOKENS, 2 * KV_WIDTH), F32),
        jax.ShapeDtypeStruct((TOKENS, 2 * LANES), BF16),
        jax.ShapeDtypeStruct((TOKENS, 2 * LANES), BF16),
        vt_shape,
        vt_shape,
        jax.ShapeDtypeStruct((BATCH, LANES, SEQ), F32),
    )
    out_specs = (row(CONV_WIDTH), row(N_HEADS * LANES), row(2 * KV_WIDTH), row(2 * LANES), row(2 * LANES),
                 vt_spec, vt_spec,
                 pl.BlockSpec((1, LANES, tm), lambda i: (i // per_seq, 0, i % per_seq)))
    return pl.pallas_call(
        _in_proj_kernel,
        grid=(TOKENS // tm,),
        in_specs=[row(D_MODEL), full(g), full(w), full(wt), full(qg), full(kgs), full(kgw), full(bd)],
        out_specs=out_specs,
        out_shape=out_shape,
        compiler_params=pltpu.CompilerParams(
            dimension_semantics=("arbitrary",), vmem_limit_bytes=VMEM_LIMIT),
        name="in_proj",
    )(x2, g, w, wt, qg, kgs, kgw, bd)


CONV_TR = 512
CONV_HALO = 32
CONV_RS = 64
CONV_PARTIALS = 4


def _conv_kernel(ucur_ref, uprev_ref, w_ref, b_ref, lg_ref, lb_ref, bd_ref, o_ref, win_ref, y_ref):
    i = pl.program_id(1)
    first = CONV_HALO - (CONV_KERNEL - 1)
    shifted_rows = CONV_HALO + CONV_TR - SUBLANES
    for c in range(CONV_WIDTH // LANES):
        lanes = slice(c * LANES, (c + 1) * LANES)
        win_ref[0, c, 0:CONV_HALO, :] = jnp.where(i > 0, uprev_ref[0, :, lanes], 0.0)
        win_ref[0, c, CONV_HALO:CONV_HALO + CONV_TR, :] = ucur_ref[0, :, lanes]
        for s in range(1, SUBLANES):
            win_ref[s, c, 0:shifted_rows, :] = win_ref[0, c, s:s + shifted_rows, :]

        def body(r, carry):
            base = pl.multiple_of(r * CONV_RS, CONV_RS)
            accs = [None] * CONV_PARTIALS
            for k in range(CONV_KERNEL):
                off = first + k
                rows = pl.ds(base + (off // SUBLANES) * SUBLANES, CONV_RS)
                term = win_ref[off % SUBLANES, c, rows, :] * w_ref[k:k + 1, lanes]
                j = k % CONV_PARTIALS
                accs[j] = term if accs[j] is None else accs[j] + term
            y_ref[c, pl.ds(base, CONV_RS), :] = (accs[0] + accs[1]) + (accs[2] + accs[3]) + b_ref[:, lanes]
            return carry

        lax.fori_loop(0, CONV_TR // CONV_RS, body, 0)

    bd = bd_ref[...]
    for c in range(CONV_WIDTH // LANES):
        lanes = slice(c * LANES, (c + 1) * LANES)
        y = y_ref[c]
        d = y - _split_dot(y, bd)
        var = _split_dot(d * d, bd)
        yn = d * lax.rsqrt(var + EPS) * lg_ref[:, lanes] + lb_ref[:, lanes]
        o_ref[0, :, lanes] = (yn * _sigmoid(yn)).astype(BF16)


def _conv_module(u3, w, b, lg, lb, bd):
    halo_blocks = CONV_TR // CONV_HALO
    full = lambda a: pl.BlockSpec(a.shape, lambda bi, i: (0,) * a.ndim)
    return pl.pallas_call(
        _conv_kernel,
        grid=(BATCH, SEQ // CONV_TR),
        in_specs=[
            pl.BlockSpec((1, CONV_TR, CONV_WIDTH), lambda bi, i: (bi, i, 0)),
            pl.BlockSpec((1, CONV_HALO, CONV_WIDTH),
                         lambda bi, i: (bi, jnp.maximum(i * halo_blocks - 1, 0), 0)),
            full(w), full(b), full(lg), full(lb), full(bd),
        ],
        out_specs=pl.BlockSpec((1, CONV_TR, CONV_WIDTH), lambda bi, i: (bi, i, 0)),
        out_shape=jax.ShapeDtypeStruct((BATCH, SEQ, CONV_WIDTH), BF16),
        scratch_shapes=[pltpu.VMEM((SUBLANES, CONV_WIDTH // LANES, CONV_HALO + CONV_TR, LANES), F32),
                        pltpu.VMEM((CONV_WIDTH // LANES, CONV_TR, LANES), F32)],
        compiler_params=pltpu.CompilerParams(dimension_semantics=("arbitrary", "arbitrary")),
        name="conv_module",
    )(u3, u3, w, b, lg, lb, bd)


CMP_ROWS = SEQ // CMP_STRIDE
CMP_CHUNK = CMP_STRIDE * HEAD_DIM


def _compress_kernel(x_ref, pos_ref, w1_ref, b1_ref, w2_ref, b2_ref, w2t_ref, b2t_ref, kg_ref, o_ref):
    kind = pl.program_id(0)
    xa = x_ref[0, 0, 0]
    xb = pltpu.roll(xa, CMP_ROWS - 1, axis=0)
    a = (xa + pos_ref[0, :, 0:CMP_CHUNK]).astype(BF16)
    b = (xb + pos_ref[0, :, CMP_CHUNK:2 * CMP_CHUNK]).astype(BF16)
    hid = _dot(a, w1_ref[0, 0:CMP_CHUNK, :]) + _dot(b, w1_ref[0, CMP_CHUNK:2 * CMP_CHUNK, :]) + b1_ref[0]
    hid = (hid * _sigmoid(hid)).astype(BF16)
    out = _dot(hid, w2_ref[0]) + b2_ref[0]
    ms = jnp.sum(out * out, axis=-1, keepdims=True) * (1.0 / HEAD_DIM)
    normed = out * lax.rsqrt(ms + EPS) * kg_ref[...]
    out_t = _dot_nt(w2t_ref[0], hid) + b2t_ref[0]
    o_ref[0, 0, 0] = jnp.where(kind == 0, normed, out_t).astype(BF16)


def _compress(xr, pos, w1, b1, w2, b2, kg):
    w2t = jnp.swapaxes(w2, 1, 2)
    b2t = jnp.swapaxes(b2, 1, 2)
    per_kind = lambda a: pl.BlockSpec((1,) + a.shape[1:], lambda k, bi, h: (k,) + (0,) * (a.ndim - 1))
    return pl.pallas_call(
        _compress_kernel,
        grid=(2, BATCH, N_KV_HEADS),
        in_specs=[
            pl.BlockSpec((1, 1, 1, CMP_ROWS, CMP_CHUNK), lambda k, bi, h: (k, bi, h, 0, 0)),
            per_kind(pos), per_kind(w1), per_kind(b1), per_kind(w2), per_kind(b2),
            per_kind(w2t), per_kind(b2t),
            pl.BlockSpec(kg.shape, lambda k, bi, h: (0, 0)),
        ],
        out_specs=pl.BlockSpec((1, 1, 1, CMP_ROWS, LANES), lambda k, bi, h: (k, bi, h, 0, 0)),
        out_shape=jax.ShapeDtypeStruct((2, BATCH, N_KV_HEADS, CMP_ROWS, LANES), BF16),
        compiler_params=pltpu.CompilerParams(
            dimension_semantics=("arbitrary", "arbitrary", "arbitrary")),
        name="compress",
    )(xr, pos, w1, b1, w2, b2, w2t, b2t, kg)


QCOLS = GQA * Q_BLOCK
SEL_CHUNK = 256
WIN_CHUNK = 128
WIN_CHUNKS = (WINDOW + Q_BLOCK) // WIN_CHUNK
MASK_BIAS = -1e30


def _window_biases():
    k = np.arange(WIN_CHUNK)[:, None]
    t = (np.arange(QCOLS) % Q_BLOCK)[None, :]
    tables, index = [], {}
    for i in range(WIN_CHUNKS):
        key = i * WIN_CHUNK - WINDOW + k
        bias = np.where((key <= t) & (key > t - WINDOW), 0.0, MASK_BIAS)
        if bias.any():
            index[i] = len(tables)
            tables.append(bias)
    return np.stack(tables).astype(np.float32), index


_WIN_BIAS, _WIN_BIAS_INDEX = _window_biases()


def _attn_kernel(q_ref, gatet_ref, kc_ref, vct_ref, ks_ref, vst_ref, kw_ref, vwt_ref,
                 ovlt_ref, wbias_ref, o_ref, ssel_sc, swin_sc, acc_sc, out_sc):
    n = pl.program_id(1)
    start = n * Q_BLOCK
    col = lax.broadcasted_iota(jnp.int32, (1, QCOLS), 1)
    t_cols = start + (col & (Q_BLOCK - 1))
    tq = start + lax.broadcasted_iota(jnp.int32, (1, Q_BLOCK), 1)
    blk_row = lax.broadcasted_iota(jnp.int32, (N_SEL, 1), 0)
    blk_row_f = blk_row.astype(F32)
    cmp_end = lax.broadcasted_iota(jnp.int32, (CMP_ROWS, 1), 0) * CMP_STRIDE + (CMP_BLOCK - 1)

    kv_heads = range(N_KV_HEADS)
    kv_lanes = [slice(kvh * LANES, (kvh + 1) * LANES) for kvh in kv_heads]

    def gate_row(kvh, g, branch):
        r = (kvh * GQA + g) * N_BRANCH + branch
        return gatet_ref[0, r:r + 1, :]

    def head_rows(kvh, g):
        h = kvh * GQA + g
        return slice(h * HEAD_DIM, (h + 1) * HEAD_DIM)

    qs = []
    for kvh in kv_heads:
        h0 = kvh * GQA
        qs.append(jnp.concatenate(
            [q_ref[0, :, (h0 + g) * LANES:(h0 + g + 1) * LANES] for g in range(GQA)], axis=0))

    imp = []
    mask_c = cmp_end <= t_cols
    for kvh in kv_heads:
        s_c = jnp.where(mask_c, _dot_nt(kc_ref[0, kvh], qs[kvh]), NEG_INF)
        m_c = jnp.max(s_c, axis=0, keepdims=True)
        p_c = jnp.where(mask_c, jnp.exp2(s_c - m_c), 0.0)
        p_c = p_c / jnp.maximum(jnp.sum(p_c, axis=0, keepdims=True), 1e-30)
        o_cmp = _dot(vct_ref[0, kvh], p_c.astype(BF16))
        for g in range(GQA):
            out_sc[head_rows(kvh, g), :] = gate_row(kvh, g, 0) * o_cmp[0:HEAD_DIM, g * Q_BLOCK:(g + 1) * Q_BLOCK]
        p_sum = p_c[:, 0:Q_BLOCK]
        for g in range(1, GQA):
            p_sum = p_sum + p_c[:, g * Q_BLOCK:(g + 1) * Q_BLOCK]
        p_hi = p_sum.astype(BF16)
        p_lo = (p_sum - p_hi.astype(F32)).astype(BF16)
        imp.append(_dot(ovlt_ref[...], p_hi) + _dot(ovlt_ref[...], p_lo))

    tq2 = jnp.concatenate([tq] * N_KV_HEADS, axis=1)
    cur = tq2 >> 6
    forced = (blk_row == 0) | (blk_row == cur) | (blk_row == cur - 1)
    causal_blk = blk_row * SEL_BLOCK <= tq2
    score = jnp.where(causal_blk, jnp.concatenate(imp, axis=1) + jnp.where(forced, FORCE_BONUS, 0.0), NEG_INF)
    sel = jnp.zeros((N_SEL, N_KV_HEADS * Q_BLOCK), F32)
    for _ in range(SEL_TOPK):
        best = jnp.max(score, axis=0, keepdims=True)
        idx = jnp.min(jnp.where(score == best, blk_row_f, float(LANES)), axis=0, keepdims=True)
        hit = blk_row_f == idx
        sel = jnp.where(hit, 1.0, sel)
        score = jnp.where(hit, -3e38, score)
    sel_bias = (sel - 1.0) * (-MASK_BIAS)
    qs_sel = []
    for kvh in kv_heads:
        rows = jnp.concatenate([jnp.zeros((HEAD_DIM, Q_BLOCK), F32),
                                sel_bias[:, kvh * Q_BLOCK:(kvh + 1) * Q_BLOCK],
                                jnp.zeros((LANES - HEAD_DIM - N_SEL, Q_BLOCK), F32)], axis=0)
        bias_q = rows.T.astype(BF16)
        qs_sel.append(qs[kvh] + jnp.concatenate([bias_q] * GQA, axis=0))

    for kvh in kv_heads:
        mxw = jnp.full((1, QCOLS), NEG_INF, F32)
        kbs = [n * (Q_BLOCK // WIN_CHUNK) - WINDOW // WIN_CHUNK + i for i in range(WIN_CHUNKS)]
        k_win = [pl.multiple_of(jnp.maximum(kb, 0) * WIN_CHUNK, WIN_CHUNK) for kb in kbs]
        k_band = jnp.concatenate([kw_ref[0, pl.ds(k0, WIN_CHUNK), kv_lanes[kvh]] for k0 in k_win], axis=0)
        s_band = _dot_nt(k_band, qs[kvh])
        for i, kb in enumerate(kbs):
            s = s_band[i * WIN_CHUNK:(i + 1) * WIN_CHUNK]
            if i in _WIN_BIAS_INDEX:
                s = s + wbias_ref[_WIN_BIAS_INDEX[i]]
            if i < WINDOW // WIN_CHUNK:
                s = s + jnp.where(kb >= 0, 0.0, MASK_BIAS)
            swin_sc[kvh, i * WIN_CHUNK:(i + 1) * WIN_CHUNK, :] = s
            mxw = jnp.maximum(mxw, jnp.max(s, axis=0, keepdims=True))
        p = jnp.exp2(swin_sc[kvh] - mxw).astype(BF16)
        v_band = jnp.concatenate([vwt_ref[0, kvh, :, pl.ds(k0, WIN_CHUNK)] for k0 in k_win], axis=1)
        accw = _dot(v_band, p)
        o_win = accw[0:HEAD_DIM] * (1.0 / accw[HEAD_DIM:HEAD_DIM + 1])
        for g in range(GQA):
            out_sc[head_rows(kvh, g), :] += gate_row(kvh, g, 2) * o_win[:, g * Q_BLOCK:(g + 1) * Q_BLOCK]

    def sel_scores(kvh, k0):
        return _dot_nt(ks_ref[0, pl.ds(k0, SEL_CHUNK), kv_lanes[kvh]], qs_sel[kvh])

    def sel_pass1(i, mx):
        k0 = pl.multiple_of(i * SEL_CHUNK, SEL_CHUNK)
        out = []
        for kvh in kv_heads:
            s = sel_scores(kvh, k0)
            ssel_sc[kvh, pl.ds(k0, SEL_CHUNK), :] = s
            out.append(jnp.maximum(mx[kvh], jnp.max(s, axis=0, keepdims=True)))
        return tuple(out)

    last = (start + Q_BLOCK - 1) // SEL_CHUNK
    mx = lax.fori_loop(0, last, sel_pass1, (jnp.full((1, QCOLS), NEG_INF, F32),) * N_KV_HEADS)
    k_last = pl.multiple_of(last * SEL_CHUNK, SEL_CHUNK)
    causal = k_last + lax.broadcasted_iota(jnp.int32, (SEL_CHUNK, 1), 0) <= t_cols
    mx = list(mx)
    for kvh in kv_heads:
        s = jnp.where(causal, sel_scores(kvh, k_last), NEG_INF)
        ssel_sc[kvh, pl.ds(k_last, SEL_CHUNK), :] = s
        mx[kvh] = jnp.maximum(mx[kvh], jnp.max(s, axis=0, keepdims=True))
        acc_sc[kvh] = jnp.zeros((LANES, QCOLS), F32)

    def sel_pass2(i, carry):
        k0 = pl.multiple_of(i * SEL_CHUNK, SEL_CHUNK)
        for kvh in kv_heads:
            p = jnp.exp2(ssel_sc[kvh, pl.ds(k0, SEL_CHUNK), :] - mx[kvh]).astype(BF16)
            acc_sc[kvh] += _dot(vst_ref[0, kvh, :, pl.ds(k0, SEL_CHUNK)], p)
        return carry

    lax.fori_loop(0, last + 1, sel_pass2, 0)
    for kvh in kv_heads:
        acc = acc_sc[kvh]
        o_sel = acc[0:HEAD_DIM] * (1.0 / acc[HEAD_DIM:HEAD_DIM + 1])
        for g in range(GQA):
            out_sc[head_rows(kvh, g), :] += gate_row(kvh, g, 1) * o_sel[:, g * Q_BLOCK:(g + 1) * Q_BLOCK]

    for c in range(N_HEADS // 2):
        o_ref[0, :, c * LANES:(c + 1) * LANES] = out_sc[c * LANES:(c + 1) * LANES, :].T.astype(BF16)


def _attention(q3, gatet, kvc, ks3, vst, kw3, vwt, ovlt):
    wbias = jnp.asarray(_WIN_BIAS)
    qblk = lambda width: pl.BlockSpec((1, Q_BLOCK, width), lambda bi, n: (bi, n, 0))
    seq = pl.BlockSpec((1, SEQ, 2 * LANES), lambda bi, n: (bi, 0, 0))
    seq_t = pl.BlockSpec((1, N_KV_HEADS, LANES, SEQ), lambda bi, n: (bi, 0, 0, 0))
    cmp_spec = lambda kind: pl.BlockSpec((None, 1, N_KV_HEADS, CMP_ROWS, LANES),
                                         lambda bi, n: (kind, bi, 0, 0, 0))
    full = lambda a: pl.BlockSpec(a.shape, lambda bi, n: (0,) * a.ndim)
    return pl.pallas_call(
        _attn_kernel,
        grid=(BATCH, SEQ // Q_BLOCK),
        in_specs=[qblk(N_HEADS * LANES), pl.BlockSpec((1, LANES, Q_BLOCK), lambda bi, n: (bi, 0, n)),
                  cmp_spec(0), cmp_spec(1), seq, seq_t, seq, seq_t, full(ovlt), full(wbias)],
        out_specs=qblk(ATTN_WIDTH),
        out_shape=jax.ShapeDtypeStruct((BATCH, SEQ, ATTN_WIDTH), BF16),
        scratch_shapes=[pltpu.VMEM((N_KV_HEADS, SEQ, QCOLS), F32),
                        pltpu.VMEM((N_KV_HEADS, WIN_CHUNKS * WIN_CHUNK, QCOLS), F32),
                        pltpu.VMEM((N_KV_HEADS, LANES, QCOLS), F32),
                        pltpu.VMEM((ATTN_WIDTH, Q_BLOCK), F32)],
        compiler_params=pltpu.CompilerParams(
            dimension_semantics=("arbitrary", "arbitrary"), vmem_limit_bytes=VMEM_LIMIT),
        name="nsa_attention",
    )(q3, gatet, kvc, kvc, ks3, vst, kw3, vwt, ovlt, wbias)


OUT_TM = 512


def _out_proj_kernel(x_ref, conv_ref, attn_ref, w_ref, g_ref, x1_ref, h_ref):
    x1 = (x_ref[...] + _dot(conv_ref[...], w_ref[0:CONV_WIDTH, :])
          + _dot(attn_ref[...], w_ref[CONV_WIDTH:CONV_WIDTH + ATTN_WIDTH, :]))
    x1_ref[...] = x1
    ms = jnp.mean(x1 * x1, axis=-1, keepdims=True)
    h_ref[...] = (x1 * lax.rsqrt(ms + EPS) * g_ref[...]).astype(BF16)


def _out_proj_router_kernel(x_ref, conv_ref, attn_ref, w_ref, g_ref, r_ref, x1_ref, h_ref, comb_ref):
    x1 = (x_ref[...] + _dot(conv_ref[...], w_ref[0:CONV_WIDTH, :])
          + _dot(attn_ref[...], w_ref[CONV_WIDTH:CONV_WIDTH + ATTN_WIDTH, :]))
    x1_ref[...] = x1
    ms = jnp.mean(x1 * x1, axis=-1, keepdims=True)
    h = x1 * lax.rsqrt(ms + EPS) * g_ref[...]
    h_ref[...] = h.astype(BF16)
    h_hi = h.astype(BF16)
    h_lo = (h - h_hi.astype(F32)).astype(BF16)
    logits = _dot(h_hi, r_ref[0]) + _dot(h_lo, r_ref[0]) + _dot(h_hi, r_ref[1])
    lane = lax.broadcasted_iota(jnp.int32, (1, LANES), 1).astype(F32)
    logits = jnp.where(lane < N_EXPERTS, logits, NEG_INF)
    v1 = jnp.max(logits, axis=-1, keepdims=True)
    i1 = jnp.min(jnp.where(logits == v1, lane, float(LANES)), axis=-1, keepdims=True)
    hit1 = lane == i1
    rest = jnp.where(hit1, -3e38, logits)
    v2 = jnp.max(rest, axis=-1, keepdims=True)
    i2 = jnp.min(jnp.where(rest == v2, lane, float(LANES)), axis=-1, keepdims=True)
    hit2 = lane == i2
    e2 = jnp.exp(v2 - v1)
    w_top = 1.0 / (1.0 + e2)
    comb_ref[...] = jnp.where(hit1, w_top, 0.0) + jnp.where(hit2, e2 * w_top, 0.0)


def _out_proj(x2, conv2, attn2, w, g, router=None):
    tm = OUT_TM
    row = lambda width: pl.BlockSpec((tm, width), lambda i: (i, 0))
    full = lambda a: pl.BlockSpec(a.shape, lambda i: (0,) * a.ndim)
    in_specs = [row(D_MODEL), row(CONV_WIDTH), row(ATTN_WIDTH), full(w), full(g)]
    out_shape = [jax.ShapeDtypeStruct((TOKENS, D_MODEL), F32), jax.ShapeDtypeStruct((TOKENS, D_MODEL), BF16)]
    out_specs = [row(D_MODEL), row(D_MODEL)]
    args = [x2, conv2, attn2, w, g]
    body = _out_proj_kernel
    if router is not None:
        body = _out_proj_router_kernel
        in_specs.append(full(router))
        args.append(router)
        out_shape.append(jax.ShapeDtypeStruct((TOKENS, LANES), F32))
        out_specs.append(row(LANES))
    return pl.pallas_call(
        body,
        grid=(TOKENS // tm,),
        in_specs=in_specs,
        out_specs=tuple(out_specs),
        out_shape=tuple(out_shape),
        compiler_params=pltpu.CompilerParams(
            dimension_semantics=("arbitrary",), vmem_limit_bytes=VMEM_LIMIT),
        name="out_proj",
    )(*args)


FFN_TM = 512
FFN_TF = 1408


def _ffn_kernel(x1_ref, h_ref, w1_ref, w3_ref, w2_ref, o_ref):
    h = h_ref[...]
    out = x1_ref[...]
    for f in range(D_FF // FFN_TF):
        cols = slice(f * FFN_TF, (f + 1) * FFN_TF)
        a = _dot(h, w1_ref[:, cols])
        act = (a * _sigmoid(a) * _dot(h, w3_ref[:, cols])).astype(BF16)
        out = out + _dot(act, w2_ref[cols, :])
    o_ref[...] = out


def _dense_ffn(x1, h, w1, w3, w2):
    tm = FFN_TM
    resident = pl.BlockSpec(memory_space=pltpu.VMEM)
    return pl.pallas_call(
        _ffn_kernel,
        grid=(TOKENS // tm,),
        in_specs=[pl.BlockSpec((tm, D_MODEL), lambda i: (i, 0)), pl.BlockSpec((tm, D_MODEL), lambda i: (i, 0)),
                  resident, resident, resident],
        out_specs=pl.BlockSpec((tm, D_MODEL), lambda i: (i, 0)),
        out_shape=jax.ShapeDtypeStruct((TOKENS, D_MODEL), F32),
        compiler_params=pltpu.CompilerParams(
            dimension_semantics=("arbitrary",), vmem_limit_bytes=VMEM_LIMIT),
        name="dense_ffn",
    )(x1, h, w1, w3, w2)


MOE_SB = 512
MOE_TM = 512
MOE_GT = 256
MOE_CT = 512
MOE_TF = 1792
MOE_NB = TOKENS // MOE_SB
MOE_TILES = (2 * TOKENS) // MOE_TM + N_EXPERTS
MOE_GTILES = MOE_TILES * (MOE_TM // MOE_GT)
MOE_CPAIRS = MOE_TILES * (MOE_TM // MOE_CT) + N_EXPERTS * MOE_NB


def _moe_rank_kernel(comb_ref, tri_ref, rank_ref, cend_ref, carry_ref):
    @pl.when(pl.program_id(0) == 0)
    def _():
        carry_ref[...] = jnp.zeros((1, LANES), F32)

    routed = jnp.where(comb_ref[...] > 0.0, 1.0, 0.0).astype(BF16)
    carry = carry_ref[...]
    rank_ref[...] = carry + _dot(tri_ref[...], routed)
    carry = carry + jnp.sum(routed.astype(F32), axis=0, keepdims=True)
    carry_ref[...] = carry
    cend_ref[0] = jnp.broadcast_to(carry, (SUBLANES, LANES))


def _moe_rank(comb):
    idx = np.arange(MOE_SB)
    tri = jnp.asarray(idx[:, None] > idx[None, :], BF16)
    return pl.pallas_call(
        _moe_rank_kernel,
        grid=(MOE_NB,),
        in_specs=[pl.BlockSpec((MOE_SB, LANES), lambda b: (b, 0)),
                  pl.BlockSpec((MOE_SB, MOE_SB), lambda b: (0, 0))],
        out_specs=(pl.BlockSpec((MOE_SB, LANES), lambda b: (b, 0)),
                   pl.BlockSpec((1, SUBLANES, LANES), lambda b: (b, 0, 0))),
        out_shape=(jax.ShapeDtypeStruct((TOKENS, LANES), F32),
                   jax.ShapeDtypeStruct((MOE_NB, SUBLANES, LANES), F32)),
        scratch_shapes=[pltpu.VMEM((1, LANES), F32)],
        compiler_params=pltpu.CompilerParams(dimension_semantics=("arbitrary",)),
        name="moe_rank",
    )(comb, tri)


def _count_le(ends, v):
    return jnp.sum(ends[None, :] <= v[:, None], axis=1).astype(jnp.int32)


def _moe_schedule(cend):
    i32 = jnp.int32
    counts = cend[-1]
    cstart = jnp.concatenate([jnp.zeros((1, N_EXPERTS), i32), cend[:-1]], axis=0)
    tiles_e = (counts + MOE_TM - 1) // MOE_TM
    tile_end = jnp.cumsum(tiles_e)
    tile_start = tile_end - tiles_e
    n_tiles = tile_end[-1]
    group_start = tile_start * MOE_TM

    d = jnp.arange(MOE_TILES, dtype=i32)
    t_valid = d < n_tiles
    t_exp = jnp.minimum(_count_le(tile_end, d), N_EXPERTS - 1)
    t_exp = jnp.where(t_valid, t_exp, t_exp[jnp.maximum(n_tiles - 1, 0)])

    per_tile = MOE_TM // MOE_GT
    g = jnp.arange(MOE_GTILES, dtype=i32)
    g_exp = t_exp[g // per_tile]
    r0 = (g // per_tile - tile_start[g_exp]) * MOE_TM + (g % per_tile) * MOE_GT
    r1 = jnp.minimum(r0 + MOE_GT, counts[g_exp])
    gt_valid = t_valid[g // per_tile] & (r0 < counts[g_exp])
    b_lo = jnp.sum(cend[:, g_exp] <= r0[None, :], axis=0).astype(i32)
    b_hi = jnp.sum(cstart[:, g_exp] < r1[None, :], axis=0).astype(i32) - 1
    gather = (g_exp, jnp.where(gt_valid, b_lo, 0), jnp.where(gt_valid, b_hi - b_lo + 1, 0))

    lo = group_start[None, :] + cstart
    hi = group_start[None, :] + cend
    t_lo = (lo // MOE_CT).reshape(-1)
    n_t = jnp.where(hi > lo, (hi - 1) // MOE_CT - lo // MOE_CT + 1, 0).reshape(-1)
    q_end = jnp.cumsum(n_t)
    q_start = q_end - n_t
    p = jnp.arange(MOE_CPAIRS, dtype=i32)
    be = jnp.minimum(_count_le(q_end, p), MOE_NB * N_EXPERTS - 1)
    c_tile = jnp.clip(t_lo[be] + p - q_start[be], 0, MOE_TILES * (MOE_TM // MOE_CT) - 1)
    combine = (q_start[::N_EXPERTS], q_end[N_EXPERTS - 1::N_EXPERTS], c_tile, be % N_EXPERTS)
    return group_start, t_exp, t_valid.astype(i32), gather, combine


def _moe_gather_kernel(exp_ref, first_ref, count_ref, h_ref, pos_ref, xs_ref):
    g = pl.program_id(0)
    rows = (g * MOE_GT + lax.broadcasted_iota(jnp.int32, (MOE_GT, 1), 0)).astype(F32)
    xs_ref[...] = jnp.zeros((MOE_GT, D_MODEL), BF16)

    def body(j, carry):
        t0 = pl.multiple_of((first_ref[g] + j) * MOE_SB, MOE_SB)
        pos = pos_ref[pl.ds(exp_ref[g], 1), pl.ds(t0, MOE_SB)]
        onehot = jnp.where(rows == pos, 1.0, 0.0).astype(BF16)
        xs_ref[...] += _dot(onehot, h_ref[pl.ds(t0, MOE_SB), :]).astype(BF16)
        return carry

    lax.fori_loop(0, count_ref[g], body, 0)


def _moe_gather(sched, h, pos_t):
    spec = pltpu.PrefetchScalarGridSpec(
        num_scalar_prefetch=3,
        grid=(MOE_GTILES,),
        in_specs=[pl.BlockSpec(memory_space=pltpu.VMEM), pl.BlockSpec(memory_space=pltpu.VMEM)],
        out_specs=pl.BlockSpec((MOE_GT, D_MODEL), lambda g, e, b, c: (g, 0)),
    )
    return pl.pallas_call(
        _moe_gather_kernel,
        grid_spec=spec,
        out_shape=jax.ShapeDtypeStruct((MOE_TILES * MOE_TM, D_MODEL), BF16),
        compiler_params=pltpu.CompilerParams(
            dimension_semantics=("arbitrary",), vmem_limit_bytes=VMEM_LIMIT),
        name="moe_gather",
    )(*sched, h, pos_t)


def _moe_expert_kernel(exp_ref, valid_ref, xs_ref, w1_ref, w3_ref, w2_ref, y_ref, acc_ref):
    d = pl.program_id(0)
    f = pl.program_id(1)

    @pl.when(valid_ref[d] == 1)
    def _():
        x = xs_ref[...]
        a = _dot(x, w1_ref[0])
        act = (a * _sigmoid(a) * _dot(x, w3_ref[0])).astype(BF16)
        part = _dot(act, w2_ref[0])

        @pl.when(f == 0)
        def _():
            acc_ref[...] = part

        @pl.when(f > 0)
        def _():
            acc_ref[...] += part

        @pl.when(f == pl.num_programs(1) - 1)
        def _():
            y_ref[...] = acc_ref[...].astype(BF16)

    @pl.when((valid_ref[d] == 0) & (f == 0))
    def _():
        y_ref[...] = jnp.zeros((MOE_TM, D_MODEL), BF16)


def _moe_experts(t_exp, t_valid, xs, w1, w3, w2):
    tf = MOE_TF
    nf = D_FF_EXPERT // tf
    fidx = lambda d, f, v: f * v[d] + (nf - 1) * (1 - v[d])
    spec = pltpu.PrefetchScalarGridSpec(
        num_scalar_prefetch=2,
        grid=(MOE_TILES, nf),
        in_specs=[pl.BlockSpec((MOE_TM, D_MODEL), lambda d, f, e, v: (d, 0)),
                  pl.BlockSpec((1, D_MODEL, tf), lambda d, f, e, v: (e[d], 0, fidx(d, f, v))),
                  pl.BlockSpec((1, D_MODEL, tf), lambda d, f, e, v: (e[d], 0, fidx(d, f, v))),
                  pl.BlockSpec((1, tf, D_MODEL), lambda d, f, e, v: (e[d], fidx(d, f, v), 0))],
        out_specs=pl.BlockSpec((MOE_TM, D_MODEL), lambda d, f, e, v: (d, 0)),
        scratch_shapes=[pltpu.VMEM((MOE_TM, D_MODEL), F32)],
    )
    return pl.pallas_call(
        _moe_expert_kernel,
        grid_spec=spec,
        out_shape=jax.ShapeDtypeStruct((MOE_TILES * MOE_TM, D_MODEL), BF16),
        compiler_params=pltpu.CompilerParams(
            dimension_semantics=("arbitrary", "arbitrary"), vmem_limit_bytes=VMEM_LIMIT),
        name="moe_experts",
    )(t_exp, t_valid, xs, w1, w3, w2)


def _moe_combine_kernel(first_ref, end_ref, tile_ref, exp_ref, y_hbm, pos_ref, comb_ref, x1_ref, o_ref,
                        ybuf, sem):
    b = pl.program_id(0)
    first = first_ref[b]
    end = end_ref[b]

    def window_copy(p, slot):
        r0 = pl.multiple_of(tile_ref[p] * MOE_CT, MOE_CT)
        return pltpu.make_async_copy(y_hbm.at[pl.ds(r0, MOE_CT), :], ybuf.at[slot], sem.at[slot])

    @pl.when(end > first)
    def _():
        window_copy(first, 0).start()

    o_ref[...] = x1_ref[...]
    lane = lax.broadcasted_iota(jnp.int32, (1, LANES), 1)
    window = lax.broadcasted_iota(jnp.int32, (1, MOE_CT), 1)

    def body(p, carry):
        slot = (p - first) & 1
        window_copy(p, slot).wait()

        @pl.when(p + 1 < end)
        def _():
            window_copy(p + 1, 1 - slot).start()

        pick = lane == exp_ref[p]
        pos = jnp.sum(jnp.where(pick, pos_ref[...], 0.0), axis=-1, keepdims=True)
        wgt = jnp.sum(jnp.where(pick, comb_ref[...], 0.0), axis=-1, keepdims=True)
        cols = (tile_ref[p] * MOE_CT + window).astype(F32)
        onehot = jnp.where(pos == cols, 1.0, 0.0).astype(BF16)
        o_ref[...] += wgt * _dot(onehot, ybuf[slot])
        return carry

    lax.fori_loop(first, end, body, 0)


def _moe_combine(sched, y, pos, comb, x1):
    tok = lambda width: pl.BlockSpec((MOE_SB, width), lambda b, *_: (b, 0))
    spec = pltpu.PrefetchScalarGridSpec(
        num_scalar_prefetch=4,
        grid=(MOE_NB,),
        in_specs=[pl.BlockSpec(memory_space=pl.ANY), tok(LANES), tok(LANES), tok(D_MODEL)],
        out_specs=tok(D_MODEL),
        scratch_shapes=[pltpu.VMEM((2, MOE_CT, D_MODEL), BF16), pltpu.SemaphoreType.DMA((2,))],
    )
    return pl.pallas_call(
        _moe_combine_kernel,
        grid_spec=spec,
        out_shape=jax.ShapeDtypeStruct((TOKENS, D_MODEL), F32),
        compiler_params=pltpu.CompilerParams(
            dimension_semantics=("arbitrary",), vmem_limit_bytes=VMEM_LIMIT),
        name="moe_combine",
    )(*sched, y, pos, comb, x1)


def _moe_ffn(x1, h, comb, w1, w3, w2):
    rank, cend = _moe_rank(comb)
    cend = jnp.round(cend[:, 0, :N_EXPERTS]).astype(jnp.int32)
    group_start, t_exp, t_valid, gather, combine = _moe_schedule(cend)
    start = jnp.zeros((LANES,), F32).at[:N_EXPERTS].set(group_start.astype(F32))
    pos = jnp.where(comb > 0.0, start[None, :] + rank, -1.0)
    xs = _moe_gather(gather, h, pos[:, :SUBLANES].T)
    y = _moe_experts(t_exp, t_valid, xs, w1, w3, w2)
    return _moe_combine(combine, y, pos, comb, x1)


def _group_mean_matrix():
    idx = np.arange(LANES)
    return jnp.asarray((idx[:, None] // CONV_GROUP == idx[None, :] // CONV_GROUP) / CONV_GROUP, BF16)


def _overlap_matrix_t():
    c_start = np.arange(CMP_ROWS) * CMP_STRIDE
    j = np.arange(N_SEL)
    ovl = ((c_start[None, :] < (j[:, None] + 1) * SEL_BLOCK)
           & (c_start[None, :] + CMP_BLOCK > j[:, None] * SEL_BLOCK)
           & (np.arange(CMP_ROWS)[None, :] < N_CMP))
    return jnp.asarray(ovl, BF16)


def _dup(v):
    return jnp.concatenate([v, v], axis=-1)


def _zero_pad(v):
    return jnp.concatenate([v, jnp.zeros_like(v)], axis=-1)


def kernel(x, attn_norm_g, w_in, conv_dw_w, conv_dw_b, conv_ln_g, conv_ln_b, q_norm_g, k_norm_g,
           cmp_pos_k, cmp_pos_v, cmp_k_w1, cmp_k_b1, cmp_k_w2, cmp_k_b2,
           cmp_v_w1, cmp_v_b1, cmp_v_w2, cmp_v_b2, w_out, ffn_norm_g,
           dense_w1, dense_w3, dense_w2, router_w, moe_w1, moe_w3, moe_w2):
    assert x.shape == (BATCH, SEQ, D_MODEL) and x.dtype == F32
    bd = _group_mean_matrix()
    ovlt = _overlap_matrix_t()

    x2 = x.reshape(TOKENS, D_MODEL)
    for layer in range(DEPTH):
        w_cols, w_rows = _in_proj_weights(w_in[layer])
        qg = _dup(q_norm_g[layer])[None, :] * (LOG2_E * HEAD_DIM ** -0.5)
        kgs = _dup(k_norm_g[layer, 1])[None, :]
        kgw = _dup(k_norm_g[layer, 2])[None, :]
        u, q, kcvc, ks, kw, vst, vwt, gatet = _in_proj(
            x2, attn_norm_g[layer][None, :], w_cols, w_rows, qg, kgs, kgw, bd)

        conv = _conv_module(u.reshape(BATCH, SEQ, CONV_WIDTH), conv_dw_w[layer], conv_dw_b[layer][None, :],
                            conv_ln_g[layer][None, :], conv_ln_b[layer][None, :], bd)

        xr = kcvc.reshape(BATCH, SEQ, 2, N_KV_HEADS, HEAD_DIM).transpose(2, 0, 3, 1, 4)
        xr = xr.reshape(2, BATCH, N_KV_HEADS, CMP_ROWS, CMP_CHUNK)
        pos = jnp.stack([cmp_pos_k[layer], cmp_pos_v[layer]]).reshape(2, 1, CMP_BLOCK * HEAD_DIM)
        w1 = jnp.stack([cmp_k_w1[layer], cmp_v_w1[layer]]).astype(BF16)
        b1 = jnp.stack([cmp_k_b1[layer], cmp_v_b1[layer]])[:, None, :]
        w2 = _zero_pad(jnp.stack([cmp_k_w2[layer], cmp_v_w2[layer]])).astype(BF16)
        b2 = _zero_pad(jnp.stack([cmp_k_b2[layer], cmp_v_b2[layer]]))[:, None, :]
        kvc = _compress(xr, pos, w1, b1, w2, b2, _zero_pad(k_norm_g[layer, 0])[None, :])

        seq3 = lambda a: a.reshape(BATCH, SEQ, a.shape[-1])
        attn = _attention(seq3(q), gatet, kvc, seq3(ks), vst, seq3(kw), vwt, ovlt)

        w_o = w_out[layer].astype(BF16)
        conv2 = conv.reshape(TOKENS, CONV_WIDTH)
        attn2 = attn.reshape(TOKENS, ATTN_WIDTH)
        g_ffn = ffn_norm_g[layer][None, :]
        i = layer // 2
        if layer % 2 == 0:
            x1, h = _out_proj(x2, conv2, attn2, w_o, g_ffn)
            x2 = _dense_ffn(x1, h, dense_w1[i].astype(BF16), dense_w3[i].astype(BF16),
                            dense_w2[i].astype(BF16))
        else:
            r = jnp.pad(router_w[i], ((0, 0), (0, LANES - N_EXPERTS)))
            r_hi = r.astype(BF16)
            r_lo = (r - r_hi.astype(F32)).astype(BF16)
            x1, h, comb = _out_proj(x2, conv2, attn2, w_o, g_ffn, jnp.stack([r_hi, r_lo]))
            x2 = _moe_ffn(x1, h, comb, moe_w1[i].astype(BF16), moe_w3[i].astype(BF16),
                          moe_w2[i].astype(BF16))
    return x2.reshape(BATCH, SEQ, D_MODEL)
```

```python
import functools

import numpy as np
import jax
import jax.numpy as jnp
from jax import lax
from jax.experimental import pallas as pl
from jax.experimental.pallas import tpu as pltpu

D_MODEL = 1024
BATCH = 8
SEQ = 2048
DEPTH = 2
TOKENS = BATCH * SEQ

CONV_WIDTH = 512
CONV_GROUP = 64
CONV_KERNEL = 31
N_HEADS = 8
HEAD_DIM = 64
N_KV_HEADS = 2
GQA = N_HEADS // N_KV_HEADS
ATTN_WIDTH = N_HEADS * HEAD_DIM
KV_WIDTH = N_KV_HEADS * HEAD_DIM
N_BRANCH = 3
CMP_BLOCK = 32
CMP_STRIDE = 16
CMP_HIDDEN = 256
N_CMP = (SEQ - CMP_BLOCK) // CMP_STRIDE + 1
SEL_BLOCK = 64
SEL_TOPK = 8
N_SEL = SEQ // SEL_BLOCK
WINDOW = 512
Q_BLOCK = 256
FORCE_BONUS = 1e4
NEG_INF = -1e30
LOG2_E = 1.4426950408889634
D_FF = 2816
N_EXPERTS = 8
D_FF_EXPERT = 3584
EPS = 1e-6

LANES = 128
SUBLANES = 8
VMEM_LIMIT = 48 * 1024 * 1024

F32 = jnp.float32
BF16 = jnp.bfloat16

_OFF_VAL = 0
_OFF_GATE = 512
_OFF_Q = 1024
_OFF_KCVC = 1536
_OFF_KS = 1792
_OFF_KW = 1920
IN_COLS = 2048
T_ROWS = 5 * LANES

_BASE_KV = 2 * CONV_WIDTH + ATTN_WIDTH


def _in_proj_weights(w):
    piece = lambda i: w[:, _BASE_KV + i * KV_WIDTH:_BASE_KV + (i + 1) * KV_WIDTH]
    w_cols = jnp.concatenate([w[:, :_BASE_KV + 2 * KV_WIDTH], piece(2), piece(4)], axis=1)
    assert w_cols.shape[1] == IN_COLS
    rows = []
    for i in (3, 5):
        vt = piece(i).T
        for head in range(N_KV_HEADS):
            rows += [vt[head * HEAD_DIM:(head + 1) * HEAD_DIM], jnp.zeros((HEAD_DIM, D_MODEL), w.dtype)]
    n_gate = N_HEADS * N_BRANCH
    glog = _BASE_KV + 6 * KV_WIDTH
    rows += [w[:, glog:glog + n_gate].T, jnp.zeros((LANES - n_gate, D_MODEL), w.dtype)]
    w_rows = jnp.concatenate(rows, axis=0)
    assert w_rows.shape[0] == T_ROWS
    return w_cols.astype(BF16), w_rows.astype(BF16)


def _sigmoid(v):
    return 1.0 / (1.0 + jnp.exp(-v))


def _dot(a, b):
    return jnp.dot(a, b, preferred_element_type=F32)


def _dot_nt(a, b):
    return lax.dot_general(a, b, (((1,), (1,)), ((), ())), preferred_element_type=F32)


def _split_dot(v, m):
    hi = v.astype(BF16)
    lo = (v - hi.astype(F32)).astype(BF16)
    return _dot(hi, m) + _dot(lo, m)


IN_TM = 512


def _in_proj_kernel(x_ref, g_ref, w_ref, wt_ref, qg_ref, kgs_ref, kgw_ref, bd_ref,
                    u_ref, q_ref, kcvc_ref, ks_ref, kw_ref, vst_ref, vwt_ref, gatet_ref):
    x = x_ref[...]
    ms = jnp.mean(x * x, axis=-1, keepdims=True)
    h = (x * lax.rsqrt(ms + EPS) * g_ref[...]).astype(BF16)

    def proj(lo, width):
        return _dot(h, w_ref[:, lo:lo + width])

    u_ref[...] = proj(_OFF_VAL, CONV_WIDTH) * _sigmoid(proj(_OFF_GATE, CONV_WIDTH))

    bd = bd_ref[...]
    lane = lax.broadcasted_iota(jnp.int32, (IN_TM, LANES), 1)
    lower = lane < HEAD_DIM

    def head_slots(pair, upper, out_ref, slot):
        out_ref[:, slot * LANES:(slot + 1) * LANES] = jnp.where(lower, pair, upper).astype(BF16)
        swapped = pltpu.roll(pair, HEAD_DIM, axis=1)
        out_ref[:, (slot + 1) * LANES:(slot + 2) * LANES] = jnp.where(lower, swapped, upper).astype(BF16)

    q = proj(_OFF_Q, ATTN_WIDTH)
    for c in range(ATTN_WIDTH // LANES):
        qc = q[:, c * LANES:(c + 1) * LANES]
        msq = _split_dot(qc * qc, bd)
        head_slots(qc * lax.rsqrt(msq + EPS) * qg_ref[...], 0.0, q_ref, 2 * c)

    kcvc_ref[...] = proj(_OFF_KCVC, 2 * KV_WIDTH)

    def normed_k(off, kg_ref):
        k = proj(off, KV_WIDTH)
        return k * lax.rsqrt(_split_dot(k * k, bd) + EPS) * kg_ref[...]

    tok = (pl.program_id(0) % (SEQ // IN_TM)) * IN_TM + lax.broadcasted_iota(jnp.int32, (IN_TM, 1), 0)
    block_hot = jnp.where(lane - HEAD_DIM == tok // SEL_BLOCK, 1.0, 0.0)
    head_slots(normed_k(_OFF_KS, kgs_ref), block_hot, ks_ref, 0)
    head_slots(normed_k(_OFF_KW, kgw_ref), 0.0, kw_ref, 0)

    zt = _dot_nt(wt_ref[...], h)
    ones_rows = lax.broadcasted_iota(jnp.int32, (LANES, IN_TM), 0) >= HEAD_DIM
    for j, out_ref in enumerate((vst_ref, vst_ref, vwt_ref, vwt_ref)):
        blk = zt[j * LANES:(j + 1) * LANES]
        out_ref[0, j % N_KV_HEADS] = jnp.where(ones_rows, 1.0, blk).astype(BF16)
    gatet_ref[0] = _sigmoid(zt[4 * LANES:5 * LANES])


def _in_proj(x2, g, w, wt, qg, kgs, kgw, bd):
    tm = IN_TM
    per_seq = SEQ // tm
    row = lambda width: pl.BlockSpec((tm, width), lambda i: (i, 0))
    full = lambda a: pl.BlockSpec(a.shape, lambda i: (0,) * a.ndim)
    vt_spec = pl.BlockSpec((1, N_KV_HEADS, LANES, tm), lambda i: (i // per_seq, 0, 0, i % per_seq))
    vt_shape = jax.ShapeDtypeStruct((BATCH, N_KV_HEADS, LANES, SEQ), BF16)
    out_shape = (
        jax.ShapeDtypeStruct((TOKENS, CONV_WIDTH), F32),
        jax.ShapeDtypeStruct((TOKENS, N_HEADS * LANES), BF16),
        jax.ShapeDtypeStruct((TOKENS, 2 * KV_WIDTH), F32),
        jax.ShapeDtypeStruct((TOKENS, 2 * LANES), BF16),
        jax.ShapeDtypeStruct((TOKENS, 2 * LANES), BF16),
        vt_shape,
        vt_shape,
        jax.ShapeDtypeStruct((BATCH, LANES, SEQ), F32),
    )
    out_specs = (row(CONV_WIDTH), row(N_HEADS * LANES), row(2 * KV_WIDTH), row(2 * LANES), row(2 * LANES),
                 vt_spec, vt_spec,
                 pl.BlockSpec((1, LANES, tm), lambda i: (i // per_seq, 0, i % per_seq)))
    return pl.pallas_call(
        _in_proj_kernel,
        grid=(TOKENS // tm,),
        in_specs=[row(D_MODEL), full(g), full(w), full(wt), full(qg), full(kgs), full(kgw), full(bd)],
        out_specs=out_specs,
        out_shape=out_shape,
        compiler_params=pltpu.CompilerParams(
            dimension_semantics=("arbitrary",), vmem_limit_bytes=VMEM_LIMIT),
        name="in_proj",
    )(x2, g, w, wt, qg, kgs, kgw, bd)


CONV_TR = 512
CONV_HALO = 32
CONV_RS = 64
CONV_PARTIALS = 4


def _conv_kernel(ucur_ref, uprev_ref, w_ref, b_ref, lg_ref, lb_ref, bd_ref, o_ref, win_ref, y_ref):
    i = pl.program_id(1)
    first = CONV_HALO - (CONV_KERNEL - 1)
    shifted_rows = CONV_HALO + CONV_TR - SUBLANES
    for c in range(CONV_WIDTH // LANES):
        lanes = slice(c * LANES, (c + 1) * LANES)
        win_ref[0, c, 0:CONV_HALO, :] = jnp.where(i > 0, uprev_ref[0, :, lanes], 0.0)
        win_ref[0, c, CONV_HALO:CONV_HALO + CONV_TR, :] = ucur_ref[0, :, lanes]
        for s in range(1, SUBLANES):
            win_ref[s, c, 0:shifted_rows, :] = win_ref[0, c, s:s + shifted_rows, :]

        def body(r, carry):
            base = pl.multiple_of(r * CONV_RS, CONV_RS)
            accs = [None] * CONV_PARTIALS
            for k in range(CONV_KERNEL):
                off = first + k
                rows = pl.ds(base + (off // SUBLANES) * SUBLANES, CONV_RS)
                term = win_ref[off % SUBLANES, c, rows, :] * w_ref[k:k + 1, lanes]
                j = k % CONV_PARTIALS
                accs[j] = term if accs[j] is None else accs[j] + term
            y_ref[c, pl.ds(base, CONV_RS), :] = (accs[0] + accs[1]) + (accs[2] + accs[3]) + b_ref[:, lanes]
            return carry

        lax.fori_loop(0, CONV_TR // CONV_RS, body, 0)

    bd = bd_ref[...]
    for c in range(CONV_WIDTH // LANES):
        lanes = slice(c * LANES, (c + 1) * LANES)
        y = y_ref[c]
        d = y - _split_dot(y, bd)
        var = _split_dot(d * d, bd)
        yn = d * lax.rsqrt(var + EPS) * lg_ref[:, lanes] + lb_ref[:, lanes]
        o_ref[0, :, lanes] = (yn * _sigmoid(yn)).astype(BF16)


def _conv_module(u3, w, b, lg, lb, bd):
    halo_blocks = CONV_TR // CONV_HALO
    full = lambda a: pl.BlockSpec(a.shape, lambda bi, i: (0,) * a.ndim)
    return pl.pallas_call(
        _conv_kernel,
        grid=(BATCH, SEQ // CONV_TR),
        in_specs=[
            pl.BlockSpec((1, CONV_TR, CONV_WIDTH), lambda bi, i: (bi, i, 0)),
            pl.BlockSpec((1, CONV_HALO, CONV_WIDTH),
                         lambda bi, i: (bi, jnp.maximum(i * halo_blocks - 1, 0), 0)),
            full(w), full(b), full(lg), full(lb), full(bd),
        ],
        out_specs=pl.BlockSpec((1, CONV_TR, CONV_WIDTH), lambda bi, i: (bi, i, 0)),
        out_shape=jax.ShapeDtypeStruct((BATCH, SEQ, CONV_WIDTH), BF16),
        scratch_shapes=[pltpu.VMEM((SUBLANES, CONV_WIDTH // LANES, CONV_HALO + CONV_TR, LANES), F32),
                        pltpu.VMEM((CONV_WIDTH // LANES, CONV_TR, LANES), F32)],
        compiler_params=pltpu.CompilerParams(dimension_semantics=("arbitrary", "arbitrary")),
        name="conv_module",
    )(u3, u3, w, b, lg, lb, bd)


CMP_ROWS = SEQ // CMP_STRIDE
CMP_CHUNK = CMP_STRIDE * HEAD_DIM


def _compress_kernel(x_ref, pos_ref, w1_ref, b1_ref, w2_ref, b2_ref, w2t_ref, b2t_ref, kg_ref, o_ref):
    kind = pl.program_id(0)
    xa = x_ref[0, 0, 0]
    xb = pltpu.roll(xa, CMP_ROWS - 1, axis=0)
    a = (xa + pos_ref[0, :, 0:CMP_CHUNK]).astype(BF16)
    b = (xb + pos_ref[0, :, CMP_CHUNK:2 * CMP_CHUNK]).astype(BF16)
    hid = _dot(a, w1_ref[0, 0:CMP_CHUNK, :]) + _dot(b, w1_ref[0, CMP_CHUNK:2 * CMP_CHUNK, :]) + b1_ref[0]
    hid = (hid * _sigmoid(hid)).astype(BF16)
    out = _dot(hid, w2_ref[0]) + b2_ref[0]
    ms = jnp.sum(out * out, axis=-1, keepdims=True) * (1.0 / HEAD_DIM)
    normed = out * lax.rsqrt(ms + EPS) * kg_ref[...]
    out_t = _dot_nt(w2t_ref[0], hid) + b2t_ref[0]
    o_ref[0, 0, 0] = jnp.where(kind == 0, normed, out_t).astype(BF16)


def _compress(xr, pos, w1, b1, w2, b2, kg):
    w2t = jnp.swapaxes(w2, 1, 2)
    b2t = jnp.swapaxes(b2, 1, 2)
    per_kind = lambda a: pl.BlockSpec((1,) + a.shape[1:], lambda k, bi, h: (k,) + (0,) * (a.ndim - 1))
    return pl.pallas_call(
        _compress_kernel,
        grid=(2, BATCH, N_KV_HEADS),
        in_specs=[
            pl.BlockSpec((1, 1, 1, CMP_ROWS, CMP_CHUNK), lambda k, bi, h: (k, bi, h, 0, 0)),
            per_kind(pos), per_kind(w1), per_kind(b1), per_kind(w2), per_kind(b2),
            per_kind(w2t), per_kind(b2t),
            pl.BlockSpec(kg.shape, lambda k, bi, h: (0, 0)),
        ],
        out_specs=pl.BlockSpec((1, 1, 1, CMP_ROWS, LANES), lambda k, bi, h: (k, bi, h, 0, 0)),
        out_shape=jax.ShapeDtypeStruct((2, BATCH, N_KV_HEADS, CMP_ROWS, LANES), BF16),
        compiler_params=pltpu.CompilerParams(
            dimension_semantics=("arbitrary", "arbitrary", "arbitrary")),
        name="compress",
    )(xr, pos, w1, b1, w2, b2, w2t, b2t, kg)


QCOLS = GQA * Q_BLOCK
SEL_CHUNK = 256
WIN_CHUNK = 128
WIN_CHUNKS = (WINDOW + Q_BLOCK) // WIN_CHUNK
MASK_BIAS = -1e30


def _window_biases():
    k = np.arange(WIN_CHUNK)[:, None]
    t = (np.arange(QCOLS) % Q_BLOCK)[None, :]
    tables, index = [], {}
    for i in range(WIN_CHUNKS):
        key = i * WIN_CHUNK - WINDOW + k
        bias = np.where((key <= t) & (key > t - WINDOW), 0.0, MASK_BIAS)
        if bias.any():
            index[i] = len(tables)
            tables.append(bias)
    return np.stack(tables).astype(np.float32), index


_WIN_BIAS, _WIN_BIAS_INDEX = _window_biases()


def _attn_kernel(q_ref, gatet_ref, kc_ref, vct_ref, ks_ref, vst_ref, kw_ref, vwt_ref,
                 ovlt_ref, wbias_ref, o_ref, ssel_sc, swin_sc, acc_sc, out_sc):
    n = pl.program_id(1)
    start = n * Q_BLOCK
    col = lax.broadcasted_iota(jnp.int32, (1, QCOLS), 1)
    t_cols = start + (col & (Q_BLOCK - 1))
    tq = start + lax.broadcasted_iota(jnp.int32, (1, Q_BLOCK), 1)
    blk_row = lax.broadcasted_iota(jnp.int32, (N_SEL, 1), 0)
    blk_row_f = blk_row.astype(F32)
    cmp_end = lax.broadcasted_iota(jnp.int32, (CMP_ROWS, 1), 0) * CMP_STRIDE + (CMP_BLOCK - 1)

    kv_heads = range(N_KV_HEADS)
    kv_lanes = [slice(kvh * LANES, (kvh + 1) * LANES) for kvh in kv_heads]

    def gate_row(kvh, g, branch):
        r = (kvh * GQA + g) * N_BRANCH + branch
        return gatet_ref[0, r:r + 1, :]

    def head_rows(kvh, g):
        h = kvh * GQA + g
        return slice(h * HEAD_DIM, (h + 1) * HEAD_DIM)

    qs = []
    for kvh in kv_heads:
        h0 = kvh * GQA
        qs.append(jnp.concatenate(
            [q_ref[0, :, (h0 + g) * LANES:(h0 + g + 1) * LANES] for g in range(GQA)], axis=0))

    imp = []
    mask_c = cmp_end <= t_cols
    for kvh in kv_heads:
        s_c = jnp.where(mask_c, _dot_nt(kc_ref[0, kvh], qs[kvh]), NEG_INF)
        m_c = jnp.max(s_c, axis=0, keepdims=True)
        p_c = jnp.where(mask_c, jnp.exp2(s_c - m_c), 0.0)
        p_c = p_c / jnp.maximum(jnp.sum(p_c, axis=0, keepdims=True), 1e-30)
        o_cmp = _dot(vct_ref[0, kvh], p_c.astype(BF16))
        for g in range(GQA):
            out_sc[head_rows(kvh, g), :] = gate_row(kvh, g, 0) * o_cmp[0:HEAD_DIM, g * Q_BLOCK:(g + 1) * Q_BLOCK]
        p_sum = p_c[:, 0:Q_BLOCK]
        for g in range(1, GQA):
            p_sum = p_sum + p_c[:, g * Q_BLOCK:(g + 1) * Q_BLOCK]
        p_hi = p_sum.astype(BF16)
        p_lo = (p_sum - p_hi.astype(F32)).astype(BF16)
        imp.append(_dot(ovlt_ref[...], p_hi) + _dot(ovlt_ref[...], p_lo))

    tq2 = jnp.concatenate([tq] * N_KV_HEADS, axis=1)
    cur = tq2 >> 6
    forced = (blk_row == 0) | (blk_row == cur) | (blk_row == cur - 1)
    causal_blk = blk_row * SEL_BLOCK <= tq2
    score = jnp.where(causal_blk, jnp.concatenate(imp, axis=1) + jnp.where(forced, FORCE_BONUS, 0.0), NEG_INF)
    sel = jnp.zeros((N_SEL, N_KV_HEADS * Q_BLOCK), F32)
    for _ in range(SEL_TOPK):
        best = jnp.max(score, axis=0, keepdims=True)
        idx = jnp.min(jnp.where(score == best, blk_row_f, float(LANES)), axis=0, keepdims=True)
        hit = blk_row_f == idx
        sel = jnp.where(hit, 1.0, sel)
        score = jnp.where(hit, -3e38, score)
    sel_bias = (sel - 1.0) * (-MASK_BIAS)
    qs_sel = []
    for kvh in kv_heads:
        rows = jnp.concatenate([jnp.zeros((HEAD_DIM, Q_BLOCK), F32),
                                sel_bias[:, kvh * Q_BLOCK:(kvh + 1) * Q_BLOCK],
                                jnp.zeros((LANES - HEAD_DIM - N_SEL, Q_BLOCK), F32)], axis=0)
        bias_q = rows.T.astype(BF16)
        qs_sel.append(qs[kvh] + jnp.concatenate([bias_q] * GQA, axis=0))

    for kvh in kv_heads:
        mxw = jnp.full((1, QCOLS), NEG_INF, F32)
        kbs = [n * (Q_BLOCK // WIN_CHUNK) - WINDOW // WIN_CHUNK + i for i in range(WIN_CHUNKS)]
        k_win = [pl.multiple_of(jnp.maximum(kb, 0) * WIN_CHUNK, WIN_CHUNK) for kb in kbs]
        k_band = jnp.concatenate([kw_ref[0, pl.ds(k0, WIN_CHUNK), kv_lanes[kvh]] for k0 in k_win], axis=0)
        s_band = _dot_nt(k_band, qs[kvh])
        for i, kb in enumerate(kbs):
            s = s_band[i * WIN_CHUNK:(i + 1) * WIN_CHUNK]
            if i in _WIN_BIAS_INDEX:
                s = s + wbias_ref[_WIN_BIAS_INDEX[i]]
            if i < WINDOW // WIN_CHUNK:
                s = s + jnp.where(kb >= 0, 0.0, MASK_BIAS)
            swin_sc[kvh, i * WIN_CHUNK:(i + 1) * WIN_CHUNK, :] = s
            mxw = jnp.maximum(mxw, jnp.max(s, axis=0, keepdims=True))
        p = jnp.exp2(swin_sc[kvh] - mxw).astype(BF16)
        v_band = jnp.concatenate([vwt_ref[0, kvh, :, pl.ds(k0, WIN_CHUNK)] for k0 in k_win], axis=1)
        accw = _dot(v_band, p)
        o_win = accw[0:HEAD_DIM] * (1.0 / accw[HEAD_DIM:HEAD_DIM + 1])
        for g in range(GQA):
            out_sc[head_rows(kvh, g), :] += gate_row(kvh, g, 2) * o_win[:, g * Q_BLOCK:(g + 1) * Q_BLOCK]

    def sel_scores(kvh, k0):
        return _dot_nt(ks_ref[0, pl.ds(k0, SEL_CHUNK), kv_lanes[kvh]], qs_sel[kvh])

    def sel_pass1(i, mx):
        k0 = pl.multiple_of(i * SEL_CHUNK, SEL_CHUNK)
        out = []
        for kvh in kv_heads:
            s = sel_scores(kvh, k0)
            ssel_sc[kvh, pl.ds(k0, SEL_CHUNK), :] = s
            out.append(jnp.maximum(mx[kvh], jnp.max(s, axis=0, keepdims=True)))
        return tuple(out)

    last = (start + Q_BLOCK - 1) // SEL_CHUNK
    mx = lax.fori_loop(0, last, sel_pass1, (jnp.full((1, QCOLS), NEG_INF, F32),) * N_KV_HEADS)
    k_last = pl.multiple_of(last * SEL_CHUNK, SEL_CHUNK)
    causal = k_last + lax.broadcasted_iota(jnp.int32, (SEL_CHUNK, 1), 0) <= t_cols
    mx = list(mx)
    for kvh in kv_heads:
        s = jnp.where(causal, sel_scores(kvh, k_last), NEG_INF)
        ssel_sc[kvh, pl.ds(k_last, SEL_CHUNK), :] = s
        mx[kvh] = jnp.maximum(mx[kvh], jnp.max(s, axis=0, keepdims=True))
        acc_sc[kvh] = jnp.zeros((LANES, QCOLS), F32)

    def sel_pass2(i, carry):
        k0 = pl.multiple_of(i * SEL_CHUNK, SEL_CHUNK)
        for kvh in kv_heads:
            p = jnp.exp2(ssel_sc[kvh, pl.ds(k0, SEL_CHUNK), :] - mx[kvh]).astype(BF16)
            acc_sc[kvh] += _dot(vst_ref[0, kvh, :, pl.ds(k0, SEL_CHUNK)], p)
        return carry

    lax.fori_loop(0, last + 1, sel_pass2, 0)
    for kvh in kv_heads:
        acc = acc_sc[kvh]
        o_sel = acc[0:HEAD_DIM] * (1.0 / acc[HEAD_DIM:HEAD_DIM + 1])
        for g in range(GQA):
            out_sc[head_rows(kvh, g), :] += gate_row(kvh, g, 1) * o_sel[:, g * Q_BLOCK:(g + 1) * Q_BLOCK]

    for c in range(N_HEADS // 2):
        o_ref[0, :, c * LANES:(c + 1) * LANES] = out_sc[c * LANES:(c + 1) * LANES, :].T.astype(BF16)


def _attention(q3, gatet, kvc, ks3, vst, kw3, vwt, ovlt):
    wbias = jnp.asarray(_WIN_BIAS)
    qblk = lambda width: pl.BlockSpec((1, Q_BLOCK, width), lambda bi, n: (bi, n, 0))
    seq = pl.BlockSpec((1, SEQ, 2 * LANES), lambda bi, n: (bi, 0, 0))
    seq_t = pl.BlockSpec((1, N_KV_HEADS, LANES, SEQ), lambda bi, n: (bi, 0, 0, 0))
    cmp_spec = lambda kind: pl.BlockSpec((None, 1, N_KV_HEADS, CMP_ROWS, LANES),
                                         lambda bi, n: (kind, bi, 0, 0, 0))
    full = lambda a: pl.BlockSpec(a.shape, lambda bi, n: (0,) * a.ndim)
    return pl.pallas_call(
        _attn_kernel,
        grid=(BATCH, SEQ // Q_BLOCK),
        in_specs=[qblk(N_HEADS * LANES), pl.BlockSpec((1, LANES, Q_BLOCK), lambda bi, n: (bi, 0, n)),
                  cmp_spec(0), cmp_spec(1), seq, seq_t, seq, seq_t, full(ovlt), full(wbias)],
        out_specs=qblk(ATTN_WIDTH),
        out_shape=jax.ShapeDtypeStruct((BATCH, SEQ, ATTN_WIDTH), BF16),
        scratch_shapes=[pltpu.VMEM((N_KV_HEADS, SEQ, QCOLS), F32),
                        pltpu.VMEM((N_KV_HEADS, WIN_CHUNKS * WIN_CHUNK, QCOLS), F32),
                        pltpu.VMEM((N_KV_HEADS, LANES, QCOLS), F32),
                        pltpu.VMEM((ATTN_WIDTH, Q_BLOCK), F32)],
        compiler_params=pltpu.CompilerParams(
            dimension_semantics=("arbitrary", "arbitrary"), vmem_limit_bytes=VMEM_LIMIT),
        name="nsa_attention",
    )(q3, gatet, kvc, kvc, ks3, vst, kw3, vwt, ovlt, wbias)


OUT_TM = 512


def _out_proj_kernel(x_ref, conv_ref, attn_ref, w_ref, g_ref, x1_ref, h_ref):
    x1 = (x_ref[...] + _dot(conv_ref[...], w_ref[0:CONV_WIDTH, :])
          + _dot(attn_ref[...], w_ref[CONV_WIDTH:CONV_WIDTH + ATTN_WIDTH, :]))
    x1_ref[...] = x1
    ms = jnp.mean(x1 * x1, axis=-1, keepdims=True)
    h_ref[...] = (x1 * lax.rsqrt(ms + EPS) * g_ref[...]).astype(BF16)


def _out_proj_router_kernel(x_ref, conv_ref, attn_ref, w_ref, g_ref, r_ref, x1_ref, h_ref, comb_ref):
    x1 = (x_ref[...] + _dot(conv_ref[...], w_ref[0:CONV_WIDTH, :])
          + _dot(attn_ref[...], w_ref[CONV_WIDTH:CONV_WIDTH + ATTN_WIDTH, :]))
    x1_ref[...] = x1
    ms = jnp.mean(x1 * x1, axis=-1, keepdims=True)
    h = x1 * lax.rsqrt(ms + EPS) * g_ref[...]
    h_ref[...] = h.astype(BF16)
    h_hi = h.astype(BF16)
    h_lo = (h - h_hi.astype(F32)).astype(BF16)
    logits = _dot(h_hi, r_ref[0]) + _dot(h_lo, r_ref[0]) + _dot(h_hi, r_ref[1])
    lane = lax.broadcasted_iota(jnp.int32, (1, LANES), 1).astype(F32)
    logits = jnp.where(lane < N_EXPERTS, logits, NEG_INF)
    v1 = jnp.max(logits, axis=-1, keepdims=True)
    i1 = jnp.min(jnp.where(logits == v1, lane, float(LANES)), axis=-1, keepdims=True)
    hit1 = lane == i1
    rest = jnp.where(hit1, -3e38, logits)
    v2 = jnp.max(rest, axis=-1, keepdims=True)
    i2 = jnp.min(jnp.where(rest == v2, lane, float(LANES)), axis=-1, keepdims=True)
    hit2 = lane == i2
    e2 = jnp.exp(v2 - v1)
    w_top = 1.0 / (1.0 + e2)
    comb_ref[...] = jnp.where(hit1, w_top, 0.0) + jnp.where(hit2, e2 * w_top, 0.0)


def _out_proj(x2, conv2, attn2, w, g, router=None):
    tm = OUT_TM
    row = lambda width: pl.BlockSpec((tm, width), lambda i: (i, 0))
    full = lambda a: pl.BlockSpec(a.shape, lambda i: (0,) * a.ndim)
    in_specs = [row(D_MODEL), row(CONV_WIDTH), row(ATTN_WIDTH), full(w), full(g)]
    out_shape = [jax.ShapeDtypeStruct((TOKENS, D_MODEL), F32), jax.ShapeDtypeStruct((TOKENS, D_MODEL), BF16)]
    out_specs = [row(D_MODEL), row(D_MODEL)]
    args = [x2, conv2, attn2, w, g]
    body = _out_proj_kernel
    if router is not None:
        body = _out_proj_router_kernel
        in_specs.append(full(router))
        args.append(router)
        out_shape.append(jax.ShapeDtypeStruct((TOKENS, LANES), F32))
        out_specs.append(row(LANES))
    return pl.pallas_call(
        body,
        grid=(TOKENS // tm,),
        in_specs=in_specs,
        out_specs=tuple(out_specs),
        out_shape=tuple(out_shape),
        compiler_params=pltpu.CompilerParams(
            dimension_semantics=("arbitrary",), vmem_limit_bytes=VMEM_LIMIT),
        name="out_proj",
    )(*args)


FFN_TM = 512
FFN_TF = 1408


def _ffn_kernel(x1_ref, h_ref, w1_ref, w3_ref, w2_ref, o_ref):
    h = h_ref[...]
    out = x1_ref[...]
    for f in range(D_FF // FFN_TF):
        cols = slice(f * FFN_TF, (f + 1) * FFN_TF)
        a = _dot(h, w1_ref[:, cols])
        act = (a * _sigmoid(a) * _dot(h, w3_ref[:, cols])).astype(BF16)
        out = out + _dot(act, w2_ref[cols, :])
    o_ref[...] = out


def _dense_ffn(x1, h, w1, w3, w2):
    tm = FFN_TM
    resident = pl.BlockSpec(memory_space=pltpu.VMEM)
    return pl.pallas_call(
        _ffn_kernel,
        grid=(TOKENS // tm,),
        in_specs=[pl.BlockSpec((tm, D_MODEL), lambda i: (i, 0)), pl.BlockSpec((tm, D_MODEL), lambda i: (i, 0)),
                  resident, resident, resident],
        out_specs=pl.BlockSpec((tm, D_MODEL), lambda i: (i, 0)),
        out_shape=jax.ShapeDtypeStruct((TOKENS, D_MODEL), F32),
        compiler_params=pltpu.CompilerParams(
            dimension_semantics=("arbitrary",), vmem_limit_bytes=VMEM_LIMIT),
        name="dense_ffn",
    )(x1, h, w1, w3, w2)


MOE_SB = 512
MOE_TM = 512
MOE_GT = 256
MOE_CT = 512
MOE_CBUF = 4
MOE_TF = 1792
MOE_NB = TOKENS // MOE_SB
MOE_TILES = (2 * TOKENS) // MOE_TM + N_EXPERTS
MOE_GTILES = MOE_TILES * (MOE_TM // MOE_GT)
MOE_CPAIRS = MOE_TILES * (MOE_TM // MOE_CT) + N_EXPERTS * MOE_NB


def _moe_rank_kernel(comb_ref, tri_ref, rank_ref, cend_ref, carry_ref):
    @pl.when(pl.program_id(0) == 0)
    def _():
        carry_ref[...] = jnp.zeros((1, LANES), F32)

    routed = jnp.where(comb_ref[...] > 0.0, 1.0, 0.0).astype(BF16)
    carry = carry_ref[...]
    rank_ref[...] = carry + _dot(tri_ref[...], routed)
    carry = carry + jnp.sum(routed.astype(F32), axis=0, keepdims=True)
    carry_ref[...] = carry
    cend_ref[0] = jnp.broadcast_to(carry, (SUBLANES, LANES))


def _moe_rank(comb):
    idx = np.arange(MOE_SB)
    tri = jnp.asarray(idx[:, None] > idx[None, :], BF16)
    return pl.pallas_call(
        _moe_rank_kernel,
        grid=(MOE_NB,),
        in_specs=[pl.BlockSpec((MOE_SB, LANES), lambda b: (b, 0)),
                  pl.BlockSpec((MOE_SB, MOE_SB), lambda b: (0, 0))],
        out_specs=(pl.BlockSpec((MOE_SB, LANES), lambda b: (b, 0)),
                   pl.BlockSpec((1, SUBLANES, LANES), lambda b: (b, 0, 0))),
        out_shape=(jax.ShapeDtypeStruct((TOKENS, LANES), F32),
                   jax.ShapeDtypeStruct((MOE_NB, SUBLANES, LANES), F32)),
        scratch_shapes=[pltpu.VMEM((1, LANES), F32)],
        compiler_params=pltpu.CompilerParams(dimension_semantics=("arbitrary",)),
        name="moe_rank",
    )(comb, tri)


def _count_le(ends, v):
    return jnp.sum(ends[None, :] <= v[:, None], axis=1).astype(jnp.int32)


def _moe_schedule(cend):
    i32 = jnp.int32
    counts = cend[-1]
    cstart = jnp.concatenate([jnp.zeros((1, N_EXPERTS), i32), cend[:-1]], axis=0)
    tiles_e = (counts + MOE_TM - 1) // MOE_TM
    tile_end = jnp.cumsum(tiles_e)
    tile_start = tile_end - tiles_e
    n_tiles = tile_end[-1]
    group_start = tile_start * MOE_TM

    d = jnp.arange(MOE_TILES, dtype=i32)
    t_valid = d < n_tiles
    t_exp = jnp.minimum(_count_le(tile_end, d), N_EXPERTS - 1)
    t_exp = jnp.where(t_valid, t_exp, t_exp[jnp.maximum(n_tiles - 1, 0)])

    per_tile = MOE_TM // MOE_GT
    g = jnp.arange(MOE_GTILES, dtype=i32)
    g_exp = t_exp[g // per_tile]
    r0 = (g // per_tile - tile_start[g_exp]) * MOE_TM + (g % per_tile) * MOE_GT
    r1 = jnp.minimum(r0 + MOE_GT, counts[g_exp])
    gt_valid = t_valid[g // per_tile] & (r0 < counts[g_exp])
    b_lo = jnp.sum(cend[:, g_exp] <= r0[None, :], axis=0).astype(i32)
    b_hi = jnp.sum(cstart[:, g_exp] < r1[None, :], axis=0).astype(i32) - 1
    gather = (g_exp, jnp.where(gt_valid, b_lo, 0), jnp.where(gt_valid, b_hi - b_lo + 1, 0))

    lo = group_start[None, :] + cstart
    hi = group_start[None, :] + cend
    t_lo = (lo // MOE_CT).reshape(-1)
    n_t = jnp.where(hi > lo, (hi - 1) // MOE_CT - lo // MOE_CT + 1, 0).reshape(-1)
    q_end = jnp.cumsum(n_t)
    q_start = q_end - n_t
    p = jnp.arange(MOE_CPAIRS, dtype=i32)
    be = jnp.minimum(_count_le(q_end, p), MOE_NB * N_EXPERTS - 1)
    c_tile = jnp.clip(t_lo[be] + p - q_start[be], 0, MOE_TILES * (MOE_TM // MOE_CT) - 1)
    combine = (q_start[::N_EXPERTS], q_end[N_EXPERTS - 1::N_EXPERTS], c_tile, be % N_EXPERTS)
    return group_start, t_exp, t_valid.astype(i32), gather, combine


def _moe_gather_kernel(exp_ref, first_ref, count_ref, h_ref, pos_ref, xs_ref):
    g = pl.program_id(0)
    rows = (g * MOE_GT + lax.broadcasted_iota(jnp.int32, (MOE_GT, 1), 0)).astype(F32)
    xs_ref[...] = jnp.zeros((MOE_GT, D_MODEL), BF16)

    def body(j, carry):
        t0 = pl.multiple_of((first_ref[g] + j) * MOE_SB, MOE_SB)
        pos = pos_ref[pl.ds(exp_ref[g], 1), pl.ds(t0, MOE_SB)]
        onehot = jnp.where(rows == pos, 1.0, 0.0).astype(BF16)
        xs_ref[...] += _dot(onehot, h_ref[pl.ds(t0, MOE_SB), :]).astype(BF16)
        return carry

    lax.fori_loop(0, count_ref[g], body, 0)


def _moe_gather(sched, h, pos_t):
    spec = pltpu.PrefetchScalarGridSpec(
        num_scalar_prefetch=3,
        grid=(MOE_GTILES,),
        in_specs=[pl.BlockSpec(memory_space=pltpu.VMEM), pl.BlockSpec(memory_space=pltpu.VMEM)],
        out_specs=pl.BlockSpec((MOE_GT, D_MODEL), lambda g, e, b, c: (g, 0)),
    )
    return pl.pallas_call(
        _moe_gather_kernel,
        grid_spec=spec,
        out_shape=jax.ShapeDtypeStruct((MOE_TILES * MOE_TM, D_MODEL), BF16),
        compiler_params=pltpu.CompilerParams(
            dimension_semantics=("arbitrary",), vmem_limit_bytes=VMEM_LIMIT),
        name="moe_gather",
    )(*sched, h, pos_t)


def _moe_expert_kernel(exp_ref, valid_ref, xs_ref, w1_ref, w3_ref, w2_ref, y_ref, acc_ref):
    d = pl.program_id(0)
    f = pl.program_id(1)

    @pl.when(valid_ref[d] == 1)
    def _():
        x = xs_ref[...]
        a = _dot(x, w1_ref[0])
        act = (a * _sigmoid(a) * _dot(x, w3_ref[0])).astype(BF16)
        part = _dot(act, w2_ref[0])

        @pl.when(f == 0)
        def _():
            acc_ref[...] = part

        @pl.when(f > 0)
        def _():
            acc_ref[...] += part

        @pl.when(f == pl.num_programs(1) - 1)
        def _():
            y_ref[...] = acc_ref[...].astype(BF16)

    @pl.when((valid_ref[d] == 0) & (f == 0))
    def _():
        y_ref[...] = jnp.zeros((MOE_TM, D_MODEL), BF16)


def _moe_experts(t_exp, t_valid, xs, w1, w3, w2):
    tf = MOE_TF
    nf = D_FF_EXPERT // tf
    fidx = lambda d, f, v: f * v[d] + (nf - 1) * (1 - v[d])
    spec = pltpu.PrefetchScalarGridSpec(
        num_scalar_prefetch=2,
        grid=(MOE_TILES, nf),
        in_specs=[pl.BlockSpec((MOE_TM, D_MODEL), lambda d, f, e, v: (d, 0)),
                  pl.BlockSpec((1, D_MODEL, tf), lambda d, f, e, v: (e[d], 0, fidx(d, f, v))),
                  pl.BlockSpec((1, D_MODEL, tf), lambda d, f, e, v: (e[d], 0, fidx(d, f, v))),
                  pl.BlockSpec((1, tf, D_MODEL), lambda d, f, e, v: (e[d], fidx(d, f, v), 0))],
        out_specs=pl.BlockSpec((MOE_TM, D_MODEL), lambda d, f, e, v: (d, 0)),
        scratch_shapes=[pltpu.VMEM((MOE_TM, D_MODEL), F32)],
    )
    return pl.pallas_call(
        _moe_expert_kernel,
        grid_spec=spec,
        out_shape=jax.ShapeDtypeStruct((MOE_TILES * MOE_TM, D_MODEL), BF16),
        compiler_params=pltpu.CompilerParams(
            dimension_semantics=("arbitrary", "arbitrary"), vmem_limit_bytes=VMEM_LIMIT),
        name="moe_experts",
    )(t_exp, t_valid, xs, w1, w3, w2)


def _moe_combine_kernel(first_ref, end_ref, tile_ref, exp_ref, y_hbm, pos_ref, comb_ref, x1_ref, o_ref,
                        ybuf, sem):
    b = pl.program_id(0)
    first = first_ref[b]
    end = end_ref[b]

    def window_copy(p, slot):
        r0 = pl.multiple_of(tile_ref[p] * MOE_CT, MOE_CT)
        return pltpu.make_async_copy(y_hbm.at[pl.ds(r0, MOE_CT), :], ybuf.at[slot], sem.at[slot])

    for k in range(MOE_CBUF - 1):
        @pl.when(first + k < end)
        def _():
            window_copy(first + k, k).start()

    o_ref[...] = x1_ref[...]
    lane = lax.broadcasted_iota(jnp.int32, (1, LANES), 1)
    window = lax.broadcasted_iota(jnp.int32, (1, MOE_CT), 1)

    def body(p, carry):
        slot = (p - first) % MOE_CBUF
        window_copy(p, slot).wait()
        ahead = p + (MOE_CBUF - 1)

        @pl.when(ahead < end)
        def _():
            window_copy(ahead, (ahead - first) % MOE_CBUF).start()

        pick = lane == exp_ref[p]
        pos = jnp.sum(jnp.where(pick, pos_ref[...], 0.0), axis=-1, keepdims=True)
        wgt = jnp.sum(jnp.where(pick, comb_ref[...], 0.0), axis=-1, keepdims=True)
        cols = (tile_ref[p] * MOE_CT + window).astype(F32)
        onehot = jnp.where(pos == cols, 1.0, 0.0).astype(BF16)
        o_ref[...] += wgt * _dot(onehot, ybuf[slot])
        return carry

    lax.fori_loop(first, end, body, 0)


def _moe_combine(sched, y, pos, comb, x1):
    tok = lambda width: pl.BlockSpec((MOE_SB, width), lambda b, *_: (b, 0))
    spec = pltpu.PrefetchScalarGridSpec(
        num_scalar_prefetch=4,
        grid=(MOE_NB,),
        in_specs=[pl.BlockSpec(memory_space=pl.ANY), tok(LANES), tok(LANES), tok(D_MODEL)],
        out_specs=tok(D_MODEL),
        scratch_shapes=[pltpu.VMEM((MOE_CBUF, MOE_CT, D_MODEL), BF16), pltpu.SemaphoreType.DMA((MOE_CBUF,))],
    )
    return pl.pallas_call(
        _moe_combine_kernel,
        grid_spec=spec,
        out_shape=jax.ShapeDtypeStruct((TOKENS, D_MODEL), F32),
        compiler_params=pltpu.CompilerParams(
            dimension_semantics=("arbitrary",), vmem_limit_bytes=VMEM_LIMIT),
        name="moe_combine",
    )(*sched, y, pos, comb, x1)


def _moe_ffn(x1, h, comb, w1, w3, w2):
    rank, cend = _moe_rank(comb)
    cend = jnp.round(cend[:, 0, :N_EXPERTS]).astype(jnp.int32)
    group_start, t_exp, t_valid, gather, combine = _moe_schedule(cend)
    start = jnp.zeros((LANES,), F32).at[:N_EXPERTS].set(group_start.astype(F32))
    pos = jnp.where(comb > 0.0, start[None, :] + rank, -1.0)
    xs = _moe_gather(gather, h, pos[:, :SUBLANES].T)
    y = _moe_experts(t_exp, t_valid, xs, w1, w3, w2)
    return _moe_combine(combine, y, pos, comb, x1)


def _group_mean_matrix():
    idx = np.arange(LANES)
    return jnp.asarray((idx[:, None] // CONV_GROUP == idx[None, :] // CONV_GROUP) / CONV_GROUP, BF16)


def _overlap_matrix_t():
    c_start = np.arange(CMP_ROWS) * CMP_STRIDE
    j = np.arange(N_SEL)
    ovl = ((c_start[None, :] < (j[:, None] + 1) * SEL_BLOCK)
           & (c_start[None, :] + CMP_BLOCK > j[:, None] * SEL_BLOCK)
           & (np.arange(CMP_ROWS)[None, :] < N_CMP))
    return jnp.asarray(ovl, BF16)


def _dup(v):
    return jnp.concatenate([v, v], axis=-1)


def _zero_pad(v):
    return jnp.concatenate([v, jnp.zeros_like(v)], axis=-1)


def kernel(x, attn_norm_g, w_in, conv_dw_w, conv_dw_b, conv_ln_g, conv_ln_b, q_norm_g, k_norm_g,
           cmp_pos_k, cmp_pos_v, cmp_k_w1, cmp_k_b1, cmp_k_w2, cmp_k_b2,
           cmp_v_w1, cmp_v_b1, cmp_v_w2, cmp_v_b2, w_out, ffn_norm_g,
           dense_w1, dense_w3, dense_w2, router_w, moe_w1, moe_w3, moe_w2):
    assert x.shape == (BATCH, SEQ, D_MODEL) and x.dtype == F32
    bd = _group_mean_matrix()
    ovlt = _overlap_matrix_t()

    x2 = x.reshape(TOKENS, D_MODEL)
    for layer in range(DEPTH):
        w_cols, w_rows = _in_proj_weights(w_in[layer])
        qg = _dup(q_norm_g[layer])[None, :] * (LOG2_E * HEAD_DIM ** -0.5)
        kgs = _dup(k_norm_g[layer, 1])[None, :]
        kgw = _dup(k_norm_g[layer, 2])[None, :]
        u, q, kcvc, ks, kw, vst, vwt, gatet = _in_proj(
            x2, attn_norm_g[layer][None, :], w_cols, w_rows, qg, kgs, kgw, bd)

        conv = _conv_module(u.reshape(BATCH, SEQ, CONV_WIDTH), conv_dw_w[layer], conv_dw_b[layer][None, :],
                            conv_ln_g[layer][None, :], conv_ln_b[layer][None, :], bd)

        xr = kcvc.reshape(BATCH, SEQ, 2, N_KV_HEADS, HEAD_DIM).transpose(2, 0, 3, 1, 4)
        xr = xr.reshape(2, BATCH, N_KV_HEADS, CMP_ROWS, CMP_CHUNK)
        pos = jnp.stack([cmp_pos_k[layer], cmp_pos_v[layer]]).reshape(2, 1, CMP_BLOCK * HEAD_DIM)
        w1 = jnp.stack([cmp_k_w1[layer], cmp_v_w1[layer]]).astype(BF16)
        b1 = jnp.stack([cmp_k_b1[layer], cmp_v_b1[layer]])[:, None, :]
        w2 = _zero_pad(jnp.stack([cmp_k_w2[layer], cmp_v_w2[layer]])).astype(BF16)
        b2 = _zero_pad(jnp.stack([cmp_k_b2[layer], cmp_v_b2[layer]]))[:, None, :]
        kvc = _compress(xr, pos, w1, b1, w2, b2, _zero_pad(k_norm_g[layer, 0])[None, :])

        seq3 = lambda a: a.reshape(BATCH, SEQ, a.shape[-1])
        attn = _attention(seq3(q), gatet, kvc, seq3(ks), vst, seq3(kw), vwt, ovlt)

        w_o = w_out[layer].astype(BF16)
        conv2 = conv.reshape(TOKENS, CONV_WIDTH)
        attn2 = attn.reshape(TOKENS, ATTN_WIDTH)
        g_ffn = ffn_norm_g[layer][None, :]
        i = layer // 2
        if layer % 2 == 0:
            x1, h = _out_proj(x2, conv2, attn2, w_o, g_ffn)
            x2 = _dense_ffn(x1, h, dense_w1[i].astype(BF16), dense_w3[i].astype(BF16),
                            dense_w2[i].astype(BF16))
        else:
            r = jnp.pad(router_w[i], ((0, 0), (0, LANES - N_EXPERTS)))
            r_hi = r.astype(BF16)
            r_lo = (r - r_hi.astype(F32)).astype(BF16)
            x1, h, comb = _out_proj(x2, conv2, attn2, w_o, g_ffn, jnp.stack([r_hi, r_lo]))
            x2 = _moe_ffn(x1, h, comb, moe_w1[i].astype(BF16), moe_w3[i].astype(BF16),
                          moe_w2[i].astype(BF16))
    return x2.reshape(BATCH, SEQ, D_MODEL)
```

```python
import functools

import numpy as np
import jax
import jax.numpy as jnp
from jax import lax
from jax.experimental import pallas as pl
from jax.experimental.pallas import tpu as pltpu

D_MODEL = 1024
BATCH = 8
SEQ = 2048
DEPTH = 2
TOKENS = BATCH * SEQ

CONV_WIDTH = 512
CONV_GROUP = 64
CONV_KERNEL = 31
N_HEADS = 8
HEAD_DIM = 64
N_KV_HEADS = 2
GQA = N_HEADS // N_KV_HEADS
ATTN_WIDTH = N_HEADS * HEAD_DIM
KV_WIDTH = N_KV_HEADS * HEAD_DIM
N_BRANCH = 3
CMP_BLOCK = 32
CMP_STRIDE = 16
CMP_HIDDEN = 256
N_CMP = (SEQ - CMP_BLOCK) // CMP_STRIDE + 1
SEL_BLOCK = 64
SEL_TOPK = 8
N_SEL = SEQ // SEL_BLOCK
WINDOW = 512
Q_BLOCK = 256
FORCE_BONUS = 1e4
NEG_INF = -1e30
LOG2_E = 1.4426950408889634
D_FF = 2816
N_EXPERTS = 8
D_FF_EXPERT = 3584
EPS = 1e-6

LANES = 128
SUBLANES = 8
VMEM_LIMIT = 48 * 1024 * 1024

F32 = jnp.float32
BF16 = jnp.bfloat16

_OFF_VAL = 0
_OFF_GATE = 512
_OFF_Q = 1024
_OFF_KCVC = 1536
_OFF_KS = 1792
_OFF_KW = 1920
IN_COLS = 2048
T_ROWS = 5 * LANES

_BASE_KV = 2 * CONV_WIDTH + ATTN_WIDTH


def _in_proj_weights(w):
    piece = lambda i: w[:, _BASE_KV + i * KV_WIDTH:_BASE_KV + (i + 1) * KV_WIDTH]
    w_cols = jnp.concatenate([w[:, :_BASE_KV + 2 * KV_WIDTH], piece(2), piece(4)], axis=1)
    assert w_cols.shape[1] == IN_COLS
    rows = []
    for i in (3, 5):
        vt = piece(i).T
        for head in range(N_KV_HEADS):
            rows += [vt[head * HEAD_DIM:(head + 1) * HEAD_DIM], jnp.zeros((HEAD_DIM, D_MODEL), w.dtype)]
    n_gate = N_HEADS * N_BRANCH
    glog = _BASE_KV + 6 * KV_WIDTH
    rows += [w[:, glog:glog + n_gate].T, jnp.zeros((LANES - n_gate, D_MODEL), w.dtype)]
    w_rows = jnp.concatenate(rows, axis=0)
    assert w_rows.shape[0] == T_ROWS
    return w_cols.astype(BF16), w_rows.astype(BF16)


def _sigmoid(v):
    return 1.0 / (1.0 + jnp.exp(-v))


def _dot(a, b):
    return jnp.dot(a, b, preferred_element_type=F32)


def _dot_nt(a, b):
    return lax.dot_general(a, b, (((1,), (1,)), ((), ())), preferred_element_type=F32)


def _split_dot(v, m):
    hi = v.astype(BF16)
    lo = (v - hi.astype(F32)).astype(BF16)
    return _dot(hi, m) + _dot(lo, m)


IN_TM = 512


def _in_proj_kernel(x_ref, g_ref, w_ref, wt_ref, qg_ref, kgs_ref, kgw_ref, bd_ref,
                    u_ref, q_ref, kcvc_ref, ks_ref, kw_ref, vst_ref, vwt_ref, gatet_ref):
    x = x_ref[...]
    ms = jnp.mean(x * x, axis=-1, keepdims=True)
    h = (x * lax.rsqrt(ms + EPS) * g_ref[...]).astype(BF16)

    def proj(lo, width):
        return _dot(h, w_ref[:, lo:lo + width])

    u_ref[...] = proj(_OFF_VAL, CONV_WIDTH) * _sigmoid(proj(_OFF_GATE, CONV_WIDTH))

    bd = bd_ref[...]
    lane = lax.broadcasted_iota(jnp.int32, (IN_TM, LANES), 1)
    lower = lane < HEAD_DIM

    def head_slots(pair, upper, out_ref, slot):
        out_ref[:, slot * LANES:(slot + 1) * LANES] = jnp.where(lower, pair, upper).astype(BF16)
        swapped = pltpu.roll(pair, HEAD_DIM, axis=1)
        out_ref[:, (slot + 1) * LANES:(slot + 2) * LANES] = jnp.where(lower, swapped, upper).astype(BF16)

    q = proj(_OFF_Q, ATTN_WIDTH)
    for c in range(ATTN_WIDTH // LANES):
        qc = q[:, c * LANES:(c + 1) * LANES]
        msq = _split_dot(qc * qc, bd)
        head_slots(qc * lax.rsqrt(msq + EPS) * qg_ref[...], 0.0, q_ref, 2 * c)

    kcvc_ref[...] = proj(_OFF_KCVC, 2 * KV_WIDTH)

    def normed_k(off, kg_ref):
        k = proj(off, KV_WIDTH)
        return k * lax.rsqrt(_split_dot(k * k, bd) + EPS) * kg_ref[...]

    tok = (pl.program_id(0) % (SEQ // IN_TM)) * IN_TM + lax.broadcasted_iota(jnp.int32, (IN_TM, 1), 0)
    block_hot = jnp.where(lane - HEAD_DIM == tok // SEL_BLOCK, 1.0, 0.0)
    head_slots(normed_k(_OFF_KS, kgs_ref), block_hot, ks_ref, 0)
    head_slots(normed_k(_OFF_KW, kgw_ref), 0.0, kw_ref, 0)

    zt = _dot_nt(wt_ref[...], h)
    ones_rows = lax.broadcasted_iota(jnp.int32, (LANES, IN_TM), 0) >= HEAD_DIM
    for j, out_ref in enumerate((vst_ref, vst_ref, vwt_ref, vwt_ref)):
        blk = zt[j * LANES:(j + 1) * LANES]
        out_ref[0, j % N_KV_HEADS] = jnp.where(ones_rows, 1.0, blk).astype(BF16)
    gatet_ref[0] = _sigmoid(zt[4 * LANES:5 * LANES])


def _in_proj(x2, g, w, wt, qg, kgs, kgw, bd):
    tm = IN_TM
    per_seq = SEQ // tm
    row = lambda width: pl.BlockSpec((tm, width), lambda i: (i, 0))
    full = lambda a: pl.BlockSpec(a.shape, lambda i: (0,) * a.ndim)
    vt_spec = pl.BlockSpec((1, N_KV_HEADS, LANES, tm), lambda i: (i // per_seq, 0, 0, i % per_seq))
    vt_shape = jax.ShapeDtypeStruct((BATCH, N_KV_HEADS, LANES, SEQ), BF16)
    out_shape = (
        jax.ShapeDtypeStruct((TOKENS, CONV_WIDTH), F32),
        jax.ShapeDtypeStruct((TOKENS, N_HEADS * LANES), BF16),
        jax.ShapeDtypeStruct((TOKENS, 2 * KV_WIDTH), F32),
        jax.ShapeDtypeStruct((TOKENS, 2 * LANES), BF16),
        jax.ShapeDtypeStruct((TOKENS, 2 * LANES), BF16),
        vt_shape,
        vt_shape,
        jax.ShapeDtypeStruct((BATCH, LANES, SEQ), F32),
    )
    out_specs = (row(CONV_WIDTH), row(N_HEADS * LANES), row(2 * KV_WIDTH), row(2 * LANES), row(2 * LANES),
                 vt_spec, vt_spec,
                 pl.BlockSpec((1, LANES, tm), lambda i: (i // per_seq, 0, i % per_seq)))
    return pl.pallas_call(
        _in_proj_kernel,
        grid=(TOKENS // tm,),
        in_specs=[row(D_MODEL), full(g), full(w), full(wt), full(qg), full(kgs), full(kgw), full(bd)],
        out_specs=out_specs,
        out_shape=out_shape,
        compiler_params=pltpu.CompilerParams(
            dimension_semantics=("arbitrary",), vmem_limit_bytes=VMEM_LIMIT),
        name="in_proj",
    )(x2, g, w, wt, qg, kgs, kgw, bd)


CONV_TR = 512
CONV_HALO = 32
CONV_RS = 64
CONV_PARTIALS = 4


def _conv_kernel(ucur_ref, uprev_ref, w_ref, b_ref, lg_ref, lb_ref, bd_ref, o_ref, win_ref, y_ref):
    i = pl.program_id(1)
    first = CONV_HALO - (CONV_KERNEL - 1)
    shifted_rows = CONV_HALO + CONV_TR - SUBLANES
    for c in range(CONV_WIDTH // LANES):
        lanes = slice(c * LANES, (c + 1) * LANES)
        win_ref[0, c, 0:CONV_HALO, :] = jnp.where(i > 0, uprev_ref[0, :, lanes], 0.0)
        win_ref[0, c, CONV_HALO:CONV_HALO + CONV_TR, :] = ucur_ref[0, :, lanes]
        for s in range(1, SUBLANES):
            win_ref[s, c, 0:shifted_rows, :] = win_ref[0, c, s:s + shifted_rows, :]

        def body(r, carry):
            base = pl.multiple_of(r * CONV_RS, CONV_RS)
            accs = [None] * CONV_PARTIALS
            for k in range(CONV_KERNEL):
                off = first + k
                rows = pl.ds(base + (off // SUBLANES) * SUBLANES, CONV_RS)
                term = win_ref[off % SUBLANES, c, rows, :] * w_ref[k:k + 1, lanes]
                j = k % CONV_PARTIALS
                accs[j] = term if accs[j] is None else accs[j] + term
            y_ref[c, pl.ds(base, CONV_RS), :] = (accs[0] + accs[1]) + (accs[2] + accs[3]) + b_ref[:, lanes]
            return carry

        lax.fori_loop(0, CONV_TR // CONV_RS, body, 0)

    bd = bd_ref[...]
    for c in range(CONV_WIDTH // LANES):
        lanes = slice(c * LANES, (c + 1) * LANES)
        y = y_ref[c]
        d = y - _split_dot(y, bd)
        var = _split_dot(d * d, bd)
        yn = d * lax.rsqrt(var + EPS) * lg_ref[:, lanes] + lb_ref[:, lanes]
        o_ref[0, :, lanes] = (yn * _sigmoid(yn)).astype(BF16)


def _conv_module(u3, w, b, lg, lb, bd):
    halo_blocks = CONV_TR // CONV_HALO
    full = lambda a: pl.BlockSpec(a.shape, lambda bi, i: (0,) * a.ndim)
    return pl.pallas_call(
        _conv_kernel,
        grid=(BATCH, SEQ // CONV_TR),
        in_specs=[
            pl.BlockSpec((1, CONV_TR, CONV_WIDTH), lambda bi, i: (bi, i, 0)),
            pl.BlockSpec((1, CONV_HALO, CONV_WIDTH),
                         lambda bi, i: (bi, jnp.maximum(i * halo_blocks - 1, 0), 0)),
            full(w), full(b), full(lg), full(lb), full(bd),
        ],
        out_specs=pl.BlockSpec((1, CONV_TR, CONV_WIDTH), lambda bi, i: (bi, i, 0)),
        out_shape=jax.ShapeDtypeStruct((BATCH, SEQ, CONV_WIDTH), BF16),
        scratch_shapes=[pltpu.VMEM((SUBLANES, CONV_WIDTH // LANES, CONV_HALO + CONV_TR, LANES), F32),
                        pltpu.VMEM((CONV_WIDTH // LANES, CONV_TR, LANES), F32)],
        compiler_params=pltpu.CompilerParams(dimension_semantics=("arbitrary", "arbitrary")),
        name="conv_module",
    )(u3, u3, w, b, lg, lb, bd)


CMP_ROWS = SEQ // CMP_STRIDE
CMP_CHUNK = CMP_STRIDE * HEAD_DIM


def _compress_kernel(x_ref, pos_ref, w1_ref, b1_ref, w2_ref, b2_ref, w2t_ref, b2t_ref, kg_ref, o_ref):
    kind = pl.program_id(0)
    xa = x_ref[0, 0, 0]
    xb = pltpu.roll(xa, CMP_ROWS - 1, axis=0)
    a = (xa + pos_ref[0, :, 0:CMP_CHUNK]).astype(BF16)
    b = (xb + pos_ref[0, :, CMP_CHUNK:2 * CMP_CHUNK]).astype(BF16)
    hid = _dot(a, w1_ref[0, 0:CMP_CHUNK, :]) + _dot(b, w1_ref[0, CMP_CHUNK:2 * CMP_CHUNK, :]) + b1_ref[0]
    hid = (hid * _sigmoid(hid)).astype(BF16)
    out = _dot(hid, w2_ref[0]) + b2_ref[0]
    ms = jnp.sum(out * out, axis=-1, keepdims=True) * (1.0 / HEAD_DIM)
    normed = out * lax.rsqrt(ms + EPS) * kg_ref[...]
    out_t = _dot_nt(w2t_ref[0], hid) + b2t_ref[0]
    o_ref[0, 0, 0] = jnp.where(kind == 0, normed, out_t).astype(BF16)


def _compress(xr, pos, w1, b1, w2, b2, kg):
    w2t = jnp.swapaxes(w2, 1, 2)
    b2t = jnp.swapaxes(b2, 1, 2)
    per_kind = lambda a: pl.BlockSpec((1,) + a.shape[1:], lambda k, bi, h: (k,) + (0,) * (a.ndim - 1))
    return pl.pallas_call(
        _compress_kernel,
        grid=(2, BATCH, N_KV_HEADS),
        in_specs=[
            pl.BlockSpec((1, 1, 1, CMP_ROWS, CMP_CHUNK), lambda k, bi, h: (k, bi, h, 0, 0)),
            per_kind(pos), per_kind(w1), per_kind(b1), per_kind(w2), per_kind(b2),
            per_kind(w2t), per_kind(b2t),
            pl.BlockSpec(kg.shape, lambda k, bi, h: (0, 0)),
        ],
        out_specs=pl.BlockSpec((1, 1, 1, CMP_ROWS, LANES), lambda k, bi, h: (k, bi, h, 0, 0)),
        out_shape=jax.ShapeDtypeStruct((2, BATCH, N_KV_HEADS, CMP_ROWS, LANES), BF16),
        compiler_params=pltpu.CompilerParams(
            dimension_semantics=("arbitrary", "arbitrary", "arbitrary")),
        name="compress",
    )(xr, pos, w1, b1, w2, b2, w2t, b2t, kg)


QCOLS = GQA * Q_BLOCK
SEL_CHUNK = 256
WIN_CHUNK = 128
WIN_CHUNKS = (WINDOW + Q_BLOCK) // WIN_CHUNK
MASK_BIAS = -1e30


def _window_biases():
    k = np.arange(WIN_CHUNK)[:, None]
    t = (np.arange(QCOLS) % Q_BLOCK)[None, :]
    tables, index = [], {}
    for i in range(WIN_CHUNKS):
        key = i * WIN_CHUNK - WINDOW + k
        bias = np.where((key <= t) & (key > t - WINDOW), 0.0, MASK_BIAS)
        if bias.any():
            index[i] = len(tables)
            tables.append(bias)
    return np.stack(tables).astype(np.float32), index


_WIN_BIAS, _WIN_BIAS_INDEX = _window_biases()


def _attn_kernel(q_ref, gatet_ref, kc_ref, vct_ref, ks_ref, vst_ref, kw_ref, vwt_ref,
                 ovlt_ref, wbias_ref, o_ref, ssel_sc, swin_sc, acc_sc, out_sc):
    n = pl.program_id(1)
    start = n * Q_BLOCK
    col = lax.broadcasted_iota(jnp.int32, (1, QCOLS), 1)
    t_cols = start + (col & (Q_BLOCK - 1))
    tq = start + lax.broadcasted_iota(jnp.int32, (1, Q_BLOCK), 1)
    blk_row = lax.broadcasted_iota(jnp.int32, (N_SEL, 1), 0)
    blk_row_f = blk_row.astype(F32)
    cmp_end = lax.broadcasted_iota(jnp.int32, (CMP_ROWS, 1), 0) * CMP_STRIDE + (CMP_BLOCK - 1)

    kv_heads = range(N_KV_HEADS)
    kv_lanes = [slice(kvh * LANES, (kvh + 1) * LANES) for kvh in kv_heads]

    def gate_row(kvh, g, branch):
        r = (kvh * GQA + g) * N_BRANCH + branch
        return gatet_ref[0, r:r + 1, :]

    def head_rows(kvh, g):
        h = kvh * GQA + g
        return slice(h * HEAD_DIM, (h + 1) * HEAD_DIM)

    qs = []
    for kvh in kv_heads:
        h0 = kvh * GQA
        qs.append(jnp.concatenate(
            [q_ref[0, :, (h0 + g) * LANES:(h0 + g + 1) * LANES] for g in range(GQA)], axis=0))

    imp = []
    mask_c = cmp_end <= t_cols
    for kvh in kv_heads:
        s_c = jnp.where(mask_c, _dot_nt(kc_ref[0, kvh], qs[kvh]), NEG_INF)
        m_c = jnp.max(s_c, axis=0, keepdims=True)
        p_c = jnp.where(mask_c, jnp.exp2(s_c - m_c), 0.0)
        p_c = p_c / jnp.maximum(jnp.sum(p_c, axis=0, keepdims=True), 1e-30)
        o_cmp = _dot(vct_ref[0, kvh], p_c.astype(BF16))
        for g in range(GQA):
            out_sc[head_rows(kvh, g), :] = gate_row(kvh, g, 0) * o_cmp[0:HEAD_DIM, g * Q_BLOCK:(g + 1) * Q_BLOCK]
        p_sum = p_c[:, 0:Q_BLOCK]
        for g in range(1, GQA):
            p_sum = p_sum + p_c[:, g * Q_BLOCK:(g + 1) * Q_BLOCK]
        p_hi = p_sum.astype(BF16)
        p_lo = (p_sum - p_hi.astype(F32)).astype(BF16)
        imp.append(_dot(ovlt_ref[...], p_hi) + _dot(ovlt_ref[...], p_lo))

    tq2 = jnp.concatenate([tq] * N_KV_HEADS, axis=1)
    cur = tq2 >> 6
    forced = (blk_row == 0) | (blk_row == cur) | (blk_row == cur - 1)
    causal_blk = blk_row * SEL_BLOCK <= tq2
    score = jnp.where(causal_blk, jnp.concatenate(imp, axis=1) + jnp.where(forced, FORCE_BONUS, 0.0), NEG_INF)
    sel = jnp.zeros((N_SEL, N_KV_HEADS * Q_BLOCK), F32)
    for _ in range(SEL_TOPK):
        best = jnp.max(score, axis=0, keepdims=True)
        idx = jnp.min(jnp.where(score == best, blk_row_f, float(LANES)), axis=0, keepdims=True)
        hit = blk_row_f == idx
        sel = jnp.where(hit, 1.0, sel)
        score = jnp.where(hit, -3e38, score)
    sel_bias = (sel - 1.0) * (-MASK_BIAS)
    qs_sel = []
    for kvh in kv_heads:
        rows = jnp.concatenate([jnp.zeros((HEAD_DIM, Q_BLOCK), F32),
                                sel_bias[:, kvh * Q_BLOCK:(kvh + 1) * Q_BLOCK],
                                jnp.zeros((LANES - HEAD_DIM - N_SEL, Q_BLOCK), F32)], axis=0)
        bias_q = rows.T.astype(BF16)
        qs_sel.append(qs[kvh] + jnp.concatenate([bias_q] * GQA, axis=0))

    for kvh in kv_heads:
        mxw = jnp.full((1, QCOLS), NEG_INF, F32)
        kbs = [n * (Q_BLOCK // WIN_CHUNK) - WINDOW // WIN_CHUNK + i for i in range(WIN_CHUNKS)]
        k_win = [pl.multiple_of(jnp.maximum(kb, 0) * WIN_CHUNK, WIN_CHUNK) for kb in kbs]
        k_band = jnp.concatenate([kw_ref[0, pl.ds(k0, WIN_CHUNK), kv_lanes[kvh]] for k0 in k_win], axis=0)
        s_band = _dot_nt(k_band, qs[kvh])
        for i, kb in enumerate(kbs):
            s = s_band[i * WIN_CHUNK:(i + 1) * WIN_CHUNK]
            if i in _WIN_BIAS_INDEX:
                s = s + wbias_ref[_WIN_BIAS_INDEX[i]]
            if i < WINDOW // WIN_CHUNK:
                s = s + jnp.where(kb >= 0, 0.0, MASK_BIAS)
            swin_sc[kvh, i * WIN_CHUNK:(i + 1) * WIN_CHUNK, :] = s
            mxw = jnp.maximum(mxw, jnp.max(s, axis=0, keepdims=True))
        p = jnp.exp2(swin_sc[kvh] - mxw).astype(BF16)
        v_band = jnp.concatenate([vwt_ref[0, kvh, :, pl.ds(k0, WIN_CHUNK)] for k0 in k_win], axis=1)
        accw = _dot(v_band, p)
        o_win = accw[0:HEAD_DIM] * (1.0 / accw[HEAD_DIM:HEAD_DIM + 1])
        for g in range(GQA):
            out_sc[head_rows(kvh, g), :] += gate_row(kvh, g, 2) * o_win[:, g * Q_BLOCK:(g + 1) * Q_BLOCK]

    def sel_scores(kvh, k0):
        return _dot_nt(ks_ref[0, pl.ds(k0, SEL_CHUNK), kv_lanes[kvh]], qs_sel[kvh])

    def sel_pass1(i, mx):
        k0 = pl.multiple_of(i * SEL_CHUNK, SEL_CHUNK)
        out = []
        for kvh in kv_heads:
            s = sel_scores(kvh, k0)
            ssel_sc[kvh, pl.ds(k0, SEL_CHUNK), :] = s
            out.append(jnp.maximum(mx[kvh], jnp.max(s, axis=0, keepdims=True)))
        return tuple(out)

    last = (start + Q_BLOCK - 1) // SEL_CHUNK
    mx = lax.fori_loop(0, last, sel_pass1, (jnp.full((1, QCOLS), NEG_INF, F32),) * N_KV_HEADS)
    k_last = pl.multiple_of(last * SEL_CHUNK, SEL_CHUNK)
    causal = k_last + lax.broadcasted_iota(jnp.int32, (SEL_CHUNK, 1), 0) <= t_cols
    mx = list(mx)
    for kvh in kv_heads:
        s = jnp.where(causal, sel_scores(kvh, k_last), NEG_INF)
        ssel_sc[kvh, pl.ds(k_last, SEL_CHUNK), :] = s
        mx[kvh] = jnp.maximum(mx[kvh], jnp.max(s, axis=0, keepdims=True))
        acc_sc[kvh] = jnp.zeros((LANES, QCOLS), F32)

    def sel_pass2(i, carry):
        k0 = pl.multiple_of(i * SEL_CHUNK, SEL_CHUNK)
        for kvh in kv_heads:
            p = jnp.exp2(ssel_sc[kvh, pl.ds(k0, SEL_CHUNK), :] - mx[kvh]).astype(BF16)
            acc_sc[kvh] += _dot(vst_ref[0, kvh, :, pl.ds(k0, SEL_CHUNK)], p)
        return carry

    lax.fori_loop(0, last + 1, sel_pass2, 0)
    for kvh in kv_heads:
        acc = acc_sc[kvh]
        o_sel = acc[0:HEAD_DIM] * (1.0 / acc[HEAD_DIM:HEAD_DIM + 1])
        for g in range(GQA):
            out_sc[head_rows(kvh, g), :] += gate_row(kvh, g, 1) * o_sel[:, g * Q_BLOCK:(g + 1) * Q_BLOCK]

    for c in range(N_HEADS // 2):
        o_ref[0, :, c * LANES:(c + 1) * LANES] = out_sc[c * LANES:(c + 1) * LANES, :].T.astype(BF16)


def _attention(q3, gatet, kvc, ks3, vst, kw3, vwt, ovlt):
    wbias = jnp.asarray(_WIN_BIAS)
    qblk = lambda width: pl.BlockSpec((1, Q_BLOCK, width), lambda bi, n: (bi, n, 0))
    seq = pl.BlockSpec((1, SEQ, 2 * LANES), lambda bi, n: (bi, 0, 0))
    seq_t = pl.BlockSpec((1, N_KV_HEADS, LANES, SEQ), lambda bi, n: (bi, 0, 0, 0))
    cmp_spec = lambda kind: pl.BlockSpec((None, 1, N_KV_HEADS, CMP_ROWS, LANES),
                                         lambda bi, n: (kind, bi, 0, 0, 0))
    full = lambda a: pl.BlockSpec(a.shape, lambda bi, n: (0,) * a.ndim)
    return pl.pallas_call(
        _attn_kernel,
        grid=(BATCH, SEQ // Q_BLOCK),
        in_specs=[qblk(N_HEADS * LANES), pl.BlockSpec((1, LANES, Q_BLOCK), lambda bi, n: (bi, 0, n)),
                  cmp_spec(0), cmp_spec(1), seq, seq_t, seq, seq_t, full(ovlt), full(wbias)],
        out_specs=qblk(ATTN_WIDTH),
        out_shape=jax.ShapeDtypeStruct((BATCH, SEQ, ATTN_WIDTH), BF16),
        scratch_shapes=[pltpu.VMEM((N_KV_HEADS, SEQ, QCOLS), F32),
                        pltpu.VMEM((N_KV_HEADS, WIN_CHUNKS * WIN_CHUNK, QCOLS), F32),
                        pltpu.VMEM((N_KV_HEADS, LANES, QCOLS), F32),
                        pltpu.VMEM((ATTN_WIDTH, Q_BLOCK), F32)],
        compiler_params=pltpu.CompilerParams(
            dimension_semantics=("arbitrary", "arbitrary"), vmem_limit_bytes=VMEM_LIMIT),
        name="nsa_attention",
    )(q3, gatet, kvc, kvc, ks3, vst, kw3, vwt, ovlt, wbias)


OUT_TM = 512


def _out_proj_kernel(x_ref, conv_ref, attn_ref, w_ref, g_ref, x1_ref, h_ref):
    x1 = (x_ref[...] + _dot(conv_ref[...], w_ref[0:CONV_WIDTH, :])
          + _dot(attn_ref[...], w_ref[CONV_WIDTH:CONV_WIDTH + ATTN_WIDTH, :]))
    x1_ref[...] = x1
    ms = jnp.mean(x1 * x1, axis=-1, keepdims=True)
    h_ref[...] = (x1 * lax.rsqrt(ms + EPS) * g_ref[...]).astype(BF16)


def _out_proj_router_kernel(x_ref, conv_ref, attn_ref, w_ref, g_ref, r_ref, x1_ref, h_ref, comb_ref):
    x1 = (x_ref[...] + _dot(conv_ref[...], w_ref[0:CONV_WIDTH, :])
          + _dot(attn_ref[...], w_ref[CONV_WIDTH:CONV_WIDTH + ATTN_WIDTH, :]))
    x1_ref[...] = x1
    ms = jnp.mean(x1 * x1, axis=-1, keepdims=True)
    h = x1 * lax.rsqrt(ms + EPS) * g_ref[...]
    h_ref[...] = h.astype(BF16)
    h_hi = h.astype(BF16)
    h_lo = (h - h_hi.astype(F32)).astype(BF16)
    logits = _dot(h_hi, r_ref[0]) + _dot(h_lo, r_ref[0]) + _dot(h_hi, r_ref[1])
    lane = lax.broadcasted_iota(jnp.int32, (1, LANES), 1).astype(F32)
    logits = jnp.where(lane < N_EXPERTS, logits, NEG_INF)
    v1 = jnp.max(logits, axis=-1, keepdims=True)
    i1 = jnp.min(jnp.where(logits == v1, lane, float(LANES)), axis=-1, keepdims=True)
    hit1 = lane == i1
    rest = jnp.where(hit1, -3e38, logits)
    v2 = jnp.max(rest, axis=-1, keepdims=True)
    i2 = jnp.min(jnp.where(rest == v2, lane, float(LANES)), axis=-1, keepdims=True)
    hit2 = lane == i2
    e2 = jnp.exp(v2 - v1)
    w_top = 1.0 / (1.0 + e2)
    comb_ref[...] = jnp.where(hit1, w_top, 0.0) + jnp.where(hit2, e2 * w_top, 0.0)


def _out_proj(x2, conv2, attn2, w, g, router=None):
    tm = OUT_TM
    row = lambda width: pl.BlockSpec((tm, width), lambda i: (i, 0))
    full = lambda a: pl.BlockSpec(a.shape, lambda i: (0,) * a.ndim)
    in_specs = [row(D_MODEL), row(CONV_WIDTH), row(ATTN_WIDTH), full(w), full(g)]
    out_shape = [jax.ShapeDtypeStruct((TOKENS, D_MODEL), F32), jax.ShapeDtypeStruct((TOKENS, D_MODEL), BF16)]
    out_specs = [row(D_MODEL), row(D_MODEL)]
    args = [x2, conv2, attn2, w, g]
    body = _out_proj_kernel
    if router is not None:
        body = _out_proj_router_kernel
        in_specs.append(full(router))
        args.append(router)
        out_shape.append(jax.ShapeDtypeStruct((TOKENS, LANES), F32))
        out_specs.append(row(LANES))
    return pl.pallas_call(
        body,
        grid=(TOKENS // tm,),
        in_specs=in_specs,
        out_specs=tuple(out_specs),
        out_shape=tuple(out_shape),
        compiler_params=pltpu.CompilerParams(
            dimension_semantics=("arbitrary",), vmem_limit_bytes=VMEM_LIMIT),
        name="out_proj",
    )(*args)


FFN_TM = 512
FFN_TF = 1408


def _ffn_kernel(x1_ref, h_ref, w1_ref, w3_ref, w2_ref, o_ref):
    h = h_ref[...]
    out = x1_ref[...]
    for f in range(D_FF // FFN_TF):
        cols = slice(f * FFN_TF, (f + 1) * FFN_TF)
        a = _dot(h, w1_ref[:, cols])
        act = (a * _sigmoid(a) * _dot(h, w3_ref[:, cols])).astype(BF16)
        out = out + _dot(act, w2_ref[cols, :])
    o_ref[...] = out


def _dense_ffn(x1, h, w1, w3, w2):
    tm = FFN_TM
    resident = pl.BlockSpec(memory_space=pltpu.VMEM)
    return pl.pallas_call(
        _ffn_kernel,
        grid=(TOKENS // tm,),
        in_specs=[pl.BlockSpec((tm, D_MODEL), lambda i: (i, 0)), pl.BlockSpec((tm, D_MODEL), lambda i: (i, 0)),
                  resident, resident, resident],
        out_specs=pl.BlockSpec((tm, D_MODEL), lambda i: (i, 0)),
        out_shape=jax.ShapeDtypeStruct((TOKENS, D_MODEL), F32),
        compiler_params=pltpu.CompilerParams(
            dimension_semantics=("arbitrary",), vmem_limit_bytes=VMEM_LIMIT),
        name="dense_ffn",
    )(x1, h, w1, w3, w2)


MOE_SB = 512
MOE_TM = 512
MOE_GT = 256
MOE_CT = 256
MOE_CALIGN = 16
MOE_CBUF = 4
MOE_TF = 1792
MOE_NB = TOKENS // MOE_SB
MOE_TILES = (2 * TOKENS) // MOE_TM + N_EXPERTS + 1
MOE_GTILES = MOE_TILES * (MOE_TM // MOE_GT)
MOE_CPAIRS = (2 * TOKENS) // MOE_CT + 2 * N_EXPERTS * MOE_NB


def _moe_rank_kernel(comb_ref, tri_ref, rank_ref, cend_ref, carry_ref):
    @pl.when(pl.program_id(0) == 0)
    def _():
        carry_ref[...] = jnp.zeros((1, LANES), F32)

    routed = jnp.where(comb_ref[...] > 0.0, 1.0, 0.0).astype(BF16)
    carry = carry_ref[...]
    rank_ref[...] = carry + _dot(tri_ref[...], routed)
    carry = carry + jnp.sum(routed.astype(F32), axis=0, keepdims=True)
    carry_ref[...] = carry
    cend_ref[0] = jnp.broadcast_to(carry, (SUBLANES, LANES))


def _moe_rank(comb):
    idx = np.arange(MOE_SB)
    tri = jnp.asarray(idx[:, None] > idx[None, :], BF16)
    return pl.pallas_call(
        _moe_rank_kernel,
        grid=(MOE_NB,),
        in_specs=[pl.BlockSpec((MOE_SB, LANES), lambda b: (b, 0)),
                  pl.BlockSpec((MOE_SB, MOE_SB), lambda b: (0, 0))],
        out_specs=(pl.BlockSpec((MOE_SB, LANES), lambda b: (b, 0)),
                   pl.BlockSpec((1, SUBLANES, LANES), lambda b: (b, 0, 0))),
        out_shape=(jax.ShapeDtypeStruct((TOKENS, LANES), F32),
                   jax.ShapeDtypeStruct((MOE_NB, SUBLANES, LANES), F32)),
        scratch_shapes=[pltpu.VMEM((1, LANES), F32)],
        compiler_params=pltpu.CompilerParams(dimension_semantics=("arbitrary",)),
        name="moe_rank",
    )(comb, tri)


def _count_le(ends, v):
    return jnp.sum(ends[None, :] <= v[:, None], axis=1).astype(jnp.int32)


def _moe_schedule(cend):
    i32 = jnp.int32
    counts = cend[-1]
    cstart = jnp.concatenate([jnp.zeros((1, N_EXPERTS), i32), cend[:-1]], axis=0)
    tiles_e = (counts + MOE_TM - 1) // MOE_TM
    tile_end = jnp.cumsum(tiles_e)
    tile_start = tile_end - tiles_e
    n_tiles = tile_end[-1]
    group_start = tile_start * MOE_TM

    d = jnp.arange(MOE_TILES, dtype=i32)
    t_valid = d < n_tiles
    t_exp = jnp.minimum(_count_le(tile_end, d), N_EXPERTS - 1)
    t_exp = jnp.where(t_valid, t_exp, t_exp[jnp.maximum(n_tiles - 1, 0)])

    per_tile = MOE_TM // MOE_GT
    g = jnp.arange(MOE_GTILES, dtype=i32)
    g_exp = t_exp[g // per_tile]
    r0 = (g // per_tile - tile_start[g_exp]) * MOE_TM + (g % per_tile) * MOE_GT
    r1 = jnp.minimum(r0 + MOE_GT, counts[g_exp])
    gt_valid = t_valid[g // per_tile] & (r0 < counts[g_exp])
    b_lo = jnp.sum(cend[:, g_exp] <= r0[None, :], axis=0).astype(i32)
    b_hi = jnp.sum(cstart[:, g_exp] < r1[None, :], axis=0).astype(i32) - 1
    gather = (g_exp, jnp.where(gt_valid, b_lo, 0), jnp.where(gt_valid, b_hi - b_lo + 1, 0))

    lo = (group_start[None, :] + cstart).reshape(-1)
    hi = (group_start[None, :] + cend).reshape(-1)
    w_lo = (lo // MOE_CALIGN) * MOE_CALIGN
    n_w = jnp.where(hi > lo, (hi - w_lo + MOE_CT - 1) // MOE_CT, 0)
    q_end = jnp.cumsum(n_w)
    q_start = q_end - n_w
    p = jnp.arange(MOE_CPAIRS, dtype=i32)
    be = jnp.minimum(_count_le(q_end, p), MOE_NB * N_EXPERTS - 1)
    c_row = jnp.clip(w_lo[be] + (p - q_start[be]) * MOE_CT, 0, (MOE_TILES - 1) * MOE_TM)
    combine = (q_start[::N_EXPERTS], q_end[N_EXPERTS - 1::N_EXPERTS], c_row, be % N_EXPERTS)
    return group_start, t_exp, t_valid.astype(i32), gather, combine


def _moe_gather_kernel(exp_ref, first_ref, count_ref, h_ref, pos_ref, xs_ref):
    g = pl.program_id(0)
    rows = (g * MOE_GT + lax.broadcasted_iota(jnp.int32, (MOE_GT, 1), 0)).astype(F32)
    xs_ref[...] = jnp.zeros((MOE_GT, D_MODEL), BF16)

    def body(j, carry):
        t0 = pl.multiple_of((first_ref[g] + j) * MOE_SB, MOE_SB)
        pos = pos_ref[pl.ds(exp_ref[g], 1), pl.ds(t0, MOE_SB)]
        onehot = jnp.where(rows == pos, 1.0, 0.0).astype(BF16)
        xs_ref[...] += _dot(onehot, h_ref[pl.ds(t0, MOE_SB), :]).astype(BF16)
        return carry

    lax.fori_loop(0, count_ref[g], body, 0)


def _moe_gather(sched, h, pos_t):
    spec = pltpu.PrefetchScalarGridSpec(
        num_scalar_prefetch=3,
        grid=(MOE_GTILES,),
        in_specs=[pl.BlockSpec(memory_space=pltpu.VMEM), pl.BlockSpec(memory_space=pltpu.VMEM)],
        out_specs=pl.BlockSpec((MOE_GT, D_MODEL), lambda g, e, b, c: (g, 0)),
    )
    return pl.pallas_call(
        _moe_gather_kernel,
        grid_spec=spec,
        out_shape=jax.ShapeDtypeStruct((MOE_TILES * MOE_TM, D_MODEL), BF16),
        compiler_params=pltpu.CompilerParams(
            dimension_semantics=("arbitrary",), vmem_limit_bytes=VMEM_LIMIT),
        name="moe_gather",
    )(*sched, h, pos_t)


def _moe_expert_kernel(exp_ref, valid_ref, xs_ref, w1_ref, w3_ref, w2_ref, y_ref, acc_ref):
    d = pl.program_id(0)
    f = pl.program_id(1)

    @pl.when(valid_ref[d] == 1)
    def _():
        x = xs_ref[...]
        a = _dot(x, w1_ref[0])
        act = (a * _sigmoid(a) * _dot(x, w3_ref[0])).astype(BF16)
        part = _dot(act, w2_ref[0])

        @pl.when(f == 0)
        def _():
            acc_ref[...] = part

        @pl.when(f > 0)
        def _():
            acc_ref[...] += part

        @pl.when(f == pl.num_programs(1) - 1)
        def _():
            y_ref[...] = acc_ref[...].astype(BF16)

    @pl.when((valid_ref[d] == 0) & (f == 0))
    def _():
        y_ref[...] = jnp.zeros((MOE_TM, D_MODEL), BF16)


def _moe_experts(t_exp, t_valid, xs, w1, w3, w2):
    tf = MOE_TF
    nf = D_FF_EXPERT // tf
    fidx = lambda d, f, v: f * v[d] + (nf - 1) * (1 - v[d])
    spec = pltpu.PrefetchScalarGridSpec(
        num_scalar_prefetch=2,
        grid=(MOE_TILES, nf),
        in_specs=[pl.BlockSpec((MOE_TM, D_MODEL), lambda d, f, e, v: (d, 0)),
                  pl.BlockSpec((1, D_MODEL, tf), lambda d, f, e, v: (e[d], 0, fidx(d, f, v))),
                  pl.BlockSpec((1, D_MODEL, tf), lambda d, f, e, v: (e[d], 0, fidx(d, f, v))),
                  pl.BlockSpec((1, tf, D_MODEL), lambda d, f, e, v: (e[d], fidx(d, f, v), 0))],
        out_specs=pl.BlockSpec((MOE_TM, D_MODEL), lambda d, f, e, v: (d, 0)),
        scratch_shapes=[pltpu.VMEM((MOE_TM, D_MODEL), F32)],
    )
    return pl.pallas_call(
        _moe_expert_kernel,
        grid_spec=spec,
        out_shape=jax.ShapeDtypeStruct((MOE_TILES * MOE_TM, D_MODEL), BF16),
        compiler_params=pltpu.CompilerParams(
            dimension_semantics=("arbitrary", "arbitrary"), vmem_limit_bytes=VMEM_LIMIT),
        name="moe_experts",
    )(t_exp, t_valid, xs, w1, w3, w2)


def _moe_combine_kernel(first_ref, end_ref, row_ref, exp_ref, y_hbm, pos_ref, comb_ref, x1_ref, o_ref,
                        ybuf, sem):
    b = pl.program_id(0)
    first = first_ref[b]
    end = end_ref[b]

    def window_copy(p, slot):
        r0 = pl.multiple_of(row_ref[p], MOE_CALIGN)
        return pltpu.make_async_copy(y_hbm.at[pl.ds(r0, MOE_CT), :], ybuf.at[slot], sem.at[slot])

    for k in range(MOE_CBUF - 1):
        @pl.when(first + k < end)
        def _():
            window_copy(first + k, k).start()

    o_ref[...] = x1_ref[...]
    lane = lax.broadcasted_iota(jnp.int32, (1, LANES), 1)
    window = lax.broadcasted_iota(jnp.int32, (1, MOE_CT), 1)

    def body(p, carry):
        slot = (p - first) % MOE_CBUF
        window_copy(p, slot).wait()
        ahead = p + (MOE_CBUF - 1)

        @pl.when(ahead < end)
        def _():
            window_copy(ahead, (ahead - first) % MOE_CBUF).start()

        pick = lane == exp_ref[p]
        pos = jnp.sum(jnp.where(pick, pos_ref[...], 0.0), axis=-1, keepdims=True)
        wgt = jnp.sum(jnp.where(pick, comb_ref[...], 0.0), axis=-1, keepdims=True)
        cols = (row_ref[p] + window).astype(F32)
        onehot = jnp.where(pos == cols, 1.0, 0.0).astype(BF16)
        o_ref[...] += wgt * _dot(onehot, ybuf[slot])
        return carry

    lax.fori_loop(first, end, body, 0)


def _moe_combine(sched, y, pos, comb, x1):
    tok = lambda width: pl.BlockSpec((MOE_SB, width), lambda b, *_: (b, 0))
    spec = pltpu.PrefetchScalarGridSpec(
        num_scalar_prefetch=4,
        grid=(MOE_NB,),
        in_specs=[pl.BlockSpec(memory_space=pl.ANY), tok(LANES), tok(LANES), tok(D_MODEL)],
        out_specs=tok(D_MODEL),
        scratch_shapes=[pltpu.VMEM((MOE_CBUF, MOE_CT, D_MODEL), BF16), pltpu.SemaphoreType.DMA((MOE_CBUF,))],
    )
    return pl.pallas_call(
        _moe_combine_kernel,
        grid_spec=spec,
        out_shape=jax.ShapeDtypeStruct((TOKENS, D_MODEL), F32),
        compiler_params=pltpu.CompilerParams(
            dimension_semantics=("arbitrary",), vmem_limit_bytes=VMEM_LIMIT),
        name="moe_combine",
    )(*sched, y, pos, comb, x1)


def _moe_ffn(x1, h, comb, w1, w3, w2):
    rank, cend = _moe_rank(comb)
    cend = jnp.round(cend[:, 0, :N_EXPERTS]).astype(jnp.int32)
    group_start, t_exp, t_valid, gather, combine = _moe_schedule(cend)
    start = jnp.zeros((LANES,), F32).at[:N_EXPERTS].set(group_start.astype(F32))
    pos = jnp.where(comb > 0.0, start[None, :] + rank, -1.0)
    xs = _moe_gather(gather, h, pos[:, :SUBLANES].T)
    y = _moe_experts(t_exp, t_valid, xs, w1, w3, w2)
    return _moe_combine(combine, y, pos, comb, x1)


def _group_mean_matrix():
    idx = np.arange(LANES)
    return jnp.asarray((idx[:, None] // CONV_GROUP == idx[None, :] // CONV_GROUP) / CONV_GROUP, BF16)


def _overlap_matrix_t():
    c_start = np.arange(CMP_ROWS) * CMP_STRIDE
    j = np.arange(N_SEL)
    ovl = ((c_start[None, :] < (j[:, None] + 1) * SEL_BLOCK)
           & (c_start[None, :] + CMP_BLOCK > j[:, None] * SEL_BLOCK)
           & (np.arange(CMP_ROWS)[None, :] < N_CMP))
    return jnp.asarray(ovl, BF16)


def _dup(v):
    return jnp.concatenate([v, v], axis=-1)


def _zero_pad(v):
    return jnp.concatenate([v, jnp.zeros_like(v)], axis=-1)


def kernel(x, attn_norm_g, w_in, conv_dw_w, conv_dw_b, conv_ln_g, conv_ln_b, q_norm_g, k_norm_g,
           cmp_pos_k, cmp_pos_v, cmp_k_w1, cmp_k_b1, cmp_k_w2, cmp_k_b2,
           cmp_v_w1, cmp_v_b1, cmp_v_w2, cmp_v_b2, w_out, ffn_norm_g,
           dense_w1, dense_w3, dense_w2, router_w, moe_w1, moe_w3, moe_w2):
    assert x.shape == (BATCH, SEQ, D_MODEL) and x.dtype == F32
    bd = _group_mean_matrix()
    ovlt = _overlap_matrix_t()

    x2 = x.reshape(TOKENS, D_MODEL)
    for layer in range(DEPTH):
        w_cols, w_rows = _in_proj_weights(w_in[layer])
        qg = _dup(q_norm_g[layer])[None, :] * (LOG2_E * HEAD_DIM ** -0.5)
        kgs = _dup(k_norm_g[layer, 1])[None, :]
        kgw = _dup(k_norm_g[layer, 2])[None, :]
        u, q, kcvc, ks, kw, vst, vwt, gatet = _in_proj(
            x2, attn_norm_g[layer][None, :], w_cols, w_rows, qg, kgs, kgw, bd)

        conv = _conv_module(u.reshape(BATCH, SEQ, CONV_WIDTH), conv_dw_w[layer], conv_dw_b[layer][None, :],
                            conv_ln_g[layer][None, :], conv_ln_b[layer][None, :], bd)

        xr = kcvc.reshape(BATCH, SEQ, 2, N_KV_HEADS, HEAD_DIM).transpose(2, 0, 3, 1, 4)
        xr = xr.reshape(2, BATCH, N_KV_HEADS, CMP_ROWS, CMP_CHUNK)
        pos = jnp.stack([cmp_pos_k[layer], cmp_pos_v[layer]]).reshape(2, 1, CMP_BLOCK * HEAD_DIM)
        w1 = jnp.stack([cmp_k_w1[layer], cmp_v_w1[layer]]).astype(BF16)
        b1 = jnp.stack([cmp_k_b1[layer], cmp_v_b1[layer]])[:, None, :]
        w2 = _zero_pad(jnp.stack([cmp_k_w2[layer], cmp_v_w2[layer]])).astype(BF16)
        b2 = _zero_pad(jnp.stack([cmp_k_b2[layer], cmp_v_b2[layer]]))[:, None, :]
        kvc = _compress(xr, pos, w1, b1, w2, b2, _zero_pad(k_norm_g[layer, 0])[None, :])

        seq3 = lambda a: a.reshape(BATCH, SEQ, a.shape[-1])
        attn = _attention(seq3(q), gatet, kvc, seq3(ks), vst, seq3(kw), vwt, ovlt)

        w_o = w_out[layer].astype(BF16)
        conv2 = conv.reshape(TOKENS, CONV_WIDTH)
        attn2 = attn.reshape(TOKENS, ATTN_WIDTH)
        g_ffn = ffn_norm_g[layer][None, :]
        i = layer // 2
        if layer % 2 == 0:
            x1, h = _out_proj(x2, conv2, attn2, w_o, g_ffn)
            x2 = _dense_ffn(x1, h, dense_w1[i].astype(BF16), dense_w3[i].astype(BF16),
                            dense_w2[i].astype(BF16))
        else:
            r = jnp.pad(router_w[i], ((0, 0), (0, LANES - N_EXPERTS)))
            r_hi = r.astype(BF16)
            r_lo = (r - r_hi.astype(F32)).astype(BF16)
            x1, h, comb = _out_proj(x2, conv2, attn2, w_o, g_ffn, jnp.stack([r_hi, r_lo]))
            x2 = _moe_ffn(x1, h, comb, moe_w1[i].astype(BF16), moe_w3[i].astype(BF16),
                          moe_w2[i].astype(BF16))
    return x2.reshape(BATCH, SEQ, D_MODEL)
```

```python
import functools

import numpy as np
import jax
import jax.numpy as jnp
from jax import lax
from jax.experimental import pallas as pl
from jax.experimental.pallas import tpu as pltpu

D_MODEL = 1024
BATCH = 8
SEQ = 2048
DEPTH = 2
TOKENS = BATCH * SEQ

CONV_WIDTH = 512
CONV_GROUP = 64
CONV_KERNEL = 31
N_HEADS = 8
HEAD_DIM = 64
N_KV_HEADS = 2
GQA = N_HEADS // N_KV_HEADS
ATTN_WIDTH = N_HEADS * HEAD_DIM
KV_WIDTH = N_KV_HEADS * HEAD_DIM
N_BRANCH = 3
CMP_BLOCK = 32
CMP_STRIDE = 16
CMP_HIDDEN = 256
N_CMP = (SEQ - CMP_BLOCK) // CMP_STRIDE + 1
SEL_BLOCK = 64
SEL_TOPK = 8
N_SEL = SEQ // SEL_BLOCK
WINDOW = 512
Q_BLOCK = 256
FORCE_BONUS = 1e4
NEG_INF = -1e30
LOG2_E = 1.4426950408889634
D_FF = 2816
N_EXPERTS = 8
D_FF_EXPERT = 3584
EPS = 1e-6

LANES = 128
SUBLANES = 8
VMEM_LIMIT = 48 * 1024 * 1024

F32 = jnp.float32
BF16 = jnp.bfloat16

_OFF_VAL = 0
_OFF_GATE = 512
_OFF_Q = 1024
_OFF_KCVC = 1536
_OFF_KS = 1792
_OFF_KW = 1920
IN_COLS = 2048
T_ROWS = 5 * LANES

_BASE_KV = 2 * CONV_WIDTH + ATTN_WIDTH


def _in_proj_weights(w):
    piece = lambda i: w[:, _BASE_KV + i * KV_WIDTH:_BASE_KV + (i + 1) * KV_WIDTH]
    w_cols = jnp.concatenate([w[:, :_BASE_KV + 2 * KV_WIDTH], piece(2), piece(4)], axis=1)
    assert w_cols.shape[1] == IN_COLS
    rows = []
    for i in (3, 5):
        vt = piece(i).T
        for head in range(N_KV_HEADS):
            rows += [vt[head * HEAD_DIM:(head + 1) * HEAD_DIM], jnp.zeros((HEAD_DIM, D_MODEL), w.dtype)]
    n_gate = N_HEADS * N_BRANCH
    glog = _BASE_KV + 6 * KV_WIDTH
    rows += [w[:, glog:glog + n_gate].T, jnp.zeros((LANES - n_gate, D_MODEL), w.dtype)]
    w_rows = jnp.concatenate(rows, axis=0)
    assert w_rows.shape[0] == T_ROWS
    return w_cols.astype(BF16), w_rows.astype(BF16)


def _sigmoid(v):
    return 1.0 / (1.0 + jnp.exp(-v))


def _dot(a, b):
    return jnp.dot(a, b, preferred_element_type=F32)


def _dot_nt(a, b):
    return lax.dot_general(a, b, (((1,), (1,)), ((), ())), preferred_element_type=F32)


def _split_dot(v, m):
    hi = v.astype(BF16)
    lo = (v - hi.astype(F32)).astype(BF16)
    return _dot(hi, m) + _dot(lo, m)


IN_TM = 512


def _in_proj_kernel(x_ref, g_ref, w_ref, wt_ref, qg_ref, kgs_ref, kgw_ref, bd_ref,
                    u_ref, q_ref, kcvc_ref, ks_ref, kw_ref, vst_ref, vwt_ref, gatet_ref):
    x = x_ref[...]
    ms = jnp.mean(x * x, axis=-1, keepdims=True)
    h = (x * lax.rsqrt(ms + EPS) * g_ref[...]).astype(BF16)

    def proj(lo, width):
        return _dot(h, w_ref[:, lo:lo + width])

    u_ref[...] = proj(_OFF_VAL, CONV_WIDTH) * _sigmoid(proj(_OFF_GATE, CONV_WIDTH))

    bd = bd_ref[...]
    lane = lax.broadcasted_iota(jnp.int32, (IN_TM, LANES), 1)
    lower = lane < HEAD_DIM

    def head_slots(pair, upper, out_ref, slot):
        out_ref[:, slot * LANES:(slot + 1) * LANES] = jnp.where(lower, pair, upper).astype(BF16)
        swapped = pltpu.roll(pair, HEAD_DIM, axis=1)
        out_ref[:, (slot + 1) * LANES:(slot + 2) * LANES] = jnp.where(lower, swapped, upper).astype(BF16)

    q = proj(_OFF_Q, ATTN_WIDTH)
    for c in range(ATTN_WIDTH // LANES):
        qc = q[:, c * LANES:(c + 1) * LANES]
        msq = _split_dot(qc * qc, bd)
        head_slots(qc * lax.rsqrt(msq + EPS) * qg_ref[...], 0.0, q_ref, 2 * c)

    kcvc_ref[...] = proj(_OFF_KCVC, 2 * KV_WIDTH)

    def normed_k(off, kg_ref):
        k = proj(off, KV_WIDTH)
        return k * lax.rsqrt(_split_dot(k * k, bd) + EPS) * kg_ref[...]

    tok = (pl.program_id(0) % (SEQ // IN_TM)) * IN_TM + lax.broadcasted_iota(jnp.int32, (IN_TM, 1), 0)
    block_hot = jnp.where(lane - HEAD_DIM == tok // SEL_BLOCK, 1.0, 0.0)
    head_slots(normed_k(_OFF_KS, kgs_ref), block_hot, ks_ref, 0)
    head_slots(normed_k(_OFF_KW, kgw_ref), 0.0, kw_ref, 0)

    zt = _dot_nt(wt_ref[...], h)
    ones_rows = lax.broadcasted_iota(jnp.int32, (LANES, IN_TM), 0) >= HEAD_DIM
    for j, out_ref in enumerate((vst_ref, vst_ref, vwt_ref, vwt_ref)):
        blk = zt[j * LANES:(j + 1) * LANES]
        out_ref[0, j % N_KV_HEADS] = jnp.where(ones_rows, 1.0, blk).astype(BF16)
    gatet_ref[0] = _sigmoid(zt[4 * LANES:5 * LANES])


def _in_proj(x2, g, w, wt, qg, kgs, kgw, bd):
    tm = IN_TM
    per_seq = SEQ // tm
    row = lambda width: pl.BlockSpec((tm, width), lambda i: (i, 0))
    full = lambda a: pl.BlockSpec(a.shape, lambda i: (0,) * a.ndim)
    vt_spec = pl.BlockSpec((1, N_KV_HEADS, LANES, tm), lambda i: (i // per_seq, 0, 0, i % per_seq))
    vt_shape = jax.ShapeDtypeStruct((BATCH, N_KV_HEADS, LANES, SEQ), BF16)
    out_shape = (
        jax.ShapeDtypeStruct((TOKENS, CONV_WIDTH), F32),
        jax.ShapeDtypeStruct((TOKENS, N_HEADS * LANES), BF16),
        jax.ShapeDtypeStruct((TOKENS, 2 * KV_WIDTH), F32),
        jax.ShapeDtypeStruct((TOKENS, 2 * LANES), BF16),
        jax.ShapeDtypeStruct((TOKENS, 2 * LANES), BF16),
        vt_shape,
        vt_shape,
        jax.ShapeDtypeStruct((BATCH, LANES, SEQ), F32),
    )
    out_specs = (row(CONV_WIDTH), row(N_HEADS * LANES), row(2 * KV_WIDTH), row(2 * LANES), row(2 * LANES),
                 vt_spec, vt_spec,
                 pl.BlockSpec((1, LANES, tm), lambda i: (i // per_seq, 0, i % per_seq)))
    return pl.pallas_call(
        _in_proj_kernel,
        grid=(TOKENS // tm,),
        in_specs=[row(D_MODEL), full(g), full(w), full(wt), full(qg), full(kgs), full(kgw), full(bd)],
        out_specs=out_specs,
        out_shape=out_shape,
        compiler_params=pltpu.CompilerParams(
            dimension_semantics=("arbitrary",), vmem_limit_bytes=VMEM_LIMIT),
        name="in_proj",
    )(x2, g, w, wt, qg, kgs, kgw, bd)


CONV_TR = 512
CONV_HALO = 32
CONV_RS = 64
CONV_PARTIALS = 4


def _conv_kernel(ucur_ref, uprev_ref, w_ref, b_ref, lg_ref, lb_ref, bd_ref, o_ref, win_ref, y_ref):
    i = pl.program_id(1)
    first = CONV_HALO - (CONV_KERNEL - 1)
    shifted_rows = CONV_HALO + CONV_TR - SUBLANES
    for c in range(CONV_WIDTH // LANES):
        lanes = slice(c * LANES, (c + 1) * LANES)
        win_ref[0, c, 0:CONV_HALO, :] = jnp.where(i > 0, uprev_ref[0, :, lanes], 0.0)
        win_ref[0, c, CONV_HALO:CONV_HALO + CONV_TR, :] = ucur_ref[0, :, lanes]
        for s in range(1, SUBLANES):
            win_ref[s, c, 0:shifted_rows, :] = win_ref[0, c, s:s + shifted_rows, :]

        def body(r, carry):
            base = pl.multiple_of(r * CONV_RS, CONV_RS)
            accs = [None] * CONV_PARTIALS
            for k in range(CONV_KERNEL):
                off = first + k
                rows = pl.ds(base + (off // SUBLANES) * SUBLANES, CONV_RS)
                term = win_ref[off % SUBLANES, c, rows, :] * w_ref[k:k + 1, lanes]
                j = k % CONV_PARTIALS
                accs[j] = term if accs[j] is None else accs[j] + term
            y_ref[c, pl.ds(base, CONV_RS), :] = (accs[0] + accs[1]) + (accs[2] + accs[3]) + b_ref[:, lanes]
            return carry

        lax.fori_loop(0, CONV_TR // CONV_RS, body, 0)

    bd = bd_ref[...]
    for c in range(CONV_WIDTH // LANES):
        lanes = slice(c * LANES, (c + 1) * LANES)
        y = y_ref[c]
        d = y - _split_dot(y, bd)
        var = _split_dot(d * d, bd)
        yn = d * lax.rsqrt(var + EPS) * lg_ref[:, lanes] + lb_ref[:, lanes]
        o_ref[0, :, lanes] = (yn * _sigmoid(yn)).astype(BF16)


def _conv_module(u3, w, b, lg, lb, bd):
    halo_blocks = CONV_TR // CONV_HALO
    full = lambda a: pl.BlockSpec(a.shape, lambda bi, i: (0,) * a.ndim)
    return pl.pallas_call(
        _conv_kernel,
        grid=(BATCH, SEQ // CONV_TR),
        in_specs=[
            pl.BlockSpec((1, CONV_TR, CONV_WIDTH), lambda bi, i: (bi, i, 0)),
            pl.BlockSpec((1, CONV_HALO, CONV_WIDTH),
                         lambda bi, i: (bi, jnp.maximum(i * halo_blocks - 1, 0), 0)),
            full(w), full(b), full(lg), full(lb), full(bd),
        ],
        out_specs=pl.BlockSpec((1, CONV_TR, CONV_WIDTH), lambda bi, i: (bi, i, 0)),
        out_shape=jax.ShapeDtypeStruct((BATCH, SEQ, CONV_WIDTH), BF16),
        scratch_shapes=[pltpu.VMEM((SUBLANES, CONV_WIDTH // LANES, CONV_HALO + CONV_TR, LANES), F32),
                        pltpu.VMEM((CONV_WIDTH // LANES, CONV_TR, LANES), F32)],
        compiler_params=pltpu.CompilerParams(dimension_semantics=("arbitrary", "arbitrary")),
        name="conv_module",
    )(u3, u3, w, b, lg, lb, bd)


CMP_ROWS = SEQ // CMP_STRIDE
CMP_CHUNK = CMP_STRIDE * HEAD_DIM


def _compress_kernel(x_ref, pos_ref, w1_ref, b1_ref, w2_ref, b2_ref, w2t_ref, b2t_ref, kg_ref, o_ref):
    kind = pl.program_id(0)
    xa = x_ref[0, 0, 0]
    xb = pltpu.roll(xa, CMP_ROWS - 1, axis=0)
    a = (xa + pos_ref[0, :, 0:CMP_CHUNK]).astype(BF16)
    b = (xb + pos_ref[0, :, CMP_CHUNK:2 * CMP_CHUNK]).astype(BF16)
    hid = _dot(a, w1_ref[0, 0:CMP_CHUNK, :]) + _dot(b, w1_ref[0, CMP_CHUNK:2 * CMP_CHUNK, :]) + b1_ref[0]
    hid = (hid * _sigmoid(hid)).astype(BF16)
    out = _dot(hid, w2_ref[0]) + b2_ref[0]
    ms = jnp.sum(out * out, axis=-1, keepdims=True) * (1.0 / HEAD_DIM)
    normed = out * lax.rsqrt(ms + EPS) * kg_ref[...]
    out_t = _dot_nt(w2t_ref[0], hid) + b2t_ref[0]
    o_ref[0, 0, 0] = jnp.where(kind == 0, normed, out_t).astype(BF16)


def _compress(xr, pos, w1, b1, w2, b2, kg):
    w2t = jnp.swapaxes(w2, 1, 2)
    b2t = jnp.swapaxes(b2, 1, 2)
    per_kind = lambda a: pl.BlockSpec((1,) + a.shape[1:], lambda k, bi, h: (k,) + (0,) * (a.ndim - 1))
    return pl.pallas_call(
        _compress_kernel,
        grid=(2, BATCH, N_KV_HEADS),
        in_specs=[
            pl.BlockSpec((1, 1, 1, CMP_ROWS, CMP_CHUNK), lambda k, bi, h: (k, bi, h, 0, 0)),
            per_kind(pos), per_kind(w1), per_kind(b1), per_kind(w2), per_kind(b2),
            per_kind(w2t), per_kind(b2t),
            pl.BlockSpec(kg.shape, lambda k, bi, h: (0, 0)),
        ],
        out_specs=pl.BlockSpec((1, 1, 1, CMP_ROWS, LANES), lambda k, bi, h: (k, bi, h, 0, 0)),
        out_shape=jax.ShapeDtypeStruct((2, BATCH, N_KV_HEADS, CMP_ROWS, LANES), BF16),
        compiler_params=pltpu.CompilerParams(
            dimension_semantics=("arbitrary", "arbitrary", "arbitrary")),
        name="compress",
    )(xr, pos, w1, b1, w2, b2, w2t, b2t, kg)


QCOLS = GQA * Q_BLOCK
SEL_CHUNK = 512
WIN_CHUNK = 128
WIN_CHUNKS = (WINDOW + Q_BLOCK) // WIN_CHUNK
MASK_BIAS = -1e30


def _window_biases():
    k = np.arange(WIN_CHUNK)[:, None]
    t = (np.arange(QCOLS) % Q_BLOCK)[None, :]
    tables, index = [], {}
    for i in range(WIN_CHUNKS):
        key = i * WIN_CHUNK - WINDOW + k
        bias = np.where((key <= t) & (key > t - WINDOW), 0.0, MASK_BIAS)
        if bias.any():
            index[i] = len(tables)
            tables.append(bias)
    return np.stack(tables).astype(np.float32), index


_WIN_BIAS, _WIN_BIAS_INDEX = _window_biases()


def _attn_kernel(q_ref, gatet_ref, kc_ref, vct_ref, ks_ref, vst_ref, kw_ref, vwt_ref,
                 ovlt_ref, wbias_ref, o_ref, ssel_sc, swin_sc, acc_sc, out_sc):
    n = pl.program_id(1)
    start = n * Q_BLOCK
    col = lax.broadcasted_iota(jnp.int32, (1, QCOLS), 1)
    t_cols = start + (col & (Q_BLOCK - 1))
    tq = start + lax.broadcasted_iota(jnp.int32, (1, Q_BLOCK), 1)
    blk_row = lax.broadcasted_iota(jnp.int32, (N_SEL, 1), 0)
    blk_row_f = blk_row.astype(F32)
    cmp_end = lax.broadcasted_iota(jnp.int32, (CMP_ROWS, 1), 0) * CMP_STRIDE + (CMP_BLOCK - 1)

    kv_heads = range(N_KV_HEADS)
    kv_lanes = [slice(kvh * LANES, (kvh + 1) * LANES) for kvh in kv_heads]

    def gate_row(kvh, g, branch):
        r = (kvh * GQA + g) * N_BRANCH + branch
        return gatet_ref[0, r:r + 1, :]

    def head_rows(kvh, g):
        h = kvh * GQA + g
        return slice(h * HEAD_DIM, (h + 1) * HEAD_DIM)

    qs = []
    for kvh in kv_heads:
        h0 = kvh * GQA
        qs.append(jnp.concatenate(
            [q_ref[0, :, (h0 + g) * LANES:(h0 + g + 1) * LANES] for g in range(GQA)], axis=0))

    imp = []
    mask_c = cmp_end <= t_cols
    for kvh in kv_heads:
        s_c = jnp.where(mask_c, _dot_nt(kc_ref[0, kvh], qs[kvh]), NEG_INF)
        m_c = jnp.max(s_c, axis=0, keepdims=True)
        p_c = jnp.where(mask_c, jnp.exp2(s_c - m_c), 0.0)
        p_c = p_c / jnp.maximum(jnp.sum(p_c, axis=0, keepdims=True), 1e-30)
        o_cmp = _dot(vct_ref[0, kvh], p_c.astype(BF16))
        for g in range(GQA):
            out_sc[head_rows(kvh, g), :] = gate_row(kvh, g, 0) * o_cmp[0:HEAD_DIM, g * Q_BLOCK:(g + 1) * Q_BLOCK]
        p_sum = p_c[:, 0:Q_BLOCK]
        for g in range(1, GQA):
            p_sum = p_sum + p_c[:, g * Q_BLOCK:(g + 1) * Q_BLOCK]
        p_hi = p_sum.astype(BF16)
        p_lo = (p_sum - p_hi.astype(F32)).astype(BF16)
        imp.append(_dot(ovlt_ref[...], p_hi) + _dot(ovlt_ref[...], p_lo))

    tq2 = jnp.concatenate([tq] * N_KV_HEADS, axis=1)
    cur = tq2 >> 6
    forced = (blk_row == 0) | (blk_row == cur) | (blk_row == cur - 1)
    causal_blk = blk_row * SEL_BLOCK <= tq2
    score = jnp.where(causal_blk, jnp.concatenate(imp, axis=1) + jnp.where(forced, FORCE_BONUS, 0.0), NEG_INF)
    sel = jnp.zeros((N_SEL, N_KV_HEADS * Q_BLOCK), F32)
    for _ in range(SEL_TOPK):
        best = jnp.max(score, axis=0, keepdims=True)
        idx = jnp.min(jnp.where(score == best, blk_row_f, float(LANES)), axis=0, keepdims=True)
        hit = blk_row_f == idx
        sel = jnp.where(hit, 1.0, sel)
        score = jnp.where(hit, -3e38, score)
    sel_bias = (sel - 1.0) * (-MASK_BIAS)
    qs_sel = []
    for kvh in kv_heads:
        rows = jnp.concatenate([jnp.zeros((HEAD_DIM, Q_BLOCK), F32),
                                sel_bias[:, kvh * Q_BLOCK:(kvh + 1) * Q_BLOCK],
                                jnp.zeros((LANES - HEAD_DIM - N_SEL, Q_BLOCK), F32)], axis=0)
        bias_q = rows.T.astype(BF16)
        qs_sel.append(qs[kvh] + jnp.concatenate([bias_q] * GQA, axis=0))

    for kvh in kv_heads:
        mxw = jnp.full((1, QCOLS), NEG_INF, F32)
        kbs = [n * (Q_BLOCK // WIN_CHUNK) - WINDOW // WIN_CHUNK + i for i in range(WIN_CHUNKS)]
        k_win = [pl.multiple_of(jnp.maximum(kb, 0) * WIN_CHUNK, WIN_CHUNK) for kb in kbs]
        k_band = jnp.concatenate([kw_ref[0, pl.ds(k0, WIN_CHUNK), kv_lanes[kvh]] for k0 in k_win], axis=0)
        s_band = _dot_nt(k_band, qs[kvh])
        for i, kb in enumerate(kbs):
            s = s_band[i * WIN_CHUNK:(i + 1) * WIN_CHUNK]
            if i in _WIN_BIAS_INDEX:
                s = s + wbias_ref[_WIN_BIAS_INDEX[i]]
            if i < WINDOW // WIN_CHUNK:
                s = s + jnp.where(kb >= 0, 0.0, MASK_BIAS)
            swin_sc[kvh, i * WIN_CHUNK:(i + 1) * WIN_CHUNK, :] = s
            mxw = jnp.maximum(mxw, jnp.max(s, axis=0, keepdims=True))
        p = jnp.exp2(swin_sc[kvh] - mxw).astype(BF16)
        v_band = jnp.concatenate([vwt_ref[0, kvh, :, pl.ds(k0, WIN_CHUNK)] for k0 in k_win], axis=1)
        accw = _dot(v_band, p)
        o_win = accw[0:HEAD_DIM] * (1.0 / accw[HEAD_DIM:HEAD_DIM + 1])
        for g in range(GQA):
            out_sc[head_rows(kvh, g), :] += gate_row(kvh, g, 2) * o_win[:, g * Q_BLOCK:(g + 1) * Q_BLOCK]

    def sel_scores(kvh, k0):
        return _dot_nt(ks_ref[0, pl.ds(k0, SEL_CHUNK), kv_lanes[kvh]], qs_sel[kvh])

    def sel_pass1(i, mx):
        k0 = pl.multiple_of(i * SEL_CHUNK, SEL_CHUNK)
        out = []
        for kvh in kv_heads:
            s = sel_scores(kvh, k0)
            ssel_sc[kvh, pl.ds(k0, SEL_CHUNK), :] = s
            out.append(jnp.maximum(mx[kvh], jnp.max(s, axis=0, keepdims=True)))
        return tuple(out)

    last = (start + Q_BLOCK - 1) // SEL_CHUNK
    mx = lax.fori_loop(0, last, sel_pass1, (jnp.full((1, QCOLS), NEG_INF, F32),) * N_KV_HEADS)
    k_last = pl.multiple_of(last * SEL_CHUNK, SEL_CHUNK)
    causal = k_last + lax.broadcasted_iota(jnp.int32, (SEL_CHUNK, 1), 0) <= t_cols
    mx = list(mx)
    for kvh in kv_heads:
        s = jnp.where(causal, sel_scores(kvh, k_last), NEG_INF)
        ssel_sc[kvh, pl.ds(k_last, SEL_CHUNK), :] = s
        mx[kvh] = jnp.maximum(mx[kvh], jnp.max(s, axis=0, keepdims=True))
        acc_sc[kvh] = jnp.zeros((LANES, QCOLS), F32)

    def sel_pass2(i, carry):
        k0 = pl.multiple_of(i * SEL_CHUNK, SEL_CHUNK)
        for kvh in kv_heads:
            p = jnp.exp2(ssel_sc[kvh, pl.ds(k0, SEL_CHUNK), :] - mx[kvh]).astype(BF16)
            acc_sc[kvh] += _dot(vst_ref[0, kvh, :, pl.ds(k0, SEL_CHUNK)], p)
        return carry

    lax.fori_loop(0, last + 1, sel_pass2, 0)
    for kvh in kv_heads:
        acc = acc_sc[kvh]
        o_sel = acc[0:HEAD_DIM] * (1.0 / acc[HEAD_DIM:HEAD_DIM + 1])
        for g in range(GQA):
            out_sc[head_rows(kvh, g), :] += gate_row(kvh, g, 1) * o_sel[:, g * Q_BLOCK:(g + 1) * Q_BLOCK]

    for c in range(N_HEADS // 2):
        o_ref[0, :, c * LANES:(c + 1) * LANES] = out_sc[c * LANES:(c + 1) * LANES, :].T.astype(BF16)


def _attention(q3, gatet, kvc, ks3, vst, kw3, vwt, ovlt):
    wbias = jnp.asarray(_WIN_BIAS)
    qblk = lambda width: pl.BlockSpec((1, Q_BLOCK, width), lambda bi, n: (bi, n, 0))
    seq = pl.BlockSpec((1, SEQ, 2 * LANES), lambda bi, n: (bi, 0, 0))
    seq_t = pl.BlockSpec((1, N_KV_HEADS, LANES, SEQ), lambda bi, n: (bi, 0, 0, 0))
    cmp_spec = lambda kind: pl.BlockSpec((None, 1, N_KV_HEADS, CMP_ROWS, LANES),
                                         lambda bi, n: (kind, bi, 0, 0, 0))
    full = lambda a: pl.BlockSpec(a.shape, lambda bi, n: (0,) * a.ndim)
    return pl.pallas_call(
        _attn_kernel,
        grid=(BATCH, SEQ // Q_BLOCK),
        in_specs=[qblk(N_HEADS * LANES), pl.BlockSpec((1, LANES, Q_BLOCK), lambda bi, n: (bi, 0, n)),
                  cmp_spec(0), cmp_spec(1), seq, seq_t, seq, seq_t, full(ovlt), full(wbias)],
        out_specs=qblk(ATTN_WIDTH),
        out_shape=jax.ShapeDtypeStruct((BATCH, SEQ, ATTN_WIDTH), BF16),
        scratch_shapes=[pltpu.VMEM((N_KV_HEADS, SEQ, QCOLS), F32),
                        pltpu.VMEM((N_KV_HEADS, WIN_CHUNKS * WIN_CHUNK, QCOLS), F32),
                        pltpu.VMEM((N_KV_HEADS, LANES, QCOLS), F32),
                        pltpu.VMEM((ATTN_WIDTH, Q_BLOCK), F32)],
        compiler_params=pltpu.CompilerParams(
            dimension_semantics=("arbitrary", "arbitrary"), vmem_limit_bytes=VMEM_LIMIT),
        name="nsa_attention",
    )(q3, gatet, kvc, kvc, ks3, vst, kw3, vwt, ovlt, wbias)


OUT_TM = 512


def _out_proj_kernel(x_ref, conv_ref, attn_ref, w_ref, g_ref, x1_ref, h_ref):
    x1 = (x_ref[...] + _dot(conv_ref[...], w_ref[0:CONV_WIDTH, :])
          + _dot(attn_ref[...], w_ref[CONV_WIDTH:CONV_WIDTH + ATTN_WIDTH, :]))
    x1_ref[...] = x1
    ms = jnp.mean(x1 * x1, axis=-1, keepdims=True)
    h_ref[...] = (x1 * lax.rsqrt(ms + EPS) * g_ref[...]).astype(BF16)


def _out_proj_router_kernel(x_ref, conv_ref, attn_ref, w_ref, g_ref, r_ref, x1_ref, h_ref, comb_ref):
    x1 = (x_ref[...] + _dot(conv_ref[...], w_ref[0:CONV_WIDTH, :])
          + _dot(attn_ref[...], w_ref[CONV_WIDTH:CONV_WIDTH + ATTN_WIDTH, :]))
    x1_ref[...] = x1
    ms = jnp.mean(x1 * x1, axis=-1, keepdims=True)
    h = x1 * lax.rsqrt(ms + EPS) * g_ref[...]
    h_ref[...] = h.astype(BF16)
    h_hi = h.astype(BF16)
    h_lo = (h - h_hi.astype(F32)).astype(BF16)
    logits = _dot(h_hi, r_ref[0]) + _dot(h_lo, r_ref[0]) + _dot(h_hi, r_ref[1])
    lane = lax.broadcasted_iota(jnp.int32, (1, LANES), 1).astype(F32)
    logits = jnp.where(lane < N_EXPERTS, logits, NEG_INF)
    v1 = jnp.max(logits, axis=-1, keepdims=True)
    i1 = jnp.min(jnp.where(logits == v1, lane, float(LANES)), axis=-1, keepdims=True)
    hit1 = lane == i1
    rest = jnp.where(hit1, -3e38, logits)
    v2 = jnp.max(rest, axis=-1, keepdims=True)
    i2 = jnp.min(jnp.where(rest == v2, lane, float(LANES)), axis=-1, keepdims=True)
    hit2 = lane == i2
    e2 = jnp.exp(v2 - v1)
    w_top = 1.0 / (1.0 + e2)
    comb_ref[...] = jnp.where(hit1, w_top, 0.0) + jnp.where(hit2, e2 * w_top, 0.0)


def _out_proj(x2, conv2, attn2, w, g, router=None):
    tm = OUT_TM
    row = lambda width: pl.BlockSpec((tm, width), lambda i: (i, 0))
    full = lambda a: pl.BlockSpec(a.shape, lambda i: (0,) * a.ndim)
    in_specs = [row(D_MODEL), row(CONV_WIDTH), row(ATTN_WIDTH), full(w), full(g)]
    out_shape = [jax.ShapeDtypeStruct((TOKENS, D_MODEL), F32), jax.ShapeDtypeStruct((TOKENS, D_MODEL), BF16)]
    out_specs = [row(D_MODEL), row(D_MODEL)]
    args = [x2, conv2, attn2, w, g]
    body = _out_proj_kernel
    if router is not None:
        body = _out_proj_router_kernel
        in_specs.append(full(router))
        args.append(router)
        out_shape.append(jax.ShapeDtypeStruct((TOKENS, LANES), F32))
        out_specs.append(row(LANES))
    return pl.pallas_call(
        body,
        grid=(TOKENS // tm,),
        in_specs=in_specs,
        out_specs=tuple(out_specs),
        out_shape=tuple(out_shape),
        compiler_params=pltpu.CompilerParams(
            dimension_semantics=("arbitrary",), vmem_limit_bytes=VMEM_LIMIT),
        name="out_proj",
    )(*args)


FFN_TM = 512
FFN_TF = 1408


def _ffn_kernel(x1_ref, h_ref, w1_ref, w3_ref, w2_ref, o_ref):
    h = h_ref[...]
    out = x1_ref[...]
    for f in range(D_FF // FFN_TF):
        cols = slice(f * FFN_TF, (f + 1) * FFN_TF)
        a = _dot(h, w1_ref[:, cols])
        act = (a * _sigmoid(a) * _dot(h, w3_ref[:, cols])).astype(BF16)
        out = out + _dot(act, w2_ref[cols, :])
    o_ref[...] = out


def _dense_ffn(x1, h, w1, w3, w2):
    tm = FFN_TM
    resident = pl.BlockSpec(memory_space=pltpu.VMEM)
    return pl.pallas_call(
        _ffn_kernel,
        grid=(TOKENS // tm,),
        in_specs=[pl.BlockSpec((tm, D_MODEL), lambda i: (i, 0)), pl.BlockSpec((tm, D_MODEL), lambda i: (i, 0)),
                  resident, resident, resident],
        out_specs=pl.BlockSpec((tm, D_MODEL), lambda i: (i, 0)),
        out_shape=jax.ShapeDtypeStruct((TOKENS, D_MODEL), F32),
        compiler_params=pltpu.CompilerParams(
            dimension_semantics=("arbitrary",), vmem_limit_bytes=VMEM_LIMIT),
        name="dense_ffn",
    )(x1, h, w1, w3, w2)


MOE_SB = 512
MOE_TM = 512
MOE_GT = 256
MOE_CT = 256
MOE_CALIGN = 16
MOE_CBUF = 6
MOE_TF = 1792
MOE_NB = TOKENS // MOE_SB
MOE_TILES = (2 * TOKENS) // MOE_TM + N_EXPERTS + 1
MOE_GTILES = MOE_TILES * (MOE_TM // MOE_GT)
MOE_CPAIRS = (2 * TOKENS) // MOE_CT + 2 * N_EXPERTS * MOE_NB


def _moe_rank_kernel(comb_ref, tri_ref, rank_ref, cend_ref, carry_ref):
    @pl.when(pl.program_id(0) == 0)
    def _():
        carry_ref[...] = jnp.zeros((1, LANES), F32)

    routed = jnp.where(comb_ref[...] > 0.0, 1.0, 0.0).astype(BF16)
    carry = carry_ref[...]
    rank_ref[...] = carry + _dot(tri_ref[...], routed)
    carry = carry + jnp.sum(routed.astype(F32), axis=0, keepdims=True)
    carry_ref[...] = carry
    cend_ref[0] = jnp.broadcast_to(carry, (SUBLANES, LANES))


def _moe_rank(comb):
    idx = np.arange(MOE_SB)
    tri = jnp.asarray(idx[:, None] > idx[None, :], BF16)
    return pl.pallas_call(
        _moe_rank_kernel,
        grid=(MOE_NB,),
        in_specs=[pl.BlockSpec((MOE_SB, LANES), lambda b: (b, 0)),
                  pl.BlockSpec((MOE_SB, MOE_SB), lambda b: (0, 0))],
        out_specs=(pl.BlockSpec((MOE_SB, LANES), lambda b: (b, 0)),
                   pl.BlockSpec((1, SUBLANES, LANES), lambda b: (b, 0, 0))),
        out_shape=(jax.ShapeDtypeStruct((TOKENS, LANES), F32),
                   jax.ShapeDtypeStruct((MOE_NB, SUBLANES, LANES), F32)),
        scratch_shapes=[pltpu.VMEM((1, LANES), F32)],
        compiler_params=pltpu.CompilerParams(dimension_semantics=("arbitrary",)),
        name="moe_rank",
    )(comb, tri)


def _count_le(ends, v):
    return jnp.sum(ends[None, :] <= v[:, None], axis=1).astype(jnp.int32)


def _moe_schedule(cend):
    i32 = jnp.int32
    counts = cend[-1]
    cstart = jnp.concatenate([jnp.zeros((1, N_EXPERTS), i32), cend[:-1]], axis=0)
    tiles_e = (counts + MOE_TM - 1) // MOE_TM
    tile_end = jnp.cumsum(tiles_e)
    tile_start = tile_end - tiles_e
    n_tiles = tile_end[-1]
    group_start = tile_start * MOE_TM

    d = jnp.arange(MOE_TILES, dtype=i32)
    t_valid = d < n_tiles
    t_exp = jnp.minimum(_count_le(tile_end, d), N_EXPERTS - 1)
    t_exp = jnp.where(t_valid, t_exp, t_exp[jnp.maximum(n_tiles - 1, 0)])

    per_tile = MOE_TM // MOE_GT
    g = jnp.arange(MOE_GTILES, dtype=i32)
    g_exp = t_exp[g // per_tile]
    r0 = (g // per_tile - tile_start[g_exp]) * MOE_TM + (g % per_tile) * MOE_GT
    r1 = jnp.minimum(r0 + MOE_GT, counts[g_exp])
    gt_valid = t_valid[g // per_tile] & (r0 < counts[g_exp])
    b_lo = jnp.sum(cend[:, g_exp] <= r0[None, :], axis=0).astype(i32)
    b_hi = jnp.sum(cstart[:, g_exp] < r1[None, :], axis=0).astype(i32) - 1
    gather = (g_exp, jnp.where(gt_valid, b_lo, 0), jnp.where(gt_valid, b_hi - b_lo + 1, 0))

    lo = (group_start[None, :] + cstart).reshape(-1)
    hi = (group_start[None, :] + cend).reshape(-1)
    w_lo = (lo // MOE_CALIGN) * MOE_CALIGN
    n_w = jnp.where(hi > lo, (hi - w_lo + MOE_CT - 1) // MOE_CT, 0)
    q_end = jnp.cumsum(n_w)
    q_start = q_end - n_w
    p = jnp.arange(MOE_CPAIRS, dtype=i32)
    be = jnp.minimum(_count_le(q_end, p), MOE_NB * N_EXPERTS - 1)
    c_row = jnp.clip(w_lo[be] + (p - q_start[be]) * MOE_CT, 0, (MOE_TILES - 1) * MOE_TM)
    combine = (q_start[::N_EXPERTS], q_end[N_EXPERTS - 1::N_EXPERTS], c_row, be % N_EXPERTS)
    return group_start, t_exp, t_valid.astype(i32), gather, combine


def _moe_gather_kernel(exp_ref, first_ref, count_ref, h_ref, pos_ref, xs_ref):
    g = pl.program_id(0)
    rows = (g * MOE_GT + lax.broadcasted_iota(jnp.int32, (MOE_GT, 1), 0)).astype(F32)
    xs_ref[...] = jnp.zeros((MOE_GT, D_MODEL), BF16)

    def body(j, carry):
        t0 = pl.multiple_of((first_ref[g] + j) * MOE_SB, MOE_SB)
        pos = pos_ref[pl.ds(exp_ref[g], 1), pl.ds(t0, MOE_SB)]
        onehot = jnp.where(rows == pos, 1.0, 0.0).astype(BF16)
        xs_ref[...] += _dot(onehot, h_ref[pl.ds(t0, MOE_SB), :]).astype(BF16)
        return carry

    lax.fori_loop(0, count_ref[g], body, 0)


def _moe_gather(sched, h, pos_t):
    spec = pltpu.PrefetchScalarGridSpec(
        num_scalar_prefetch=3,
        grid=(MOE_GTILES,),
        in_specs=[pl.BlockSpec(memory_space=pltpu.VMEM), pl.BlockSpec(memory_space=pltpu.VMEM)],
        out_specs=pl.BlockSpec((MOE_GT, D_MODEL), lambda g, e, b, c: (g, 0)),
    )
    return pl.pallas_call(
        _moe_gather_kernel,
        grid_spec=spec,
        out_shape=jax.ShapeDtypeStruct((MOE_TILES * MOE_TM, D_MODEL), BF16),
        compiler_params=pltpu.CompilerParams(
            dimension_semantics=("arbitrary",), vmem_limit_bytes=VMEM_LIMIT),
        name="moe_gather",
    )(*sched, h, pos_t)


def _moe_expert_kernel(exp_ref, valid_ref, xs_ref, w1_ref, w3_ref, w2_ref, y_ref, acc_ref):
    d = pl.program_id(0)
    f = pl.program_id(1)

    @pl.when(valid_ref[d] == 1)
    def _():
        x = xs_ref[...]
        a = _dot(x, w1_ref[0])
        act = (a * _sigmoid(a) * _dot(x, w3_ref[0])).astype(BF16)
        part = _dot(act, w2_ref[0])

        @pl.when(f == 0)
        def _():
            acc_ref[...] = part

        @pl.when(f > 0)
        def _():
            acc_ref[...] += part

        @pl.when(f == pl.num_programs(1) - 1)
        def _():
            y_ref[...] = acc_ref[...].astype(BF16)

    @pl.when((valid_ref[d] == 0) & (f == 0))
    def _():
        y_ref[...] = jnp.zeros((MOE_TM, D_MODEL), BF16)


def _moe_experts(t_exp, t_valid, xs, w1, w3, w2):
    tf = MOE_TF
    nf = D_FF_EXPERT // tf
    fidx = lambda d, f, v: f * v[d] + (nf - 1) * (1 - v[d])
    spec = pltpu.PrefetchScalarGridSpec(
        num_scalar_prefetch=2,
        grid=(MOE_TILES, nf),
        in_specs=[pl.BlockSpec((MOE_TM, D_MODEL), lambda d, f, e, v: (d, 0)),
                  pl.BlockSpec((1, D_MODEL, tf), lambda d, f, e, v: (e[d], 0, fidx(d, f, v))),
                  pl.BlockSpec((1, D_MODEL, tf), lambda d, f, e, v: (e[d], 0, fidx(d, f, v))),
                  pl.BlockSpec((1, tf, D_MODEL), lambda d, f, e, v: (e[d], fidx(d, f, v), 0))],
        out_specs=pl.BlockSpec((MOE_TM, D_MODEL), lambda d, f, e, v: (d, 0)),
        scratch_shapes=[pltpu.VMEM((MOE_TM, D_MODEL), F32)],
    )
    return pl.pallas_call(
        _moe_expert_kernel,
        grid_spec=spec,
        out_shape=jax.ShapeDtypeStruct((MOE_TILES * MOE_TM, D_MODEL), BF16),
        compiler_params=pltpu.CompilerParams(
            dimension_semantics=("arbitrary", "arbitrary"), vmem_limit_bytes=VMEM_LIMIT),
        name="moe_experts",
    )(t_exp, t_valid, xs, w1, w3, w2)


def _moe_combine_kernel(first_ref, end_ref, row_ref, exp_ref, y_hbm, pos_ref, comb_ref, x1_ref, o_ref,
                        ybuf, sem):
    b = pl.program_id(0)
    first = first_ref[b]
    end = end_ref[b]

    def window_copy(p, slot):
        r0 = pl.multiple_of(row_ref[p], MOE_CALIGN)
        return pltpu.make_async_copy(y_hbm.at[pl.ds(r0, MOE_CT), :], ybuf.at[slot], sem.at[slot])

    for k in range(MOE_CBUF - 1):
        @pl.when(first + k < end)
        def _():
            window_copy(first + k, k).start()

    o_ref[...] = x1_ref[...]
    lane = lax.broadcasted_iota(jnp.int32, (1, LANES), 1)
    window = lax.broadcasted_iota(jnp.int32, (1, MOE_CT), 1)

    def body(p, carry):
        slot = (p - first) % MOE_CBUF
        window_copy(p, slot).wait()
        ahead = p + (MOE_CBUF - 1)

        @pl.when(ahead < end)
        def _():
            window_copy(ahead, (ahead - first) % MOE_CBUF).start()

        pick = lane == exp_ref[p]
        pos = jnp.sum(jnp.where(pick, pos_ref[...], 0.0), axis=-1, keepdims=True)
        wgt = jnp.sum(jnp.where(pick, comb_ref[...], 0.0), axis=-1, keepdims=True)
        cols = (row_ref[p] + window).astype(F32)
        onehot = jnp.where(pos == cols, 1.0, 0.0).astype(BF16)
        o_ref[...] += wgt * _dot(onehot, ybuf[slot])
        return carry

    lax.fori_loop(first, end, body, 0)


def _moe_combine(sched, y, pos, comb, x1):
    tok = lambda width: pl.BlockSpec((MOE_SB, width), lambda b, *_: (b, 0))
    spec = pltpu.PrefetchScalarGridSpec(
        num_scalar_prefetch=4,
        grid=(MOE_NB,),
        in_specs=[pl.BlockSpec(memory_space=pl.ANY), tok(LANES), tok(LANES), tok(D_MODEL)],
        out_specs=tok(D_MODEL),
        scratch_shapes=[pltpu.VMEM((MOE_CBUF, MOE_CT, D_MODEL), BF16), pltpu.SemaphoreType.DMA((MOE_CBUF,))],
    )
    return pl.pallas_call(
        _moe_combine_kernel,
        grid_spec=spec,
        out_shape=jax.ShapeDtypeStruct((TOKENS, D_MODEL), F32),
        compiler_params=pltpu.CompilerParams(
            dimension_semantics=("arbitrary",), vmem_limit_bytes=VMEM_LIMIT),
        name="moe_combine",
    )(*sched, y, pos, comb, x1)


def _moe_ffn(x1, h, comb, w1, w3, w2):
    rank, cend = _moe_rank(comb)
    cend = jnp.round(cend[:, 0, :N_EXPERTS]).astype(jnp.int32)
    group_start, t_exp, t_valid, gather, combine = _moe_schedule(cend)
    start = jnp.zeros((LANES,), F32).at[:N_EXPERTS].set(group_start.astype(F32))
    pos = jnp.where(comb > 0.0, start[None, :] + rank, -1.0)
    xs = _moe_gather(gather, h, pos[:, :SUBLANES].T)
    y = _moe_experts(t_exp, t_valid, xs, w1, w3, w2)
    return _moe_combine(combine, y, pos, comb, x1)


def _group_mean_matrix():
    idx = np.arange(LANES)
    return jnp.asarray((idx[:, None] // CONV_GROUP == idx[None, :] // CONV_GROUP) / CONV_GROUP, BF16)


def _overlap_matrix_t():
    c_start = np.arange(CMP_ROWS) * CMP_STRIDE
    j = np.arange(N_SEL)
    ovl = ((c_start[None, :] < (j[:, None] + 1) * SEL_BLOCK)
           & (c_start[None, :] + CMP_BLOCK > j[:, None] * SEL_BLOCK)
           & (np.arange(CMP_ROWS)[None, :] < N_CMP))
    return jnp.asarray(ovl, BF16)


def _dup(v):
    return jnp.concatenate([v, v], axis=-1)


def _zero_pad(v):
    return jnp.concatenate([v, jnp.zeros_like(v)], axis=-1)


def kernel(x, attn_norm_g, w_in, conv_dw_w, conv_dw_b, conv_ln_g, conv_ln_b, q_norm_g, k_norm_g,
           cmp_pos_k, cmp_pos_v, cmp_k_w1, cmp_k_b1, cmp_k_w2, cmp_k_b2,
           cmp_v_w1, cmp_v_b1, cmp_v_w2, cmp_v_b2, w_out, ffn_norm_g,
           dense_w1, dense_w3, dense_w2, router_w, moe_w1, moe_w3, moe_w2):
    assert x.shape == (BATCH, SEQ, D_MODEL) and x.dtype == F32
    bd = _group_mean_matrix()
    ovlt = _overlap_matrix_t()

    x2 = x.reshape(TOKENS, D_MODEL)
    for layer in range(DEPTH):
        w_cols, w_rows = _in_proj_weights(w_in[layer])
        qg = _dup(q_norm_g[layer])[None, :] * (LOG2_E * HEAD_DIM ** -0.5)
        kgs = _dup(k_norm_g[layer, 1])[None, :]
        kgw = _dup(k_norm_g[layer, 2])[None, :]
        u, q, kcvc, ks, kw, vst, vwt, gatet = _in_proj(
            x2, attn_norm_g[layer][None, :], w_cols, w_rows, qg, kgs, kgw, bd)

        conv = _conv_module(u.reshape(BATCH, SEQ, CONV_WIDTH), conv_dw_w[layer], conv_dw_b[layer][None, :],
                            conv_ln_g[layer][None, :], conv_ln_b[layer][None, :], bd)

        xr = kcvc.reshape(BATCH, SEQ, 2, N_KV_HEADS, HEAD_DIM).transpose(2, 0, 3, 1, 4)
        xr = xr.reshape(2, BATCH, N_KV_HEADS, CMP_ROWS, CMP_CHUNK)
        pos = jnp.stack([cmp_pos_k[layer], cmp_pos_v[layer]]).reshape(2, 1, CMP_BLOCK * HEAD_DIM)
        w1 = jnp.stack([cmp_k_w1[layer], cmp_v_w1[layer]]).astype(BF16)
        b1 = jnp.stack([cmp_k_b1[layer], cmp_v_b1[layer]])[:, None, :]
        w2 = _zero_pad(jnp.stack([cmp_k_w2[layer], cmp_v_w2[layer]])).astype(BF16)
        b2 = _zero_pad(jnp.stack([cmp_k_b2[layer], cmp_v_b2[layer]]))[:, None, :]
        kvc = _compress(xr, pos, w1, b1, w2, b2, _zero_pad(k_norm_g[layer, 0])[None, :])

        seq3 = lambda a: a.reshape(BATCH, SEQ, a.shape[-1])
        attn = _attention(seq3(q), gatet, kvc, seq3(ks), vst, seq3(kw), vwt, ovlt)

        w_o = w_out[layer].astype(BF16)
        conv2 = conv.reshape(TOKENS, CONV_WIDTH)
        attn2 = attn.reshape(TOKENS, ATTN_WIDTH)
        g_ffn = ffn_norm_g[layer][None, :]
        i = layer // 2
        if layer % 2 == 0:
            x1, h = _out_proj(x2, conv2, attn2, w_o, g_ffn)
            x2 = _dense_ffn(x1, h, dense_w1[i].astype(BF16), dense_w3[i].astype(BF16),
                            dense_w2[i].astype(BF16))
        else:
            r = jnp.pad(router_w[i], ((0, 0), (0, LANES - N_EXPERTS)))
            r_hi = r.astype(BF16)
            r_lo = (r - r_hi.astype(F32)).astype(BF16)
            x1, h, comb = _out_proj(x2, conv2, attn2, w_o, g_ffn, jnp.stack([r_hi, r_lo]))
            x2 = _moe_ffn(x1, h, comb, moe_w1[i].astype(BF16), moe_w3[i].astype(BF16),
                          moe_w2[i].astype(BF16))
    return x2.reshape(BATCH, SEQ, D_MODEL)
```

```python
import numpy as np
import jax
import jax.numpy as jnp
from jax import lax
from jax.experimental import pallas as pl
from jax.experimental.pallas import tpu as pltpu

D_MODEL = 1024
BATCH = 8
SEQ = 2048
DEPTH = 2
TOKENS = BATCH * SEQ

CONV_WIDTH = 512
CONV_GROUP = 64
CONV_KERNEL = 31
N_HEADS = 8
HEAD_DIM = 64
N_KV_HEADS = 2
GQA = N_HEADS // N_KV_HEADS
ATTN_WIDTH = N_HEADS * HEAD_DIM
KV_WIDTH = N_KV_HEADS * HEAD_DIM
N_BRANCH = 3
CMP_BLOCK = 32
CMP_STRIDE = 16
N_CMP = (SEQ - CMP_BLOCK) // CMP_STRIDE + 1
SEL_BLOCK = 64
SEL_TOPK = 8
N_SEL = SEQ // SEL_BLOCK
WINDOW = 512
Q_BLOCK = 256
FORCE_BONUS = 1e4
NEG_INF = -1e30
TAKEN = -3e38
LOG2_E = 1.4426950408889634
D_FF = 2816
N_EXPERTS = 8
D_FF_EXPERT = 3584
EPS = 1e-6

LANES = 128
SUBLANES = 8
VMEM_LIMIT = 48 * 1024 * 1024

F32 = jnp.float32
BF16 = jnp.bfloat16

_OFF_VAL = 0
_OFF_GATE = 512
_OFF_Q = 1024
_OFF_KCVC = 1536
_OFF_KS = 1792
_OFF_KW = 1920
IN_COLS = 2048
T_ROWS = 5 * LANES

_BASE_KV = 2 * CONV_WIDTH + ATTN_WIDTH


def _in_proj_weights(w):
    piece = lambda i: w[:, _BASE_KV + i * KV_WIDTH:_BASE_KV + (i + 1) * KV_WIDTH]
    w_cols = jnp.concatenate([w[:, :_BASE_KV + 2 * KV_WIDTH], piece(2), piece(4)], axis=1)
    assert w_cols.shape[1] == IN_COLS
    rows = []
    for i in (3, 5):
        vt = piece(i).T
        for head in range(N_KV_HEADS):
            rows += [vt[head * HEAD_DIM:(head + 1) * HEAD_DIM], jnp.zeros((HEAD_DIM, D_MODEL), w.dtype)]
    n_gate = N_HEADS * N_BRANCH
    glog = _BASE_KV + 6 * KV_WIDTH
    rows += [w[:, glog:glog + n_gate].T, jnp.zeros((LANES - n_gate, D_MODEL), w.dtype)]
    w_rows = jnp.concatenate(rows, axis=0)
    assert w_rows.shape[0] == T_ROWS
    return w_cols.astype(BF16), w_rows.astype(BF16)


def _sigmoid(v):
    return 1.0 / (1.0 + jnp.exp(-v))


def _dot(a, b):
    return jnp.dot(a, b, preferred_element_type=F32)


def _dot_nt(a, b):
    return lax.dot_general(a, b, (((1,), (1,)), ((), ())), preferred_element_type=F32)


def _split_dot(v, m):
    hi = v.astype(BF16)
    lo = (v - hi.astype(F32)).astype(BF16)
    return _dot(hi, m) + _dot(lo, m)


IN_TM = 512


def _in_proj_kernel(x_ref, g_ref, w_ref, wt_ref, qg_ref, kgs_ref, kgw_ref, bd_ref,
                    u_ref, q_ref, kcvc_ref, ks_ref, kw_ref, vst_ref, vwt_ref, gatet_ref):
    x = x_ref[...]
    ms = jnp.mean(x * x, axis=-1, keepdims=True)
    h = (x * lax.rsqrt(ms + EPS) * g_ref[...]).astype(BF16)

    def proj(lo, width):
        return _dot(h, w_ref[:, lo:lo + width])

    u_ref[...] = proj(_OFF_VAL, CONV_WIDTH) * _sigmoid(proj(_OFF_GATE, CONV_WIDTH))

    bd = bd_ref[...]
    lane = lax.broadcasted_iota(jnp.int32, (IN_TM, LANES), 1)
    lower = lane < HEAD_DIM

    def head_slots(pair, upper, out_ref, slot):
        out_ref[:, slot * LANES:(slot + 1) * LANES] = jnp.where(lower, pair, upper).astype(BF16)
        swapped = pltpu.roll(pair, HEAD_DIM, axis=1)
        out_ref[:, (slot + 1) * LANES:(slot + 2) * LANES] = jnp.where(lower, swapped, upper).astype(BF16)

    q = proj(_OFF_Q, ATTN_WIDTH)
    for c in range(ATTN_WIDTH // LANES):
        qc = q[:, c * LANES:(c + 1) * LANES]
        msq = _split_dot(qc * qc, bd)
        head_slots(qc * lax.rsqrt(msq + EPS) * qg_ref[...], 0.0, q_ref, 2 * c)

    kcvc_ref[...] = proj(_OFF_KCVC, 2 * KV_WIDTH)

    def normed_k(off, kg_ref):
        k = proj(off, KV_WIDTH)
        return k * lax.rsqrt(_split_dot(k * k, bd) + EPS) * kg_ref[...]

    tok = (pl.program_id(0) % (SEQ // IN_TM)) * IN_TM + lax.broadcasted_iota(jnp.int32, (IN_TM, 1), 0)
    block_hot = jnp.where(lane - HEAD_DIM == tok // SEL_BLOCK, 1.0, 0.0)
    head_slots(normed_k(_OFF_KS, kgs_ref), block_hot, ks_ref, 0)
    head_slots(normed_k(_OFF_KW, kgw_ref), 0.0, kw_ref, 0)

    zt = _dot_nt(wt_ref[...], h)
    ones_rows = lax.broadcasted_iota(jnp.int32, (LANES, IN_TM), 0) >= HEAD_DIM
    for j, out_ref in enumerate((vst_ref, vst_ref, vwt_ref, vwt_ref)):
        blk = zt[j * LANES:(j + 1) * LANES]
        out_ref[0, j % N_KV_HEADS] = jnp.where(ones_rows, 1.0, blk).astype(BF16)
    gatet_ref[0] = _sigmoid(zt[4 * LANES:5 * LANES])


def _in_proj(x2, g, w, wt, qg, kgs, kgw, bd):
    tm = IN_TM
    per_seq = SEQ // tm
    row = lambda width: pl.BlockSpec((tm, width), lambda i: (i, 0))
    full = lambda a: pl.BlockSpec(a.shape, lambda i: (0,) * a.ndim)
    vt_spec = pl.BlockSpec((1, N_KV_HEADS, LANES, tm), lambda i: (i // per_seq, 0, 0, i % per_seq))
    vt_shape = jax.ShapeDtypeStruct((BATCH, N_KV_HEADS, LANES, SEQ), BF16)
    out_shape = (
        jax.ShapeDtypeStruct((TOKENS, CONV_WIDTH), F32),
        jax.ShapeDtypeStruct((TOKENS, N_HEADS * LANES), BF16),
        jax.ShapeDtypeStruct((TOKENS, 2 * KV_WIDTH), F32),
        jax.ShapeDtypeStruct((TOKENS, 2 * LANES), BF16),
        jax.ShapeDtypeStruct((TOKENS, 2 * LANES), BF16),
        vt_shape,
        vt_shape,
        jax.ShapeDtypeStruct((BATCH, LANES, SEQ), F32),
    )
    out_specs = (row(CONV_WIDTH), row(N_HEADS * LANES), row(2 * KV_WIDTH), row(2 * LANES), row(2 * LANES),
                 vt_spec, vt_spec,
                 pl.BlockSpec((1, LANES, tm), lambda i: (i // per_seq, 0, i % per_seq)))
    return pl.pallas_call(
        _in_proj_kernel,
        grid=(TOKENS // tm,),
        in_specs=[row(D_MODEL), full(g), full(w), full(wt), full(qg), full(kgs), full(kgw), full(bd)],
        out_specs=out_specs,
        out_shape=out_shape,
        compiler_params=pltpu.CompilerParams(
            dimension_semantics=("arbitrary",), vmem_limit_bytes=VMEM_LIMIT),
        name="in_proj",
    )(x2, g, w, wt, qg, kgs, kgw, bd)


CONV_TR = 512
CONV_HALO = 32
CONV_RS = 64
CONV_PARTIALS = 4


def _conv_kernel(ucur_ref, uprev_ref, w_ref, b_ref, lg_ref, lb_ref, bd_ref, o_ref, win_ref, y_ref):
    i = pl.program_id(1)
    first = CONV_HALO - (CONV_KERNEL - 1)
    shifted_rows = CONV_HALO + CONV_TR - SUBLANES
    for c in range(CONV_WIDTH // LANES):
        lanes = slice(c * LANES, (c + 1) * LANES)
        win_ref[0, c, 0:CONV_HALO, :] = jnp.where(i > 0, uprev_ref[0, :, lanes], 0.0)
        win_ref[0, c, CONV_HALO:CONV_HALO + CONV_TR, :] = ucur_ref[0, :, lanes]
        for s in range(1, SUBLANES):
            win_ref[s, c, 0:shifted_rows, :] = win_ref[0, c, s:s + shifted_rows, :]

        def body(r, carry):
            base = pl.multiple_of(r * CONV_RS, CONV_RS)
            accs = [None] * CONV_PARTIALS
            for k in range(CONV_KERNEL):
                off = first + k
                rows = pl.ds(base + (off // SUBLANES) * SUBLANES, CONV_RS)
                term = win_ref[off % SUBLANES, c, rows, :] * w_ref[k:k + 1, lanes]
                j = k % CONV_PARTIALS
                accs[j] = term if accs[j] is None else accs[j] + term
            y_ref[c, pl.ds(base, CONV_RS), :] = (accs[0] + accs[1]) + (accs[2] + accs[3]) + b_ref[:, lanes]
            return carry

        lax.fori_loop(0, CONV_TR // CONV_RS, body, 0)

    bd = bd_ref[...]
    for c in range(CONV_WIDTH // LANES):
        lanes = slice(c * LANES, (c + 1) * LANES)
        y = y_ref[c]
        d = y - _split_dot(y, bd)
        var = _split_dot(d * d, bd)
        yn = d * lax.rsqrt(var + EPS) * lg_ref[:, lanes] + lb_ref[:, lanes]
        o_ref[0, :, lanes] = (yn * _sigmoid(yn)).astype(BF16)


def _conv_module(u3, w, b, lg, lb, bd):
    halo_blocks = CONV_TR // CONV_HALO
    full = lambda a: pl.BlockSpec(a.shape, lambda bi, i: (0,) * a.ndim)
    return pl.pallas_call(
        _conv_kernel,
        grid=(BATCH, SEQ // CONV_TR),
        in_specs=[
            pl.BlockSpec((1, CONV_TR, CONV_WIDTH), lambda bi, i: (bi, i, 0)),
            pl.BlockSpec((1, CONV_HALO, CONV_WIDTH),
                         lambda bi, i: (bi, jnp.maximum(i * halo_blocks - 1, 0), 0)),
            full(w), full(b), full(lg), full(lb), full(bd),
        ],
        out_specs=pl.BlockSpec((1, CONV_TR, CONV_WIDTH), lambda bi, i: (bi, i, 0)),
        out_shape=jax.ShapeDtypeStruct((BATCH, SEQ, CONV_WIDTH), BF16),
        scratch_shapes=[pltpu.VMEM((SUBLANES, CONV_WIDTH // LANES, CONV_HALO + CONV_TR, LANES), F32),
                        pltpu.VMEM((CONV_WIDTH // LANES, CONV_TR, LANES), F32)],
        compiler_params=pltpu.CompilerParams(dimension_semantics=("arbitrary", "arbitrary")),
        name="conv_module",
    )(u3, u3, w, b, lg, lb, bd)


CMP_ROWS = SEQ // CMP_STRIDE
CMP_CHUNK = CMP_STRIDE * HEAD_DIM


def _compress_kernel(x_ref, pos_ref, w1_ref, b1_ref, w2_ref, b2_ref, w2t_ref, b2t_ref, kg_ref, o_ref):
    kind = pl.program_id(0)
    xa = x_ref[0, 0, 0]
    xb = pltpu.roll(xa, CMP_ROWS - 1, axis=0)
    a = (xa + pos_ref[0, :, 0:CMP_CHUNK]).astype(BF16)
    b = (xb + pos_ref[0, :, CMP_CHUNK:2 * CMP_CHUNK]).astype(BF16)
    hid = _dot(a, w1_ref[0, 0:CMP_CHUNK, :]) + _dot(b, w1_ref[0, CMP_CHUNK:2 * CMP_CHUNK, :]) + b1_ref[0]
    hid = (hid * _sigmoid(hid)).astype(BF16)
    out = _dot(hid, w2_ref[0]) + b2_ref[0]
    ms = jnp.sum(out * out, axis=-1, keepdims=True) * (1.0 / HEAD_DIM)
    normed = out * lax.rsqrt(ms + EPS) * kg_ref[...]
    out_t = _dot_nt(w2t_ref[0], hid) + b2t_ref[0]
    o_ref[0, 0, 0] = jnp.where(kind == 0, normed, out_t).astype(BF16)


def _compress(xr, pos, w1, b1, w2, b2, kg):
    w2t = jnp.swapaxes(w2, 1, 2)
    b2t = jnp.swapaxes(b2, 1, 2)
    per_kind = lambda a: pl.BlockSpec((1,) + a.shape[1:], lambda k, bi, h: (k,) + (0,) * (a.ndim - 1))
    return pl.pallas_call(
        _compress_kernel,
        grid=(2, BATCH, N_KV_HEADS),
        in_specs=[
            pl.BlockSpec((1, 1, 1, CMP_ROWS, CMP_CHUNK), lambda k, bi, h: (k, bi, h, 0, 0)),
            per_kind(pos), per_kind(w1), per_kind(b1), per_kind(w2), per_kind(b2),
            per_kind(w2t), per_kind(b2t),
            pl.BlockSpec(kg.shape, lambda k, bi, h: (0, 0)),
        ],
        out_specs=pl.BlockSpec((1, 1, 1, CMP_ROWS, LANES), lambda k, bi, h: (k, bi, h, 0, 0)),
        out_shape=jax.ShapeDtypeStruct((2, BATCH, N_KV_HEADS, CMP_ROWS, LANES), BF16),
        compiler_params=pltpu.CompilerParams(
            dimension_semantics=("arbitrary", "arbitrary", "arbitrary")),
        name="compress",
    )(xr, pos, w1, b1, w2, b2, w2t, b2t, kg)


QCOLS = GQA * Q_BLOCK
SEL_CHUNK = 512
WIN_CHUNK = 128
WIN_CHUNKS = (WINDOW + Q_BLOCK) // WIN_CHUNK
MASK_BIAS = -1e30


def _window_biases():
    k = np.arange(WIN_CHUNK)[:, None]
    t = (np.arange(QCOLS) % Q_BLOCK)[None, :]
    tables, index = [], {}
    for i in range(WIN_CHUNKS):
        key = i * WIN_CHUNK - WINDOW + k
        bias = np.where((key <= t) & (key > t - WINDOW), 0.0, MASK_BIAS)
        if bias.any():
            index[i] = len(tables)
            tables.append(bias)
    return np.stack(tables).astype(np.float32), index


_WIN_BIAS, _WIN_BIAS_INDEX = _window_biases()


def _attn_kernel(q_ref, gatet_ref, kc_ref, vct_ref, ks_ref, vst_ref, kw_ref, vwt_ref,
                 ovlt_ref, wbias_ref, o_ref, ssel_sc, swin_sc, acc_sc, out_sc):
    n = pl.program_id(1)
    start = n * Q_BLOCK
    col = lax.broadcasted_iota(jnp.int32, (1, QCOLS), 1)
    t_cols = start + (col & (Q_BLOCK - 1))
    tq = start + lax.broadcasted_iota(jnp.int32, (1, Q_BLOCK), 1)
    blk_row = lax.broadcasted_iota(jnp.int32, (N_SEL, 1), 0)
    blk_row_f = blk_row.astype(F32)
    cmp_end = lax.broadcasted_iota(jnp.int32, (CMP_ROWS, 1), 0) * CMP_STRIDE + (CMP_BLOCK - 1)

    kv_heads = range(N_KV_HEADS)
    kv_lanes = [slice(kvh * LANES, (kvh + 1) * LANES) for kvh in kv_heads]

    def gate_row(kvh, g, branch):
        r = (kvh * GQA + g) * N_BRANCH + branch
        return gatet_ref[0, r:r + 1, :]

    def head_rows(kvh, g):
        h = kvh * GQA + g
        return slice(h * HEAD_DIM, (h + 1) * HEAD_DIM)

    qs = []
    for kvh in kv_heads:
        h0 = kvh * GQA
        qs.append(jnp.concatenate(
            [q_ref[0, :, (h0 + g) * LANES:(h0 + g + 1) * LANES] for g in range(GQA)], axis=0))

    imp = []
    mask_c = cmp_end <= t_cols
    for kvh in kv_heads:
        s_c = jnp.where(mask_c, _dot_nt(kc_ref[0, kvh], qs[kvh]), NEG_INF)
        m_c = jnp.max(s_c, axis=0, keepdims=True)
        p_c = jnp.where(mask_c, jnp.exp2(s_c - m_c), 0.0)
        p_c = p_c / jnp.maximum(jnp.sum(p_c, axis=0, keepdims=True), 1e-30)
        o_cmp = _dot(vct_ref[0, kvh], p_c.astype(BF16))
        for g in range(GQA):
            out_sc[head_rows(kvh, g), :] = gate_row(kvh, g, 0) * o_cmp[0:HEAD_DIM, g * Q_BLOCK:(g + 1) * Q_BLOCK]
        p_sum = p_c[:, 0:Q_BLOCK]
        for g in range(1, GQA):
            p_sum = p_sum + p_c[:, g * Q_BLOCK:(g + 1) * Q_BLOCK]
        p_hi = p_sum.astype(BF16)
        p_lo = (p_sum - p_hi.astype(F32)).astype(BF16)
        imp.append(_dot(ovlt_ref[...], p_hi) + _dot(ovlt_ref[...], p_lo))

    tq2 = jnp.concatenate([tq] * N_KV_HEADS, axis=1)
    cur = tq2 // SEL_BLOCK
    forced = (blk_row == 0) | (blk_row == cur) | (blk_row == cur - 1)
    causal_blk = blk_row * SEL_BLOCK <= tq2
    score = jnp.where(causal_blk, jnp.concatenate(imp, axis=1) + jnp.where(forced, FORCE_BONUS, 0.0), NEG_INF)
    sel = jnp.zeros((N_SEL, N_KV_HEADS * Q_BLOCK), F32)
    for _ in range(SEL_TOPK):
        best = jnp.max(score, axis=0, keepdims=True)
        idx = jnp.min(jnp.where(score == best, blk_row_f, float(LANES)), axis=0, keepdims=True)
        hit = blk_row_f == idx
        sel = jnp.where(hit, 1.0, sel)
        score = jnp.where(hit, TAKEN, score)
    sel_bias = (sel - 1.0) * (-MASK_BIAS)
    qs_sel = []
    for kvh in kv_heads:
        rows = jnp.concatenate([jnp.zeros((HEAD_DIM, Q_BLOCK), F32),
                                sel_bias[:, kvh * Q_BLOCK:(kvh + 1) * Q_BLOCK],
                                jnp.zeros((LANES - HEAD_DIM - N_SEL, Q_BLOCK), F32)], axis=0)
        bias_q = rows.T.astype(BF16)
        qs_sel.append(qs[kvh] + jnp.concatenate([bias_q] * GQA, axis=0))

    for kvh in kv_heads:
        mxw = jnp.full((1, QCOLS), NEG_INF, F32)
        kbs = [n * (Q_BLOCK // WIN_CHUNK) - WINDOW // WIN_CHUNK + i for i in range(WIN_CHUNKS)]
        k_win = [pl.multiple_of(jnp.maximum(kb, 0) * WIN_CHUNK, WIN_CHUNK) for kb in kbs]
        k_band = jnp.concatenate([kw_ref[0, pl.ds(k0, WIN_CHUNK), kv_lanes[kvh]] for k0 in k_win], axis=0)
        s_band = _dot_nt(k_band, qs[kvh])
        for i, kb in enumerate(kbs):
            s = s_band[i * WIN_CHUNK:(i + 1) * WIN_CHUNK]
            if i in _WIN_BIAS_INDEX:
                s = s + wbias_ref[_WIN_BIAS_INDEX[i]]
            if i < WINDOW // WIN_CHUNK:
                s = s + jnp.where(kb >= 0, 0.0, MASK_BIAS)
            swin_sc[kvh, i * WIN_CHUNK:(i + 1) * WIN_CHUNK, :] = s
            mxw = jnp.maximum(mxw, jnp.max(s, axis=0, keepdims=True))
        p = jnp.exp2(swin_sc[kvh] - mxw).astype(BF16)
        v_band = jnp.concatenate([vwt_ref[0, kvh, :, pl.ds(k0, WIN_CHUNK)] for k0 in k_win], axis=1)
        accw = _dot(v_band, p)
        o_win = accw[0:HEAD_DIM] * (1.0 / accw[HEAD_DIM:HEAD_DIM + 1])
        for g in range(GQA):
            out_sc[head_rows(kvh, g), :] += gate_row(kvh, g, 2) * o_win[:, g * Q_BLOCK:(g + 1) * Q_BLOCK]

    def sel_scores(kvh, k0):
        return _dot_nt(ks_ref[0, pl.ds(k0, SEL_CHUNK), kv_lanes[kvh]], qs_sel[kvh])

    def sel_pass1(i, mx):
        k0 = pl.multiple_of(i * SEL_CHUNK, SEL_CHUNK)
        out = []
        for kvh in kv_heads:
            s = sel_scores(kvh, k0)
            ssel_sc[kvh, pl.ds(k0, SEL_CHUNK), :] = s
            out.append(jnp.maximum(mx[kvh], jnp.max(s, axis=0, keepdims=True)))
        return tuple(out)

    last = (start + Q_BLOCK - 1) // SEL_CHUNK
    mx = lax.fori_loop(0, last, sel_pass1, (jnp.full((1, QCOLS), NEG_INF, F32),) * N_KV_HEADS)
    k_last = pl.multiple_of(last * SEL_CHUNK, SEL_CHUNK)
    causal = k_last + lax.broadcasted_iota(jnp.int32, (SEL_CHUNK, 1), 0) <= t_cols
    mx = list(mx)
    for kvh in kv_heads:
        s = jnp.where(causal, sel_scores(kvh, k_last), NEG_INF)
        ssel_sc[kvh, pl.ds(k_last, SEL_CHUNK), :] = s
        mx[kvh] = jnp.maximum(mx[kvh], jnp.max(s, axis=0, keepdims=True))
        acc_sc[kvh] = jnp.zeros((LANES, QCOLS), F32)

    def sel_pass2(i, carry):
        k0 = pl.multiple_of(i * SEL_CHUNK, SEL_CHUNK)
        for kvh in kv_heads:
            p = jnp.exp2(ssel_sc[kvh, pl.ds(k0, SEL_CHUNK), :] - mx[kvh]).astype(BF16)
            acc_sc[kvh] += _dot(vst_ref[0, kvh, :, pl.ds(k0, SEL_CHUNK)], p)
        return carry

    lax.fori_loop(0, last + 1, sel_pass2, 0)
    for kvh in kv_heads:
        acc = acc_sc[kvh]
        o_sel = acc[0:HEAD_DIM] * (1.0 / acc[HEAD_DIM:HEAD_DIM + 1])
        for g in range(GQA):
            out_sc[head_rows(kvh, g), :] += gate_row(kvh, g, 1) * o_sel[:, g * Q_BLOCK:(g + 1) * Q_BLOCK]

    for c in range(N_HEADS // 2):
        o_ref[0, :, c * LANES:(c + 1) * LANES] = out_sc[c * LANES:(c + 1) * LANES, :].T.astype(BF16)


def _attention(q3, gatet, kvc, ks3, vst, kw3, vwt, ovlt):
    wbias = jnp.asarray(_WIN_BIAS)
    qblk = lambda width: pl.BlockSpec((1, Q_BLOCK, width), lambda bi, n: (bi, n, 0))
    seq = pl.BlockSpec((1, SEQ, 2 * LANES), lambda bi, n: (bi, 0, 0))
    seq_t = pl.BlockSpec((1, N_KV_HEADS, LANES, SEQ), lambda bi, n: (bi, 0, 0, 0))
    cmp_spec = lambda kind: pl.BlockSpec((None, 1, N_KV_HEADS, CMP_ROWS, LANES),
                                         lambda bi, n: (kind, bi, 0, 0, 0))
    full = lambda a: pl.BlockSpec(a.shape, lambda bi, n: (0,) * a.ndim)
    return pl.pallas_call(
        _attn_kernel,
        grid=(BATCH, SEQ // Q_BLOCK),
        in_specs=[qblk(N_HEADS * LANES), pl.BlockSpec((1, LANES, Q_BLOCK), lambda bi, n: (bi, 0, n)),
                  cmp_spec(0), cmp_spec(1), seq, seq_t, seq, seq_t, full(ovlt), full(wbias)],
        out_specs=qblk(ATTN_WIDTH),
        out_shape=jax.ShapeDtypeStruct((BATCH, SEQ, ATTN_WIDTH), BF16),
        scratch_shapes=[pltpu.VMEM((N_KV_HEADS, SEQ, QCOLS), F32),
                        pltpu.VMEM((N_KV_HEADS, WIN_CHUNKS * WIN_CHUNK, QCOLS), F32),
                        pltpu.VMEM((N_KV_HEADS, LANES, QCOLS), F32),
                        pltpu.VMEM((ATTN_WIDTH, Q_BLOCK), F32)],
        compiler_params=pltpu.CompilerParams(
            dimension_semantics=("arbitrary", "arbitrary"), vmem_limit_bytes=VMEM_LIMIT),
        name="nsa_attention",
    )(q3, gatet, kvc, kvc, ks3, vst, kw3, vwt, ovlt, wbias)


OUT_TM = 512


def _out_proj_kernel(x_ref, conv_ref, attn_ref, w_ref, g_ref, x1_ref, h_ref):
    x1 = (x_ref[...] + _dot(conv_ref[...], w_ref[0:CONV_WIDTH, :])
          + _dot(attn_ref[...], w_ref[CONV_WIDTH:CONV_WIDTH + ATTN_WIDTH, :]))
    x1_ref[...] = x1
    ms = jnp.mean(x1 * x1, axis=-1, keepdims=True)
    h_ref[...] = (x1 * lax.rsqrt(ms + EPS) * g_ref[...]).astype(BF16)


def _out_proj_router_kernel(x_ref, conv_ref, attn_ref, w_ref, g_ref, r_ref, x1_ref, h_ref, comb_ref):
    x1 = (x_ref[...] + _dot(conv_ref[...], w_ref[0:CONV_WIDTH, :])
          + _dot(attn_ref[...], w_ref[CONV_WIDTH:CONV_WIDTH + ATTN_WIDTH, :]))
    x1_ref[...] = x1
    ms = jnp.mean(x1 * x1, axis=-1, keepdims=True)
    h = x1 * lax.rsqrt(ms + EPS) * g_ref[...]
    h_ref[...] = h.astype(BF16)
    h_hi = h.astype(BF16)
    h_lo = (h - h_hi.astype(F32)).astype(BF16)
    logits = _dot(h_hi, r_ref[0]) + _dot(h_lo, r_ref[0]) + _dot(h_hi, r_ref[1])
    lane = lax.broadcasted_iota(jnp.int32, (1, LANES), 1).astype(F32)
    logits = jnp.where(lane < N_EXPERTS, logits, NEG_INF)
    v1 = jnp.max(logits, axis=-1, keepdims=True)
    i1 = jnp.min(jnp.where(logits == v1, lane, float(LANES)), axis=-1, keepdims=True)
    hit1 = lane == i1
    rest = jnp.where(hit1, TAKEN, logits)
    v2 = jnp.max(rest, axis=-1, keepdims=True)
    i2 = jnp.min(jnp.where(rest == v2, lane, float(LANES)), axis=-1, keepdims=True)
    hit2 = lane == i2
    e2 = jnp.exp(v2 - v1)
    w_top = 1.0 / (1.0 + e2)
    comb_ref[...] = jnp.where(hit1, w_top, 0.0) + jnp.where(hit2, e2 * w_top, 0.0)


def _out_proj(x2, conv2, attn2, w, g, router=None):
    tm = OUT_TM
    row = lambda width: pl.BlockSpec((tm, width), lambda i: (i, 0))
    full = lambda a: pl.BlockSpec(a.shape, lambda i: (0,) * a.ndim)
    in_specs = [row(D_MODEL), row(CONV_WIDTH), row(ATTN_WIDTH), full(w), full(g)]
    out_shape = [jax.ShapeDtypeStruct((TOKENS, D_MODEL), F32), jax.ShapeDtypeStruct((TOKENS, D_MODEL), BF16)]
    out_specs = [row(D_MODEL), row(D_MODEL)]
    args = [x2, conv2, attn2, w, g]
    body = _out_proj_kernel
    if router is not None:
        body = _out_proj_router_kernel
        in_specs.append(full(router))
        args.append(router)
        out_shape.append(jax.ShapeDtypeStruct((TOKENS, LANES), F32))
        out_specs.append(row(LANES))
    return pl.pallas_call(
        body,
        grid=(TOKENS // tm,),
        in_specs=in_specs,
        out_specs=tuple(out_specs),
        out_shape=tuple(out_shape),
        compiler_params=pltpu.CompilerParams(
            dimension_semantics=("arbitrary",), vmem_limit_bytes=VMEM_LIMIT),
        name="out_proj",
    )(*args)


FFN_TM = 512
FFN_TF = 1408


def _ffn_kernel(x1_ref, h_ref, w1_ref, w3_ref, w2_ref, o_ref):
    h = h_ref[...]
    out = x1_ref[...]
    for f in range(D_FF // FFN_TF):
        cols = slice(f * FFN_TF, (f + 1) * FFN_TF)
        a = _dot(h, w1_ref[:, cols])
        act = (a * _sigmoid(a) * _dot(h, w3_ref[:, cols])).astype(BF16)
        out = out + _dot(act, w2_ref[cols, :])
    o_ref[...] = out


def _dense_ffn(x1, h, w1, w3, w2):
    tm = FFN_TM
    resident = pl.BlockSpec(memory_space=pltpu.VMEM)
    return pl.pallas_call(
        _ffn_kernel,
        grid=(TOKENS // tm,),
        in_specs=[pl.BlockSpec((tm, D_MODEL), lambda i: (i, 0)), pl.BlockSpec((tm, D_MODEL), lambda i: (i, 0)),
                  resident, resident, resident],
        out_specs=pl.BlockSpec((tm, D_MODEL), lambda i: (i, 0)),
        out_shape=jax.ShapeDtypeStruct((TOKENS, D_MODEL), F32),
        compiler_params=pltpu.CompilerParams(
            dimension_semantics=("arbitrary",), vmem_limit_bytes=VMEM_LIMIT),
        name="dense_ffn",
    )(x1, h, w1, w3, w2)


MOE_SB = 512
MOE_TM = 512
MOE_GT = 256
MOE_CT = 256
MOE_CALIGN = 16
MOE_CBUF = 6
MOE_TF = 1792
MOE_NB = TOKENS // MOE_SB
MOE_TILES = (2 * TOKENS) // MOE_TM + N_EXPERTS + 1
MOE_GTILES = MOE_TILES * (MOE_TM // MOE_GT)
MOE_CPAIRS = (2 * TOKENS) // MOE_CT + 2 * N_EXPERTS * MOE_NB


def _moe_rank_kernel(comb_ref, tri_ref, rank_ref, cend_ref, carry_ref):
    @pl.when(pl.program_id(0) == 0)
    def _():
        carry_ref[...] = jnp.zeros((1, LANES), F32)

    routed = jnp.where(comb_ref[...] > 0.0, 1.0, 0.0).astype(BF16)
    carry = carry_ref[...]
    rank_ref[...] = carry + _dot(tri_ref[...], routed)
    carry = carry + jnp.sum(routed.astype(F32), axis=0, keepdims=True)
    carry_ref[...] = carry
    cend_ref[0] = jnp.broadcast_to(carry, (SUBLANES, LANES))


def _moe_rank(comb):
    idx = np.arange(MOE_SB)
    tri = jnp.asarray(idx[:, None] > idx[None, :], BF16)
    return pl.pallas_call(
        _moe_rank_kernel,
        grid=(MOE_NB,),
        in_specs=[pl.BlockSpec((MOE_SB, LANES), lambda b: (b, 0)),
                  pl.BlockSpec((MOE_SB, MOE_SB), lambda b: (0, 0))],
        out_specs=(pl.BlockSpec((MOE_SB, LANES), lambda b: (b, 0)),
                   pl.BlockSpec((1, SUBLANES, LANES), lambda b: (b, 0, 0))),
        out_shape=(jax.ShapeDtypeStruct((TOKENS, LANES), F32),
                   jax.ShapeDtypeStruct((MOE_NB, SUBLANES, LANES), F32)),
        scratch_shapes=[pltpu.VMEM((1, LANES), F32)],
        compiler_params=pltpu.CompilerParams(dimension_semantics=("arbitrary",)),
        name="moe_rank",
    )(comb, tri)


def _count_le(ends, v):
    return jnp.sum(ends[None, :] <= v[:, None], axis=1).astype(jnp.int32)


def _moe_schedule(cend):
    i32 = jnp.int32
    counts = cend[-1]
    cstart = jnp.concatenate([jnp.zeros((1, N_EXPERTS), i32), cend[:-1]], axis=0)
    tiles_e = (counts + MOE_TM - 1) // MOE_TM
    tile_end = jnp.cumsum(tiles_e)
    tile_start = tile_end - tiles_e
    n_tiles = tile_end[-1]
    group_start = tile_start * MOE_TM

    d = jnp.arange(MOE_TILES, dtype=i32)
    t_valid = d < n_tiles
    t_exp = jnp.minimum(_count_le(tile_end, d), N_EXPERTS - 1)
    t_exp = jnp.where(t_valid, t_exp, t_exp[jnp.maximum(n_tiles - 1, 0)])

    per_tile = MOE_TM // MOE_GT
    g = jnp.arange(MOE_GTILES, dtype=i32)
    g_exp = t_exp[g // per_tile]
    r0 = (g // per_tile - tile_start[g_exp]) * MOE_TM + (g % per_tile) * MOE_GT
    r1 = jnp.minimum(r0 + MOE_GT, counts[g_exp])
    gt_valid = t_valid[g // per_tile] & (r0 < counts[g_exp])
    b_lo = jnp.sum(cend[:, g_exp] <= r0[None, :], axis=0).astype(i32)
    b_hi = jnp.sum(cstart[:, g_exp] < r1[None, :], axis=0).astype(i32) - 1
    gather = (g_exp, jnp.where(gt_valid, b_lo, 0), jnp.where(gt_valid, b_hi - b_lo + 1, 0))

    lo = (group_start[None, :] + cstart).reshape(-1)
    hi = (group_start[None, :] + cend).reshape(-1)
    w_lo = (lo // MOE_CALIGN) * MOE_CALIGN
    n_w = jnp.where(hi > lo, (hi - w_lo + MOE_CT - 1) // MOE_CT, 0)
    q_end = jnp.cumsum(n_w)
    q_start = q_end - n_w
    p = jnp.arange(MOE_CPAIRS, dtype=i32)
    be = jnp.minimum(_count_le(q_end, p), MOE_NB * N_EXPERTS - 1)
    c_row = jnp.clip(w_lo[be] + (p - q_start[be]) * MOE_CT, 0, (MOE_TILES - 1) * MOE_TM)
    combine = (q_start[::N_EXPERTS], q_end[N_EXPERTS - 1::N_EXPERTS], c_row, be % N_EXPERTS)
    return group_start, t_exp, t_valid.astype(i32), gather, combine


def _moe_gather_kernel(exp_ref, first_ref, count_ref, h_ref, pos_ref, xs_ref):
    g = pl.program_id(0)
    rows = (g * MOE_GT + lax.broadcasted_iota(jnp.int32, (MOE_GT, 1), 0)).astype(F32)
    xs_ref[...] = jnp.zeros((MOE_GT, D_MODEL), BF16)

    def body(j, carry):
        t0 = pl.multiple_of((first_ref[g] + j) * MOE_SB, MOE_SB)
        pos = pos_ref[pl.ds(exp_ref[g], 1), pl.ds(t0, MOE_SB)]
        onehot = jnp.where(rows == pos, 1.0, 0.0).astype(BF16)
        xs_ref[...] += _dot(onehot, h_ref[pl.ds(t0, MOE_SB), :]).astype(BF16)
        return carry

    lax.fori_loop(0, count_ref[g], body, 0)


def _moe_gather(sched, h, pos_t):
    spec = pltpu.PrefetchScalarGridSpec(
        num_scalar_prefetch=3,
        grid=(MOE_GTILES,),
        in_specs=[pl.BlockSpec(memory_space=pltpu.VMEM), pl.BlockSpec(memory_space=pltpu.VMEM)],
        out_specs=pl.BlockSpec((MOE_GT, D_MODEL), lambda g, e, b, c: (g, 0)),
    )
    return pl.pallas_call(
        _moe_gather_kernel,
        grid_spec=spec,
        out_shape=jax.ShapeDtypeStruct((MOE_TILES * MOE_TM, D_MODEL), BF16),
        compiler_params=pltpu.CompilerParams(
            dimension_semantics=("arbitrary",), vmem_limit_bytes=VMEM_LIMIT),
        name="moe_gather",
    )(*sched, h, pos_t)


def _moe_expert_kernel(exp_ref, valid_ref, xs_ref, w1_ref, w3_ref, w2_ref, y_ref, acc_ref):
    d = pl.program_id(0)
    f = pl.program_id(1)

    @pl.when(valid_ref[d] == 1)
    def _():
        x = xs_ref[...]
        a = _dot(x, w1_ref[0])
        act = (a * _sigmoid(a) * _dot(x, w3_ref[0])).astype(BF16)
        part = _dot(act, w2_ref[0])

        @pl.when(f == 0)
        def _():
            acc_ref[...] = part

        @pl.when(f > 0)
        def _():
            acc_ref[...] += part

        @pl.when(f == pl.num_programs(1) - 1)
        def _():
            y_ref[...] = acc_ref[...].astype(BF16)

    @pl.when((valid_ref[d] == 0) & (f == 0))
    def _():
        y_ref[...] = jnp.zeros((MOE_TM, D_MODEL), BF16)


def _moe_experts(t_exp, t_valid, xs, w1, w3, w2):
    tf = MOE_TF
    nf = D_FF_EXPERT // tf
    fidx = lambda d, f, v: f * v[d] + (nf - 1) * (1 - v[d])
    spec = pltpu.PrefetchScalarGridSpec(
        num_scalar_prefetch=2,
        grid=(MOE_TILES, nf),
        in_specs=[pl.BlockSpec((MOE_TM, D_MODEL), lambda d, f, e, v: (d, 0)),
                  pl.BlockSpec((1, D_MODEL, tf), lambda d, f, e, v: (e[d], 0, fidx(d, f, v))),
                  pl.BlockSpec((1, D_MODEL, tf), lambda d, f, e, v: (e[d], 0, fidx(d, f, v))),
                  pl.BlockSpec((1, tf, D_MODEL), lambda d, f, e, v: (e[d], fidx(d, f, v), 0))],
        out_specs=pl.BlockSpec((MOE_TM, D_MODEL), lambda d, f, e, v: (d, 0)),
        scratch_shapes=[pltpu.VMEM((MOE_TM, D_MODEL), F32)],
    )
    return pl.pallas_call(
        _moe_expert_kernel,
        grid_spec=spec,
        out_shape=jax.ShapeDtypeStruct((MOE_TILES * MOE_TM, D_MODEL), BF16),
        compiler_params=pltpu.CompilerParams(
            dimension_semantics=("arbitrary", "arbitrary"), vmem_limit_bytes=VMEM_LIMIT),
        name="moe_experts",
    )(t_exp, t_valid, xs, w1, w3, w2)


def _moe_combine_kernel(first_ref, end_ref, row_ref, exp_ref, y_hbm, pos_ref, comb_ref, x1_ref, o_ref,
                        ybuf, sem):
    b = pl.program_id(0)
    first = first_ref[b]
    end = end_ref[b]

    def window_copy(p, slot):
        r0 = pl.multiple_of(row_ref[p], MOE_CALIGN)
        return pltpu.make_async_copy(y_hbm.at[pl.ds(r0, MOE_CT), :], ybuf.at[slot], sem.at[slot])

    for k in range(MOE_CBUF - 1):
        @pl.when(first + k < end)
        def _():
            window_copy(first + k, k).start()

    o_ref[...] = x1_ref[...]
    lane = lax.broadcasted_iota(jnp.int32, (1, LANES), 1)
    window = lax.broadcasted_iota(jnp.int32, (1, MOE_CT), 1)

    def body(p, carry):
        slot = (p - first) % MOE_CBUF
        window_copy(p, slot).wait()
        ahead = p + (MOE_CBUF - 1)

        @pl.when(ahead < end)
        def _():
            window_copy(ahead, (ahead - first) % MOE_CBUF).start()

        pick = lane == exp_ref[p]
        pos = jnp.sum(jnp.where(pick, pos_ref[...], 0.0), axis=-1, keepdims=True)
        wgt = jnp.sum(jnp.where(pick, comb_ref[...], 0.0), axis=-1, keepdims=True)
        cols = (row_ref[p] + window).astype(F32)
        onehot = jnp.where(pos == cols, 1.0, 0.0).astype(BF16)
        o_ref[...] += wgt * _dot(onehot, ybuf[slot])
        return carry

    lax.fori_loop(first, end, body, 0)


def _moe_combine(sched, y, pos, comb, x1):
    tok = lambda width: pl.BlockSpec((MOE_SB, width), lambda b, *_: (b, 0))
    spec = pltpu.PrefetchScalarGridSpec(
        num_scalar_prefetch=4,
        grid=(MOE_NB,),
        in_specs=[pl.BlockSpec(memory_space=pl.ANY), tok(LANES), tok(LANES), tok(D_MODEL)],
        out_specs=tok(D_MODEL),
        scratch_shapes=[pltpu.VMEM((MOE_CBUF, MOE_CT, D_MODEL), BF16), pltpu.SemaphoreType.DMA((MOE_CBUF,))],
    )
    return pl.pallas_call(
        _moe_combine_kernel,
        grid_spec=spec,
        out_shape=jax.ShapeDtypeStruct((TOKENS, D_MODEL), F32),
        compiler_params=pltpu.CompilerParams(
            dimension_semantics=("arbitrary",), vmem_limit_bytes=VMEM_LIMIT),
        name="moe_combine",
    )(*sched, y, pos, comb, x1)


def _moe_ffn(x1, h, comb, w1, w3, w2):
    rank, cend = _moe_rank(comb)
    cend = jnp.round(cend[:, 0, :N_EXPERTS]).astype(jnp.int32)
    group_start, t_exp, t_valid, gather, combine = _moe_schedule(cend)
    start = jnp.zeros((LANES,), F32).at[:N_EXPERTS].set(group_start.astype(F32))
    pos = jnp.where(comb > 0.0, start[None, :] + rank, -1.0)
    xs = _moe_gather(gather, h, pos[:, :N_EXPERTS].T)
    y = _moe_experts(t_exp, t_valid, xs, w1, w3, w2)
    return _moe_combine(combine, y, pos, comb, x1)


def _group_mean_matrix():
    idx = np.arange(LANES)
    return jnp.asarray((idx[:, None] // CONV_GROUP == idx[None, :] // CONV_GROUP) / CONV_GROUP, BF16)


def _overlap_matrix_t():
    c_start = np.arange(CMP_ROWS) * CMP_STRIDE
    j = np.arange(N_SEL)
    ovl = ((c_start[None, :] < (j[:, None] + 1) * SEL_BLOCK)
           & (c_start[None, :] + CMP_BLOCK > j[:, None] * SEL_BLOCK)
           & (np.arange(CMP_ROWS)[None, :] < N_CMP))
    return jnp.asarray(ovl, BF16)


def _dup(v):
    return jnp.concatenate([v, v], axis=-1)


def _zero_pad(v):
    return jnp.concatenate([v, jnp.zeros_like(v)], axis=-1)


def kernel(x, attn_norm_g, w_in, conv_dw_w, conv_dw_b, conv_ln_g, conv_ln_b, q_norm_g, k_norm_g,
           cmp_pos_k, cmp_pos_v, cmp_k_w1, cmp_k_b1, cmp_k_w2, cmp_k_b2,
           cmp_v_w1, cmp_v_b1, cmp_v_w2, cmp_v_b2, w_out, ffn_norm_g,
           dense_w1, dense_w3, dense_w2, router_w, moe_w1, moe_w3, moe_w2):
    assert x.shape == (BATCH, SEQ, D_MODEL) and x.dtype == F32
    bd = _group_mean_matrix()
    ovlt = _overlap_matrix_t()

    x2 = x.reshape(TOKENS, D_MODEL)
    for layer in range(DEPTH):
        w_cols, w_rows = _in_proj_weights(w_in[layer])
        qg = _dup(q_norm_g[layer])[None, :] * (LOG2_E * HEAD_DIM ** -0.5)
        kgs = _dup(k_norm_g[layer, 1])[None, :]
        kgw = _dup(k_norm_g[layer, 2])[None, :]
        u, q, kcvc, ks, kw, vst, vwt, gatet = _in_proj(
            x2, attn_norm_g[layer][None, :], w_cols, w_rows, qg, kgs, kgw, bd)

        conv = _conv_module(u.reshape(BATCH, SEQ, CONV_WIDTH), conv_dw_w[layer], conv_dw_b[layer][None, :],
                            conv_ln_g[layer][None, :], conv_ln_b[layer][None, :], bd)

        xr = kcvc.reshape(BATCH, SEQ, 2, N_KV_HEADS, HEAD_DIM).transpose(2, 0, 3, 1, 4)
        xr = xr.reshape(2, BATCH, N_KV_HEADS, CMP_ROWS, CMP_CHUNK)
        pos = jnp.stack([cmp_pos_k[layer], cmp_pos_v[layer]]).reshape(2, 1, CMP_BLOCK * HEAD_DIM)
        w1 = jnp.stack([cmp_k_w1[layer], cmp_v_w1[layer]]).astype(BF16)
        b1 = jnp.stack([cmp_k_b1[layer], cmp_v_b1[layer]])[:, None, :]
        w2 = _zero_pad(jnp.stack([cmp_k_w2[layer], cmp_v_w2[layer]])).astype(BF16)
        b2 = _zero_pad(jnp.stack([cmp_k_b2[layer], cmp_v_b2[layer]]))[:, None, :]
        kvc = _compress(xr, pos, w1, b1, w2, b2, _zero_pad(k_norm_g[layer, 0])[None, :])

        seq3 = lambda a: a.reshape(BATCH, SEQ, a.shape[-1])
        attn = _attention(seq3(q), gatet, kvc, seq3(ks), vst, seq3(kw), vwt, ovlt)

        w_o = w_out[layer].astype(BF16)
        conv2 = conv.reshape(TOKENS, CONV_WIDTH)
        attn2 = attn.reshape(TOKENS, ATTN_WIDTH)
        g_ffn = ffn_norm_g[layer][None, :]
        i = layer // 2
        if layer % 2 == 0:
            x1, h = _out_proj(x2, conv2, attn2, w_o, g_ffn)
            x2 = _dense_ffn(x1, h, dense_w1[i].astype(BF16), dense_w3[i].astype(BF16),
                            dense_w2[i].astype(BF16))
        else:
            r = jnp.pad(router_w[i], ((0, 0), (0, LANES - N_EXPERTS)))
            r_hi = r.astype(BF16)
            r_lo = (r - r_hi.astype(F32)).astype(BF16)
            x1, h, comb = _out_proj(x2, conv2, attn2, w_o, g_ffn, jnp.stack([r_hi, r_lo]))
            x2 = _moe_ffn(x1, h, comb, moe_w1[i].astype(BF16), moe_w3[i].astype(BF16),
                          moe_w2[i].astype(BF16))
    return x2.reshape(BATCH, SEQ, D_MODEL)
```

```python
import numpy as np
import jax
import jax.numpy as jnp
from jax import lax
from jax.experimental import pallas as pl
from jax.experimental.pallas import tpu as pltpu

D_MODEL = 1024
BATCH = 8
SEQ = 2048
DEPTH = 2
TOKENS = BATCH * SEQ

CONV_WIDTH = 512
CONV_GROUP = 64
CONV_KERNEL = 31
N_HEADS = 8
HEAD_DIM = 64
N_KV_HEADS = 2
GQA = N_HEADS // N_KV_HEADS
ATTN_WIDTH = N_HEADS * HEAD_DIM
KV_WIDTH = N_KV_HEADS * HEAD_DIM
N_BRANCH = 3
CMP_BLOCK = 32
CMP_STRIDE = 16
N_CMP = (SEQ - CMP_BLOCK) // CMP_STRIDE + 1
SEL_BLOCK = 64
SEL_TOPK = 8
N_SEL = SEQ // SEL_BLOCK
WINDOW = 512
Q_BLOCK = 256
FORCE_BONUS = 1e4
NEG_INF = -1e30
TAKEN = -3e38
LOG2_E = 1.4426950408889634
D_FF = 2816
N_EXPERTS = 8
D_FF_EXPERT = 3584
EPS = 1e-6

LANES = 128
SUBLANES = 8
VMEM_LIMIT = 48 * 1024 * 1024

F32 = jnp.float32
BF16 = jnp.bfloat16

_OFF_VAL = 0
_OFF_GATE = 512
_OFF_Q = 1024
_OFF_KCVC = 1536
_OFF_KS = 1792
_OFF_KW = 1920
IN_COLS = 2048
T_ROWS = 5 * LANES

_BASE_KV = 2 * CONV_WIDTH + ATTN_WIDTH


def _in_proj_weights(w):
    piece = lambda i: w[:, _BASE_KV + i * KV_WIDTH:_BASE_KV + (i + 1) * KV_WIDTH]
    w_cols = jnp.concatenate([w[:, :_BASE_KV + 2 * KV_WIDTH], piece(2), piece(4)], axis=1)
    assert w_cols.shape[1] == IN_COLS
    rows = []
    for i in (3, 5):
        vt = piece(i).T
        for head in range(N_KV_HEADS):
            rows += [vt[head * HEAD_DIM:(head + 1) * HEAD_DIM], jnp.zeros((HEAD_DIM, D_MODEL), w.dtype)]
    n_gate = N_HEADS * N_BRANCH
    glog = _BASE_KV + 6 * KV_WIDTH
    rows += [w[:, glog:glog + n_gate].T, jnp.zeros((LANES - n_gate, D_MODEL), w.dtype)]
    w_rows = jnp.concatenate(rows, axis=0)
    assert w_rows.shape[0] == T_ROWS
    return w_cols.astype(BF16), w_rows.astype(BF16)


def _sigmoid(v):
    return 1.0 / (1.0 + jnp.exp(-v))


def _dot(a, b):
    return jnp.dot(a, b, preferred_element_type=F32)


def _dot_nt(a, b):
    return lax.dot_general(a, b, (((1,), (1,)), ((), ())), preferred_element_type=F32)


def _split_dot(v, m):
    hi = v.astype(BF16)
    lo = (v - hi.astype(F32)).astype(BF16)
    return _dot(hi, m) + _dot(lo, m)


IN_TM = 512


def _in_proj_kernel(x_ref, g_ref, w_ref, wt_ref, qg_ref, kgs_ref, kgw_ref, bd_ref,
                    u_ref, q_ref, kcvc_ref, ks_ref, kw_ref, vst_ref, vwt_ref, gatet_ref):
    x = x_ref[...]
    ms = jnp.mean(x * x, axis=-1, keepdims=True)
    h = (x * lax.rsqrt(ms + EPS) * g_ref[...]).astype(BF16)

    def proj(lo, width):
        return _dot(h, w_ref[:, lo:lo + width])

    u_ref[...] = proj(_OFF_VAL, CONV_WIDTH) * _sigmoid(proj(_OFF_GATE, CONV_WIDTH))

    bd = bd_ref[...]
    lane = lax.broadcasted_iota(jnp.int32, (IN_TM, LANES), 1)
    lower = lane < HEAD_DIM

    def head_slots(pair, upper, out_ref, slot):
        out_ref[:, slot * LANES:(slot + 1) * LANES] = jnp.where(lower, pair, upper).astype(BF16)
        swapped = pltpu.roll(pair, HEAD_DIM, axis=1)
        out_ref[:, (slot + 1) * LANES:(slot + 2) * LANES] = jnp.where(lower, swapped, upper).astype(BF16)

    q = proj(_OFF_Q, ATTN_WIDTH)
    for c in range(ATTN_WIDTH // LANES):
        qc = q[:, c * LANES:(c + 1) * LANES]
        msq = _split_dot(qc * qc, bd)
        head_slots(qc * lax.rsqrt(msq + EPS) * qg_ref[...], 0.0, q_ref, 2 * c)

    kcvc_ref[...] = proj(_OFF_KCVC, 2 * KV_WIDTH)

    def normed_k(off, kg_ref):
        k = proj(off, KV_WIDTH)
        return k * lax.rsqrt(_split_dot(k * k, bd) + EPS) * kg_ref[...]

    tok = (pl.program_id(0) % (SEQ // IN_TM)) * IN_TM + lax.broadcasted_iota(jnp.int32, (IN_TM, 1), 0)
    block_hot = jnp.where(lane - HEAD_DIM == tok // SEL_BLOCK, 1.0, 0.0)
    head_slots(normed_k(_OFF_KS, kgs_ref), block_hot, ks_ref, 0)
    head_slots(normed_k(_OFF_KW, kgw_ref), 0.0, kw_ref, 0)

    zt = _dot_nt(wt_ref[...], h)
    ones_rows = lax.broadcasted_iota(jnp.int32, (LANES, IN_TM), 0) >= HEAD_DIM
    for j, out_ref in enumerate((vst_ref, vst_ref, vwt_ref, vwt_ref)):
        blk = zt[j * LANES:(j + 1) * LANES]
        out_ref[0, j % N_KV_HEADS] = jnp.where(ones_rows, 1.0, blk).astype(BF16)
    gatet_ref[0] = _sigmoid(zt[4 * LANES:5 * LANES])


def _in_proj(x2, g, w, wt, qg, kgs, kgw, bd):
    tm = IN_TM
    per_seq = SEQ // tm
    row = lambda width: pl.BlockSpec((tm, width), lambda i: (i, 0))
    full = lambda a: pl.BlockSpec(a.shape, lambda i: (0,) * a.ndim)
    vt_spec = pl.BlockSpec((1, N_KV_HEADS, LANES, tm), lambda i: (i // per_seq, 0, 0, i % per_seq))
    vt_shape = jax.ShapeDtypeStruct((BATCH, N_KV_HEADS, LANES, SEQ), BF16)
    out_shape = (
        jax.ShapeDtypeStruct((TOKENS, CONV_WIDTH), F32),
        jax.ShapeDtypeStruct((TOKENS, N_HEADS * LANES), BF16),
        jax.ShapeDtypeStruct((TOKENS, 2 * KV_WIDTH), F32),
        jax.ShapeDtypeStruct((TOKENS, 2 * LANES), BF16),
        jax.ShapeDtypeStruct((TOKENS, 2 * LANES), BF16),
        vt_shape,
        vt_shape,
        jax.ShapeDtypeStruct((BATCH, LANES, SEQ), F32),
    )
    out_specs = (row(CONV_WIDTH), row(N_HEADS * LANES), row(2 * KV_WIDTH), row(2 * LANES), row(2 * LANES),
                 vt_spec, vt_spec,
                 pl.BlockSpec((1, LANES, tm), lambda i: (i // per_seq, 0, i % per_seq)))
    return pl.pallas_call(
        _in_proj_kernel,
        grid=(TOKENS // tm,),
        in_specs=[row(D_MODEL), full(g), full(w), full(wt), full(qg), full(kgs), full(kgw), full(bd)],
        out_specs=out_specs,
        out_shape=out_shape,
        compiler_params=pltpu.CompilerParams(
            dimension_semantics=("arbitrary",), vmem_limit_bytes=VMEM_LIMIT),
        name="in_proj",
    )(x2, g, w, wt, qg, kgs, kgw, bd)


CONV_TR = 512
CONV_HALO = 32
CONV_RS = 64
CONV_PARTIALS = 4


def _conv_kernel(ucur_ref, uprev_ref, w_ref, b_ref, lg_ref, lb_ref, bd_ref, o_ref, win_ref, y_ref):
    i = pl.program_id(1)
    first = CONV_HALO - (CONV_KERNEL - 1)
    shifted_rows = CONV_HALO + CONV_TR - SUBLANES
    for c in range(CONV_WIDTH // LANES):
        lanes = slice(c * LANES, (c + 1) * LANES)
        win_ref[0, c, 0:CONV_HALO, :] = jnp.where(i > 0, uprev_ref[0, :, lanes], 0.0)
        win_ref[0, c, CONV_HALO:CONV_HALO + CONV_TR, :] = ucur_ref[0, :, lanes]
        for s in range(1, SUBLANES):
            win_ref[s, c, 0:shifted_rows, :] = win_ref[0, c, s:s + shifted_rows, :]

        def body(r, carry):
            base = pl.multiple_of(r * CONV_RS, CONV_RS)
            accs = [None] * CONV_PARTIALS
            for k in range(CONV_KERNEL):
                off = first + k
                rows = pl.ds(base + (off // SUBLANES) * SUBLANES, CONV_RS)
                term = win_ref[off % SUBLANES, c, rows, :] * w_ref[k:k + 1, lanes]
                j = k % CONV_PARTIALS
                accs[j] = term if accs[j] is None else accs[j] + term
            y_ref[c, pl.ds(base, CONV_RS), :] = (accs[0] + accs[1]) + (accs[2] + accs[3]) + b_ref[:, lanes]
            return carry

        lax.fori_loop(0, CONV_TR // CONV_RS, body, 0)

    bd = bd_ref[...]
    for c in range(CONV_WIDTH // LANES):
        lanes = slice(c * LANES, (c + 1) * LANES)
        y = y_ref[c]
        d = y - _split_dot(y, bd)
        var = _split_dot(d * d, bd)
        yn = d * lax.rsqrt(var + EPS) * lg_ref[:, lanes] + lb_ref[:, lanes]
        o_ref[0, :, lanes] = (yn * _sigmoid(yn)).astype(BF16)


def _conv_module(u3, w, b, lg, lb, bd):
    halo_blocks = CONV_TR // CONV_HALO
    full = lambda a: pl.BlockSpec(a.shape, lambda bi, i: (0,) * a.ndim)
    return pl.pallas_call(
        _conv_kernel,
        grid=(BATCH, SEQ // CONV_TR),
        in_specs=[
            pl.BlockSpec((1, CONV_TR, CONV_WIDTH), lambda bi, i: (bi, i, 0)),
            pl.BlockSpec((1, CONV_HALO, CONV_WIDTH),
                         lambda bi, i: (bi, jnp.maximum(i * halo_blocks - 1, 0), 0)),
            full(w), full(b), full(lg), full(lb), full(bd),
        ],
        out_specs=pl.BlockSpec((1, CONV_TR, CONV_WIDTH), lambda bi, i: (bi, i, 0)),
        out_shape=jax.ShapeDtypeStruct((BATCH, SEQ, CONV_WIDTH), BF16),
        scratch_shapes=[pltpu.VMEM((SUBLANES, CONV_WIDTH // LANES, CONV_HALO + CONV_TR, LANES), F32),
                        pltpu.VMEM((CONV_WIDTH // LANES, CONV_TR, LANES), F32)],
        compiler_params=pltpu.CompilerParams(dimension_semantics=("arbitrary", "arbitrary")),
        name="conv_module",
    )(u3, u3, w, b, lg, lb, bd)


CMP_ROWS = SEQ // CMP_STRIDE
CMP_CHUNK = CMP_STRIDE * HEAD_DIM


def _compress_kernel(x_ref, pos_ref, w1_ref, b1_ref, w2_ref, b2_ref, w2t_ref, b2t_ref, kg_ref, o_ref):
    kind = pl.program_id(0)
    xa = x_ref[0, 0, 0]
    xb = pltpu.roll(xa, CMP_ROWS - 1, axis=0)
    a = (xa + pos_ref[0, :, 0:CMP_CHUNK]).astype(BF16)
    b = (xb + pos_ref[0, :, CMP_CHUNK:2 * CMP_CHUNK]).astype(BF16)
    hid = _dot(a, w1_ref[0, 0:CMP_CHUNK, :]) + _dot(b, w1_ref[0, CMP_CHUNK:2 * CMP_CHUNK, :]) + b1_ref[0]
    hid = (hid * _sigmoid(hid)).astype(BF16)
    out = _dot(hid, w2_ref[0]) + b2_ref[0]
    ms = jnp.sum(out * out, axis=-1, keepdims=True) * (1.0 / HEAD_DIM)
    normed = out * lax.rsqrt(ms + EPS) * kg_ref[...]
    out_t = _dot_nt(w2t_ref[0], hid) + b2t_ref[0]
    o_ref[0, 0, 0] = jnp.where(kind == 0, normed, out_t).astype(BF16)


def _compress(xr, pos, w1, b1, w2, b2, kg):
    w2t = jnp.swapaxes(w2, 1, 2)
    b2t = jnp.swapaxes(b2, 1, 2)
    per_kind = lambda a: pl.BlockSpec((1,) + a.shape[1:], lambda k, bi, h: (k,) + (0,) * (a.ndim - 1))
    return pl.pallas_call(
        _compress_kernel,
        grid=(2, BATCH, N_KV_HEADS),
        in_specs=[
            pl.BlockSpec((1, 1, 1, CMP_ROWS, CMP_CHUNK), lambda k, bi, h: (k, bi, h, 0, 0)),
            per_kind(pos), per_kind(w1), per_kind(b1), per_kind(w2), per_kind(b2),
            per_kind(w2t), per_kind(b2t),
            pl.BlockSpec(kg.shape, lambda k, bi, h: (0, 0)),
        ],
        out_specs=pl.BlockSpec((1, 1, 1, CMP_ROWS, LANES), lambda k, bi, h: (k, bi, h, 0, 0)),
        out_shape=jax.ShapeDtypeStruct((2, BATCH, N_KV_HEADS, CMP_ROWS, LANES), BF16),
        compiler_params=pltpu.CompilerParams(
            dimension_semantics=("arbitrary", "arbitrary", "arbitrary")),
        name="compress",
    )(xr, pos, w1, b1, w2, b2, w2t, b2t, kg)


QCOLS = GQA * Q_BLOCK
SEL_CHUNK = 512
WIN_CHUNK = 128
WIN_CHUNKS = (WINDOW + Q_BLOCK) // WIN_CHUNK
MASK_BIAS = -1e30


def _window_biases():
    k = np.arange(WIN_CHUNK)[:, None]
    t = (np.arange(QCOLS) % Q_BLOCK)[None, :]
    tables, index = [], {}
    for i in range(WIN_CHUNKS):
        key = i * WIN_CHUNK - WINDOW + k
        bias = np.where((key <= t) & (key > t - WINDOW), 0.0, MASK_BIAS)
        if bias.any():
            index[i] = len(tables)
            tables.append(bias)
    return np.stack(tables).astype(np.float32), index


_WIN_BIAS, _WIN_BIAS_INDEX = _window_biases()


def _attn_kernel(q_ref, gatet_ref, kc_ref, vct_ref, ks_ref, vst_ref, kw_ref, vwt_ref,
                 ovlt_ref, wbias_ref, o_ref, ssel_sc, swin_sc, acc_sc, out_sc):
    n = pl.program_id(1)
    start = n * Q_BLOCK
    col = lax.broadcasted_iota(jnp.int32, (1, QCOLS), 1)
    t_cols = start + (col & (Q_BLOCK - 1))
    tq = start + lax.broadcasted_iota(jnp.int32, (1, Q_BLOCK), 1)
    blk_row = lax.broadcasted_iota(jnp.int32, (N_SEL, 1), 0)
    blk_row_f = blk_row.astype(F32)
    cmp_end = lax.broadcasted_iota(jnp.int32, (CMP_ROWS, 1), 0) * CMP_STRIDE + (CMP_BLOCK - 1)

    kv_heads = range(N_KV_HEADS)
    kv_lanes = [slice(kvh * LANES, (kvh + 1) * LANES) for kvh in kv_heads]

    def gate_row(kvh, g, branch):
        r = (kvh * GQA + g) * N_BRANCH + branch
        return gatet_ref[0, r:r + 1, :]

    def head_rows(kvh, g):
        h = kvh * GQA + g
        return slice(h * HEAD_DIM, (h + 1) * HEAD_DIM)

    qs = []
    for kvh in kv_heads:
        h0 = kvh * GQA
        qs.append(jnp.concatenate(
            [q_ref[0, :, (h0 + g) * LANES:(h0 + g + 1) * LANES] for g in range(GQA)], axis=0))

    imp = []
    mask_c = cmp_end <= t_cols
    for kvh in kv_heads:
        s_c = jnp.where(mask_c, _dot_nt(kc_ref[0, kvh], qs[kvh]), NEG_INF)
        m_c = jnp.max(s_c, axis=0, keepdims=True)
        p_c = jnp.where(mask_c, jnp.exp2(s_c - m_c), 0.0)
        p_c = p_c / jnp.maximum(jnp.sum(p_c, axis=0, keepdims=True), 1e-30)
        o_cmp = _dot(vct_ref[0, kvh], p_c.astype(BF16))
        for g in range(GQA):
            out_sc[head_rows(kvh, g), :] = gate_row(kvh, g, 0) * o_cmp[0:HEAD_DIM, g * Q_BLOCK:(g + 1) * Q_BLOCK]
        p_sum = p_c[:, 0:Q_BLOCK]
        for g in range(1, GQA):
            p_sum = p_sum + p_c[:, g * Q_BLOCK:(g + 1) * Q_BLOCK]
        p_hi = p_sum.astype(BF16)
        p_lo = (p_sum - p_hi.astype(F32)).astype(BF16)
        imp.append(_dot(ovlt_ref[...], p_hi) + _dot(ovlt_ref[...], p_lo))

    tq2 = jnp.concatenate([tq] * N_KV_HEADS, axis=1)
    cur = tq2 // SEL_BLOCK
    forced = (blk_row == 0) | (blk_row == cur) | (blk_row == cur - 1)
    causal_blk = blk_row * SEL_BLOCK <= tq2
    score = jnp.where(causal_blk, jnp.concatenate(imp, axis=1) + jnp.where(forced, FORCE_BONUS, 0.0), NEG_INF)
    sel = jnp.zeros((N_SEL, N_KV_HEADS * Q_BLOCK), F32)
    for _ in range(SEL_TOPK):
        best = jnp.max(score, axis=0, keepdims=True)
        idx = jnp.min(jnp.where(score == best, blk_row_f, float(LANES)), axis=0, keepdims=True)
        hit = blk_row_f == idx
        sel = jnp.where(hit, 1.0, sel)
        score = jnp.where(hit, TAKEN, score)
    sel_bias = (sel - 1.0) * (-MASK_BIAS)
    qs_sel = []
    for kvh in kv_heads:
        rows = jnp.concatenate([jnp.zeros((HEAD_DIM, Q_BLOCK), F32),
                                sel_bias[:, kvh * Q_BLOCK:(kvh + 1) * Q_BLOCK],
                                jnp.zeros((LANES - HEAD_DIM - N_SEL, Q_BLOCK), F32)], axis=0)
        bias_q = rows.T.astype(BF16)
        qs_sel.append(qs[kvh] + jnp.concatenate([bias_q] * GQA, axis=0))

    for kvh in kv_heads:
        mxw = jnp.full((1, QCOLS), NEG_INF, F32)
        kbs = [n * (Q_BLOCK // WIN_CHUNK) - WINDOW // WIN_CHUNK + i for i in range(WIN_CHUNKS)]
        k_win = [pl.multiple_of(jnp.maximum(kb, 0) * WIN_CHUNK, WIN_CHUNK) for kb in kbs]
        k_band = jnp.concatenate([kw_ref[0, pl.ds(k0, WIN_CHUNK), kv_lanes[kvh]] for k0 in k_win], axis=0)
        s_band = _dot_nt(k_band, qs[kvh])
        for i, kb in enumerate(kbs):
            s = s_band[i * WIN_CHUNK:(i + 1) * WIN_CHUNK]
            if i in _WIN_BIAS_INDEX:
                s = s + wbias_ref[_WIN_BIAS_INDEX[i]]
            if i < WINDOW // WIN_CHUNK:
                s = s + jnp.where(kb >= 0, 0.0, MASK_BIAS)
            swin_sc[kvh, i * WIN_CHUNK:(i + 1) * WIN_CHUNK, :] = s
            mxw = jnp.maximum(mxw, jnp.max(s, axis=0, keepdims=True))
        p = jnp.exp2(swin_sc[kvh] - mxw).astype(BF16)
        v_band = jnp.concatenate([vwt_ref[0, kvh, :, pl.ds(k0, WIN_CHUNK)] for k0 in k_win], axis=1)
        accw = _dot(v_band, p)
        o_win = accw[0:HEAD_DIM] * (1.0 / accw[HEAD_DIM:HEAD_DIM + 1])
        for g in range(GQA):
            out_sc[head_rows(kvh, g), :] += gate_row(kvh, g, 2) * o_win[:, g * Q_BLOCK:(g + 1) * Q_BLOCK]

    def sel_scores(kvh, k0):
        return _dot_nt(ks_ref[0, pl.ds(k0, SEL_CHUNK), kv_lanes[kvh]], qs_sel[kvh])

    def sel_pass1(i, mx):
        k0 = pl.multiple_of(i * SEL_CHUNK, SEL_CHUNK)
        out = []
        for kvh in kv_heads:
            s = sel_scores(kvh, k0)
            ssel_sc[kvh, pl.ds(k0, SEL_CHUNK), :] = s
            out.append(jnp.maximum(mx[kvh], jnp.max(s, axis=0, keepdims=True)))
        return tuple(out)

    last = (start + Q_BLOCK - 1) // SEL_CHUNK
    mx = lax.fori_loop(0, last, sel_pass1, (jnp.full((1, QCOLS), NEG_INF, F32),) * N_KV_HEADS)
    k_last = pl.multiple_of(last * SEL_CHUNK, SEL_CHUNK)
    causal = k_last + lax.broadcasted_iota(jnp.int32, (SEL_CHUNK, 1), 0) <= t_cols
    mx = list(mx)
    for kvh in kv_heads:
        s = jnp.where(causal, sel_scores(kvh, k_last), NEG_INF)
        ssel_sc[kvh, pl.ds(k_last, SEL_CHUNK), :] = s
        mx[kvh] = jnp.maximum(mx[kvh], jnp.max(s, axis=0, keepdims=True))
        acc_sc[kvh] = jnp.zeros((LANES, QCOLS), F32)

    def sel_pass2(i, carry):
        k0 = pl.multiple_of(i * SEL_CHUNK, SEL_CHUNK)
        for kvh in kv_heads:
            p = jnp.exp2(ssel_sc[kvh, pl.ds(k0, SEL_CHUNK), :] - mx[kvh]).astype(BF16)
            acc_sc[kvh] += _dot(vst_ref[0, kvh, :, pl.ds(k0, SEL_CHUNK)], p)
        return carry

    lax.fori_loop(0, last + 1, sel_pass2, 0)
    for kvh in kv_heads:
        acc = acc_sc[kvh]
        o_sel = acc[0:HEAD_DIM] * (1.0 / acc[HEAD_DIM:HEAD_DIM + 1])
        for g in range(GQA):
            out_sc[head_rows(kvh, g), :] += gate_row(kvh, g, 1) * o_sel[:, g * Q_BLOCK:(g + 1) * Q_BLOCK]

    for c in range(N_HEADS // 2):
        o_ref[0, :, c * LANES:(c + 1) * LANES] = out_sc[c * LANES:(c + 1) * LANES, :].T.astype(BF16)


def _attention(q3, gatet, kvc, ks3, vst, kw3, vwt, ovlt):
    wbias = jnp.asarray(_WIN_BIAS)
    qblk = lambda width: pl.BlockSpec((1, Q_BLOCK, width), lambda bi, n: (bi, n, 0))
    seq = pl.BlockSpec((1, SEQ, 2 * LANES), lambda bi, n: (bi, 0, 0))
    seq_t = pl.BlockSpec((1, N_KV_HEADS, LANES, SEQ), lambda bi, n: (bi, 0, 0, 0))
    cmp_spec = lambda kind: pl.BlockSpec((None, 1, N_KV_HEADS, CMP_ROWS, LANES),
                                         lambda bi, n: (kind, bi, 0, 0, 0))
    full = lambda a: pl.BlockSpec(a.shape, lambda bi, n: (0,) * a.ndim)
    return pl.pallas_call(
        _attn_kernel,
        grid=(BATCH, SEQ // Q_BLOCK),
        in_specs=[qblk(N_HEADS * LANES), pl.BlockSpec((1, LANES, Q_BLOCK), lambda bi, n: (bi, 0, n)),
                  cmp_spec(0), cmp_spec(1), seq, seq_t, seq, seq_t, full(ovlt), full(wbias)],
        out_specs=qblk(ATTN_WIDTH),
        out_shape=jax.ShapeDtypeStruct((BATCH, SEQ, ATTN_WIDTH), BF16),
        scratch_shapes=[pltpu.VMEM((N_KV_HEADS, SEQ, QCOLS), F32),
                        pltpu.VMEM((N_KV_HEADS, WIN_CHUNKS * WIN_CHUNK, QCOLS), F32),
                        pltpu.VMEM((N_KV_HEADS, LANES, QCOLS), F32),
                        pltpu.VMEM((ATTN_WIDTH, Q_BLOCK), F32)],
        compiler_params=pltpu.CompilerParams(
            dimension_semantics=("arbitrary", "arbitrary"), vmem_limit_bytes=VMEM_LIMIT),
        name="nsa_attention",
    )(q3, gatet, kvc, kvc, ks3, vst, kw3, vwt, ovlt, wbias)


OUT_TM = 512


def _out_proj_kernel(x_ref, conv_ref, attn_ref, w_ref, g_ref, x1_ref, h_ref):
    x1 = (x_ref[...] + _dot(conv_ref[...], w_ref[0:CONV_WIDTH, :])
          + _dot(attn_ref[...], w_ref[CONV_WIDTH:CONV_WIDTH + ATTN_WIDTH, :]))
    x1_ref[...] = x1
    ms = jnp.mean(x1 * x1, axis=-1, keepdims=True)
    h_ref[...] = (x1 * lax.rsqrt(ms + EPS) * g_ref[...]).astype(BF16)


def _out_proj_router_kernel(x_ref, conv_ref, attn_ref, w_ref, g_ref, r_ref, x1_ref, h_ref, comb_ref):
    x1 = (x_ref[...] + _dot(conv_ref[...], w_ref[0:CONV_WIDTH, :])
          + _dot(attn_ref[...], w_ref[CONV_WIDTH:CONV_WIDTH + ATTN_WIDTH, :]))
    x1_ref[...] = x1
    ms = jnp.mean(x1 * x1, axis=-1, keepdims=True)
    h = x1 * lax.rsqrt(ms + EPS) * g_ref[...]
    h_ref[...] = h.astype(BF16)
    h_hi = h.astype(BF16)
    h_lo = (h - h_hi.astype(F32)).astype(BF16)
    logits = _dot(h_hi, r_ref[0]) + _dot(h_lo, r_ref[0]) + _dot(h_hi, r_ref[1])
    lane = lax.broadcasted_iota(jnp.int32, (1, LANES), 1).astype(F32)
    logits = jnp.where(lane < N_EXPERTS, logits, NEG_INF)
    v1 = jnp.max(logits, axis=-1, keepdims=True)
    i1 = jnp.min(jnp.where(logits == v1, lane, float(LANES)), axis=-1, keepdims=True)
    hit1 = lane == i1
    rest = jnp.where(hit1, TAKEN, logits)
    v2 = jnp.max(rest, axis=-1, keepdims=True)
    i2 = jnp.min(jnp.where(rest == v2, lane, float(LANES)), axis=-1, keepdims=True)
    hit2 = lane == i2
    e2 = jnp.exp(v2 - v1)
    w_top = 1.0 / (1.0 + e2)
    comb_ref[...] = jnp.where(hit1, w_top, 0.0) + jnp.where(hit2, e2 * w_top, 0.0)


def _out_proj(x2, conv2, attn2, w, g, router=None):
    tm = OUT_TM
    row = lambda width: pl.BlockSpec((tm, width), lambda i: (i, 0))
    full = lambda a: pl.BlockSpec(a.shape, lambda i: (0,) * a.ndim)
    in_specs = [row(D_MODEL), row(CONV_WIDTH), row(ATTN_WIDTH), full(w), full(g)]
    out_shape = [jax.ShapeDtypeStruct((TOKENS, D_MODEL), F32), jax.ShapeDtypeStruct((TOKENS, D_MODEL), BF16)]
    out_specs = [row(D_MODEL), row(D_MODEL)]
    args = [x2, conv2, attn2, w, g]
    body = _out_proj_kernel
    if router is not None:
        body = _out_proj_router_kernel
        in_specs.append(full(router))
        args.append(router)
        out_shape.append(jax.ShapeDtypeStruct((TOKENS, LANES), F32))
        out_specs.append(row(LANES))
    return pl.pallas_call(
        body,
        grid=(TOKENS // tm,),
        in_specs=in_specs,
        out_specs=tuple(out_specs),
        out_shape=tuple(out_shape),
        compiler_params=pltpu.CompilerParams(
            dimension_semantics=("arbitrary",), vmem_limit_bytes=VMEM_LIMIT),
        name="out_proj",
    )(*args)


FFN_TM = 512
FFN_TF = 1408


def _ffn_kernel(x1_ref, h_ref, w1_ref, w3_ref, w2_ref, o_ref):
    h = h_ref[...]
    out = x1_ref[...]
    for f in range(D_FF // FFN_TF):
        cols = slice(f * FFN_TF, (f + 1) * FFN_TF)
        a = _dot(h, w1_ref[:, cols])
        act = (a * _sigmoid(a) * _dot(h, w3_ref[:, cols])).astype(BF16)
        out = out + _dot(act, w2_ref[cols, :])
    o_ref[...] = out


def _dense_ffn(x1, h, w1, w3, w2):
    tm = FFN_TM
    resident = pl.BlockSpec(memory_space=pltpu.VMEM)
    return pl.pallas_call(
        _ffn_kernel,
        grid=(TOKENS // tm,),
        in_specs=[pl.BlockSpec((tm, D_MODEL), lambda i: (i, 0)), pl.BlockSpec((tm, D_MODEL), lambda i: (i, 0)),
                  resident, resident, resident],
        out_specs=pl.BlockSpec((tm, D_MODEL), lambda i: (i, 0)),
        out_shape=jax.ShapeDtypeStruct((TOKENS, D_MODEL), F32),
        compiler_params=pltpu.CompilerParams(
            dimension_semantics=("arbitrary",), vmem_limit_bytes=VMEM_LIMIT),
        name="dense_ffn",
    )(x1, h, w1, w3, w2)


MOE_SB = 512
MOE_TM = 512
MOE_GT = 256
MOE_CT = 256
MOE_CALIGN = 16
MOE_CBUF = 6
MOE_TF = 1792
MOE_NB = TOKENS // MOE_SB
MOE_PAIR = 2
MOE_TILES = (2 * TOKENS) // MOE_TM + N_EXPERTS + 2
MOE_GTILES = MOE_TILES * (MOE_TM // MOE_GT)
MOE_CPAIRS = (2 * TOKENS) // MOE_CT + 2 * N_EXPERTS * MOE_NB


def _moe_rank_kernel(comb_ref, tri_ref, rank_ref, cend_ref, carry_ref):
    @pl.when(pl.program_id(0) == 0)
    def _():
        carry_ref[...] = jnp.zeros((1, LANES), F32)

    routed = jnp.where(comb_ref[...] > 0.0, 1.0, 0.0).astype(BF16)
    carry = carry_ref[...]
    rank_ref[...] = carry + _dot(tri_ref[...], routed)
    carry = carry + jnp.sum(routed.astype(F32), axis=0, keepdims=True)
    carry_ref[...] = carry
    cend_ref[0] = jnp.broadcast_to(carry, (SUBLANES, LANES))


def _moe_rank(comb):
    idx = np.arange(MOE_SB)
    tri = jnp.asarray(idx[:, None] > idx[None, :], BF16)
    return pl.pallas_call(
        _moe_rank_kernel,
        grid=(MOE_NB,),
        in_specs=[pl.BlockSpec((MOE_SB, LANES), lambda b: (b, 0)),
                  pl.BlockSpec((MOE_SB, MOE_SB), lambda b: (0, 0))],
        out_specs=(pl.BlockSpec((MOE_SB, LANES), lambda b: (b, 0)),
                   pl.BlockSpec((1, SUBLANES, LANES), lambda b: (b, 0, 0))),
        out_shape=(jax.ShapeDtypeStruct((TOKENS, LANES), F32),
                   jax.ShapeDtypeStruct((MOE_NB, SUBLANES, LANES), F32)),
        scratch_shapes=[pltpu.VMEM((1, LANES), F32)],
        compiler_params=pltpu.CompilerParams(dimension_semantics=("arbitrary",)),
        name="moe_rank",
    )(comb, tri)


def _count_le(ends, v):
    return jnp.sum(ends[None, :] <= v[:, None], axis=1).astype(jnp.int32)


def _moe_schedule(cend):
    i32 = jnp.int32
    counts = cend[-1]
    cstart = jnp.concatenate([jnp.zeros((1, N_EXPERTS), i32), cend[:-1]], axis=0)
    tiles_e = (counts + MOE_TM - 1) // MOE_TM
    tile_end = jnp.cumsum(tiles_e)
    tile_start = tile_end - tiles_e
    n_tiles = tile_end[-1]
    group_start = tile_start * MOE_TM

    d = jnp.arange(MOE_TILES, dtype=i32)
    t_valid = d < n_tiles
    t_exp = jnp.minimum(_count_le(tile_end, d), N_EXPERTS - 1)
    t_exp = jnp.where(t_valid, t_exp, t_exp[jnp.maximum(n_tiles - 1, 0)])

    per_tile = MOE_TM // MOE_GT
    g = jnp.arange(MOE_GTILES, dtype=i32)
    g_exp = t_exp[g // per_tile]
    r0 = (g // per_tile - tile_start[g_exp]) * MOE_TM + (g % per_tile) * MOE_GT
    r1 = jnp.minimum(r0 + MOE_GT, counts[g_exp])
    gt_valid = t_valid[g // per_tile] & (r0 < counts[g_exp])
    b_lo = jnp.sum(cend[:, g_exp] <= r0[None, :], axis=0).astype(i32)
    b_hi = jnp.sum(cstart[:, g_exp] < r1[None, :], axis=0).astype(i32) - 1
    gather = (g_exp, jnp.where(gt_valid, b_lo, 0), jnp.where(gt_valid, b_hi - b_lo + 1, 0))

    lo = (group_start[None, :] + cstart).reshape(-1)
    hi = (group_start[None, :] + cend).reshape(-1)
    w_lo = (lo // MOE_CALIGN) * MOE_CALIGN
    n_w = jnp.where(hi > lo, (hi - w_lo + MOE_CT - 1) // MOE_CT, 0)
    q_end = jnp.cumsum(n_w)
    q_start = q_end - n_w
    p = jnp.arange(MOE_CPAIRS, dtype=i32)
    be = jnp.minimum(_count_le(q_end, p), MOE_NB * N_EXPERTS - 1)
    c_row = jnp.clip(w_lo[be] + (p - q_start[be]) * MOE_CT, 0, (MOE_TILES - 1) * MOE_TM)
    combine = (q_start[::N_EXPERTS], q_end[N_EXPERTS - 1::N_EXPERTS], c_row, be % N_EXPERTS)
    return group_start, t_exp, t_valid.astype(i32), gather, combine


def _moe_gather_kernel(exp_ref, first_ref, count_ref, h_ref, pos_ref, xs_ref):
    g = pl.program_id(0)
    rows = (g * MOE_GT + lax.broadcasted_iota(jnp.int32, (MOE_GT, 1), 0)).astype(F32)
    xs_ref[...] = jnp.zeros((MOE_GT, D_MODEL), BF16)

    def body(j, carry):
        t0 = pl.multiple_of((first_ref[g] + j) * MOE_SB, MOE_SB)
        pos = pos_ref[pl.ds(exp_ref[g], 1), pl.ds(t0, MOE_SB)]
        onehot = jnp.where(rows == pos, 1.0, 0.0).astype(BF16)
        xs_ref[...] += _dot(onehot, h_ref[pl.ds(t0, MOE_SB), :]).astype(BF16)
        return carry

    lax.fori_loop(0, count_ref[g], body, 0)


def _moe_gather(sched, h, pos_t):
    spec = pltpu.PrefetchScalarGridSpec(
        num_scalar_prefetch=3,
        grid=(MOE_GTILES,),
        in_specs=[pl.BlockSpec(memory_space=pltpu.VMEM), pl.BlockSpec(memory_space=pltpu.VMEM)],
        out_specs=pl.BlockSpec((MOE_GT, D_MODEL), lambda g, e, b, c: (g, 0)),
    )
    return pl.pallas_call(
        _moe_gather_kernel,
        grid_spec=spec,
        out_shape=jax.ShapeDtypeStruct((MOE_TILES * MOE_TM, D_MODEL), BF16),
        compiler_params=pltpu.CompilerParams(
            dimension_semantics=("arbitrary",), vmem_limit_bytes=VMEM_LIMIT),
        name="moe_gather",
    )(*sched, h, pos_t)


def _moe_expert_kernel(exp_ref, valid_ref, xs_ref, w1_ref, w3_ref, w2_ref, y_ref, acc_ref):
    s = pl.program_id(2)
    d = pl.program_id(0) * MOE_PAIR + s
    f = pl.program_id(1)
    last = pl.num_programs(1) - 1

    @pl.when(valid_ref[d] == 1)
    def _():
        x = xs_ref[...]
        a = _dot(x, w1_ref[0])
        act = (a * _sigmoid(a) * _dot(x, w3_ref[0])).astype(BF16)
        part = _dot(act, w2_ref[0])

        @pl.when(f == 0)
        def _():
            acc_ref[s] = part

        @pl.when((f > 0) & (f < last))
        def _():
            acc_ref[s] += part

        @pl.when(f == last)
        def _():
            y_ref[...] = (acc_ref[s] + part).astype(BF16)

    @pl.when((valid_ref[d] == 0) & (f == last))
    def _():
        y_ref[...] = jnp.zeros((MOE_TM, D_MODEL), BF16)


def _moe_experts(t_exp, t_valid, xs, w1, w3, w2):
    tf = MOE_TF
    nf = D_FF_EXPERT // tf
    assert nf >= 2
    tile = lambda p, s: p * MOE_PAIR + s
    fidx = lambda d, f, v: f * v[d] + (nf - 1) * (1 - v[d])
    out_tile = lambda p, f, s: tile(p, jnp.where(f == nf - 1, s, 0))
    spec = pltpu.PrefetchScalarGridSpec(
        num_scalar_prefetch=2,
        grid=(MOE_TILES // MOE_PAIR, nf, MOE_PAIR),
        in_specs=[pl.BlockSpec((MOE_TM, D_MODEL), lambda p, f, s, e, v: (tile(p, s), 0)),
                  pl.BlockSpec((1, D_MODEL, tf), lambda p, f, s, e, v: (e[tile(p, s)], 0, fidx(tile(p, s), f, v))),
                  pl.BlockSpec((1, D_MODEL, tf), lambda p, f, s, e, v: (e[tile(p, s)], 0, fidx(tile(p, s), f, v))),
                  pl.BlockSpec((1, tf, D_MODEL), lambda p, f, s, e, v: (e[tile(p, s)], fidx(tile(p, s), f, v), 0))],
        out_specs=pl.BlockSpec((MOE_TM, D_MODEL), lambda p, f, s, e, v: (out_tile(p, f, s), 0)),
        scratch_shapes=[pltpu.VMEM((MOE_PAIR, MOE_TM, D_MODEL), F32)],
    )
    return pl.pallas_call(
        _moe_expert_kernel,
        grid_spec=spec,
        out_shape=jax.ShapeDtypeStruct((MOE_TILES * MOE_TM, D_MODEL), BF16),
        compiler_params=pltpu.CompilerParams(
            dimension_semantics=("arbitrary", "arbitrary", "arbitrary"), vmem_limit_bytes=VMEM_LIMIT),
        name="moe_experts",
    )(t_exp, t_valid, xs, w1, w3, w2)


def _moe_combine_kernel(first_ref, end_ref, row_ref, exp_ref, y_hbm, pos_ref, comb_ref, x1_ref, o_ref,
                        ybuf, sem):
    b = pl.program_id(0)
    first = first_ref[b]
    end = end_ref[b]

    def window_copy(p, slot):
        r0 = pl.multiple_of(row_ref[p], MOE_CALIGN)
        return pltpu.make_async_copy(y_hbm.at[pl.ds(r0, MOE_CT), :], ybuf.at[slot], sem.at[slot])

    for k in range(MOE_CBUF - 1):
        @pl.when(first + k < end)
        def _():
            window_copy(first + k, k).start()

    o_ref[...] = x1_ref[...]
    lane = lax.broadcasted_iota(jnp.int32, (1, LANES), 1)
    window = lax.broadcasted_iota(jnp.int32, (1, MOE_CT), 1)

    def body(p, carry):
        slot = (p - first) % MOE_CBUF
        window_copy(p, slot).wait()
        ahead = p + (MOE_CBUF - 1)

        @pl.when(ahead < end)
        def _():
            window_copy(ahead, (ahead - first) % MOE_CBUF).start()

        pick = lane == exp_ref[p]
        pos = jnp.sum(jnp.where(pick, pos_ref[...], 0.0), axis=-1, keepdims=True)
        wgt = jnp.sum(jnp.where(pick, comb_ref[...], 0.0), axis=-1, keepdims=True)
        cols = (row_ref[p] + window).astype(F32)
        onehot = jnp.where(pos == cols, 1.0, 0.0).astype(BF16)
        o_ref[...] += wgt * _dot(onehot, ybuf[slot])
        return carry

    lax.fori_loop(first, end, body, 0)


def _moe_combine(sched, y, pos, comb, x1):
    tok = lambda width: pl.BlockSpec((MOE_SB, width), lambda b, *_: (b, 0))
    spec = pltpu.PrefetchScalarGridSpec(
        num_scalar_prefetch=4,
        grid=(MOE_NB,),
        in_specs=[pl.BlockSpec(memory_space=pl.ANY), tok(LANES), tok(LANES), tok(D_MODEL)],
        out_specs=tok(D_MODEL),
        scratch_shapes=[pltpu.VMEM((MOE_CBUF, MOE_CT, D_MODEL), BF16), pltpu.SemaphoreType.DMA((MOE_CBUF,))],
    )
    return pl.pallas_call(
        _moe_combine_kernel,
        grid_spec=spec,
        out_shape=jax.ShapeDtypeStruct((TOKENS, D_MODEL), F32),
        compiler_params=pltpu.CompilerParams(
            dimension_semantics=("arbitrary",), vmem_limit_bytes=VMEM_LIMIT),
        name="moe_combine",
    )(*sched, y, pos, comb, x1)


def _moe_ffn(x1, h, comb, w1, w3, w2):
    rank, cend = _moe_rank(comb)
    cend = jnp.round(cend[:, 0, :N_EXPERTS]).astype(jnp.int32)
    group_start, t_exp, t_valid, gather, combine = _moe_schedule(cend)
    start = jnp.zeros((LANES,), F32).at[:N_EXPERTS].set(group_start.astype(F32))
    pos = jnp.where(comb > 0.0, start[None, :] + rank, -1.0)
    xs = _moe_gather(gather, h, pos[:, :N_EXPERTS].T)
    y = _moe_experts(t_exp, t_valid, xs, w1, w3, w2)
    return _moe_combine(combine, y, pos, comb, x1)


def _group_mean_matrix():
    idx = np.arange(LANES)
    return jnp.asarray((idx[:, None] // CONV_GROUP == idx[None, :] // CONV_GROUP) / CONV_GROUP, BF16)


def _overlap_matrix_t():
    c_start = np.arange(CMP_ROWS) * CMP_STRIDE
    j = np.arange(N_SEL)
    ovl = ((c_start[None, :] < (j[:, None] + 1) * SEL_BLOCK)
           & (c_start[None, :] + CMP_BLOCK > j[:, None] * SEL_BLOCK)
           & (np.arange(CMP_ROWS)[None, :] < N_CMP))
    return jnp.asarray(ovl, BF16)


def _dup(v):
    return jnp.concatenate([v, v], axis=-1)


def _zero_pad(v):
    return jnp.concatenate([v, jnp.zeros_like(v)], axis=-1)


def kernel(x, attn_norm_g, w_in, conv_dw_w, conv_dw_b, conv_ln_g, conv_ln_b, q_norm_g, k_norm_g,
           cmp_pos_k, cmp_pos_v, cmp_k_w1, cmp_k_b1, cmp_k_w2, cmp_k_b2,
           cmp_v_w1, cmp_v_b1, cmp_v_w2, cmp_v_b2, w_out, ffn_norm_g,
           dense_w1, dense_w3, dense_w2, router_w, moe_w1, moe_w3, moe_w2):
    assert x.shape == (BATCH, SEQ, D_MODEL) and x.dtype == F32
    bd = _group_mean_matrix()
    ovlt = _overlap_matrix_t()

    x2 = x.reshape(TOKENS, D_MODEL)
    for layer in range(DEPTH):
        w_cols, w_rows = _in_proj_weights(w_in[layer])
        qg = _dup(q_norm_g[layer])[None, :] * (LOG2_E * HEAD_DIM ** -0.5)
        kgs = _dup(k_norm_g[layer, 1])[None, :]
        kgw = _dup(k_norm_g[layer, 2])[None, :]
        u, q, kcvc, ks, kw, vst, vwt, gatet = _in_proj(
            x2, attn_norm_g[layer][None, :], w_cols, w_rows, qg, kgs, kgw, bd)

        conv = _conv_module(u.reshape(BATCH, SEQ, CONV_WIDTH), conv_dw_w[layer], conv_dw_b[layer][None, :],
                            conv_ln_g[layer][None, :], conv_ln_b[layer][None, :], bd)

        xr = kcvc.reshape(BATCH, SEQ, 2, N_KV_HEADS, HEAD_DIM).transpose(2, 0, 3, 1, 4)
        xr = xr.reshape(2, BATCH, N_KV_HEADS, CMP_ROWS, CMP_CHUNK)
        pos = jnp.stack([cmp_pos_k[layer], cmp_pos_v[layer]]).reshape(2, 1, CMP_BLOCK * HEAD_DIM)
        w1 = jnp.stack([cmp_k_w1[layer], cmp_v_w1[layer]]).astype(BF16)
        b1 = jnp.stack([cmp_k_b1[layer], cmp_v_b1[layer]])[:, None, :]
        w2 = _zero_pad(jnp.stack([cmp_k_w2[layer], cmp_v_w2[layer]])).astype(BF16)
        b2 = _zero_pad(jnp.stack([cmp_k_b2[layer], cmp_v_b2[layer]]))[:, None, :]
        kvc = _compress(xr, pos, w1, b1, w2, b2, _zero_pad(k_norm_g[layer, 0])[None, :])

        seq3 = lambda a: a.reshape(BATCH, SEQ, a.shape[-1])
        attn = _attention(seq3(q), gatet, kvc, seq3(ks), vst, seq3(kw), vwt, ovlt)

        w_o = w_out[layer].astype(BF16)
        conv2 = conv.reshape(TOKENS, CONV_WIDTH)
        attn2 = attn.reshape(TOKENS, ATTN_WIDTH)
        g_ffn = ffn_norm_g[layer][None, :]
        i = layer // 2
        if layer % 2 == 0:
            x1, h = _out_proj(x2, conv2, attn2, w_o, g_ffn)
            x2 = _dense_ffn(x1, h, dense_w1[i].astype(BF16), dense_w3[i].astype(BF16),
                            dense_w2[i].astype(BF16))
        else:
            r = jnp.pad(router_w[i], ((0, 0), (0, LANES - N_EXPERTS)))
            r_hi = r.astype(BF16)
            r_lo = (r - r_hi.astype(F32)).astype(BF16)
            x1, h, comb = _out_proj(x2, conv2, attn2, w_o, g_ffn, jnp.stack([r_hi, r_lo]))
            x2 = _moe_ffn(x1, h, comb, moe_w1[i].astype(BF16), moe_w3[i].astype(BF16),
                          moe_w2[i].astype(BF16))
    return x2.reshape(BATCH, SEQ, D_MODEL)
```
